```python
import jax, jax.numpy as jnp
from jax import lax
import numpy as np

D_MODEL = 2048
BATCH = 1
SEQ = 16384
DEPTH = 1
DEC_BATCH = 32
DEC_SEQ = 64
PAST_LEN = 4096

CHUNK = 64
QBLOCK = 128
N_HEADS = 16
N_KV_HEADS = 4
HEAD_DIM = 128
ROT_DIM = HEAD_DIM // 4
N_IDX_HEADS = 16
IDX_DIM = 64
IDX_ROT_DIM = IDX_DIM // 4
TOPK_MAX = 256
ROPE_THETA = 500000.0
D_CONV = D_MODEL
CONV_WIDTH = 3
D_FF = 5632
FFN_CONV_WIDTH = 3
RMS_EPS = 1e-6
NEG_INF = -1e30
Q_W = N_HEADS * HEAD_DIM
KV_W = N_KV_HEADS * HEAD_DIM
IQ_W = N_IDX_HEADS * IDX_DIM
IN_W = Q_W + 2 * KV_W + IQ_W + IDX_DIM + N_IDX_HEADS + 3 * D_CONV + 2 * D_MODEL

kernel_name = 'hybrid_dsa_shortconv_convffn_step'


def _rms_norm(x, g):
    xf = x.astype(jnp.float32)
    y = xf * lax.rsqrt(jnp.mean(xf * xf, axis=-1, keepdims=True) + RMS_EPS)
    return (y * g.astype(jnp.float32)).astype(x.dtype)


def _rope(x, pos, rot_dim):
    half = rot_dim // 2
    freqs = ROPE_THETA ** (-jnp.arange(half, dtype=jnp.float32) / half)
    ang = pos.astype(jnp.float32)[:, None] * freqs[None, :]
    cos = jnp.cos(ang)[None, :, None, :]
    sin = jnp.sin(ang)[None, :, None, :]
    xr = x[..., :rot_dim].astype(jnp.float32)
    x1, x2 = xr[..., :half], xr[..., half:]
    rot = jnp.concatenate([x1 * cos - x2 * sin, x2 * cos + x1 * sin], axis=-1).astype(x.dtype)
    return jnp.concatenate([rot, x[..., rot_dim:]], axis=-1)


def _causal_dwconv(x, prev, w):
    width = w.shape[0]
    t = x.shape[1]
    xp = jnp.concatenate([prev.astype(x.dtype), x], axis=1)
    y = w[0] * xp[:, 0:t]
    for j in range(1, width):
        y = y + w[j] * xp[:, j:j + t]
    return y, xp[:, xp.shape[1] - (width - 1):]


def _split_in(h):
    sizes = (Q_W, KV_W, KV_W, IQ_W, IDX_DIM, N_IDX_HEADS, D_CONV, D_CONV, D_CONV, D_MODEL, D_MODEL)
    offs = np.cumsum(sizes)[:-1].tolist()
    return jnp.split(h, offs, axis=-1)


def _dsa_block(q, qi, wi, q_pos, k, v, ki, k_pos, topk):
    b, t = q.shape[0], q.shape[1]
    logits = jnp.einsum('bthd,bsd->bths', qi, ki).astype(jnp.float32) * (IDX_DIM ** -0.5)
    score = jnp.einsum('bths,bth->bts', jax.nn.relu(logits), wi.astype(jnp.float32)) * (N_IDX_HEADS ** -0.5)
    adm = (k_pos[None, :] // CHUNK) <= (q_pos[:, None] // CHUNK)
    score = jnp.where(adm[None], score, NEG_INF)
    vals, idx = lax.top_k(score, topk)
    valid = vals > 0.5 * NEG_INF
    bidx = jnp.arange(b)[:, None, None]
    k_sel = k[bidx, idx]
    v_sel = v[bidx, idx]
    qg = q.reshape(b, t, N_KV_HEADS, N_HEADS // N_KV_HEADS, HEAD_DIM)
    s = jnp.einsum('btkgd,btskd->btkgs', qg, k_sel).astype(jnp.float32) * (HEAD_DIM ** -0.5)
    s = jnp.where(valid[:, :, None, None, :], s, NEG_INF)
    p = jax.nn.softmax(s, axis=-1)
    o = jnp.einsum('btkgs,btskd->btkgd', p.astype(v.dtype), v_sel)
    return o.reshape(b, t, Q_W)


def _mixer(xn, start, past, w_in, conv_w, w_o_attn, w_o_conv, w_out):
    b, t, _ = xn.shape
    pos = jnp.arange(t, dtype=jnp.int32) + start
    h = xn @ w_in
    q, k, v, qi, ki, wi, cb, cc, ch, ga, gc = _split_in(h)
    q = _rope(q.reshape(b, t, N_HEADS, HEAD_DIM), pos, ROT_DIM)
    k = _rope(k.reshape(b, t, N_KV_HEADS, HEAD_DIM), pos, ROT_DIM)
    v = v.reshape(b, t, N_KV_HEADS, HEAD_DIM)
    qi = _rope(qi.reshape(b, t, N_IDX_HEADS, IDX_DIM), pos, IDX_ROT_DIM)
    ki = _rope(ki[:, :, None, :], pos, IDX_ROT_DIM)[:, :, 0, :]
    if past is None:
        k_all, v_all, ki_all = k, v, ki
        conv_prev = jnp.zeros((b, CONV_WIDTH - 1, D_CONV), xn.dtype)
    else:
        k_c, v_c, ki_c, conv_prev = past
        k_all = jnp.concatenate([k_c.astype(k.dtype), k], axis=1)
        v_all = jnp.concatenate([v_c.astype(v.dtype), v], axis=1)
        ki_all = jnp.concatenate([ki_c.astype(ki.dtype), ki], axis=1)
    n_keys = k_all.shape[1]
    k_pos = jnp.arange(n_keys, dtype=jnp.int32)
    topk = min(TOPK_MAX, n_keys // 4)
    if past is None:
        nb = t // QBLOCK
        def blk(args):
            qb, qib, wib, pb = args
            return _dsa_block(qb, qib, wib, pb, k_all, v_all, ki_all, k_pos, topk)
        xs = (q.reshape(b, nb, QBLOCK, N_HEADS, HEAD_DIM).swapaxes(0, 1),
              qi.reshape(b, nb, QBLOCK, N_IDX_HEADS, IDX_DIM).swapaxes(0, 1),
              wi.reshape(b, nb, QBLOCK, N_IDX_HEADS).swapaxes(0, 1),
              pos.reshape(nb, QBLOCK))
        o_attn = lax.map(blk, xs).swapaxes(0, 1).reshape(b, t, Q_W)
    else:
        o_attn = _dsa_block(q, qi, wi, pos, k_all, v_all, ki_all, k_pos, topk)
    yconv, conv_new = _causal_dwconv(cc * ch, conv_prev, conv_w)
    yconv = cb * yconv
    g_a = jax.nn.sigmoid(ga.astype(jnp.float32)).astype(xn.dtype)
    g_c = jax.nn.sigmoid(gc.astype(jnp.float32)).astype(xn.dtype)
    merged = g_a * (o_attn @ w_o_attn) + g_c * (yconv @ w_o_conv)
    return merged @ w_out, (k, v, ki, conv_new)


def _conv_ffn(xn, prev, w_up, conv_w, w_down):
    up = xn @ w_up
    y, new_prev = _causal_dwconv(up, prev, conv_w)
    gate, val = jnp.split(y, 2, axis=-1)
    hid = jax.nn.gelu(gate, approximate=True) * val
    return hid @ w_down, new_prev


def _layer(x, start, past, n_mp, n_mq, w_in, conv_w, w_o_attn, w_o_conv, w_out, n_fp, n_fq, w_up, fconv_w, w_down):
    if past is None:
        mix_past = None
        ffn_prev = jnp.zeros((x.shape[0], FFN_CONV_WIDTH - 1, 2 * D_FF), x.dtype)
    else:
        mix_past = past[:4]
        ffn_prev = past[4]
    m, (k, v, ki, conv_new) = _mixer(_rms_norm(x, n_mp), start, mix_past, w_in, conv_w, w_o_attn, w_o_conv, w_out)
    x = x + _rms_norm(m, n_mq)
    f, ffn_new = _conv_ffn(_rms_norm(x, n_fp), ffn_prev, w_up, fconv_w, w_down)
    x = x + _rms_norm(f, n_fq)
    return x, (k, v, ki, conv_new, ffn_new)


def setup_inputs(seed: int = 0) -> dict:
    key = jax.random.key(seed)
    ks = jax.random.split(key, 20)
    f32 = jnp.float32
    def nrm(k, shape, scale):
        return jax.random.normal(k, shape, f32) * scale
    def gain(k):
        return 1.0 + 0.05 * jax.random.normal(k, (DEPTH, D_MODEL), f32)
    return {
        'x_prompt': nrm(ks[0], (BATCH, SEQ, D_MODEL), 1.0),
        'x_sample': nrm(ks[1], (DEC_BATCH, DEC_SEQ, D_MODEL), 1.0),
        'cache_k': nrm(ks[2], (DEPTH, DEC_BATCH, PAST_LEN, N_KV_HEADS, HEAD_DIM), 1.0),
        'cache_v': nrm(ks[3], (DEPTH, DEC_BATCH, PAST_LEN, N_KV_HEADS, HEAD_DIM), 1.0),
        'cache_k_idx': nrm(ks[4], (DEPTH, DEC_BATCH, PAST_LEN, IDX_DIM), 1.0),
        'state_conv': nrm(ks[5], (DEPTH, DEC_BATCH, CONV_WIDTH - 1, D_CONV), 0.5),
        'state_ffn_conv': nrm(ks[6], (DEPTH, DEC_BATCH, FFN_CONV_WIDTH - 1, 2 * D_FF), 1.0),
        'norm_mix_pre': gain(ks[7]),
        'norm_mix_post': gain(ks[8]),
        'w_in': nrm(ks[9], (DEPTH, D_MODEL, IN_W), D_MODEL ** -0.5),
        'conv_w': nrm(ks[10], (DEPTH, CONV_WIDTH, D_CONV), CONV_WIDTH ** -0.5),
        'w_o_attn': nrm(ks[11], (DEPTH, Q_W, D_MODEL), Q_W ** -0.5),
        'w_o_conv': nrm(ks[12], (DEPTH, D_CONV, D_MODEL), D_CONV ** -0.5),
        'w_out': nrm(ks[13], (DEPTH, D_MODEL, D_MODEL), D_MODEL ** -0.5),
        'norm_ffn_pre': gain(ks[14]),
        'norm_ffn_post': gain(ks[15]),
        'w_ffn_up': nrm(ks[16], (DEPTH, D_MODEL, 2 * D_FF), D_MODEL ** -0.5),
        'ffn_conv_w': nrm(ks[17], (DEPTH, FFN_CONV_WIDTH, 2 * D_FF), FFN_CONV_WIDTH ** -0.5),
        'w_ffn_down': nrm(ks[18], (DEPTH, D_FF, D_MODEL), D_FF ** -0.5),
    }


def reference(x_prompt, x_sample, cache_k, cache_v, cache_k_idx, state_conv, state_ffn_conv,
              norm_mix_pre, norm_mix_post, w_in, conv_w, w_o_attn, w_o_conv, w_out,
              norm_ffn_pre, norm_ffn_post, w_ffn_up, ffn_conv_w, w_ffn_down):
    yp, ys = x_prompt, x_sample
    past_len = cache_k.shape[2]
    sp = [[], [], [], [], []]
    ss = [[], [], [], [], []]
    for l in range(DEPTH):
        wl = (norm_mix_pre[l], norm_mix_post[l], w_in[l], conv_w[l], w_o_attn[l], w_o_conv[l], w_out[l],
              norm_ffn_pre[l], norm_ffn_post[l], w_ffn_up[l], ffn_conv_w[l], w_ffn_down[l])
        yp, st_p = _layer(yp, 0, None, *wl)
        past = (cache_k[l], cache_v[l], cache_k_idx[l], state_conv[l], state_ffn_conv[l])
        ys, st_s = _layer(ys, past_len, past, *wl)
        for i in range(5):
            sp[i].append(st_p[i])
            ss[i].append(st_s[i])
    k_p, v_p, ki_p, conv_p, ffn_p = [jnp.stack(a, axis=0) for a in sp]
    k_s, v_s, ki_s, conv_s, ffn_s = [jnp.stack(a, axis=0) for a in ss]
    return (yp, ys, k_p, v_p, ki_p, conv_p, ffn_p, k_s, v_s, ki_s, conv_s, ffn_s)
```

```python
import functools

import jax
import jax.numpy as jnp
import numpy as np
from jax import lax
from jax.experimental import pallas as pl
from jax.experimental.pallas import tpu as pltpu

F32 = jnp.float32
BF16 = jnp.bfloat16

D_MODEL = 2048
N_HEADS = 16
N_KV_HEADS = 4
HEAD_DIM = 128
ROT_DIM = HEAD_DIM // 4
N_IDX_HEADS = 16
IDX_DIM = 64
IDX_ROT_DIM = IDX_DIM // 4
CHUNK = 64
TOPK_MAX = 256
ROPE_THETA = 500000.0
D_FF = 5632
RMS_EPS = 1e-6
NEG_INF = -1e30
Q_W = N_HEADS * HEAD_DIM
KV_W = N_KV_HEADS * HEAD_DIM
IQ_W = N_IDX_HEADS * IDX_DIM
GROUP = N_HEADS // N_KV_HEADS

LANES = 128
TM = 512
TN = 512
TF = 512
VMEM_LIMIT = 56 * 1024 * 1024

J_Q = 0
J_K = J_Q + Q_W // TN
J_V = J_K + 1
J_QI = J_V + 1
J_KIWI = J_QI + IQ_W // TN
J_CONV = J_KIWI + 1
N_CBLK = D_MODEL // TN
J_GA = J_CONV + 3 * N_CBLK
J_GC = J_GA + N_CBLK
NJ = J_GC + N_CBLK

INT_MIN = -(2 ** 31)
_VALID_KEY = int(np.array(0.5 * NEG_INF, np.float32).view(np.int32)) ^ 0x7FFFFFFF


def _rms(x, g):
    return x * lax.rsqrt(jnp.mean(x * x, axis=-1, keepdims=True) + RMS_EPS) * g


def _sortable(x):
    bits = lax.bitcast_convert_type(x, jnp.int32)
    return jnp.where(bits < 0, bits ^ jnp.int32(0x7FFFFFFF), bits)


def _dwconv_seg(u, prev2, w3):
    row = lax.broadcasted_iota(jnp.int32, u.shape, 0)
    p0, p1 = prev2[0:1], prev2[1:2]
    s1 = jnp.where(row == 0, p1, pltpu.roll(u, 1, 0))
    s2 = jnp.where(row == 0, p0, jnp.where(row == 1, p1, pltpu.roll(u, 2, 0)))
    return w3[0:1] * s2 + w3[1:2] * s1 + w3[2:3] * u


def _conv_tile(u, w3, prevs, seg):
    nseg = len(prevs)
    ys, news = [], []
    for s in range(nseg):
        us = u[s * seg:(s + 1) * seg]
        ys.append(_dwconv_seg(us, prevs[s], w3))
        news.append(us[seg - 2:seg])
    y = ys[0] if nseg == 1 else jnp.concatenate(ys, axis=0)
    return y, news


def _proj_kernel(x_ref, g_ref, w_ref, c128_ref, d128_ref, c64_ref, d64_ref, cw_ref, st_ref,
                 q_ref, k_ref, kb_ref, v_ref, vb_ref, qi_ref, kiwi_ref, kia_ref, kib_ref,
                 yc_ref, cn_ref, ga_ref, gc_ref,
                 xn_s, cb_s, cc_s, carry_s, *, seg, carried):
    i = pl.program_id(0)
    j = pl.program_id(1)

    @pl.when(j == 0)
    def _():
        xn_s[...] = _rms(x_ref[...], g_ref[...]).astype(BF16)

    acc = jnp.dot(xn_s[...], w_ref[...], preferred_element_type=F32)
    lane = lax.broadcasted_iota(jnp.int32, (TM, LANES), 1)

    def rope128(xh):
        partner = jnp.where(lane < ROT_DIM // 2, pltpu.roll(xh, LANES - ROT_DIM // 2, 1),
                            pltpu.roll(xh, ROT_DIM // 2, 1))
        return xh * c128_ref[...] + partner * d128_ref[...]

    def rope64(xh, c, d):
        first = (lane & (IDX_DIM - 1)) < IDX_ROT_DIM // 2
        partner = jnp.where(first, pltpu.roll(xh, LANES - IDX_ROT_DIM // 2, 1),
                            pltpu.roll(xh, IDX_ROT_DIM // 2, 1))
        return xh * c + partner * d

    @pl.when(j < J_K)
    def _():
        for h in range(TN // LANES):
            sl = slice(h * LANES, (h + 1) * LANES)
            q_ref[:, sl] = rope128(acc[:, sl]).astype(BF16)

    @pl.when(j == J_K)
    def _():
        for h in range(TN // LANES):
            sl = slice(h * LANES, (h + 1) * LANES)
            r = rope128(acc[:, sl])
            k_ref[:, sl] = r
            kb_ref[:, sl] = r.astype(BF16)

    @pl.when(j == J_V)
    def _():
        v_ref[...] = acc
        vb_ref[...] = acc.astype(BF16)

    @pl.when(jnp.logical_and(j >= J_QI, j < J_KIWI))
    def _():
        for h in range(TN // LANES):
            sl = slice(h * LANES, (h + 1) * LANES)
            qi_ref[:, sl] = rope64(acc[:, sl], c64_ref[...], d64_ref[...]).astype(BF16)

    @pl.when(j == J_KIWI)
    def _():
        is_ki = lane < IDX_DIM
        c = jnp.where(is_ki, c64_ref[...], 1.0)
        d = jnp.where(is_ki, d64_ref[...], 0.0)
        r = rope64(acc[:, 0:LANES], c, d)
        kiwi_ref[...] = r
        ka = jnp.where(is_ki, r, 0.0)
        kia_ref[...] = ka.astype(BF16)
        kib_ref[...] = pltpu.roll(ka, IDX_DIM, 1).astype(BF16)

    jc = jnp.maximum(j - J_CONV, 0)
    in_conv = jnp.logical_and(j >= J_CONV, j < J_GA)
    cblk = jnp.minimum(jc // 3, N_CBLK - 1)
    part = jc % 3

    @pl.when(jnp.logical_and(in_conv, part == 0))
    def _():
        cb_s[...] = acc

    @pl.when(jnp.logical_and(in_conv, part == 1))
    def _():
        cc_s[...] = acc

    @pl.when(jnp.logical_and(in_conv, part == 2))
    def _():
        u = cc_s[...] * acc
        nseg = TM // seg
        if carried:
            @pl.when(i == 0)
            def _():
                carry_s[cblk] = jnp.zeros((8, TN), F32)
                carry_s[cblk, 0:2, :] = st_ref[0]
            prevs = [carry_s[cblk, 0:2, :]]
        else:
            prevs = [st_ref[s] for s in range(nseg)]
        y, news = _conv_tile(u, cw_ref[...], prevs, seg)
        yc_ref[...] = (cb_s[...] * y).astype(BF16)
        for s in range(nseg):
            cn_ref[s] = news[s]
        if carried:
            carry_s[cblk, 0:2, :] = news[0]

    @pl.when(jnp.logical_and(j >= J_GA, j < J_GC))
    def _():
        ga_ref[...] = 1.0 / (1.0 + jnp.exp(-acc))

    @pl.when(j >= J_GC)
    def _():
        gc_ref[...] = 1.0 / (1.0 + jnp.exp(-acc))


def _proj(x, g, w, tabs, conv_w, state, seg):
    t = x.shape[0]
    ni = t // TM
    carried = seg == t
    nseg_tile = 1 if carried else TM // seg
    c128, d128, c64, d64 = tabs

    def row(i, j):
        return (i, 0)

    def const(i, j):
        return (0, 0)

    def cblk(j):
        return jnp.minimum(jnp.maximum(j - J_CONV, 0) // 3, N_CBLK - 1)

    def st_map(i, j):
        return (0 if carried else i, 0, cblk(j))

    in_specs = [
        pl.BlockSpec((TM, D_MODEL), row),
        pl.BlockSpec((1, D_MODEL), const),
        pl.BlockSpec((D_MODEL, TN), lambda i, j: (0, j)),
        pl.BlockSpec((TM, LANES), row),
        pl.BlockSpec((TM, LANES), row),
        pl.BlockSpec((TM, LANES), row),
        pl.BlockSpec((TM, LANES), row),
        pl.BlockSpec((3, TN), lambda i, j: (0, cblk(j))),
        pl.BlockSpec((nseg_tile, 2, TN), st_map),
    ]
    out_shape = [
        jax.ShapeDtypeStruct((t, Q_W), BF16),
        jax.ShapeDtypeStruct((t, KV_W), F32),
        jax.ShapeDtypeStruct((t, KV_W), BF16),
        jax.ShapeDtypeStruct((t, KV_W), F32),
        jax.ShapeDtypeStruct((t, KV_W), BF16),
        jax.ShapeDtypeStruct((t, IQ_W), BF16),
        jax.ShapeDtypeStruct((t, LANES), F32),
        jax.ShapeDtypeStruct((t, LANES), BF16),
        jax.ShapeDtypeStruct((t, LANES), BF16),
        jax.ShapeDtypeStruct((t, D_MODEL), BF16),
        jax.ShapeDtypeStruct((ni * nseg_tile, 2, D_MODEL), F32),
        jax.ShapeDtypeStruct((t, D_MODEL), F32),
        jax.ShapeDtypeStruct((t, D_MODEL), F32),
    ]
    out_specs = [
        pl.BlockSpec((TM, TN), lambda i, j: (i, jnp.clip(j - J_Q, 0, J_K - J_Q - 1))),
        pl.BlockSpec((TM, TN), row),
        pl.BlockSpec((TM, TN), row),
        pl.BlockSpec((TM, TN), row),
        pl.BlockSpec((TM, TN), row),
        pl.BlockSpec((TM, TN), lambda i, j: (i, jnp.clip(j - J_QI, 0, J_KIWI - J_QI - 1))),
        pl.BlockSpec((TM, LANES), row),
        pl.BlockSpec((TM, LANES), row),
        pl.BlockSpec((TM, LANES), row),
        pl.BlockSpec((TM, TN), lambda i, j: (i, cblk(j))),
        pl.BlockSpec((nseg_tile, 2, TN), lambda i, j: (i, 0, cblk(j))),
        pl.BlockSpec((TM, TN), lambda i, j: (i, jnp.clip(j - J_GA, 0, N_CBLK - 1))),
        pl.BlockSpec((TM, TN), lambda i, j: (i, jnp.clip(j - J_GC, 0, N_CBLK - 1))),
    ]
    return pl.pallas_call(
        functools.partial(_proj_kernel, seg=min(seg, TM), carried=carried),
        grid=(ni, NJ),
        in_specs=in_specs,
        out_specs=out_specs,
        out_shape=out_shape,
        scratch_shapes=[
            pltpu.VMEM((TM, D_MODEL), BF16),
            pltpu.VMEM((TM, TN), F32),
            pltpu.VMEM((TM, TN), F32),
            pltpu.VMEM((N_CBLK, 8, TN), F32),
        ],
        compiler_params=pltpu.CompilerParams(
            dimension_semantics=("arbitrary", "arbitrary"), vmem_limit_bytes=VMEM_LIMIT),
        name="proj",
    )(x, g, w, c128, d128, c64, d64, conv_w, state)


def _select_kernel(qi_ref, kiwi_ref, kia_ref, kib_ref, bias_ref, key_s, wb_s, *,
                   tq, spb, nkb_total, topk, causal, n_valid, q_pos0):
    n = pl.program_id(0)
    npair = N_IDX_HEADS // 2
    kiwi = kiwi_ref[...]
    wscale = (IDX_DIM ** -0.5) * (N_IDX_HEADS ** -0.5)
    for h in range(N_IDX_HEADS):
        wb_s[h] = jnp.broadcast_to(kiwi[:, IDX_DIM + h:IDX_DIM + h + 1], (tq, LANES)) * wscale
    q2 = jnp.concatenate([qi_ref[:, p * LANES:(p + 1) * LANES] for p in range(npair)], axis=0)

    if causal:
        qpos0 = n * tq
        nkb = (qpos0 + tq + spb * LANES - 1) // (spb * LANES)
    else:
        qpos0 = q_pos0
        nkb = nkb_total
    lane = lax.broadcasted_iota(jnp.int32, (tq, LANES), 1)
    qchunk = (qpos0 + lax.broadcasted_iota(jnp.int32, (tq, LANES), 0)) >> 6
    nt = (((1,), (1,)), ((), ()))

    def score_blk(kb, carry):
        base = pl.multiple_of(kb * (spb * LANES), spb * LANES)
        ka = kia_ref[pl.ds(base, spb * LANES), :]
        kbm = kib_ref[pl.ds(base, spb * LANES), :]
        le = lax.dot_general(q2, ka, nt, preferred_element_type=F32)
        lo = lax.dot_general(q2, kbm, nt, preferred_element_type=F32)
        for c in range(spb):
            cs = slice(c * LANES, (c + 1) * LANES)
            acc = jnp.zeros((tq, LANES), F32)
            for p in range(npair):
                rs = slice(p * tq, (p + 1) * tq)
                acc = acc + jnp.maximum(le[rs, cs], 0.0) * wb_s[2 * p]
                acc = acc + jnp.maximum(lo[rs, cs], 0.0) * wb_s[2 * p + 1]
            col = base + c * LANES + lane
            adm = jnp.logical_and((col >> 6) <= qchunk, col < n_valid)
            key_s[kb * spb + c] = _sortable(jnp.where(adm, acc, NEG_INF))
        return carry

    lax.fori_loop(0, nkb, score_blk, 0)

    def count_ge(cand):
        def body(kb, acc):
            for c in range(spb):
                acc = acc + jnp.where(key_s[kb * spb + c] >= cand, 1.0, 0.0)
            return acc
        acc = lax.fori_loop(0, nkb, body, jnp.zeros((tq, LANES), F32))
        return jnp.sum(acc, axis=1, keepdims=True)

    def bit_body(it, thr):
        cand = thr + lax.shift_left(jnp.int32(1), 31 - it)
        return jnp.where(count_ge(cand) >= float(topk), cand, thr)

    thr = lax.fori_loop(0, 32, bit_body, jnp.full((tq, LANES), INT_MIN, jnp.int32))

    def emit(kb, carry):
        for c in range(spb):
            k = key_s[kb * spb + c]
            sel = jnp.logical_and(k >= thr, k > _VALID_KEY)
            bias_ref[kb * spb + c] = jnp.where(sel, 0.0, NEG_INF).astype(BF16)
        return carry

    lax.fori_loop(0, nkb, emit, 0)

    def fill(kb, carry):
        for c in range(spb):
            bias_ref[kb * spb + c] = jnp.full((tq, LANES), NEG_INF, BF16)
        return carry

    lax.fori_loop(nkb, nkb_total, fill, 0)


def _select(qi, kiwi, kia, kib, *, tq, spb, topk, causal, n_valid, q_pos0):
    t = qi.shape[0]
    nb = t // tq
    lk = kia.shape[1]
    nslab = lk // LANES
    nkb_total = nslab // spb

    def kmap(n):
        return (0 if causal else n, 0, 0)

    return pl.pallas_call(
        functools.partial(_select_kernel, tq=tq, spb=spb, nkb_total=nkb_total, topk=topk,
                          causal=causal, n_valid=n_valid, q_pos0=q_pos0),
        grid=(nb,),
        in_specs=[
            pl.BlockSpec((tq, IQ_W), lambda n: (n, 0)),
            pl.BlockSpec((tq, LANES), lambda n: (n, 0)),
            pl.BlockSpec((None, lk, LANES), kmap),
            pl.BlockSpec((None, lk, LANES), kmap),
        ],
        out_specs=pl.BlockSpec((nslab, tq, LANES), lambda n: (0, n, 0)),
        out_shape=jax.ShapeDtypeStruct((nslab, t, LANES), BF16),
        scratch_shapes=[
            pltpu.VMEM((nslab, tq, LANES), jnp.int32),
            pltpu.VMEM((N_IDX_HEADS, tq, LANES), F32),
        ],
        compiler_params=pltpu.CompilerParams(
            dimension_semantics=("arbitrary",), vmem_limit_bytes=VMEM_LIMIT),
        name="select",
    )(qi, kiwi, kia, kib)


def _attend_kernel(qb_ref, kb_ref, kbat_ref, last_ref, q_ref, k_ref, v_ref, b_ref, o_ref,
                   m_s, l_s, acc_s, *, tq, spb):
    s = pl.program_id(1)

    @pl.when(kb_ref[s] == 0)
    def _():
        m_s[...] = jnp.full(m_s.shape, 0.1 * NEG_INF, F32)
        l_s[...] = jnp.zeros(l_s.shape, F32)
        acc_s[...] = jnp.zeros(acc_s.shape, F32)

    q4 = jnp.concatenate([q_ref[:, h * LANES:(h + 1) * LANES] for h in range(GROUP)], axis=0)
    sc = lax.dot_general(q4, k_ref[...], (((1,), (1,)), ((), ())), preferred_element_type=F32)
    scale = HEAD_DIM ** -0.5
    slabs = []
    mx = None
    for c in range(spb):
        bc = b_ref[c].astype(F32)
        b4 = jnp.concatenate([bc] * GROUP, axis=0)
        sl = sc[:, c * LANES:(c + 1) * LANES] * scale + b4
        slabs.append(sl)
        mx = sl if mx is None else jnp.maximum(mx, sl)
    m_prev = m_s[...]
    m_new = jnp.maximum(m_prev, jnp.max(mx, axis=1, keepdims=True))
    alpha = jnp.exp(m_prev - m_new)
    psum = jnp.zeros((GROUP * tq, LANES), F32)
    ps = []
    for c in range(spb):
        p = jnp.exp(slabs[c] - m_new)
        psum = psum + p
        ps.append(p.astype(BF16))
    pmat = jnp.concatenate(ps, axis=1)
    l_s[...] = alpha * l_s[...] + jnp.sum(psum, axis=1, keepdims=True)
    acc_s[...] = alpha * acc_s[...] + jnp.dot(pmat, v_ref[...], preferred_element_type=F32)
    m_s[...] = m_new

    @pl.when(last_ref[s] == 1)
    def _():
        o = acc_s[...] / l_s[...]
        for h in range(GROUP):
            o_ref[:, h * LANES:(h + 1) * LANES] = o[h * tq:(h + 1) * tq].astype(BF16)


def _attend(q, k_all, v_all, bias, sched, *, tq, spb):
    t = q.shape[0]
    qb, kb, kbat, last = sched
    nsteps = qb.shape[0]
    blk = spb * LANES
    grid_spec = pltpu.PrefetchScalarGridSpec(
        num_scalar_prefetch=4,
        grid=(N_KV_HEADS, nsteps),
        in_specs=[
            pl.BlockSpec((tq, GROUP * HEAD_DIM), lambda g, s, qb, kb, kbat, last: (qb[s], g)),
            pl.BlockSpec((None, blk, HEAD_DIM), lambda g, s, qb, kb, kbat, last: (kbat[s], kb[s], g)),
            pl.BlockSpec((None, blk, HEAD_DIM), lambda g, s, qb, kb, kbat, last: (kbat[s], kb[s], g)),
            pl.BlockSpec((spb, tq, LANES), lambda g, s, qb, kb, kbat, last: (kb[s], qb[s], 0)),
        ],
        out_specs=pl.BlockSpec((tq, GROUP * HEAD_DIM), lambda g, s, qb, kb, kbat, last: (qb[s], g)),
        scratch_shapes=[
            pltpu.VMEM((GROUP * tq, LANES), F32),
            pltpu.VMEM((GROUP * tq, LANES), F32),
            pltpu.VMEM((GROUP * tq, HEAD_DIM), F32),
        ],
    )
    return pl.pallas_call(
        functools.partial(_attend_kernel, tq=tq, spb=spb),
        grid_spec=grid_spec,
        out_shape=jax.ShapeDtypeStruct((t, Q_W), BF16),
        compiler_params=pltpu.CompilerParams(
            dimension_semantics=("arbitrary", "arbitrary"), vmem_limit_bytes=VMEM_LIMIT),
        name="attend",
    )(qb, kb, kbat, last, q, k_all, v_all, bias)


def _causal_schedule(t, tq, blk):
    qb, kb, last = [], [], []
    for n in range(t // tq):
        nk = ((n + 1) * tq + blk - 1) // blk
        for k in range(nk):
            qb.append(n)
            kb.append(k)
            last.append(1 if k == nk - 1 else 0)
    z = np.zeros(len(qb), np.int32)
    return (jnp.asarray(qb, jnp.int32), jnp.asarray(kb, jnp.int32), jnp.asarray(z), jnp.asarray(last, jnp.int32))


def _batched_schedule(nbatch, nk):
    qb = np.repeat(np.arange(nbatch, dtype=np.int32), nk)
    kb = np.tile(np.arange(nk, dtype=np.int32), nbatch)
    last = (kb == nk - 1).astype(np.int32)
    return (jnp.asarray(qb), jnp.asarray(kb), jnp.asarray(qb), jnp.asarray(last))


def _merge_kernel(o_ref, yc_ref, ga_ref, gc_ref, woa_ref, woc_ref, wout_ref, x_ref, gq_ref, gp_ref,
                  x1_ref, xn2_ref, mg_s, m_s):
    j = pl.program_id(1)
    nblk = D_MODEL // TN

    @pl.when(j < nblk)
    def _():
        a = jnp.dot(o_ref[...], woa_ref[...], preferred_element_type=F32)
        c = jnp.dot(yc_ref[...], woc_ref[...], preferred_element_type=F32)
        mg_s[j] = (ga_ref[...] * a + gc_ref[...] * c).astype(BF16)

    @pl.when(j >= nblk)
    def _():
        mg = jnp.concatenate([mg_s[b] for b in range(nblk)], axis=1)
        m_s[j - nblk] = jnp.dot(mg, wout_ref[...], preferred_element_type=F32)

    @pl.when(j == 2 * nblk - 1)
    def _():
        m = jnp.concatenate([m_s[b] for b in range(nblk)], axis=1)
        x1 = x_ref[...] + _rms(m, gq_ref[...])
        x1_ref[...] = x1
        xn2_ref[...] = _rms(x1, gp_ref[...]).astype(BF16)


def _merge(o, yc, ga, gc, woa, woc, wout, x, gq, gp):
    t = x.shape[0]
    nblk = D_MODEL // TN

    def row(i, j):
        return (i, 0)

    def lo(i, j):
        return (i, jnp.minimum(j, nblk - 1))

    return pl.pallas_call(
        _merge_kernel,
        grid=(t // TM, 2 * nblk),
        in_specs=[
            pl.BlockSpec((TM, Q_W), row),
            pl.BlockSpec((TM, D_MODEL), row),
            pl.BlockSpec((TM, TN), lo),
            pl.BlockSpec((TM, TN), lo),
            pl.BlockSpec((Q_W, TN), lambda i, j: (0, jnp.minimum(j, nblk - 1))),
            pl.BlockSpec((D_MODEL, TN), lambda i, j: (0, jnp.minimum(j, nblk - 1))),
            pl.BlockSpec((D_MODEL, TN), lambda i, j: (0, jnp.maximum(j - nblk, 0))),
            pl.BlockSpec((TM, D_MODEL), row),
            pl.BlockSpec((1, D_MODEL), lambda i, j: (0, 0)),
            pl.BlockSpec((1, D_MODEL), lambda i, j: (0, 0)),
        ],
        out_specs=[pl.BlockSpec((TM, D_MODEL), row), pl.BlockSpec((TM, D_MODEL), row)],
        out_shape=[jax.ShapeDtypeStruct((t, D_MODEL), F32), jax.ShapeDtypeStruct((t, D_MODEL), BF16)],
        scratch_shapes=[pltpu.VMEM((nblk, TM, TN), BF16), pltpu.VMEM((nblk, TM, TN), F32)],
        compiler_params=pltpu.CompilerParams(
            dimension_semantics=("arbitrary", "arbitrary"), vmem_limit_bytes=VMEM_LIMIT),
        name="merge",
    )(o, yc, ga, gc, woa, woc, wout, x, gq, gp)


def _ffn_kernel(xn_ref, wg_ref, wv_ref, cwg_ref, cwv_ref, wd_ref, x1_ref, gq_ref, stg_ref, stv_ref,
                y_ref, ng_ref, nv_ref, acc_s, cg_s, cv_s, *, seg, carried):
    i = pl.program_id(0)
    jf = pl.program_id(1)
    nseg = TM // seg

    @pl.when(jf == 0)
    def _():
        acc_s[...] = jnp.zeros(acc_s.shape, F32)

    xn = xn_ref[...]

    def branch(w_ref, cw_ref, st_ref, carry_s, new_ref):
        up = jnp.dot(xn, w_ref[...], preferred_element_type=F32)
        if carried:
            @pl.when(i == 0)
            def _():
                carry_s[jf] = jnp.zeros((8, TF), F32)
                carry_s[jf, 0:2, :] = st_ref[0]
            prevs = [carry_s[jf, 0:2, :]]
        else:
            prevs = [st_ref[s] for s in range(nseg)]
        y, news = _conv_tile(up, cw_ref[...], prevs, seg)
        for s in range(nseg):
            new_ref[s] = news[s]
        if carried:
            carry_s[jf, 0:2, :] = news[0]
        return y

    gate = branch(wg_ref, cwg_ref, stg_ref, cg_s, ng_ref)
    val = branch(wv_ref, cwv_ref, stv_ref, cv_s, nv_ref)
    c0 = np.float32(np.sqrt(2.0 / np.pi))
    gelu = 0.5 * gate * (1.0 + jnp.tanh(c0 * (gate + 0.044715 * (gate * gate * gate))))
    hid = (gelu * val).astype(BF16)
    acc_s[...] += jnp.dot(hid, wd_ref[...], preferred_element_type=F32)

    @pl.when(jf == pl.num_programs(1) - 1)
    def _():
        y_ref[...] = x1_ref[...] + _rms(acc_s[...], gq_ref[...])


def _ffn(xn2, w_up, cw, w_down, x1, gq, state, seg):
    t = x1.shape[0]
    nf = D_FF // TF
    carried = seg == t
    nseg_tile = 1 if carried else TM // seg

    def row(i, j):
        return (i, 0)

    def stg(i, j):
        return (0 if carried else i, 0, j)

    def stv(i, j):
        return (0 if carried else i, 0, j + nf)

    return pl.pallas_call(
        functools.partial(_ffn_kernel, seg=min(seg, TM), carried=carried),
        grid=(t // TM, nf),
        in_specs=[
            pl.BlockSpec((TM, D_MODEL), row),
            pl.BlockSpec((D_MODEL, TF), lambda i, j: (0, j)),
            pl.BlockSpec((D_MODEL, TF), lambda i, j: (0, j + nf)),
            pl.BlockSpec((3, TF), lambda i, j: (0, j)),
            pl.BlockSpec((3, TF), lambda i, j: (0, j + nf)),
            pl.BlockSpec((TF, D_MODEL), lambda i, j: (j, 0)),
            pl.BlockSpec((TM, D_MODEL), row),
            pl.BlockSpec((1, D_MODEL), lambda i, j: (0, 0)),
            pl.BlockSpec((nseg_tile, 2, TF), stg),
            pl.BlockSpec((nseg_tile, 2, TF), stv),
        ],
        out_specs=[
            pl.BlockSpec((TM, D_MODEL), row),
            pl.BlockSpec((nseg_tile, 2, TF), lambda i, j: (i, 0, j)),
            pl.BlockSpec((nseg_tile, 2, TF), lambda i, j: (i, 0, j)),
        ],
        out_shape=[
            jax.ShapeDtypeStruct((t, D_MODEL), F32),
            jax.ShapeDtypeStruct((t // TM * nseg_tile, 2, D_FF), F32),
            jax.ShapeDtypeStruct((t // TM * nseg_tile, 2, D_FF), F32),
        ],
        scratch_shapes=[
            pltpu.VMEM((TM, D_MODEL), F32),
            pltpu.VMEM((nf, 8, TF), F32),
            pltpu.VMEM((nf, 8, TF), F32),
        ],
        compiler_params=pltpu.CompilerParams(
            dimension_semantics=("arbitrary", "arbitrary"), vmem_limit_bytes=VMEM_LIMIT),
        name="ffn",
    )(xn2, w_up, w_up, cw, cw, w_down, x1, gq, state, state)


def _rope_tables(pos, rot, width):
    half = rot // 2
    freqs = ROPE_THETA ** (-jnp.arange(half, dtype=F32) / half)
    ang = pos.astype(F32)[:, None] * freqs[None, :]
    cos, sin = jnp.cos(ang), jnp.sin(ang)
    t = pos.shape[0]
    c = jnp.concatenate([cos, cos, jnp.ones((t, width - rot), F32)], axis=1)
    d = jnp.concatenate([-sin, sin, jnp.zeros((t, width - rot), F32)], axis=1)
    reps = LANES // width
    return jnp.tile(c, (1, reps)), jnp.tile(d, (1, reps))


def _relayout_w_in(w):
    o = np.cumsum([0, Q_W, KV_W, KV_W, IQ_W, IDX_DIM, N_IDX_HEADS, D_MODEL, D_MODEL, D_MODEL, D_MODEL, D_MODEL])
    q, k, v, qi = (w[:, o[a]:o[a + 1]] for a in range(4))
    kiwi = w[:, o[4]:o[6]]
    cb, cc, ch, ga, gc = (w[:, o[a]:o[a + 1]] for a in range(6, 11))
    parts = [q, k, v, qi, kiwi, jnp.zeros((D_MODEL, TN - kiwi.shape[1]), w.dtype)]
    for c in range(N_CBLK):
        sl = slice(c * TN, (c + 1) * TN)
        parts += [cb[:, sl], cc[:, sl], ch[:, sl]]
    parts += [ga, gc]
    return jnp.concatenate(parts, axis=1).astype(BF16)


def _stream(x, pos, seg, past, weights, *, tq, spb_sel, spb_att):
    (g_mp, g_mq, w_in_r, conv_w, woa, woc, wout, g_fp, g_fq, w_up, fconv_w, w_down) = weights
    t = x.shape[0]
    nseq = t // seg
    tabs = _rope_tables(pos, ROT_DIM, HEAD_DIM) + _rope_tables(pos, IDX_ROT_DIM, IDX_DIM)
    if past is None:
        conv_state = jnp.zeros((nseq, 2, D_MODEL), F32)
        ffn_state = jnp.zeros((nseq, 2, 2 * D_FF), F32)
    else:
        conv_state, ffn_state = past[3], past[4]

    (q, k, kb, v, vb, qi, kiwi, kia, kib, yc, conv_new, ga, gc) = _proj(
        x, g_mp, w_in_r, tabs, conv_w, conv_state, seg)

    if past is None:
        k_all, v_all = kb[None], vb[None]
        kia_all, kib_all = kia[None], kib[None]
        n_keys = t
        sched = _causal_schedule(t, tq, spb_att * LANES)
        q_pos0 = 0
    else:
        cache_k, cache_v, cache_ki = past[0], past[1], past[2]
        plen = cache_k.shape[1]
        n_keys = plen + seg
        lk = -(-n_keys // (spb_att * LANES)) * (spb_att * LANES)
        pad = lk - n_keys

        def cat(c, new, width):
            parts = [c, new.reshape(nseq, seg, width)]
            if pad:
                parts.append(jnp.zeros((nseq, pad, width), BF16))
            return jnp.concatenate(parts, axis=1)

        k_all = cat(cache_k.reshape(nseq, plen, KV_W).astype(BF16), kb, KV_W)
        v_all = cat(cache_v.reshape(nseq, plen, KV_W).astype(BF16), vb, KV_W)
        cki = cache_ki.astype(BF16)
        zk = jnp.zeros_like(cki)
        kia_all = cat(jnp.concatenate([cki, zk], axis=-1), kia, LANES)
        kib_all = cat(jnp.concatenate([zk, cki], axis=-1), kib, LANES)
        sched = _batched_schedule(nseq, lk // (spb_att * LANES))
        q_pos0 = plen
    topk = min(TOPK_MAX, n_keys // 4)

    bias = _select(qi, kiwi, kia_all, kib_all, tq=tq, spb=spb_sel, topk=topk,
                   causal=past is None, n_valid=n_keys, q_pos0=q_pos0)
    o = _attend(q, k_all, v_all, bias, sched, tq=tq, spb=spb_att)
    x1, xn2 = _merge(o, yc, ga, gc, woa, woc, wout, x, g_mq, g_fp)
    y, ffn_g, ffn_v = _ffn(xn2, w_up, fconv_w, w_down, x1, g_fq, ffn_state, seg)
    ffn_new = jnp.concatenate([ffn_g, ffn_v], axis=-1)
    return y, k, v, kiwi[:, :IDX_DIM], conv_new[-nseq:], ffn_new[-nseq:]


def kernel(x_prompt, x_sample, cache_k, cache_v, cache_k_idx, state_conv, state_ffn_conv, norm_mix_pre, norm_mix_post, w_in, conv_w, w_o_attn, w_o_conv, w_out, norm_ffn_pre, norm_ffn_post, w_ffn_up, ffn_conv_w, w_ffn_down):
    depth = w_in.shape[0]
    assert depth == 1, "single-layer step"
    b, seq, _ = x_prompt.shape
    assert b == 1
    db, dseq, _ = x_sample.shape
    plen = cache_k.shape[2]
    assert dseq == CHUNK and plen % CHUNK == 0

    weights = (
        norm_mix_pre, norm_mix_post, _relayout_w_in(w_in[0]), conv_w[0],
        w_o_attn[0].astype(BF16), w_o_conv[0].astype(BF16), w_out[0].astype(BF16),
        norm_ffn_pre, norm_ffn_post, w_ffn_up[0].astype(BF16), ffn_conv_w[0], w_ffn_down[0].astype(BF16),
    )

    pos_p = jnp.arange(seq, dtype=jnp.int32)
    yp, kp, vp, kip, convp, ffnp = _stream(
        x_prompt.reshape(seq, D_MODEL), pos_p, seq, None, weights, tq=128, spb_sel=4, spb_att=4)

    pos_s = jnp.tile(jnp.arange(dseq, dtype=jnp.int32) + plen, db)
    past = (cache_k[0], cache_v[0], cache_k_idx[0], state_conv[0], state_ffn_conv[0])
    n_keys = plen + dseq
    spb_s = _sample_slabs(n_keys)
    ys, ks, vs, kis, convs, ffns = _stream(
        x_sample.reshape(db * dseq, D_MODEL), pos_s, dseq, past, weights, tq=dseq, spb_sel=spb_s, spb_att=spb_s)

    return (
        yp.reshape(1, seq, D_MODEL), ys.reshape(db, dseq, D_MODEL),
        kp.reshape(1, 1, seq, N_KV_HEADS, HEAD_DIM), vp.reshape(1, 1, seq, N_KV_HEADS, HEAD_DIM),
        kip.reshape(1, 1, seq, IDX_DIM), convp.reshape(1, 1, 2, D_MODEL), ffnp.reshape(1, 1, 2, 2 * D_FF),
        ks.reshape(1, db, dseq, N_KV_HEADS, HEAD_DIM), vs.reshape(1, db, dseq, N_KV_HEADS, HEAD_DIM),
        kis.reshape(1, db, dseq, IDX_DIM), convs.reshape(1, db, 2, D_MODEL), ffns.reshape(1, db, 2, 2 * D_FF),
    )


def _sample_slabs(n_keys):
    nslab = -(-n_keys // LANES)
    best = 1
    for d in range(1, nslab + 1):
        if nslab % d == 0 and d <= 11:
            best = d
    return best
```

```python
import functools

import jax
import jax.numpy as jnp
import numpy as np
from jax import lax
from jax.experimental import pallas as pl
from jax.experimental.pallas import tpu as pltpu

F32 = jnp.float32
BF16 = jnp.bfloat16

D_MODEL = 2048
N_HEADS = 16
N_KV_HEADS = 4
HEAD_DIM = 128
ROT_DIM = HEAD_DIM // 4
N_IDX_HEADS = 16
IDX_DIM = 64
IDX_ROT_DIM = IDX_DIM // 4
CHUNK = 64
TOPK_MAX = 256
ROPE_THETA = 500000.0
D_FF = 5632
RMS_EPS = 1e-6
NEG_INF = -1e30
Q_W = N_HEADS * HEAD_DIM
KV_W = N_KV_HEADS * HEAD_DIM
IQ_W = N_IDX_HEADS * IDX_DIM
GROUP = N_HEADS // N_KV_HEADS

LANES = 128
TM = 512
TN = 512
TF = 512
VMEM_LIMIT = 56 * 1024 * 1024

J_Q = 0
J_K = J_Q + Q_W // TN
J_V = J_K + 1
J_QI = J_V + 1
J_KIWI = J_QI + IQ_W // TN
J_CONV = J_KIWI + 1
N_CBLK = D_MODEL // TN
J_GA = J_CONV + 3 * N_CBLK
J_GC = J_GA + N_CBLK
NJ = J_GC + N_CBLK

INT_MIN = -(2 ** 31)
_VALID_KEY = int(np.array(0.5 * NEG_INF, np.float32).view(np.int32)) ^ 0x7FFFFFFF


def _rms(x, g):
    return x * lax.rsqrt(jnp.mean(x * x, axis=-1, keepdims=True) + RMS_EPS) * g


def _sortable(x):
    bits = lax.bitcast_convert_type(x, jnp.int32)
    return jnp.where(bits < 0, bits ^ jnp.int32(0x7FFFFFFF), bits)


def _dwconv_seg(u, prev2, w3):
    row = lax.broadcasted_iota(jnp.int32, u.shape, 0)
    p0, p1 = prev2[0:1], prev2[1:2]
    s1 = jnp.where(row == 0, p1, pltpu.roll(u, 1, 0))
    s2 = jnp.where(row == 0, p0, jnp.where(row == 1, p1, pltpu.roll(u, 2, 0)))
    return w3[0:1] * s2 + w3[1:2] * s1 + w3[2:3] * u


def _conv_tile(u, w3, prevs, seg):
    nseg = len(prevs)
    ys, news = [], []
    for s in range(nseg):
        us = u[s * seg:(s + 1) * seg]
        ys.append(_dwconv_seg(us, prevs[s], w3))
        news.append(us[seg - 2:seg])
    y = ys[0] if nseg == 1 else jnp.concatenate(ys, axis=0)
    return y, news


def _proj_kernel(x_ref, g_ref, w_ref, c128_ref, d128_ref, c64_ref, d64_ref, cw_ref, st_ref,
                 q_ref, k_ref, kb_ref, v_ref, vb_ref, qi_ref, kiwi_ref, kia_ref, kib_ref,
                 yc_ref, cn_ref, ga_ref, gc_ref,
                 xn_s, cb_s, cc_s, carry_s, *, seg, carried):
    i = pl.program_id(0)
    j = pl.program_id(1)

    @pl.when(j == 0)
    def _():
        xn_s[...] = _rms(x_ref[...], g_ref[...]).astype(BF16)

    acc = jnp.dot(xn_s[...], w_ref[...], preferred_element_type=F32)
    lane = lax.broadcasted_iota(jnp.int32, (TM, LANES), 1)

    def rope128(xh):
        partner = jnp.where(lane < ROT_DIM // 2, pltpu.roll(xh, LANES - ROT_DIM // 2, 1),
                            pltpu.roll(xh, ROT_DIM // 2, 1))
        return xh * c128_ref[...] + partner * d128_ref[...]

    def rope64(xh, c, d):
        first = (lane & (IDX_DIM - 1)) < IDX_ROT_DIM // 2
        partner = jnp.where(first, pltpu.roll(xh, LANES - IDX_ROT_DIM // 2, 1),
                            pltpu.roll(xh, IDX_ROT_DIM // 2, 1))
        return xh * c + partner * d

    @pl.when(j < J_K)
    def _():
        for h in range(TN // LANES):
            sl = slice(h * LANES, (h + 1) * LANES)
            q_ref[:, sl] = rope128(acc[:, sl]).astype(BF16)

    @pl.when(j == J_K)
    def _():
        for h in range(TN // LANES):
            sl = slice(h * LANES, (h + 1) * LANES)
            r = rope128(acc[:, sl])
            k_ref[:, sl] = r
            kb_ref[:, sl] = r.astype(BF16)

    @pl.when(j == J_V)
    def _():
        v_ref[...] = acc
        vb_ref[...] = acc.astype(BF16)

    @pl.when(jnp.logical_and(j >= J_QI, j < J_KIWI))
    def _():
        for h in range(TN // LANES):
            sl = slice(h * LANES, (h + 1) * LANES)
            qi_ref[:, sl] = rope64(acc[:, sl], c64_ref[...], d64_ref[...]).astype(BF16)

    @pl.when(j == J_KIWI)
    def _():
        is_ki = lane < IDX_DIM
        c = jnp.where(is_ki, c64_ref[...], 1.0)
        d = jnp.where(is_ki, d64_ref[...], 0.0)
        r = rope64(acc[:, 0:LANES], c, d)
        kiwi_ref[...] = r
        ka = jnp.where(is_ki, r, 0.0)
        kia_ref[...] = ka.astype(BF16)
        kib_ref[...] = pltpu.roll(ka, IDX_DIM, 1).astype(BF16)

    jc = jnp.maximum(j - J_CONV, 0)
    in_conv = jnp.logical_and(j >= J_CONV, j < J_GA)
    cblk = jnp.minimum(jc // 3, N_CBLK - 1)
    part = jc % 3

    @pl.when(jnp.logical_and(in_conv, part == 0))
    def _():
        cb_s[...] = acc

    @pl.when(jnp.logical_and(in_conv, part == 1))
    def _():
        cc_s[...] = acc

    @pl.when(jnp.logical_and(in_conv, part == 2))
    def _():
        u = cc_s[...] * acc
        nseg = TM // seg
        if carried:
            @pl.when(i == 0)
            def _():
                carry_s[cblk] = jnp.zeros((8, TN), F32)
                carry_s[cblk, 0:2, :] = st_ref[0]
            prevs = [carry_s[cblk, 0:2, :]]
        else:
            prevs = [st_ref[s] for s in range(nseg)]
        y, news = _conv_tile(u, cw_ref[...], prevs, seg)
        yc_ref[...] = (cb_s[...] * y).astype(BF16)
        for s in range(nseg):
            cn_ref[s] = news[s]
        if carried:
            carry_s[cblk, 0:2, :] = news[0]

    @pl.when(jnp.logical_and(j >= J_GA, j < J_GC))
    def _():
        ga_ref[...] = 1.0 / (1.0 + jnp.exp(-acc))

    @pl.when(j >= J_GC)
    def _():
        gc_ref[...] = 1.0 / (1.0 + jnp.exp(-acc))


def _proj(x, g, w, tabs, conv_w, state, seg):
    t = x.shape[0]
    ni = t // TM
    carried = seg == t
    nseg_tile = 1 if carried else TM // seg
    c128, d128, c64, d64 = tabs

    def row(i, j):
        return (i, 0)

    def const(i, j):
        return (0, 0)

    def cblk(j):
        return jnp.minimum(jnp.maximum(j - J_CONV, 0) // 3, N_CBLK - 1)

    def st_map(i, j):
        return (0 if carried else i, 0, cblk(j))

    in_specs = [
        pl.BlockSpec((TM, D_MODEL), row),
        pl.BlockSpec((1, D_MODEL), const),
        pl.BlockSpec((D_MODEL, TN), lambda i, j: (0, j)),
        pl.BlockSpec((TM, LANES), row),
        pl.BlockSpec((TM, LANES), row),
        pl.BlockSpec((TM, LANES), row),
        pl.BlockSpec((TM, LANES), row),
        pl.BlockSpec((3, TN), lambda i, j: (0, cblk(j))),
        pl.BlockSpec((nseg_tile, 2, TN), st_map),
    ]
    out_shape = [
        jax.ShapeDtypeStruct((t, Q_W), BF16),
        jax.ShapeDtypeStruct((t, KV_W), F32),
        jax.ShapeDtypeStruct((t, KV_W), BF16),
        jax.ShapeDtypeStruct((t, KV_W), F32),
        jax.ShapeDtypeStruct((t, KV_W), BF16),
        jax.ShapeDtypeStruct((t, IQ_W), BF16),
        jax.ShapeDtypeStruct((t, LANES), F32),
        jax.ShapeDtypeStruct((t, LANES), BF16),
        jax.ShapeDtypeStruct((t, LANES), BF16),
        jax.ShapeDtypeStruct((t, D_MODEL), BF16),
        jax.ShapeDtypeStruct((ni * nseg_tile, 2, D_MODEL), F32),
        jax.ShapeDtypeStruct((t, D_MODEL), F32),
        jax.ShapeDtypeStruct((t, D_MODEL), F32),
    ]
    out_specs = [
        pl.BlockSpec((TM, TN), lambda i, j: (i, jnp.clip(j - J_Q, 0, J_K - J_Q - 1))),
        pl.BlockSpec((TM, TN), row),
        pl.BlockSpec((TM, TN), row),
        pl.BlockSpec((TM, TN), row),
        pl.BlockSpec((TM, TN), row),
        pl.BlockSpec((TM, TN), lambda i, j: (i, jnp.clip(j - J_QI, 0, J_KIWI - J_QI - 1))),
        pl.BlockSpec((TM, LANES), row),
        pl.BlockSpec((TM, LANES), row),
        pl.BlockSpec((TM, LANES), row),
        pl.BlockSpec((TM, TN), lambda i, j: (i, cblk(j))),
        pl.BlockSpec((nseg_tile, 2, TN), lambda i, j: (i, 0, cblk(j))),
        pl.BlockSpec((TM, TN), lambda i, j: (i, jnp.clip(j - J_GA, 0, N_CBLK - 1))),
        pl.BlockSpec((TM, TN), lambda i, j: (i, jnp.clip(j - J_GC, 0, N_CBLK - 1))),
    ]
    return pl.pallas_call(
        functools.partial(_proj_kernel, seg=min(seg, TM), carried=carried),
        grid=(ni, NJ),
        in_specs=in_specs,
        out_specs=out_specs,
        out_shape=out_shape,
        scratch_shapes=[
            pltpu.VMEM((TM, D_MODEL), BF16),
            pltpu.VMEM((TM, TN), F32),
            pltpu.VMEM((TM, TN), F32),
            pltpu.VMEM((N_CBLK, 8, TN), F32),
        ],
        compiler_params=pltpu.CompilerParams(
            dimension_semantics=("arbitrary", "arbitrary"), vmem_limit_bytes=VMEM_LIMIT),
        name="proj",
    )(x, g, w, c128, d128, c64, d64, conv_w, state)


def _select_kernel(qi_ref, kiwi_ref, kia_ref, kib_ref, bias_ref, key_s, wb_s, *,
                   tq, spb, nkb_total, topk, causal, n_valid, q_pos0):
    n = pl.program_id(0)
    npair = N_IDX_HEADS // 2
    kiwi = kiwi_ref[...]
    wscale = (IDX_DIM ** -0.5) * (N_IDX_HEADS ** -0.5)
    for h in range(N_IDX_HEADS):
        wb_s[h] = jnp.broadcast_to(kiwi[:, IDX_DIM + h:IDX_DIM + h + 1], (tq, LANES)) * wscale
    q2 = jnp.concatenate([qi_ref[:, p * LANES:(p + 1) * LANES] for p in range(npair)], axis=0)

    if causal:
        qpos0 = n * tq
        nkb = (qpos0 + tq + spb * LANES - 1) // (spb * LANES)
    else:
        qpos0 = q_pos0
        nkb = nkb_total
    lane = lax.broadcasted_iota(jnp.int32, (tq, LANES), 1)
    qchunk = (qpos0 + lax.broadcasted_iota(jnp.int32, (tq, LANES), 0)) >> 6
    nt = (((1,), (1,)), ((), ()))

    def score_blk(kb, carry):
        m1, m2 = carry
        base = pl.multiple_of(kb * (spb * LANES), spb * LANES)
        ka = kia_ref[pl.ds(base, spb * LANES), :]
        kbm = kib_ref[pl.ds(base, spb * LANES), :]
        le = lax.dot_general(q2, ka, nt, preferred_element_type=F32)
        lo = lax.dot_general(q2, kbm, nt, preferred_element_type=F32)
        for c in range(spb):
            cs = slice(c * LANES, (c + 1) * LANES)
            acc = jnp.zeros((tq, LANES), F32)
            for p in range(npair):
                rs = slice(p * tq, (p + 1) * tq)
                acc = acc + jnp.maximum(le[rs, cs], 0.0) * wb_s[2 * p]
                acc = acc + jnp.maximum(lo[rs, cs], 0.0) * wb_s[2 * p + 1]
            col = base + c * LANES + lane
            adm = jnp.logical_and((col >> 6) <= qchunk, col < n_valid)
            sc = jnp.where(adm, acc, NEG_INF)
            m2 = jnp.maximum(m2, jnp.minimum(m1, sc))
            m1 = jnp.maximum(m1, sc)
            key_s[kb * spb + c] = _sortable(sc)
        return m1, m2

    neg = jnp.full((tq, LANES), NEG_INF, F32)
    m1, m2 = lax.fori_loop(0, nkb, score_blk, (neg, neg))

    def count_ge(cand):
        def body(kb, acc):
            for c in range(spb):
                acc = acc + jnp.where(key_s[kb * spb + c] >= cand, 1.0, 0.0)
            return acc
        acc = lax.fori_loop(0, nkb, body, jnp.zeros((tq, LANES), F32))
        return jnp.sum(acc, axis=1, keepdims=True)

    ones = jnp.ones((tq, LANES), jnp.int32)
    lo0 = _sortable(jnp.min(m2, axis=1, keepdims=True)) * ones
    hi0 = _sortable(jnp.max(m1, axis=1, keepdims=True)) * ones + 1
    kf = float(topk)

    def unresolved(lo_k, hi_k, c_lo):
        open_ = jnp.logical_and(c_lo != kf, (hi_k - lo_k) != 1)
        return jnp.max(jnp.where(open_, 1.0, 0.0))

    def bis_cond(st):
        it, _, _, _, flag = st
        return jnp.logical_and(it < 33, flag > 0.0)

    def bis_body(st):
        it, lo_k, hi_k, c_lo, _ = st
        mid = lo_k + lax.shift_right_logical(hi_k - lo_k, 1)
        cnt = count_ge(mid)
        ge = cnt >= kf
        lo_k = jnp.where(ge, mid, lo_k)
        hi_k = jnp.where(ge, hi_k, mid)
        c_lo = jnp.where(ge, cnt, c_lo)
        return it + 1, lo_k, hi_k, c_lo, unresolved(lo_k, hi_k, c_lo)

    c0 = jnp.full((tq, LANES), -1.0, F32)
    _, thr, _, _, _ = lax.while_loop(
        bis_cond, bis_body, (jnp.int32(0), lo0, hi0, c0, unresolved(lo0, hi0, c0)))

    def emit(kb, carry):
        for c in range(spb):
            k = key_s[kb * spb + c]
            sel = jnp.logical_and(k >= thr, k > _VALID_KEY)
            bias_ref[kb * spb + c] = jnp.where(sel, 0.0, NEG_INF).astype(BF16)
        return carry

    lax.fori_loop(0, nkb, emit, 0)

    def fill(kb, carry):
        for c in range(spb):
            bias_ref[kb * spb + c] = jnp.full((tq, LANES), NEG_INF, BF16)
        return carry

    lax.fori_loop(nkb, nkb_total, fill, 0)


def _select(qi, kiwi, kia, kib, *, tq, spb, topk, causal, n_valid, q_pos0):
    t = qi.shape[0]
    nb = t // tq
    lk = kia.shape[1]
    nslab = lk // LANES
    nkb_total = nslab // spb
    assert topk <= 2 * LANES and nslab >= 2, "the bisection's starting lower bound needs two keys per lane"

    def kmap(n):
        return (0 if causal else n, 0, 0)

    return pl.pallas_call(
        functools.partial(_select_kernel, tq=tq, spb=spb, nkb_total=nkb_total, topk=topk,
                          causal=causal, n_valid=n_valid, q_pos0=q_pos0),
        grid=(nb,),
        in_specs=[
            pl.BlockSpec((tq, IQ_W), lambda n: (n, 0)),
            pl.BlockSpec((tq, LANES), lambda n: (n, 0)),
            pl.BlockSpec((None, lk, LANES), kmap),
            pl.BlockSpec((None, lk, LANES), kmap),
        ],
        out_specs=pl.BlockSpec((nslab, tq, LANES), lambda n: (0, n, 0)),
        out_shape=jax.ShapeDtypeStruct((nslab, t, LANES), BF16),
        scratch_shapes=[
            pltpu.VMEM((nslab, tq, LANES), jnp.int32),
            pltpu.VMEM((N_IDX_HEADS, tq, LANES), F32),
        ],
        compiler_params=pltpu.CompilerParams(
            dimension_semantics=("arbitrary",), vmem_limit_bytes=VMEM_LIMIT),
        name="select",
    )(qi, kiwi, kia, kib)


def _attend_kernel(qb_ref, kb_ref, kbat_ref, last_ref, q_ref, k_ref, v_ref, b_ref, o_ref,
                   m_s, l_s, acc_s, *, tq, spb):
    s = pl.program_id(0)

    @pl.when(kb_ref[s] == 0)
    def _():
        m_s[...] = jnp.full(m_s.shape, 0.1 * NEG_INF, F32)
        l_s[...] = jnp.zeros(l_s.shape, F32)
        acc_s[...] = jnp.zeros(acc_s.shape, F32)

    c2 = np.float32((HEAD_DIM ** -0.5) * np.log2(np.e))
    b4s = []
    for c in range(spb):
        bc = b_ref[c].astype(F32)
        b4s.append(jnp.concatenate([bc] * GROUP, axis=0))

    for g in range(N_KV_HEADS):
        qg = [q_ref[:, (g * GROUP + h) * HEAD_DIM:(g * GROUP + h + 1) * HEAD_DIM] for h in range(GROUP)]
        q4 = jnp.concatenate(qg, axis=0)
        kg = k_ref[:, g * HEAD_DIM:(g + 1) * HEAD_DIM]
        sc = lax.dot_general(q4, kg, (((1,), (1,)), ((), ())), preferred_element_type=F32)
        slabs = []
        mx = None
        for c in range(spb):
            sl = sc[:, c * LANES:(c + 1) * LANES] * c2 + b4s[c]
            slabs.append(sl)
            mx = sl if mx is None else jnp.maximum(mx, sl)
        m_prev = m_s[g]
        m_new = jnp.maximum(m_prev, jnp.max(mx, axis=1, keepdims=True))
        alpha = jnp.exp2(m_prev - m_new)
        psum = jnp.zeros((GROUP * tq, LANES), F32)
        ps = []
        for c in range(spb):
            p = jnp.exp2(slabs[c] - m_new)
            psum = psum + p
            ps.append(p.astype(BF16))
        pmat = jnp.concatenate(ps, axis=1)
        vg = v_ref[:, g * HEAD_DIM:(g + 1) * HEAD_DIM]
        l_s[g] = alpha * l_s[g] + jnp.sum(psum, axis=1, keepdims=True)
        acc_s[g] = alpha * acc_s[g] + jnp.dot(pmat, vg, preferred_element_type=F32)
        m_s[g] = m_new

    @pl.when(last_ref[s] == 1)
    def _():
        for g in range(N_KV_HEADS):
            o = acc_s[g] / l_s[g]
            for h in range(GROUP):
                col = (g * GROUP + h) * HEAD_DIM
                o_ref[:, col:col + HEAD_DIM] = o[h * tq:(h + 1) * tq].astype(BF16)


def _attend(q, k_all, v_all, bias, sched, *, tq, spb):
    t = q.shape[0]
    qb, kb, kbat, last = sched
    nsteps = qb.shape[0]
    blk = spb * LANES
    grid_spec = pltpu.PrefetchScalarGridSpec(
        num_scalar_prefetch=4,
        grid=(nsteps,),
        in_specs=[
            pl.BlockSpec((tq, Q_W), lambda s, qb, kb, kbat, last: (qb[s], 0)),
            pl.BlockSpec((None, blk, KV_W), lambda s, qb, kb, kbat, last: (kbat[s], kb[s], 0)),
            pl.BlockSpec((None, blk, KV_W), lambda s, qb, kb, kbat, last: (kbat[s], kb[s], 0)),
            pl.BlockSpec((spb, tq, LANES), lambda s, qb, kb, kbat, last: (kb[s], qb[s], 0)),
        ],
        out_specs=pl.BlockSpec((tq, Q_W), lambda s, qb, kb, kbat, last: (qb[s], 0)),
        scratch_shapes=[
            pltpu.VMEM((N_KV_HEADS, GROUP * tq, LANES), F32),
            pltpu.VMEM((N_KV_HEADS, GROUP * tq, LANES), F32),
            pltpu.VMEM((N_KV_HEADS, GROUP * tq, HEAD_DIM), F32),
        ],
    )
    return pl.pallas_call(
        functools.partial(_attend_kernel, tq=tq, spb=spb),
        grid_spec=grid_spec,
        out_shape=jax.ShapeDtypeStruct((t, Q_W), BF16),
        compiler_params=pltpu.CompilerParams(
            dimension_semantics=("arbitrary",), vmem_limit_bytes=VMEM_LIMIT),
        name="attend",
    )(qb, kb, kbat, last, q, k_all, v_all, bias)


def _causal_schedule(t, tq, blk):
    qb, kb, last = [], [], []
    for n in range(t // tq):
        nk = ((n + 1) * tq + blk - 1) // blk
        for k in range(nk):
            qb.append(n)
            kb.append(k)
            last.append(1 if k == nk - 1 else 0)
    z = np.zeros(len(qb), np.int32)
    return (jnp.asarray(qb, jnp.int32), jnp.asarray(kb, jnp.int32), jnp.asarray(z), jnp.asarray(last, jnp.int32))


def _batched_schedule(nbatch, nk):
    qb = np.repeat(np.arange(nbatch, dtype=np.int32), nk)
    kb = np.tile(np.arange(nk, dtype=np.int32), nbatch)
    last = (kb == nk - 1).astype(np.int32)
    return (jnp.asarray(qb), jnp.asarray(kb), jnp.asarray(qb), jnp.asarray(last))


def _merge_kernel(o_ref, yc_ref, ga_ref, gc_ref, woa_ref, woc_ref, wout_ref, x_ref, gq_ref, gp_ref,
                  x1_ref, xn2_ref, mg_s, m_s):
    j = pl.program_id(1)
    nblk = D_MODEL // TN

    @pl.when(j < nblk)
    def _():
        a = jnp.dot(o_ref[...], woa_ref[...], preferred_element_type=F32)
        c = jnp.dot(yc_ref[...], woc_ref[...], preferred_element_type=F32)
        mg_s[j] = (ga_ref[...] * a + gc_ref[...] * c).astype(BF16)

    @pl.when(j >= nblk)
    def _():
        mg = jnp.concatenate([mg_s[b] for b in range(nblk)], axis=1)
        m_s[j - nblk] = jnp.dot(mg, wout_ref[...], preferred_element_type=F32)

    @pl.when(j == 2 * nblk - 1)
    def _():
        m = jnp.concatenate([m_s[b] for b in range(nblk)], axis=1)
        x1 = x_ref[...] + _rms(m, gq_ref[...])
        x1_ref[...] = x1
        xn2_ref[...] = _rms(x1, gp_ref[...]).astype(BF16)


def _merge(o, yc, ga, gc, woa, woc, wout, x, gq, gp):
    t = x.shape[0]
    nblk = D_MODEL // TN

    def row(i, j):
        return (i, 0)

    def lo(i, j):
        return (i, jnp.minimum(j, nblk - 1))

    return pl.pallas_call(
        _merge_kernel,
        grid=(t // TM, 2 * nblk),
        in_specs=[
            pl.BlockSpec((TM, Q_W), row),
            pl.BlockSpec((TM, D_MODEL), row),
            pl.BlockSpec((TM, TN), lo),
            pl.BlockSpec((TM, TN), lo),
            pl.BlockSpec((Q_W, TN), lambda i, j: (0, jnp.minimum(j, nblk - 1))),
            pl.BlockSpec((D_MODEL, TN), lambda i, j: (0, jnp.minimum(j, nblk - 1))),
            pl.BlockSpec((D_MODEL, TN), lambda i, j: (0, jnp.maximum(j - nblk, 0))),
            pl.BlockSpec((TM, D_MODEL), row),
            pl.BlockSpec((1, D_MODEL), lambda i, j: (0, 0)),
            pl.BlockSpec((1, D_MODEL), lambda i, j: (0, 0)),
        ],
        out_specs=[pl.BlockSpec((TM, D_MODEL), row), pl.BlockSpec((TM, D_MODEL), row)],
        out_shape=[jax.ShapeDtypeStruct((t, D_MODEL), F32), jax.ShapeDtypeStruct((t, D_MODEL), BF16)],
        scratch_shapes=[pltpu.VMEM((nblk, TM, TN), BF16), pltpu.VMEM((nblk, TM, TN), F32)],
        compiler_params=pltpu.CompilerParams(
            dimension_semantics=("arbitrary", "arbitrary"), vmem_limit_bytes=VMEM_LIMIT),
        name="merge",
    )(o, yc, ga, gc, woa, woc, wout, x, gq, gp)


def _ffn_kernel(xn_ref, wg_ref, wv_ref, cwg_ref, cwv_ref, wd_ref, x1_ref, gq_ref, stg_ref, stv_ref,
                y_ref, ng_ref, nv_ref, acc_s, cg_s, cv_s, *, seg, carried):
    i = pl.program_id(0)
    jf = pl.program_id(1)
    nseg = TM // seg

    @pl.when(jf == 0)
    def _():
        acc_s[...] = jnp.zeros(acc_s.shape, F32)

    xn = xn_ref[...]

    def branch(w_ref, cw_ref, st_ref, carry_s, new_ref):
        up = jnp.dot(xn, w_ref[...], preferred_element_type=F32)
        if carried:
            @pl.when(i == 0)
            def _():
                carry_s[jf] = jnp.zeros((8, TF), F32)
                carry_s[jf, 0:2, :] = st_ref[0]
            prevs = [carry_s[jf, 0:2, :]]
        else:
            prevs = [st_ref[s] for s in range(nseg)]
        y, news = _conv_tile(up, cw_ref[...], prevs, seg)
        for s in range(nseg):
            new_ref[s] = news[s]
        if carried:
            carry_s[jf, 0:2, :] = news[0]
        return y

    gate = branch(wg_ref, cwg_ref, stg_ref, cg_s, ng_ref)
    val = branch(wv_ref, cwv_ref, stv_ref, cv_s, nv_ref)
    c0 = np.float32(np.sqrt(2.0 / np.pi))
    gelu = 0.5 * gate * (1.0 + jnp.tanh(c0 * (gate + 0.044715 * (gate * gate * gate))))
    hid = (gelu * val).astype(BF16)
    acc_s[...] += jnp.dot(hid, wd_ref[...], preferred_element_type=F32)

    @pl.when(jf == pl.num_programs(1) - 1)
    def _():
        y_ref[...] = x1_ref[...] + _rms(acc_s[...], gq_ref[...])


def _ffn(xn2, w_up, cw, w_down, x1, gq, state, seg):
    t = x1.shape[0]
    nf = D_FF // TF
    carried = seg == t
    nseg_tile = 1 if carried else TM // seg

    def row(i, j):
        return (i, 0)

    def stg(i, j):
        return (0 if carried else i, 0, j)

    def stv(i, j):
        return (0 if carried else i, 0, j + nf)

    return pl.pallas_call(
        functools.partial(_ffn_kernel, seg=min(seg, TM), carried=carried),
        grid=(t // TM, nf),
        in_specs=[
            pl.BlockSpec((TM, D_MODEL), row),
            pl.BlockSpec((D_MODEL, TF), lambda i, j: (0, j)),
            pl.BlockSpec((D_MODEL, TF), lambda i, j: (0, j + nf)),
            pl.BlockSpec((3, TF), lambda i, j: (0, j)),
            pl.BlockSpec((3, TF), lambda i, j: (0, j + nf)),
            pl.BlockSpec((TF, D_MODEL), lambda i, j: (j, 0)),
            pl.BlockSpec((TM, D_MODEL), row),
            pl.BlockSpec((1, D_MODEL), lambda i, j: (0, 0)),
            pl.BlockSpec((nseg_tile, 2, TF), stg),
            pl.BlockSpec((nseg_tile, 2, TF), stv),
        ],
        out_specs=[
            pl.BlockSpec((TM, D_MODEL), row),
            pl.BlockSpec((nseg_tile, 2, TF), lambda i, j: (i, 0, j)),
            pl.BlockSpec((nseg_tile, 2, TF), lambda i, j: (i, 0, j)),
        ],
        out_shape=[
            jax.ShapeDtypeStruct((t, D_MODEL), F32),
            jax.ShapeDtypeStruct((t // TM * nseg_tile, 2, D_FF), F32),
            jax.ShapeDtypeStruct((t // TM * nseg_tile, 2, D_FF), F32),
        ],
        scratch_shapes=[
            pltpu.VMEM((TM, D_MODEL), F32),
            pltpu.VMEM((nf, 8, TF), F32),
            pltpu.VMEM((nf, 8, TF), F32),
        ],
        compiler_params=pltpu.CompilerParams(
            dimension_semantics=("arbitrary", "arbitrary"), vmem_limit_bytes=VMEM_LIMIT),
        name="ffn",
    )(xn2, w_up, w_up, cw, cw, w_down, x1, gq, state, state)


def _rope_tables(pos, rot, width):
    half = rot // 2
    freqs = ROPE_THETA ** (-jnp.arange(half, dtype=F32) / half)
    ang = pos.astype(F32)[:, None] * freqs[None, :]
    cos, sin = jnp.cos(ang), jnp.sin(ang)
    t = pos.shape[0]
    c = jnp.concatenate([cos, cos, jnp.ones((t, width - rot), F32)], axis=1)
    d = jnp.concatenate([-sin, sin, jnp.zeros((t, width - rot), F32)], axis=1)
    reps = LANES // width
    return jnp.tile(c, (1, reps)), jnp.tile(d, (1, reps))


def _relayout_w_in(w):
    o = np.cumsum([0, Q_W, KV_W, KV_W, IQ_W, IDX_DIM, N_IDX_HEADS, D_MODEL, D_MODEL, D_MODEL, D_MODEL, D_MODEL])
    q, k, v, qi = (w[:, o[a]:o[a + 1]] for a in range(4))
    kiwi = w[:, o[4]:o[6]]
    cb, cc, ch, ga, gc = (w[:, o[a]:o[a + 1]] for a in range(6, 11))
    parts = [q, k, v, qi, kiwi, jnp.zeros((D_MODEL, TN - kiwi.shape[1]), w.dtype)]
    for c in range(N_CBLK):
        sl = slice(c * TN, (c + 1) * TN)
        parts += [cb[:, sl], cc[:, sl], ch[:, sl]]
    parts += [ga, gc]
    return jnp.concatenate(parts, axis=1).astype(BF16)


def _stream(x, pos, seg, past, weights, *, tq, spb_sel, spb_att):
    (g_mp, g_mq, w_in_r, conv_w, woa, woc, wout, g_fp, g_fq, w_up, fconv_w, w_down) = weights
    t = x.shape[0]
    nseq = t // seg
    tabs = _rope_tables(pos, ROT_DIM, HEAD_DIM) + _rope_tables(pos, IDX_ROT_DIM, IDX_DIM)
    if past is None:
        conv_state = jnp.zeros((nseq, 2, D_MODEL), F32)
        ffn_state = jnp.zeros((nseq, 2, 2 * D_FF), F32)
    else:
        conv_state, ffn_state = past[3], past[4]

    (q, k, kb, v, vb, qi, kiwi, kia, kib, yc, conv_new, ga, gc) = _proj(
        x, g_mp, w_in_r, tabs, conv_w, conv_state, seg)

    if past is None:
        k_all, v_all = kb[None], vb[None]
        kia_all, kib_all = kia[None], kib[None]
        n_keys = t
        sched = _causal_schedule(t, tq, spb_att * LANES)
        q_pos0 = 0
    else:
        cache_k, cache_v, cache_ki = past[0], past[1], past[2]
        plen = cache_k.shape[1]
        n_keys = plen + seg
        lk = -(-n_keys // (spb_att * LANES)) * (spb_att * LANES)
        pad = lk - n_keys

        def cat(c, new, width):
            parts = [c, new.reshape(nseq, seg, width)]
            if pad:
                parts.append(jnp.zeros((nseq, pad, width), BF16))
            return jnp.concatenate(parts, axis=1)

        k_all = cat(cache_k.reshape(nseq, plen, KV_W).astype(BF16), kb, KV_W)
        v_all = cat(cache_v.reshape(nseq, plen, KV_W).astype(BF16), vb, KV_W)
        cki = cache_ki.astype(BF16)
        zk = jnp.zeros_like(cki)
        kia_all = cat(jnp.concatenate([cki, zk], axis=-1), kia, LANES)
        kib_all = cat(jnp.concatenate([zk, cki], axis=-1), kib, LANES)
        sched = _batched_schedule(nseq, lk // (spb_att * LANES))
        q_pos0 = plen
    topk = min(TOPK_MAX, n_keys // 4)

    bias = _select(qi, kiwi, kia_all, kib_all, tq=tq, spb=spb_sel, topk=topk,
                   causal=past is None, n_valid=n_keys, q_pos0=q_pos0)
    o = _attend(q, k_all, v_all, bias, sched, tq=tq, spb=spb_att)
    x1, xn2 = _merge(o, yc, ga, gc, woa, woc, wout, x, g_mq, g_fp)
    y, ffn_g, ffn_v = _ffn(xn2, w_up, fconv_w, w_down, x1, g_fq, ffn_state, seg)
    ffn_new = jnp.concatenate([ffn_g, ffn_v], axis=-1)
    return y, k, v, kiwi[:, :IDX_DIM], conv_new[-nseq:], ffn_new[-nseq:]


def kernel(x_prompt, x_sample, cache_k, cache_v, cache_k_idx, state_conv, state_ffn_conv, norm_mix_pre, norm_mix_post, w_in, conv_w, w_o_attn, w_o_conv, w_out, norm_ffn_pre, norm_ffn_post, w_ffn_up, ffn_conv_w, w_ffn_down):
    depth = w_in.shape[0]
    assert depth == 1, "single-layer step"
    b, seq, _ = x_prompt.shape
    assert b == 1
    db, dseq, _ = x_sample.shape
    plen = cache_k.shape[2]
    assert dseq == CHUNK and plen % CHUNK == 0

    weights = (
        norm_mix_pre, norm_mix_post, _relayout_w_in(w_in[0]), conv_w[0],
        w_o_attn[0].astype(BF16), w_o_conv[0].astype(BF16), w_out[0].astype(BF16),
        norm_ffn_pre, norm_ffn_post, w_ffn_up[0].astype(BF16), ffn_conv_w[0], w_ffn_down[0].astype(BF16),
    )

    pos_p = jnp.arange(seq, dtype=jnp.int32)
    yp, kp, vp, kip, convp, ffnp = _stream(
        x_prompt.reshape(seq, D_MODEL), pos_p, seq, None, weights, tq=128, spb_sel=4, spb_att=4)

    pos_s = jnp.tile(jnp.arange(dseq, dtype=jnp.int32) + plen, db)
    past = (cache_k[0], cache_v[0], cache_k_idx[0], state_conv[0], state_ffn_conv[0])
    n_keys = plen + dseq
    spb_s = _sample_slabs(n_keys)
    ys, ks, vs, kis, convs, ffns = _stream(
        x_sample.reshape(db * dseq, D_MODEL), pos_s, dseq, past, weights, tq=dseq, spb_sel=spb_s, spb_att=spb_s)

    return (
        yp.reshape(1, seq, D_MODEL), ys.reshape(db, dseq, D_MODEL),
        kp.reshape(1, 1, seq, N_KV_HEADS, HEAD_DIM), vp.reshape(1, 1, seq, N_KV_HEADS, HEAD_DIM),
        kip.reshape(1, 1, seq, IDX_DIM), convp.reshape(1, 1, 2, D_MODEL), ffnp.reshape(1, 1, 2, 2 * D_FF),
        ks.reshape(1, db, dseq, N_KV_HEADS, HEAD_DIM), vs.reshape(1, db, dseq, N_KV_HEADS, HEAD_DIM),
        kis.reshape(1, db, dseq, IDX_DIM), convs.reshape(1, db, 2, D_MODEL), ffns.reshape(1, db, 2, 2 * D_FF),
    )


def _sample_slabs(n_keys):
    nslab = -(-n_keys // LANES)
    best = 1
    for d in range(1, nslab + 1):
        if nslab % d == 0 and d <= 11:
            best = d
    return best
```

```python
import functools

import jax
import jax.numpy as jnp
import numpy as np
from jax import lax
from jax.experimental import pallas as pl
from jax.experimental.pallas import tpu as pltpu

F32 = jnp.float32
BF16 = jnp.bfloat16

D_MODEL = 2048
N_HEADS = 16
N_KV_HEADS = 4
HEAD_DIM = 128
ROT_DIM = HEAD_DIM // 4
N_IDX_HEADS = 16
IDX_DIM = 64
IDX_ROT_DIM = IDX_DIM // 4
CHUNK = 64
TOPK_MAX = 256
ROPE_THETA = 500000.0
D_FF = 5632
RMS_EPS = 1e-6
NEG_INF = -1e30
Q_W = N_HEADS * HEAD_DIM
KV_W = N_KV_HEADS * HEAD_DIM
IQ_W = N_IDX_HEADS * IDX_DIM
GROUP = N_HEADS // N_KV_HEADS

LANES = 128
TM = 512
TN = 512
TF = 512
VMEM_LIMIT = 56 * 1024 * 1024

J_Q = 0
J_K = J_Q + Q_W // TN
J_V = J_K + 1
J_QI = J_V + 1
J_KIWI = J_QI + IQ_W // TN
J_CONV = J_KIWI + 1
N_CBLK = D_MODEL // TN
J_GA = J_CONV + 3 * N_CBLK
J_GC = J_GA + N_CBLK
NJ = J_GC + N_CBLK

INT_MIN = -(2 ** 31)
_VALID_KEY = int(np.array(0.5 * NEG_INF, np.float32).view(np.int32)) ^ 0x7FFFFFFF


def _rms(x, g):
    return x * lax.rsqrt(jnp.mean(x * x, axis=-1, keepdims=True) + RMS_EPS) * g


def _sortable(x):
    bits = lax.bitcast_convert_type(x, jnp.int32)
    return jnp.where(bits < 0, bits ^ jnp.int32(0x7FFFFFFF), bits)


def _dwconv_seg(u, prev2, w3):
    row = lax.broadcasted_iota(jnp.int32, u.shape, 0)
    p0, p1 = prev2[0:1], prev2[1:2]
    s1 = jnp.where(row == 0, p1, pltpu.roll(u, 1, 0))
    s2 = jnp.where(row == 0, p0, jnp.where(row == 1, p1, pltpu.roll(u, 2, 0)))
    return w3[0:1] * s2 + w3[1:2] * s1 + w3[2:3] * u


def _conv_tile(u, w3, prevs, seg):
    nseg = len(prevs)
    ys, news = [], []
    for s in range(nseg):
        us = u[s * seg:(s + 1) * seg]
        ys.append(_dwconv_seg(us, prevs[s], w3))
        news.append(us[seg - 2:seg])
    y = ys[0] if nseg == 1 else jnp.concatenate(ys, axis=0)
    return y, news


def _proj_kernel(x_ref, g_ref, w_ref, c128_ref, d128_ref, c64_ref, d64_ref, cw_ref, st_ref,
                 q_ref, k_ref, kb_ref, v_ref, vb_ref, qi_ref, kiwi_ref, kia_ref, kib_ref,
                 yc_ref, cn_ref, ga_ref, gc_ref,
                 xn_s, cb_s, cc_s, carry_s, *, seg, carried):
    i = pl.program_id(0)
    j = pl.program_id(1)

    @pl.when(j == 0)
    def _():
        xn_s[...] = _rms(x_ref[...], g_ref[...]).astype(BF16)

    def mm():
        return jnp.dot(xn_s[...], w_ref[...], preferred_element_type=F32)

    lane = lax.broadcasted_iota(jnp.int32, (TM, LANES), 1)

    def rope128(xh):
        partner = jnp.where(lane < ROT_DIM // 2, pltpu.roll(xh, LANES - ROT_DIM // 2, 1),
                            pltpu.roll(xh, ROT_DIM // 2, 1))
        return xh * c128_ref[...] + partner * d128_ref[...]

    def rope64(xh, c, d):
        first = (lane & (IDX_DIM - 1)) < IDX_ROT_DIM // 2
        partner = jnp.where(first, pltpu.roll(xh, LANES - IDX_ROT_DIM // 2, 1),
                            pltpu.roll(xh, IDX_ROT_DIM // 2, 1))
        return xh * c + partner * d

    @pl.when(j < J_K)
    def _():
        acc = mm()
        for h in range(TN // LANES):
            sl = slice(h * LANES, (h + 1) * LANES)
            q_ref[:, sl] = rope128(acc[:, sl]).astype(BF16)

    @pl.when(j == J_K)
    def _():
        acc = mm()
        for h in range(TN // LANES):
            sl = slice(h * LANES, (h + 1) * LANES)
            r = rope128(acc[:, sl])
            k_ref[:, sl] = r
            kb_ref[:, sl] = r.astype(BF16)

    @pl.when(j == J_V)
    def _():
        acc = mm()
        v_ref[...] = acc
        vb_ref[...] = acc.astype(BF16)

    @pl.when(jnp.logical_and(j >= J_QI, j < J_KIWI))
    def _():
        acc = mm()
        for h in range(TN // LANES):
            sl = slice(h * LANES, (h + 1) * LANES)
            qi_ref[:, sl] = rope64(acc[:, sl], c64_ref[...], d64_ref[...]).astype(BF16)

    @pl.when(j == J_KIWI)
    def _():
        is_ki = lane < IDX_DIM
        c = jnp.where(is_ki, c64_ref[...], 1.0)
        d = jnp.where(is_ki, d64_ref[...], 0.0)
        acc = jnp.dot(xn_s[...], w_ref[:, 0:LANES], preferred_element_type=F32)
        r = rope64(acc, c, d)
        kiwi_ref[...] = r
        ka = jnp.where(is_ki, r, 0.0)
        kia_ref[...] = ka.astype(BF16)
        kib_ref[...] = pltpu.roll(ka, IDX_DIM, 1).astype(BF16)

    jc = jnp.maximum(j - J_CONV, 0)
    in_conv = jnp.logical_and(j >= J_CONV, j < J_GA)
    cblk = jnp.minimum(jc // 3, N_CBLK - 1)
    part = jc % 3

    @pl.when(jnp.logical_and(in_conv, part == 0))
    def _():
        cb_s[...] = mm()

    @pl.when(jnp.logical_and(in_conv, part == 1))
    def _():
        cc_s[...] = mm()

    @pl.when(jnp.logical_and(in_conv, part == 2))
    def _():
        nseg = TM // seg
        if carried:
            @pl.when(i == 0)
            def _():
                carry_s[cblk] = jnp.zeros((8, TN), F32)
                carry_s[cblk, 0:2, :] = st_ref[0]
            prevs = [carry_s[cblk, 0:2, :]]
        else:
            prevs = [st_ref[s] for s in range(nseg)]
        u = cc_s[...] * mm()
        y, news = _conv_tile(u, cw_ref[...], prevs, seg)
        yc_ref[...] = (cb_s[...] * y).astype(BF16)
        for s in range(nseg):
            cn_ref[s] = news[s]
        if carried:
            carry_s[cblk, 0:2, :] = news[0]

    @pl.when(jnp.logical_and(j >= J_GA, j < J_GC))
    def _():
        ga_ref[...] = 1.0 / (1.0 + jnp.exp(-mm()))

    @pl.when(j >= J_GC)
    def _():
        gc_ref[...] = 1.0 / (1.0 + jnp.exp(-mm()))


def _proj(x, g, w, tabs, conv_w, state, seg):
    t = x.shape[0]
    ni = t // TM
    carried = seg == t
    nseg_tile = 1 if carried else TM // seg
    c128, d128, c64, d64 = tabs

    def row(i, j):
        return (i, 0)

    def const(i, j):
        return (0, 0)

    def cblk(j):
        return jnp.minimum(jnp.maximum(j - J_CONV, 0) // 3, N_CBLK - 1)

    def st_map(i, j):
        return (0 if carried else i, 0, cblk(j))

    in_specs = [
        pl.BlockSpec((TM, D_MODEL), row),
        pl.BlockSpec((1, D_MODEL), const),
        pl.BlockSpec((D_MODEL, TN), lambda i, j: (0, j)),
        pl.BlockSpec((TM, LANES), row),
        pl.BlockSpec((TM, LANES), row),
        pl.BlockSpec((TM, LANES), row),
        pl.BlockSpec((TM, LANES), row),
        pl.BlockSpec((3, TN), lambda i, j: (0, cblk(j))),
        pl.BlockSpec((nseg_tile, 2, TN), st_map),
    ]
    out_shape = [
        jax.ShapeDtypeStruct((t, Q_W), BF16),
        jax.ShapeDtypeStruct((t, KV_W), F32),
        jax.ShapeDtypeStruct((t, KV_W), BF16),
        jax.ShapeDtypeStruct((t, KV_W), F32),
        jax.ShapeDtypeStruct((t, KV_W), BF16),
        jax.ShapeDtypeStruct((t, IQ_W), BF16),
        jax.ShapeDtypeStruct((t, LANES), F32),
        jax.ShapeDtypeStruct((t, LANES), BF16),
        jax.ShapeDtypeStruct((t, LANES), BF16),
        jax.ShapeDtypeStruct((t, D_MODEL), BF16),
        jax.ShapeDtypeStruct((ni * nseg_tile, 2, D_MODEL), F32),
        jax.ShapeDtypeStruct((t, D_MODEL), F32),
        jax.ShapeDtypeStruct((t, D_MODEL), F32),
    ]
    out_specs = [
        pl.BlockSpec((TM, TN), lambda i, j: (i, jnp.clip(j - J_Q, 0, J_K - J_Q - 1))),
        pl.BlockSpec((TM, TN), row),
        pl.BlockSpec((TM, TN), row),
        pl.BlockSpec((TM, TN), row),
        pl.BlockSpec((TM, TN), row),
        pl.BlockSpec((TM, TN), lambda i, j: (i, jnp.clip(j - J_QI, 0, J_KIWI - J_QI - 1))),
        pl.BlockSpec((TM, LANES), row),
        pl.BlockSpec((TM, LANES), row),
        pl.BlockSpec((TM, LANES), row),
        pl.BlockSpec((TM, TN), lambda i, j: (i, cblk(j))),
        pl.BlockSpec((nseg_tile, 2, TN), lambda i, j: (i, 0, cblk(j))),
        pl.BlockSpec((TM, TN), lambda i, j: (i, jnp.clip(j - J_GA, 0, N_CBLK - 1))),
        pl.BlockSpec((TM, TN), lambda i, j: (i, jnp.clip(j - J_GC, 0, N_CBLK - 1))),
    ]
    return pl.pallas_call(
        functools.partial(_proj_kernel, seg=min(seg, TM), carried=carried),
        grid=(ni, NJ),
        in_specs=in_specs,
        out_specs=out_specs,
        out_shape=out_shape,
        scratch_shapes=[
            pltpu.VMEM((TM, D_MODEL), BF16),
            pltpu.VMEM((TM, TN), F32),
            pltpu.VMEM((TM, TN), F32),
            pltpu.VMEM((N_CBLK, 8, TN), F32),
        ],
        compiler_params=pltpu.CompilerParams(
            dimension_semantics=("arbitrary", "arbitrary"), vmem_limit_bytes=VMEM_LIMIT),
        name="proj",
    )(x, g, w, c128, d128, c64, d64, conv_w, state)


def _select_kernel(qi_ref, kiwi_ref, kia_ref, kib_ref, bias_ref, key_s, wb_s, *,
                   tq, spb, nkb_total, topk, causal, n_valid, q_pos0):
    n = pl.program_id(0)
    npair = N_IDX_HEADS // 2
    kiwi = kiwi_ref[...]
    wscale = (IDX_DIM ** -0.5) * (N_IDX_HEADS ** -0.5)
    for h in range(N_IDX_HEADS):
        wb_s[h] = jnp.broadcast_to(kiwi[:, IDX_DIM + h:IDX_DIM + h + 1], (tq, LANES)) * wscale
    q2 = jnp.concatenate([qi_ref[:, p * LANES:(p + 1) * LANES] for p in range(npair)], axis=0)

    if causal:
        qpos0 = n * tq
        nkb = (qpos0 + tq + spb * LANES - 1) // (spb * LANES)
    else:
        qpos0 = q_pos0
        nkb = nkb_total
    lane = lax.broadcasted_iota(jnp.int32, (tq, LANES), 1)
    qchunk = (qpos0 + lax.broadcasted_iota(jnp.int32, (tq, LANES), 0)) >> 6
    nt = (((1,), (1,)), ((), ()))

    def score_blk(kb, carry):
        m1, m2 = carry
        base = pl.multiple_of(kb * (spb * LANES), spb * LANES)
        ka = kia_ref[pl.ds(base, spb * LANES), :]
        kbm = kib_ref[pl.ds(base, spb * LANES), :]
        le = lax.dot_general(q2, ka, nt, preferred_element_type=F32)
        lo = lax.dot_general(q2, kbm, nt, preferred_element_type=F32)
        for c in range(spb):
            cs = slice(c * LANES, (c + 1) * LANES)
            acc = jnp.zeros((tq, LANES), F32)
            for p in range(npair):
                rs = slice(p * tq, (p + 1) * tq)
                acc = acc + jnp.maximum(le[rs, cs], 0.0) * wb_s[2 * p]
                acc = acc + jnp.maximum(lo[rs, cs], 0.0) * wb_s[2 * p + 1]
            col = base + c * LANES + lane
            adm = jnp.logical_and((col >> 6) <= qchunk, col < n_valid)
            sc = jnp.where(adm, acc, NEG_INF)
            m2 = jnp.maximum(m2, jnp.minimum(m1, sc))
            m1 = jnp.maximum(m1, sc)
            key_s[kb * spb + c] = _sortable(sc)
        return m1, m2

    neg = jnp.full((tq, LANES), NEG_INF, F32)
    m1, m2 = lax.fori_loop(0, nkb, score_blk, (neg, neg))

    def count_ge(cand):
        def body(kb, acc):
            for c in range(spb):
                acc = acc + jnp.where(key_s[kb * spb + c] >= cand, 1.0, 0.0)
            return acc
        acc = lax.fori_loop(0, nkb, body, jnp.zeros((tq, LANES), F32))
        return jnp.sum(acc, axis=1, keepdims=True)

    ones = jnp.ones((tq, LANES), jnp.int32)
    lo0 = _sortable(jnp.min(m2, axis=1, keepdims=True)) * ones
    hi0 = _sortable(jnp.max(m1, axis=1, keepdims=True)) * ones + 1
    kf = float(topk)

    def unresolved(lo_k, hi_k, c_lo):
        open_ = jnp.logical_and(c_lo != kf, (hi_k - lo_k) != 1)
        return jnp.max(jnp.where(open_, 1.0, 0.0))

    def bis_cond(st):
        it, _, _, _, flag = st
        return jnp.logical_and(it < 33, flag > 0.0)

    def bis_body(st):
        it, lo_k, hi_k, c_lo, _ = st
        mid = lo_k + lax.shift_right_logical(hi_k - lo_k, 1)
        cnt = count_ge(mid)
        ge = cnt >= kf
        lo_k = jnp.where(ge, mid, lo_k)
        hi_k = jnp.where(ge, hi_k, mid)
        c_lo = jnp.where(ge, cnt, c_lo)
        return it + 1, lo_k, hi_k, c_lo, unresolved(lo_k, hi_k, c_lo)

    c0 = jnp.full((tq, LANES), -1.0, F32)
    _, thr, _, _, _ = lax.while_loop(
        bis_cond, bis_body, (jnp.int32(0), lo0, hi0, c0, unresolved(lo0, hi0, c0)))

    def emit(kb, carry):
        for c in range(spb):
            k = key_s[kb * spb + c]
            sel = jnp.logical_and(k >= thr, k > _VALID_KEY)
            bias_ref[kb * spb + c] = jnp.where(sel, 0.0, NEG_INF).astype(BF16)
        return carry

    lax.fori_loop(0, nkb, emit, 0)

    def fill(kb, carry):
        for c in range(spb):
            bias_ref[kb * spb + c] = jnp.full((tq, LANES), NEG_INF, BF16)
        return carry

    lax.fori_loop(nkb, nkb_total, fill, 0)


def _select(qi, kiwi, kia, kib, *, tq, spb, topk, causal, n_valid, q_pos0):
    t = qi.shape[0]
    nb = t // tq
    lk = kia.shape[1]
    nslab = lk // LANES
    nkb_total = nslab // spb
    assert topk <= 2 * LANES and nslab >= 2, "the bisection's starting lower bound needs two keys per lane"

    def kmap(n):
        return (0 if causal else n, 0, 0)

    return pl.pallas_call(
        functools.partial(_select_kernel, tq=tq, spb=spb, nkb_total=nkb_total, topk=topk,
                          causal=causal, n_valid=n_valid, q_pos0=q_pos0),
        grid=(nb,),
        in_specs=[
            pl.BlockSpec((tq, IQ_W), lambda n: (n, 0)),
            pl.BlockSpec((tq, LANES), lambda n: (n, 0)),
            pl.BlockSpec((None, lk, LANES), kmap),
            pl.BlockSpec((None, lk, LANES), kmap),
        ],
        out_specs=pl.BlockSpec((nslab, tq, LANES), lambda n: (0, n, 0)),
        out_shape=jax.ShapeDtypeStruct((nslab, t, LANES), BF16),
        scratch_shapes=[
            pltpu.VMEM((nslab, tq, LANES), jnp.int32),
            pltpu.VMEM((N_IDX_HEADS, tq, LANES), F32),
        ],
        compiler_params=pltpu.CompilerParams(
            dimension_semantics=("arbitrary",), vmem_limit_bytes=VMEM_LIMIT),
        name="select",
    )(qi, kiwi, kia, kib)


def _attend_kernel(qb_ref, kb_ref, kbat_ref, last_ref, q_ref, k_ref, v_ref, b_ref, o_ref,
                   m_s, l_s, acc_s, *, tq, spb):
    s = pl.program_id(0)

    @pl.when(kb_ref[s] == 0)
    def _():
        m_s[...] = jnp.full(m_s.shape, 0.1 * NEG_INF, F32)
        l_s[...] = jnp.zeros(l_s.shape, F32)
        acc_s[...] = jnp.zeros(acc_s.shape, F32)

    c2 = np.float32((HEAD_DIM ** -0.5) * np.log2(np.e))
    b4s = []
    for c in range(spb):
        bc = b_ref[c].astype(F32)
        b4s.append(jnp.concatenate([bc] * GROUP, axis=0))

    for g in range(N_KV_HEADS):
        qg = [q_ref[:, (g * GROUP + h) * HEAD_DIM:(g * GROUP + h + 1) * HEAD_DIM] for h in range(GROUP)]
        q4 = jnp.concatenate(qg, axis=0)
        kg = k_ref[:, g * HEAD_DIM:(g + 1) * HEAD_DIM]
        sc = lax.dot_general(q4, kg, (((1,), (1,)), ((), ())), preferred_element_type=F32)
        slabs = []
        mx = None
        for c in range(spb):
            sl = sc[:, c * LANES:(c + 1) * LANES] * c2 + b4s[c]
            slabs.append(sl)
            mx = sl if mx is None else jnp.maximum(mx, sl)
        m_prev = m_s[g]
        m_new = jnp.maximum(m_prev, jnp.max(mx, axis=1, keepdims=True))
        alpha = jnp.exp2(m_prev - m_new)
        psum = jnp.zeros((GROUP * tq, LANES), F32)
        ps = []
        for c in range(spb):
            p = jnp.exp2(slabs[c] - m_new)
            psum = psum + p
            ps.append(p.astype(BF16))
        pmat = jnp.concatenate(ps, axis=1)
        vg = v_ref[:, g * HEAD_DIM:(g + 1) * HEAD_DIM]
        l_s[g] = alpha * l_s[g] + jnp.sum(psum, axis=1, keepdims=True)
        acc_s[g] = alpha * acc_s[g] + jnp.dot(pmat, vg, preferred_element_type=F32)
        m_s[g] = m_new

    @pl.when(last_ref[s] == 1)
    def _():
        for g in range(N_KV_HEADS):
            o = acc_s[g] / l_s[g]
            for h in range(GROUP):
                col = (g * GROUP + h) * HEAD_DIM
                o_ref[:, col:col + HEAD_DIM] = o[h * tq:(h + 1) * tq].astype(BF16)


def _attend(q, k_all, v_all, bias, sched, *, tq, spb):
    t = q.shape[0]
    qb, kb, kbat, last = sched
    nsteps = qb.shape[0]
    blk = spb * LANES
    grid_spec = pltpu.PrefetchScalarGridSpec(
        num_scalar_prefetch=4,
        grid=(nsteps,),
        in_specs=[
            pl.BlockSpec((tq, Q_W), lambda s, qb, kb, kbat, last: (qb[s], 0)),
            pl.BlockSpec((None, blk, KV_W), lambda s, qb, kb, kbat, last: (kbat[s], kb[s], 0)),
            pl.BlockSpec((None, blk, KV_W), lambda s, qb, kb, kbat, last: (kbat[s], kb[s], 0)),
            pl.BlockSpec((spb, tq, LANES), lambda s, qb, kb, kbat, last: (kb[s], qb[s], 0)),
        ],
        out_specs=pl.BlockSpec((tq, Q_W), lambda s, qb, kb, kbat, last: (qb[s], 0)),
        scratch_shapes=[
            pltpu.VMEM((N_KV_HEADS, GROUP * tq, LANES), F32),
            pltpu.VMEM((N_KV_HEADS, GROUP * tq, LANES), F32),
            pltpu.VMEM((N_KV_HEADS, GROUP * tq, HEAD_DIM), F32),
        ],
    )
    return pl.pallas_call(
        functools.partial(_attend_kernel, tq=tq, spb=spb),
        grid_spec=grid_spec,
        out_shape=jax.ShapeDtypeStruct((t, Q_W), BF16),
        compiler_params=pltpu.CompilerParams(
            dimension_semantics=("arbitrary",), vmem_limit_bytes=VMEM_LIMIT),
        name="attend",
    )(qb, kb, kbat, last, q, k_all, v_all, bias)


def _causal_schedule(t, tq, blk):
    qb, kb, last = [], [], []
    for n in range(t // tq):
        nk = ((n + 1) * tq + blk - 1) // blk
        for k in range(nk):
            qb.append(n)
            kb.append(k)
            last.append(1 if k == nk - 1 else 0)
    z = np.zeros(len(qb), np.int32)
    return (jnp.asarray(qb, jnp.int32), jnp.asarray(kb, jnp.int32), jnp.asarray(z), jnp.asarray(last, jnp.int32))


def _batched_schedule(nbatch, nk):
    qb = np.repeat(np.arange(nbatch, dtype=np.int32), nk)
    kb = np.tile(np.arange(nk, dtype=np.int32), nbatch)
    last = (kb == nk - 1).astype(np.int32)
    return (jnp.asarray(qb), jnp.asarray(kb), jnp.asarray(qb), jnp.asarray(last))


def _merge_kernel(o_ref, yc_ref, ga_ref, gc_ref, woa_ref, woc_ref, wout_ref, x_ref, gq_ref, gp_ref,
                  x1_ref, xn2_ref, mg_s, m_s):
    j = pl.program_id(1)
    nblk = D_MODEL // TN

    @pl.when(j < nblk)
    def _():
        a = jnp.dot(o_ref[...], woa_ref[...], preferred_element_type=F32)
        c = jnp.dot(yc_ref[...], woc_ref[...], preferred_element_type=F32)
        mg_s[j] = (ga_ref[...] * a + gc_ref[...] * c).astype(BF16)

    @pl.when(j >= nblk)
    def _():
        mg = jnp.concatenate([mg_s[b] for b in range(nblk)], axis=1)
        m_s[j - nblk] = jnp.dot(mg, wout_ref[...], preferred_element_type=F32)

    @pl.when(j == 2 * nblk - 1)
    def _():
        m = jnp.concatenate([m_s[b] for b in range(nblk)], axis=1)
        x1 = x_ref[...] + _rms(m, gq_ref[...])
        x1_ref[...] = x1
        xn2_ref[...] = _rms(x1, gp_ref[...]).astype(BF16)


def _merge(o, yc, ga, gc, woa, woc, wout, x, gq, gp):
    t = x.shape[0]
    nblk = D_MODEL // TN

    def row(i, j):
        return (i, 0)

    def lo(i, j):
        return (i, jnp.minimum(j, nblk - 1))

    return pl.pallas_call(
        _merge_kernel,
        grid=(t // TM, 2 * nblk),
        in_specs=[
            pl.BlockSpec((TM, Q_W), row),
            pl.BlockSpec((TM, D_MODEL), row),
            pl.BlockSpec((TM, TN), lo),
            pl.BlockSpec((TM, TN), lo),
            pl.BlockSpec((Q_W, TN), lambda i, j: (0, jnp.minimum(j, nblk - 1))),
            pl.BlockSpec((D_MODEL, TN), lambda i, j: (0, jnp.minimum(j, nblk - 1))),
            pl.BlockSpec((D_MODEL, TN), lambda i, j: (0, jnp.maximum(j - nblk, 0))),
            pl.BlockSpec((TM, D_MODEL), row),
            pl.BlockSpec((1, D_MODEL), lambda i, j: (0, 0)),
            pl.BlockSpec((1, D_MODEL), lambda i, j: (0, 0)),
        ],
        out_specs=[pl.BlockSpec((TM, D_MODEL), row), pl.BlockSpec((TM, D_MODEL), row)],
        out_shape=[jax.ShapeDtypeStruct((t, D_MODEL), F32), jax.ShapeDtypeStruct((t, D_MODEL), BF16)],
        scratch_shapes=[pltpu.VMEM((nblk, TM, TN), BF16), pltpu.VMEM((nblk, TM, TN), F32)],
        compiler_params=pltpu.CompilerParams(
            dimension_semantics=("arbitrary", "arbitrary"), vmem_limit_bytes=VMEM_LIMIT),
        name="merge",
    )(o, yc, ga, gc, woa, woc, wout, x, gq, gp)


def _ffn_kernel(xn_ref, wg_ref, wv_ref, cwg_ref, cwv_ref, wd_ref, x1_ref, gq_ref, stg_ref, stv_ref,
                y_ref, ng_ref, nv_ref, acc_s, cg_s, cv_s, *, seg, carried):
    i = pl.program_id(0)
    jf = pl.program_id(1)
    nseg = TM // seg

    @pl.when(jf == 0)
    def _():
        acc_s[...] = jnp.zeros(acc_s.shape, F32)

    if carried:
        @pl.when(i == 0)
        def _():
            for carry_s, st_ref in ((cg_s, stg_ref), (cv_s, stv_ref)):
                carry_s[jf] = jnp.zeros((8, TF), F32)
                carry_s[jf, 0:2, :] = st_ref[0]

    xn = xn_ref[...]

    def branch(w_ref, cw_ref, st_ref, carry_s, new_ref):
        up = jnp.dot(xn, w_ref[...], preferred_element_type=F32)
        if carried:
            prevs = [carry_s[jf, 0:2, :]]
        else:
            prevs = [st_ref[s] for s in range(nseg)]
        y, news = _conv_tile(up, cw_ref[...], prevs, seg)
        for s in range(nseg):
            new_ref[s] = news[s]
        if carried:
            carry_s[jf, 0:2, :] = news[0]
        return y

    gate = branch(wg_ref, cwg_ref, stg_ref, cg_s, ng_ref)
    val = branch(wv_ref, cwv_ref, stv_ref, cv_s, nv_ref)
    c0 = np.float32(np.sqrt(2.0 / np.pi))
    gelu = 0.5 * gate * (1.0 + jnp.tanh(c0 * (gate + 0.044715 * (gate * gate * gate))))
    hid = (gelu * val).astype(BF16)
    acc_s[...] += jnp.dot(hid, wd_ref[...], preferred_element_type=F32)

    @pl.when(jf == pl.num_programs(1) - 1)
    def _():
        y_ref[...] = x1_ref[...] + _rms(acc_s[...], gq_ref[...])


def _ffn(xn2, w_up, cw, w_down, x1, gq, state, seg):
    t = x1.shape[0]
    nf = D_FF // TF
    carried = seg == t
    nseg_tile = 1 if carried else TM // seg

    def row(i, j):
        return (i, 0)

    def stg(i, j):
        return (0 if carried else i, 0, j)

    def stv(i, j):
        return (0 if carried else i, 0, j + nf)

    return pl.pallas_call(
        functools.partial(_ffn_kernel, seg=min(seg, TM), carried=carried),
        grid=(t // TM, nf),
        in_specs=[
            pl.BlockSpec((TM, D_MODEL), row),
            pl.BlockSpec((D_MODEL, TF), lambda i, j: (0, j)),
            pl.BlockSpec((D_MODEL, TF), lambda i, j: (0, j + nf)),
            pl.BlockSpec((3, TF), lambda i, j: (0, j)),
            pl.BlockSpec((3, TF), lambda i, j: (0, j + nf)),
            pl.BlockSpec((TF, D_MODEL), lambda i, j: (j, 0)),
            pl.BlockSpec((TM, D_MODEL), row),
            pl.BlockSpec((1, D_MODEL), lambda i, j: (0, 0)),
            pl.BlockSpec((nseg_tile, 2, TF), stg),
            pl.BlockSpec((nseg_tile, 2, TF), stv),
        ],
        out_specs=[
            pl.BlockSpec((TM, D_MODEL), row),
            pl.BlockSpec((nseg_tile, 2, TF), lambda i, j: (i, 0, j)),
            pl.BlockSpec((nseg_tile, 2, TF), lambda i, j: (i, 0, j)),
        ],
        out_shape=[
            jax.ShapeDtypeStruct((t, D_MODEL), F32),
            jax.ShapeDtypeStruct((t // TM * nseg_tile, 2, D_FF), F32),
            jax.ShapeDtypeStruct((t // TM * nseg_tile, 2, D_FF), F32),
        ],
        scratch_shapes=[
            pltpu.VMEM((TM, D_MODEL), F32),
            pltpu.VMEM((nf, 8, TF), F32),
            pltpu.VMEM((nf, 8, TF), F32),
        ],
        compiler_params=pltpu.CompilerParams(
            dimension_semantics=("arbitrary", "arbitrary"), vmem_limit_bytes=VMEM_LIMIT),
        name="ffn",
    )(xn2, w_up, w_up, cw, cw, w_down, x1, gq, state, state)


PACK_ROWS = 1024


def _pack_kernel(ck_ref, cv_ref, kt_ref, vt_ref, ko_ref, vo_ref, *, rows):
    del kt_ref, vt_ref
    for src, dst in ((ck_ref, ko_ref), (cv_ref, vo_ref)):
        for g in range(N_KV_HEADS):
            dst[:, g * HEAD_DIM:(g + 1) * HEAD_DIM] = src[pl.ds(g, rows, stride=N_KV_HEADS), :].astype(BF16)


def _pack_cache(cache_k, cache_v, k_tail, v_tail):
    nb, plen = cache_k.shape[0], cache_k.shape[1]
    rows = int(np.gcd(plen, PACK_ROWS))
    assert rows % 16 == 0
    cspec = pl.BlockSpec((None, rows * N_KV_HEADS, HEAD_DIM), lambda b, r: (b, r, 0))
    ospec = pl.BlockSpec((None, rows, KV_W), lambda b, r: (b, r, 0))
    anyspec = pl.BlockSpec(memory_space=pl.ANY)
    flat = (nb, plen * N_KV_HEADS, HEAD_DIM)
    return pl.pallas_call(
        functools.partial(_pack_kernel, rows=rows),
        grid=(nb, plen // rows),
        in_specs=[cspec, cspec, anyspec, anyspec],
        out_specs=[ospec, ospec],
        out_shape=[jax.ShapeDtypeStruct(k_tail.shape, BF16), jax.ShapeDtypeStruct(v_tail.shape, BF16)],
        input_output_aliases={2: 0, 3: 1},
        compiler_params=pltpu.CompilerParams(
            dimension_semantics=("arbitrary", "arbitrary"), vmem_limit_bytes=VMEM_LIMIT),
        name="pack_cache",
    )(cache_k.reshape(flat), cache_v.reshape(flat), k_tail, v_tail)


def _rope_tables(pos, rot, width):
    half = rot // 2
    freqs = ROPE_THETA ** (-jnp.arange(half, dtype=F32) / half)
    ang = pos.astype(F32)[:, None] * freqs[None, :]
    cos, sin = jnp.cos(ang), jnp.sin(ang)
    t = pos.shape[0]
    c = jnp.concatenate([cos, cos, jnp.ones((t, width - rot), F32)], axis=1)
    d = jnp.concatenate([-sin, sin, jnp.zeros((t, width - rot), F32)], axis=1)
    reps = LANES // width
    return jnp.tile(c, (1, reps)), jnp.tile(d, (1, reps))


def _relayout_w_in(w):
    o = np.cumsum([0, Q_W, KV_W, KV_W, IQ_W, IDX_DIM, N_IDX_HEADS, D_MODEL, D_MODEL, D_MODEL, D_MODEL, D_MODEL])
    q, k, v, qi = (w[:, o[a]:o[a + 1]] for a in range(4))
    kiwi = w[:, o[4]:o[6]]
    cb, cc, ch, ga, gc = (w[:, o[a]:o[a + 1]] for a in range(6, 11))
    parts = [q, k, v, qi, kiwi, jnp.zeros((D_MODEL, TN - kiwi.shape[1]), w.dtype)]
    for c in range(N_CBLK):
        sl = slice(c * TN, (c + 1) * TN)
        parts += [cb[:, sl], cc[:, sl], ch[:, sl]]
    parts += [ga, gc]
    return jnp.concatenate(parts, axis=1).astype(BF16)


def _stream(x, pos, seg, past, weights, *, tq, spb_sel, spb_att):
    (g_mp, g_mq, w_in_r, conv_w, woa, woc, wout, g_fp, g_fq, w_up, fconv_w, w_down) = weights
    t = x.shape[0]
    nseq = t // seg
    tabs = _rope_tables(pos, ROT_DIM, HEAD_DIM) + _rope_tables(pos, IDX_ROT_DIM, IDX_DIM)
    if past is None:
        conv_state = jnp.zeros((nseq, 2, D_MODEL), F32)
        ffn_state = jnp.zeros((nseq, 2, 2 * D_FF), F32)
    else:
        conv_state, ffn_state = past[3], past[4]

    (q, k, kb, v, vb, qi, kiwi, kia, kib, yc, conv_new, ga, gc) = _proj(
        x, g_mp, w_in_r, tabs, conv_w, conv_state, seg)

    if past is None:
        k_all, v_all = kb[None], vb[None]
        kia_all, kib_all = kia[None], kib[None]
        n_keys = t
        sched = _causal_schedule(t, tq, spb_att * LANES)
        q_pos0 = 0
    else:
        cache_k, cache_v, cache_ki = past[0], past[1], past[2]
        plen = cache_k.shape[1]
        n_keys = plen + seg
        lk = -(-n_keys // (spb_att * LANES)) * (spb_att * LANES)
        pad = lk - n_keys

        def cat(c, new, width):
            parts = [c, new.reshape(nseq, seg, width)]
            if pad:
                parts.append(jnp.zeros((nseq, pad, width), BF16))
            return jnp.concatenate(parts, axis=1)

        def tail(new):
            return jnp.pad(new.reshape(nseq, seg, KV_W), ((0, 0), (plen, pad), (0, 0)))

        k_all, v_all = _pack_cache(cache_k, cache_v, tail(kb), tail(vb))
        cki = cache_ki.astype(BF16)
        zk = jnp.zeros_like(cki)
        kia_all = cat(jnp.concatenate([cki, zk], axis=-1), kia, LANES)
        kib_all = cat(jnp.concatenate([zk, cki], axis=-1), kib, LANES)
        sched = _batched_schedule(nseq, lk // (spb_att * LANES))
        q_pos0 = plen
    topk = min(TOPK_MAX, n_keys // 4)

    bias = _select(qi, kiwi, kia_all, kib_all, tq=tq, spb=spb_sel, topk=topk,
                   causal=past is None, n_valid=n_keys, q_pos0=q_pos0)
    o = _attend(q, k_all, v_all, bias, sched, tq=tq, spb=spb_att)
    x1, xn2 = _merge(o, yc, ga, gc, woa, woc, wout, x, g_mq, g_fp)
    y, ffn_g, ffn_v = _ffn(xn2, w_up, fconv_w, w_down, x1, g_fq, ffn_state, seg)
    ffn_new = jnp.concatenate([ffn_g, ffn_v], axis=-1)
    return y, k, v, kiwi[:, :IDX_DIM], conv_new[-nseq:], ffn_new[-nseq:]


def kernel(x_prompt, x_sample, cache_k, cache_v, cache_k_idx, state_conv, state_ffn_conv, norm_mix_pre, norm_mix_post, w_in, conv_w, w_o_attn, w_o_conv, w_out, norm_ffn_pre, norm_ffn_post, w_ffn_up, ffn_conv_w, w_ffn_down):
    depth = w_in.shape[0]
    assert depth == 1, "single-layer step"
    b, seq, _ = x_prompt.shape
    assert b == 1
    db, dseq, _ = x_sample.shape
    plen = cache_k.shape[2]
    assert dseq == CHUNK and plen % CHUNK == 0

    weights = (
        norm_mix_pre, norm_mix_post, _relayout_w_in(w_in[0]), conv_w[0],
        w_o_attn[0].astype(BF16), w_o_conv[0].astype(BF16), w_out[0].astype(BF16),
        norm_ffn_pre, norm_ffn_post, w_ffn_up[0].astype(BF16), ffn_conv_w[0], w_ffn_down[0].astype(BF16),
    )

    pos_p = jnp.arange(seq, dtype=jnp.int32)
    yp, kp, vp, kip, convp, ffnp = _stream(
        x_prompt.reshape(seq, D_MODEL), pos_p, seq, None, weights, tq=128, spb_sel=4, spb_att=4)

    pos_s = jnp.tile(jnp.arange(dseq, dtype=jnp.int32) + plen, db)
    past = (cache_k[0], cache_v[0], cache_k_idx[0], state_conv[0], state_ffn_conv[0])
    n_keys = plen + dseq
    spb_s = _sample_slabs(n_keys)
    ys, ks, vs, kis, convs, ffns = _stream(
        x_sample.reshape(db * dseq, D_MODEL), pos_s, dseq, past, weights, tq=dseq, spb_sel=spb_s, spb_att=spb_s)

    return (
        yp.reshape(1, seq, D_MODEL), ys.reshape(db, dseq, D_MODEL),
        kp.reshape(1, 1, seq, N_KV_HEADS, HEAD_DIM), vp.reshape(1, 1, seq, N_KV_HEADS, HEAD_DIM),
        kip.reshape(1, 1, seq, IDX_DIM), convp.reshape(1, 1, 2, D_MODEL), ffnp.reshape(1, 1, 2, 2 * D_FF),
        ks.reshape(1, db, dseq, N_KV_HEADS, HEAD_DIM), vs.reshape(1, db, dseq, N_KV_HEADS, HEAD_DIM),
        kis.reshape(1, db, dseq, IDX_DIM), convs.reshape(1, db, 2, D_MODEL), ffns.reshape(1, db, 2, 2 * D_FF),
    )


def _sample_slabs(n_keys):
    nslab = -(-n_keys // LANES)
    best = 1
    for d in range(1, nslab + 1):
        if nslab % d == 0 and d <= 11:
            best = d
    return best
```

```python
import functools

import jax
import jax.numpy as jnp
import numpy as np
from jax import lax
from jax.experimental import pallas as pl
from jax.experimental.pallas import tpu as pltpu

F32 = jnp.float32
BF16 = jnp.bfloat16

D_MODEL = 2048
N_HEADS = 16
N_KV_HEADS = 4
HEAD_DIM = 128
ROT_DIM = HEAD_DIM // 4
N_IDX_HEADS = 16
IDX_DIM = 64
IDX_ROT_DIM = IDX_DIM // 4
CHUNK = 64
TOPK_MAX = 256
ROPE_THETA = 500000.0
D_FF = 5632
RMS_EPS = 1e-6
NEG_INF = -1e30
Q_W = N_HEADS * HEAD_DIM
KV_W = N_KV_HEADS * HEAD_DIM
IQ_W = N_IDX_HEADS * IDX_DIM
GROUP = N_HEADS // N_KV_HEADS

LANES = 128
TM = 512
TN = 512
TF = 512
VMEM_LIMIT = 56 * 1024 * 1024

J_Q = 0
J_K = J_Q + Q_W // TN
J_V = J_K + 1
J_QI = J_V + 1
J_KIWI = J_QI + IQ_W // TN
J_CONV = J_KIWI + 1
N_CBLK = D_MODEL // TN
J_GA = J_CONV + 3 * N_CBLK
J_GC = J_GA + N_CBLK
NJ = J_GC + N_CBLK

Q_PRESCALE = float(np.float32((HEAD_DIM ** -0.5) * np.log2(np.e)))

INT_MIN = -(2 ** 31)
_VALID_KEY = int(np.array(0.5 * NEG_INF, np.float32).view(np.int32)) ^ 0x7FFFFFFF


def _rms(x, g):
    return x * lax.rsqrt(jnp.mean(x * x, axis=-1, keepdims=True) + RMS_EPS) * g


def _sortable(x):
    bits = lax.bitcast_convert_type(x, jnp.int32)
    return jnp.where(bits < 0, bits ^ jnp.int32(0x7FFFFFFF), bits)


def _dwconv_seg(u, prev2, w3):
    row = lax.broadcasted_iota(jnp.int32, u.shape, 0)
    p0, p1 = prev2[0:1], prev2[1:2]
    s1 = jnp.where(row == 0, p1, pltpu.roll(u, 1, 0))
    s2 = jnp.where(row == 0, p0, jnp.where(row == 1, p1, pltpu.roll(u, 2, 0)))
    return w3[0:1] * s2 + w3[1:2] * s1 + w3[2:3] * u


def _conv_tile(u, w3, prevs, seg):
    nseg = len(prevs)
    ys, news = [], []
    for s in range(nseg):
        us = u[s * seg:(s + 1) * seg]
        ys.append(_dwconv_seg(us, prevs[s], w3))
        news.append(us[seg - 2:seg])
    y = ys[0] if nseg == 1 else jnp.concatenate(ys, axis=0)
    return y, news


def _proj_kernel(x_ref, g_ref, w_ref, c128_ref, d128_ref, c64_ref, d64_ref, cw_ref, st_ref,
                 q_ref, k_ref, kb_ref, v_ref, vb_ref, qi_ref, kiwi_ref, kia_ref, kib_ref,
                 yc_ref, cn_ref, ga_ref, gc_ref,
                 xn_s, cb_s, cc_s, carry_s, *, seg, carried):
    i = pl.program_id(0)
    j = pl.program_id(1)

    @pl.when(j == 0)
    def _():
        xn_s[...] = _rms(x_ref[...], g_ref[...]).astype(BF16)

    def mm():
        return jnp.dot(xn_s[...], w_ref[...], preferred_element_type=F32)

    lane = lax.broadcasted_iota(jnp.int32, (TM, LANES), 1)

    def rope128(xh):
        partner = jnp.where(lane < ROT_DIM // 2, pltpu.roll(xh, LANES - ROT_DIM // 2, 1),
                            pltpu.roll(xh, ROT_DIM // 2, 1))
        return xh * c128_ref[...] + partner * d128_ref[...]

    def rope64(xh, c, d):
        first = (lane & (IDX_DIM - 1)) < IDX_ROT_DIM // 2
        partner = jnp.where(first, pltpu.roll(xh, LANES - IDX_ROT_DIM // 2, 1),
                            pltpu.roll(xh, IDX_ROT_DIM // 2, 1))
        return xh * c + partner * d

    @pl.when(j < J_K)
    def _():
        acc = mm()
        for h in range(TN // LANES):
            sl = slice(h * LANES, (h + 1) * LANES)
            q_ref[:, sl] = (rope128(acc[:, sl]) * Q_PRESCALE).astype(BF16)

    @pl.when(j == J_K)
    def _():
        acc = mm()
        for h in range(TN // LANES):
            sl = slice(h * LANES, (h + 1) * LANES)
            r = rope128(acc[:, sl])
            k_ref[:, sl] = r
            kb_ref[:, sl] = r.astype(BF16)

    @pl.when(j == J_V)
    def _():
        acc = mm()
        v_ref[...] = acc
        vb_ref[...] = acc.astype(BF16)

    @pl.when(jnp.logical_and(j >= J_QI, j < J_KIWI))
    def _():
        acc = mm()
        for h in range(TN // LANES):
            sl = slice(h * LANES, (h + 1) * LANES)
            qi_ref[:, sl] = rope64(acc[:, sl], c64_ref[...], d64_ref[...]).astype(BF16)

    @pl.when(j == J_KIWI)
    def _():
        is_ki = lane < IDX_DIM
        c = jnp.where(is_ki, c64_ref[...], 1.0)
        d = jnp.where(is_ki, d64_ref[...], 0.0)
        acc = jnp.dot(xn_s[...], w_ref[:, 0:LANES], preferred_element_type=F32)
        r = rope64(acc, c, d)
        kiwi_ref[...] = r
        ka = jnp.where(is_ki, r, 0.0)
        kia_ref[...] = ka.astype(BF16)
        kib_ref[...] = pltpu.roll(ka, IDX_DIM, 1).astype(BF16)

    jc = jnp.maximum(j - J_CONV, 0)
    in_conv = jnp.logical_and(j >= J_CONV, j < J_GA)
    cblk = jnp.minimum(jc // 3, N_CBLK - 1)
    part = jc % 3

    @pl.when(jnp.logical_and(in_conv, part == 0))
    def _():
        cb_s[...] = mm()

    @pl.when(jnp.logical_and(in_conv, part == 1))
    def _():
        cc_s[...] = mm()

    @pl.when(jnp.logical_and(in_conv, part == 2))
    def _():
        nseg = TM // seg
        if carried:
            @pl.when(i == 0)
            def _():
                carry_s[cblk] = jnp.zeros((8, TN), F32)
                carry_s[cblk, 0:2, :] = st_ref[0]
            prevs = [carry_s[cblk, 0:2, :]]
        else:
            prevs = [st_ref[s] for s in range(nseg)]
        u = cc_s[...] * mm()
        y, news = _conv_tile(u, cw_ref[...], prevs, seg)
        yc_ref[...] = (cb_s[...] * y).astype(BF16)
        for s in range(nseg):
            cn_ref[s] = news[s]
        if carried:
            carry_s[cblk, 0:2, :] = news[0]

    @pl.when(jnp.logical_and(j >= J_GA, j < J_GC))
    def _():
        ga_ref[...] = 1.0 / (1.0 + jnp.exp(-mm()))

    @pl.when(j >= J_GC)
    def _():
        gc_ref[...] = 1.0 / (1.0 + jnp.exp(-mm()))


def _proj(x, g, w, tabs, conv_w, state, seg):
    t = x.shape[0]
    ni = t // TM
    carried = seg == t
    nseg_tile = 1 if carried else TM // seg
    c128, d128, c64, d64 = tabs

    def row(i, j):
        return (i, 0)

    def const(i, j):
        return (0, 0)

    def cblk(j):
        return jnp.minimum(jnp.maximum(j - J_CONV, 0) // 3, N_CBLK - 1)

    def st_map(i, j):
        return (0 if carried else i, 0, cblk(j))

    in_specs = [
        pl.BlockSpec((TM, D_MODEL), row),
        pl.BlockSpec((1, D_MODEL), const),
        pl.BlockSpec((D_MODEL, TN), lambda i, j: (0, j)),
        pl.BlockSpec((TM, LANES), row),
        pl.BlockSpec((TM, LANES), row),
        pl.BlockSpec((TM, LANES), row),
        pl.BlockSpec((TM, LANES), row),
        pl.BlockSpec((3, TN), lambda i, j: (0, cblk(j))),
        pl.BlockSpec((nseg_tile, 2, TN), st_map),
    ]
    out_shape = [
        jax.ShapeDtypeStruct((t, Q_W), BF16),
        jax.ShapeDtypeStruct((t, KV_W), F32),
        jax.ShapeDtypeStruct((t, KV_W), BF16),
        jax.ShapeDtypeStruct((t, KV_W), F32),
        jax.ShapeDtypeStruct((t, KV_W), BF16),
        jax.ShapeDtypeStruct((t, IQ_W), BF16),
        jax.ShapeDtypeStruct((t, LANES), F32),
        jax.ShapeDtypeStruct((t, LANES), BF16),
        jax.ShapeDtypeStruct((t, LANES), BF16),
        jax.ShapeDtypeStruct((t, D_MODEL), BF16),
        jax.ShapeDtypeStruct((ni * nseg_tile, 2, D_MODEL), F32),
        jax.ShapeDtypeStruct((t, D_MODEL), F32),
        jax.ShapeDtypeStruct((t, D_MODEL), F32),
    ]
    out_specs = [
        pl.BlockSpec((TM, TN), lambda i, j: (i, jnp.clip(j - J_Q, 0, J_K - J_Q - 1))),
        pl.BlockSpec((TM, TN), row),
        pl.BlockSpec((TM, TN), row),
        pl.BlockSpec((TM, TN), row),
        pl.BlockSpec((TM, TN), row),
        pl.BlockSpec((TM, TN), lambda i, j: (i, jnp.clip(j - J_QI, 0, J_KIWI - J_QI - 1))),
        pl.BlockSpec((TM, LANES), row),
        pl.BlockSpec((TM, LANES), row),
        pl.BlockSpec((TM, LANES), row),
        pl.BlockSpec((TM, TN), lambda i, j: (i, cblk(j))),
        pl.BlockSpec((nseg_tile, 2, TN), lambda i, j: (i, 0, cblk(j))),
        pl.BlockSpec((TM, TN), lambda i, j: (i, jnp.clip(j - J_GA, 0, N_CBLK - 1))),
        pl.BlockSpec((TM, TN), lambda i, j: (i, jnp.clip(j - J_GC, 0, N_CBLK - 1))),
    ]
    return pl.pallas_call(
        functools.partial(_proj_kernel, seg=min(seg, TM), carried=carried),
        grid=(ni, NJ),
        in_specs=in_specs,
        out_specs=out_specs,
        out_shape=out_shape,
        scratch_shapes=[
            pltpu.VMEM((TM, D_MODEL), BF16),
            pltpu.VMEM((TM, TN), F32),
            pltpu.VMEM((TM, TN), F32),
            pltpu.VMEM((N_CBLK, 8, TN), F32),
        ],
        compiler_params=pltpu.CompilerParams(
            dimension_semantics=("arbitrary", "arbitrary"), vmem_limit_bytes=VMEM_LIMIT),
        name="proj",
    )(x, g, w, c128, d128, c64, d64, conv_w, state)


def _select_kernel(qi_ref, kiwi_ref, kia_ref, kib_ref, bias_ref, key_s, wb_s, *,
                   tq, spb, nkb_total, topk, causal, n_valid, q_pos0):
    n = pl.program_id(0)
    npair = N_IDX_HEADS // 2
    kiwi = kiwi_ref[...]
    wscale = (IDX_DIM ** -0.5) * (N_IDX_HEADS ** -0.5)
    for h in range(N_IDX_HEADS):
        wb_s[h] = jnp.broadcast_to(kiwi[:, IDX_DIM + h:IDX_DIM + h + 1], (tq, LANES)) * wscale
    q2 = jnp.concatenate([qi_ref[:, p * LANES:(p + 1) * LANES] for p in range(npair)], axis=0)

    if causal:
        qpos0 = n * tq
        nkb = (qpos0 + tq + spb * LANES - 1) // (spb * LANES)
    else:
        qpos0 = q_pos0
        nkb = nkb_total
    lane = lax.broadcasted_iota(jnp.int32, (tq, LANES), 1)
    qchunk = (qpos0 + lax.broadcasted_iota(jnp.int32, (tq, LANES), 0)) >> 6
    nt = (((1,), (1,)), ((), ()))

    def score_blk(kb, carry):
        m1, m2 = carry
        base = pl.multiple_of(kb * (spb * LANES), spb * LANES)
        ka = kia_ref[pl.ds(base, spb * LANES), :]
        kbm = kib_ref[pl.ds(base, spb * LANES), :]
        le = lax.dot_general(q2, ka, nt, preferred_element_type=F32)
        lo = lax.dot_general(q2, kbm, nt, preferred_element_type=F32)
        for c in range(spb):
            cs = slice(c * LANES, (c + 1) * LANES)
            acc = jnp.zeros((tq, LANES), F32)
            for p in range(npair):
                rs = slice(p * tq, (p + 1) * tq)
                acc = acc + jnp.maximum(le[rs, cs], 0.0) * wb_s[2 * p]
                acc = acc + jnp.maximum(lo[rs, cs], 0.0) * wb_s[2 * p + 1]
            col = base + c * LANES + lane
            adm = jnp.logical_and((col >> 6) <= qchunk, col < n_valid)
            sc = jnp.where(adm, acc, NEG_INF)
            m2 = jnp.maximum(m2, jnp.minimum(m1, sc))
            m1 = jnp.maximum(m1, sc)
            key_s[kb * spb + c] = _sortable(sc)
        return m1, m2

    neg = jnp.full((tq, LANES), NEG_INF, F32)
    m1, m2 = lax.fori_loop(0, nkb, score_blk, (neg, neg))

    def count_ge(cand):
        def body(kb, acc):
            for c in range(spb):
                acc = acc + jnp.where(key_s[kb * spb + c] >= cand, 1.0, 0.0)
            return acc
        acc = lax.fori_loop(0, nkb, body, jnp.zeros((tq, LANES), F32))
        return jnp.sum(acc, axis=1, keepdims=True)

    ones = jnp.ones((tq, LANES), jnp.int32)
    lo0 = _sortable(jnp.min(m2, axis=1, keepdims=True)) * ones
    hi0 = _sortable(jnp.max(m1, axis=1, keepdims=True)) * ones + 1
    kf = float(topk)

    def width_f(lo_k, hi_k):
        w = hi_k - lo_k
        wf = w.astype(F32)
        return jnp.where(w < 0, wf + 4294967296.0, wf)

    def unresolved(lo_k, hi_k, c_lo):
        open_ = jnp.logical_and(c_lo != kf, (hi_k - lo_k) != 1)
        return jnp.max(jnp.where(open_, 1.0, 0.0))

    def search_cond(st):
        return jnp.logical_and(st[0] < 70, st[-1] > 0.0)

    def search_body(st):
        it, lo_k, hi_k, c_lo, c_hi, w_prev, _ = st
        w = hi_k - lo_k
        wf = width_f(lo_k, hi_k)
        l_lo = jnp.log2(c_lo)
        frac = (l_lo - np.float32(np.log2(kf - 0.5))) / (l_lo - jnp.log2(jnp.maximum(c_hi, 0.25)))
        frac = jnp.where(wf > 0.5 * w_prev, 0.5, frac)
        step = jnp.minimum(jnp.maximum(frac * wf, 1.0), jnp.maximum(wf - 1.0, 1.0))
        step = jnp.where(step >= 2147483648.0, step - 4294967296.0, step).astype(jnp.int32)
        half = lax.shift_right_logical(w, 1)
        inside = jnp.logical_and((step ^ INT_MIN) < (w ^ INT_MIN), step != 0)
        mid = lo_k + jnp.where(inside, step, jnp.maximum(half, 1))
        cnt = count_ge(mid)
        ge = cnt >= kf
        lo_k = jnp.where(ge, mid, lo_k)
        hi_k = jnp.where(ge, hi_k, mid)
        c_lo = jnp.where(ge, cnt, c_lo)
        c_hi = jnp.where(ge, c_hi, cnt)
        return it + 1, lo_k, hi_k, c_lo, c_hi, wf, unresolved(lo_k, hi_k, c_lo)

    c_lo0 = count_ge(lo0) * jnp.ones((tq, LANES), F32)
    c_hi0 = jnp.zeros((tq, LANES), F32)
    big = jnp.full((tq, LANES), 3.0e38, F32)
    st = lax.while_loop(search_cond, search_body,
                        (jnp.int32(0), lo0, hi0, c_lo0, c_hi0, big, unresolved(lo0, hi0, c_lo0)))
    thr = st[1]

    def emit(kb, carry):
        for c in range(spb):
            k = key_s[kb * spb + c]
            sel = jnp.logical_and(k >= thr, k > _VALID_KEY)
            bias_ref[kb * spb + c] = jnp.where(sel, 0.0, NEG_INF).astype(BF16)
        return carry

    lax.fori_loop(0, nkb, emit, 0)

    def fill(kb, carry):
        for c in range(spb):
            bias_ref[kb * spb + c] = jnp.full((tq, LANES), NEG_INF, BF16)
        return carry

    lax.fori_loop(nkb, nkb_total, fill, 0)


def _select(qi, kiwi, kia, kib, *, tq, spb, topk, causal, n_valid, q_pos0):
    t = qi.shape[0]
    nb = t // tq
    lk = kia.shape[1]
    nslab = lk // LANES
    nkb_total = nslab // spb
    assert topk <= 2 * LANES and nslab >= 2, "the bisection's starting lower bound needs two keys per lane"

    def kmap(n):
        return (0 if causal else n, 0, 0)

    return pl.pallas_call(
        functools.partial(_select_kernel, tq=tq, spb=spb, nkb_total=nkb_total, topk=topk,
                          causal=causal, n_valid=n_valid, q_pos0=q_pos0),
        grid=(nb,),
        in_specs=[
            pl.BlockSpec((tq, IQ_W), lambda n: (n, 0)),
            pl.BlockSpec((tq, LANES), lambda n: (n, 0)),
            pl.BlockSpec((None, lk, LANES), kmap),
            pl.BlockSpec((None, lk, LANES), kmap),
        ],
        out_specs=pl.BlockSpec((nslab, tq, LANES), lambda n: (0, n, 0)),
        out_shape=jax.ShapeDtypeStruct((nslab, t, LANES), BF16),
        scratch_shapes=[
            pltpu.VMEM((nslab, tq, LANES), jnp.int32),
            pltpu.VMEM((N_IDX_HEADS, tq, LANES), F32),
        ],
        compiler_params=pltpu.CompilerParams(
            dimension_semantics=("arbitrary",), vmem_limit_bytes=VMEM_LIMIT),
        name="select",
    )(qi, kiwi, kia, kib)


def _attend_kernel(qb_ref, kb_ref, kbat_ref, last_ref, q_ref, k_ref, v_ref, b_ref, o_ref,
                   m_s, l_s, acc_s, *, tq, spb):
    s = pl.program_id(0)

    @pl.when(kb_ref[s] == 0)
    def _():
        m_s[...] = jnp.full(m_s.shape, 0.1 * NEG_INF, F32)
        l_s[...] = jnp.zeros(l_s.shape, F32)
        acc_s[...] = jnp.zeros(acc_s.shape, F32)

    biases = [b_ref[c].astype(F32)[None] for c in range(spb)]
    ones = jnp.ones((spb * LANES, LANES), BF16)

    for g in range(N_KV_HEADS):
        qg = [q_ref[:, (g * GROUP + h) * HEAD_DIM:(g * GROUP + h + 1) * HEAD_DIM] for h in range(GROUP)]
        q4 = jnp.concatenate(qg, axis=0)
        kg = k_ref[:, g * HEAD_DIM:(g + 1) * HEAD_DIM]
        sc = lax.dot_general(q4, kg, (((1,), (1,)), ((), ())), preferred_element_type=F32)
        slabs = []
        mx = None
        for c in range(spb):
            sl = sc[:, c * LANES:(c + 1) * LANES].reshape(GROUP, tq, LANES) + biases[c]
            sl = sl.reshape(GROUP * tq, LANES)
            slabs.append(sl)
            mx = sl if mx is None else jnp.maximum(mx, sl)
        m_prev = m_s[g]
        m_new = jnp.maximum(m_prev, jnp.max(mx, axis=1, keepdims=True))
        alpha = jnp.exp2(m_prev - m_new)
        pmat = jnp.concatenate([jnp.exp2(sl - m_new).astype(BF16) for sl in slabs], axis=1)
        v1 = jnp.concatenate([v_ref[:, g * HEAD_DIM:(g + 1) * HEAD_DIM], ones], axis=1)
        pv = jnp.dot(pmat, v1, preferred_element_type=F32)
        acc_s[g] = alpha * acc_s[g] + pv[:, 0:HEAD_DIM]
        l_s[g] = alpha * l_s[g] + pv[:, HEAD_DIM:HEAD_DIM + LANES]
        m_s[g] = m_new

    @pl.when(last_ref[s] == 1)
    def _():
        for g in range(N_KV_HEADS):
            o = acc_s[g] / l_s[g]
            for h in range(GROUP):
                col = (g * GROUP + h) * HEAD_DIM
                o_ref[:, col:col + HEAD_DIM] = o[h * tq:(h + 1) * tq].astype(BF16)


def _attend(q, k_all, v_all, bias, sched, *, tq, spb):
    t = q.shape[0]
    qb, kb, kbat, last = sched
    nsteps = qb.shape[0]
    blk = spb * LANES
    grid_spec = pltpu.PrefetchScalarGridSpec(
        num_scalar_prefetch=4,
        grid=(nsteps,),
        in_specs=[
            pl.BlockSpec((tq, Q_W), lambda s, qb, kb, kbat, last: (qb[s], 0)),
            pl.BlockSpec((None, blk, KV_W), lambda s, qb, kb, kbat, last: (kbat[s], kb[s], 0)),
            pl.BlockSpec((None, blk, KV_W), lambda s, qb, kb, kbat, last: (kbat[s], kb[s], 0)),
            pl.BlockSpec((spb, tq, LANES), lambda s, qb, kb, kbat, last: (kb[s], qb[s], 0)),
        ],
        out_specs=pl.BlockSpec((tq, Q_W), lambda s, qb, kb, kbat, last: (qb[s], 0)),
        scratch_shapes=[
            pltpu.VMEM((N_KV_HEADS, GROUP * tq, LANES), F32),
            pltpu.VMEM((N_KV_HEADS, GROUP * tq, LANES), F32),
            pltpu.VMEM((N_KV_HEADS, GROUP * tq, HEAD_DIM), F32),
        ],
    )
    return pl.pallas_call(
        functools.partial(_attend_kernel, tq=tq, spb=spb),
        grid_spec=grid_spec,
        out_shape=jax.ShapeDtypeStruct((t, Q_W), BF16),
        compiler_params=pltpu.CompilerParams(
            dimension_semantics=("arbitrary",), vmem_limit_bytes=VMEM_LIMIT),
        name="attend",
    )(qb, kb, kbat, last, q, k_all, v_all, bias)


def _causal_schedule(t, tq, blk):
    qb, kb, last = [], [], []
    for n in range(t // tq):
        nk = ((n + 1) * tq + blk - 1) // blk
        for k in range(nk):
            qb.append(n)
            kb.append(k)
            last.append(1 if k == nk - 1 else 0)
    z = np.zeros(len(qb), np.int32)
    return (jnp.asarray(qb, jnp.int32), jnp.asarray(kb, jnp.int32), jnp.asarray(z), jnp.asarray(last, jnp.int32))


def _batched_schedule(nbatch, nk):
    qb = np.repeat(np.arange(nbatch, dtype=np.int32), nk)
    kb = np.tile(np.arange(nk, dtype=np.int32), nbatch)
    last = (kb == nk - 1).astype(np.int32)
    return (jnp.asarray(qb), jnp.asarray(kb), jnp.asarray(qb), jnp.asarray(last))


def _merge_kernel(o_ref, yc_ref, ga_ref, gc_ref, woa_ref, woc_ref, wout_ref, x_ref, gq_ref, gp_ref,
                  x1_ref, xn2_ref, mg_s, m_s):
    j = pl.program_id(1)
    nblk = D_MODEL // TN

    @pl.when(j < nblk)
    def _():
        a = jnp.dot(o_ref[...], woa_ref[...], preferred_element_type=F32)
        c = jnp.dot(yc_ref[...], woc_ref[...], preferred_element_type=F32)
        mg_s[j] = (ga_ref[...] * a + gc_ref[...] * c).astype(BF16)

    @pl.when(j >= nblk)
    def _():
        mg = jnp.concatenate([mg_s[b] for b in range(nblk)], axis=1)
        m_s[j - nblk] = jnp.dot(mg, wout_ref[...], preferred_element_type=F32)

    @pl.when(j == 2 * nblk - 1)
    def _():
        m = jnp.concatenate([m_s[b] for b in range(nblk)], axis=1)
        x1 = x_ref[...] + _rms(m, gq_ref[...])
        x1_ref[...] = x1
        xn2_ref[...] = _rms(x1, gp_ref[...]).astype(BF16)


def _merge(o, yc, ga, gc, woa, woc, wout, x, gq, gp):
    t = x.shape[0]
    nblk = D_MODEL // TN

    def row(i, j):
        return (i, 0)

    def lo(i, j):
        return (i, jnp.minimum(j, nblk - 1))

    return pl.pallas_call(
        _merge_kernel,
        grid=(t // TM, 2 * nblk),
        in_specs=[
            pl.BlockSpec((TM, Q_W), row),
            pl.BlockSpec((TM, D_MODEL), row),
            pl.BlockSpec((TM, TN), lo),
            pl.BlockSpec((TM, TN), lo),
            pl.BlockSpec((Q_W, TN), lambda i, j: (0, jnp.minimum(j, nblk - 1))),
            pl.BlockSpec((D_MODEL, TN), lambda i, j: (0, jnp.minimum(j, nblk - 1))),
            pl.BlockSpec((D_MODEL, TN), lambda i, j: (0, jnp.maximum(j - nblk, 0))),
            pl.BlockSpec((TM, D_MODEL), row),
            pl.BlockSpec((1, D_MODEL), lambda i, j: (0, 0)),
            pl.BlockSpec((1, D_MODEL), lambda i, j: (0, 0)),
        ],
        out_specs=[pl.BlockSpec((TM, D_MODEL), row), pl.BlockSpec((TM, D_MODEL), row)],
        out_shape=[jax.ShapeDtypeStruct((t, D_MODEL), F32), jax.ShapeDtypeStruct((t, D_MODEL), BF16)],
        scratch_shapes=[pltpu.VMEM((nblk, TM, TN), BF16), pltpu.VMEM((nblk, TM, TN), F32)],
        compiler_params=pltpu.CompilerParams(
            dimension_semantics=("arbitrary", "arbitrary"), vmem_limit_bytes=VMEM_LIMIT),
        name="merge",
    )(o, yc, ga, gc, woa, woc, wout, x, gq, gp)


def _ffn_kernel(xn_ref, wg_ref, wv_ref, cwg_ref, cwv_ref, wd_ref, x1_ref, gq_ref, stg_ref, stv_ref,
                y_ref, ng_ref, nv_ref, acc_s, cg_s, cv_s, *, seg, carried):
    i = pl.program_id(0)
    jf = pl.program_id(1)
    nseg = TM // seg

    @pl.when(jf == 0)
    def _():
        acc_s[...] = jnp.zeros(acc_s.shape, F32)

    if carried:
        @pl.when(i == 0)
        def _():
            for carry_s, st_ref in ((cg_s, stg_ref), (cv_s, stv_ref)):
                carry_s[jf] = jnp.zeros((8, TF), F32)
                carry_s[jf, 0:2, :] = st_ref[0]

    xn = xn_ref[...]

    def branch(w_ref, cw_ref, st_ref, carry_s, new_ref):
        up = jnp.dot(xn, w_ref[...], preferred_element_type=F32)
        if carried:
            prevs = [carry_s[jf, 0:2, :]]
        else:
            prevs = [st_ref[s] for s in range(nseg)]
        y, news = _conv_tile(up, cw_ref[...], prevs, seg)
        for s in range(nseg):
            new_ref[s] = news[s]
        if carried:
            carry_s[jf, 0:2, :] = news[0]
        return y

    gate = branch(wg_ref, cwg_ref, stg_ref, cg_s, ng_ref)
    val = branch(wv_ref, cwv_ref, stv_ref, cv_s, nv_ref)
    c0 = np.float32(np.sqrt(2.0 / np.pi))
    gelu = 0.5 * gate * (1.0 + jnp.tanh(c0 * (gate + 0.044715 * (gate * gate * gate))))
    hid = (gelu * val).astype(BF16)
    acc_s[...] += jnp.dot(hid, wd_ref[...], preferred_element_type=F32)

    @pl.when(jf == pl.num_programs(1) - 1)
    def _():
        y_ref[...] = x1_ref[...] + _rms(acc_s[...], gq_ref[...])


def _ffn(xn2, w_up, cw, w_down, x1, gq, state, seg):
    t = x1.shape[0]
    nf = D_FF // TF
    carried = seg == t
    nseg_tile = 1 if carried else TM // seg

    def row(i, j):
        return (i, 0)

    def stg(i, j):
        return (0 if carried else i, 0, j)

    def stv(i, j):
        return (0 if carried else i, 0, j + nf)

    return pl.pallas_call(
        functools.partial(_ffn_kernel, seg=min(seg, TM), carried=carried),
        grid=(t // TM, nf),
        in_specs=[
            pl.BlockSpec((TM, D_MODEL), row),
            pl.BlockSpec((D_MODEL, TF), lambda i, j: (0, j)),
            pl.BlockSpec((D_MODEL, TF), lambda i, j: (0, j + nf)),
            pl.BlockSpec((3, TF), lambda i, j: (0, j)),
            pl.BlockSpec((3, TF), lambda i, j: (0, j + nf)),
            pl.BlockSpec((TF, D_MODEL), lambda i, j: (j, 0)),
            pl.BlockSpec((TM, D_MODEL), row),
            pl.BlockSpec((1, D_MODEL), lambda i, j: (0, 0)),
            pl.BlockSpec((nseg_tile, 2, TF), stg),
            pl.BlockSpec((nseg_tile, 2, TF), stv),
        ],
        out_specs=[
            pl.BlockSpec((TM, D_MODEL), row),
            pl.BlockSpec((nseg_tile, 2, TF), lambda i, j: (i, 0, j)),
            pl.BlockSpec((nseg_tile, 2, TF), lambda i, j: (i, 0, j)),
        ],
        out_shape=[
            jax.ShapeDtypeStruct((t, D_MODEL), F32),
            jax.ShapeDtypeStruct((t // TM * nseg_tile, 2, D_FF), F32),
            jax.ShapeDtypeStruct((t // TM * nseg_tile, 2, D_FF), F32),
        ],
        scratch_shapes=[
            pltpu.VMEM((TM, D_MODEL), F32),
            pltpu.VMEM((nf, 8, TF), F32),
            pltpu.VMEM((nf, 8, TF), F32),
        ],
        compiler_params=pltpu.CompilerParams(
            dimension_semantics=("arbitrary", "arbitrary"), vmem_limit_bytes=VMEM_LIMIT),
        name="ffn",
    )(xn2, w_up, w_up, cw, cw, w_down, x1, gq, state, state)


PACK_ROWS = 1024


def _pack_kernel(ck_ref, cv_ref, kt_ref, vt_ref, ko_ref, vo_ref, *, rows):
    del kt_ref, vt_ref
    for src, dst in ((ck_ref, ko_ref), (cv_ref, vo_ref)):
        for g in range(N_KV_HEADS):
            dst[:, g * HEAD_DIM:(g + 1) * HEAD_DIM] = src[pl.ds(g, rows, stride=N_KV_HEADS), :].astype(BF16)


def _pack_cache(cache_k, cache_v, k_tail, v_tail):
    nb, plen = cache_k.shape[0], cache_k.shape[1]
    rows = int(np.gcd(plen, PACK_ROWS))
    assert rows % 16 == 0
    cspec = pl.BlockSpec((None, rows * N_KV_HEADS, HEAD_DIM), lambda b, r: (b, r, 0))
    ospec = pl.BlockSpec((None, rows, KV_W), lambda b, r: (b, r, 0))
    anyspec = pl.BlockSpec(memory_space=pl.ANY)
    flat = (nb, plen * N_KV_HEADS, HEAD_DIM)
    return pl.pallas_call(
        functools.partial(_pack_kernel, rows=rows),
        grid=(nb, plen // rows),
        in_specs=[cspec, cspec, anyspec, anyspec],
        out_specs=[ospec, ospec],
        out_shape=[jax.ShapeDtypeStruct(k_tail.shape, BF16), jax.ShapeDtypeStruct(v_tail.shape, BF16)],
        input_output_aliases={2: 0, 3: 1},
        compiler_params=pltpu.CompilerParams(
            dimension_semantics=("arbitrary", "arbitrary"), vmem_limit_bytes=VMEM_LIMIT),
        name="pack_cache",
    )(cache_k.reshape(flat), cache_v.reshape(flat), k_tail, v_tail)


def _rope_tables(pos, rot, width):
    half = rot // 2
    freqs = ROPE_THETA ** (-jnp.arange(half, dtype=F32) / half)
    ang = pos.astype(F32)[:, None] * freqs[None, :]
    cos, sin = jnp.cos(ang), jnp.sin(ang)
    t = pos.shape[0]
    c = jnp.concatenate([cos, cos, jnp.ones((t, width - rot), F32)], axis=1)
    d = jnp.concatenate([-sin, sin, jnp.zeros((t, width - rot), F32)], axis=1)
    reps = LANES // width
    return jnp.tile(c, (1, reps)), jnp.tile(d, (1, reps))


def _relayout_w_in(w):
    o = np.cumsum([0, Q_W, KV_W, KV_W, IQ_W, IDX_DIM, N_IDX_HEADS, D_MODEL, D_MODEL, D_MODEL, D_MODEL, D_MODEL])
    q, k, v, qi = (w[:, o[a]:o[a + 1]] for a in range(4))
    kiwi = w[:, o[4]:o[6]]
    cb, cc, ch, ga, gc = (w[:, o[a]:o[a + 1]] for a in range(6, 11))
    parts = [q, k, v, qi, kiwi, jnp.zeros((D_MODEL, TN - kiwi.shape[1]), w.dtype)]
    for c in range(N_CBLK):
        sl = slice(c * TN, (c + 1) * TN)
        parts += [cb[:, sl], cc[:, sl], ch[:, sl]]
    parts += [ga, gc]
    return jnp.concatenate(parts, axis=1).astype(BF16)


def _stream(x, pos, seg, past, weights, *, tq, spb_sel, spb_att):
    (g_mp, g_mq, w_in_r, conv_w, woa, woc, wout, g_fp, g_fq, w_up, fconv_w, w_down) = weights
    t = x.shape[0]
    nseq = t // seg
    tabs = _rope_tables(pos, ROT_DIM, HEAD_DIM) + _rope_tables(pos, IDX_ROT_DIM, IDX_DIM)
    if past is None:
        conv_state = jnp.zeros((nseq, 2, D_MODEL), F32)
        ffn_state = jnp.zeros((nseq, 2, 2 * D_FF), F32)
    else:
        conv_state, ffn_state = past[3], past[4]

    (q, k, kb, v, vb, qi, kiwi, kia, kib, yc, conv_new, ga, gc) = _proj(
        x, g_mp, w_in_r, tabs, conv_w, conv_state, seg)

    if past is None:
        k_all, v_all = kb[None], vb[None]
        kia_all, kib_all = kia[None], kib[None]
        n_keys = t
        sched = _causal_schedule(t, tq, spb_att * LANES)
        q_pos0 = 0
    else:
        cache_k, cache_v, cache_ki = past[0], past[1], past[2]
        plen = cache_k.shape[1]
        n_keys = plen + seg
        lk = -(-n_keys // (spb_att * LANES)) * (spb_att * LANES)
        pad = lk - n_keys

        def cat(c, new, width):
            parts = [c, new.reshape(nseq, seg, width)]
            if pad:
                parts.append(jnp.zeros((nseq, pad, width), BF16))
            return jnp.concatenate(parts, axis=1)

        def tail(new):
            return jnp.pad(new.reshape(nseq, seg, KV_W), ((0, 0), (plen, pad), (0, 0)))

        k_all, v_all = _pack_cache(cache_k, cache_v, tail(kb), tail(vb))
        cki = cache_ki.astype(BF16)
        zk = jnp.zeros_like(cki)
        kia_all = cat(jnp.concatenate([cki, zk], axis=-1), kia, LANES)
        kib_all = cat(jnp.concatenate([zk, cki], axis=-1), kib, LANES)
        sched = _batched_schedule(nseq, lk // (spb_att * LANES))
        q_pos0 = plen
    topk = min(TOPK_MAX, n_keys // 4)

    bias = _select(qi, kiwi, kia_all, kib_all, tq=tq, spb=spb_sel, topk=topk,
                   causal=past is None, n_valid=n_keys, q_pos0=q_pos0)
    o = _attend(q, k_all, v_all, bias, sched, tq=tq, spb=spb_att)
    x1, xn2 = _merge(o, yc, ga, gc, woa, woc, wout, x, g_mq, g_fp)
    y, ffn_g, ffn_v = _ffn(xn2, w_up, fconv_w, w_down, x1, g_fq, ffn_state, seg)
    ffn_new = jnp.concatenate([ffn_g, ffn_v], axis=-1)
    return y, k, v, kiwi[:, :IDX_DIM], conv_new[-nseq:], ffn_new[-nseq:]


def kernel(x_prompt, x_sample, cache_k, cache_v, cache_k_idx, state_conv, state_ffn_conv, norm_mix_pre, norm_mix_post, w_in, conv_w, w_o_attn, w_o_conv, w_out, norm_ffn_pre, norm_ffn_post, w_ffn_up, ffn_conv_w, w_ffn_down):
    depth = w_in.shape[0]
    assert depth == 1, "single-layer step"
    b, seq, _ = x_prompt.shape
    assert b == 1
    db, dseq, _ = x_sample.shape
    plen = cache_k.shape[2]
    assert dseq == CHUNK and plen % CHUNK == 0

    weights = (
        norm_mix_pre, norm_mix_post, _relayout_w_in(w_in[0]), conv_w[0],
        w_o_attn[0].astype(BF16), w_o_conv[0].astype(BF16), w_out[0].astype(BF16),
        norm_ffn_pre, norm_ffn_post, w_ffn_up[0].astype(BF16), ffn_conv_w[0], w_ffn_down[0].astype(BF16),
    )

    pos_p = jnp.arange(seq, dtype=jnp.int32)
    yp, kp, vp, kip, convp, ffnp = _stream(
        x_prompt.reshape(seq, D_MODEL), pos_p, seq, None, weights, tq=128, spb_sel=4, spb_att=8)

    pos_s = jnp.tile(jnp.arange(dseq, dtype=jnp.int32) + plen, db)
    past = (cache_k[0], cache_v[0], cache_k_idx[0], state_conv[0], state_ffn_conv[0])
    n_keys = plen + dseq
    spb_s = _sample_slabs(n_keys)
    ys, ks, vs, kis, convs, ffns = _stream(
        x_sample.reshape(db * dseq, D_MODEL), pos_s, dseq, past, weights, tq=dseq, spb_sel=spb_s, spb_att=spb_s)

    return (
        yp.reshape(1, seq, D_MODEL), ys.reshape(db, dseq, D_MODEL),
        kp.reshape(1, 1, seq, N_KV_HEADS, HEAD_DIM), vp.reshape(1, 1, seq, N_KV_HEADS, HEAD_DIM),
        kip.reshape(1, 1, seq, IDX_DIM), convp.reshape(1, 1, 2, D_MODEL), ffnp.reshape(1, 1, 2, 2 * D_FF),
        ks.reshape(1, db, dseq, N_KV_HEADS, HEAD_DIM), vs.reshape(1, db, dseq, N_KV_HEADS, HEAD_DIM),
        kis.reshape(1, db, dseq, IDX_DIM), convs.reshape(1, db, 2, D_MODEL), ffns.reshape(1, db, 2, 2 * D_FF),
    )


def _sample_slabs(n_keys):
    nslab = -(-n_keys // LANES)
    best = 1
    for d in range(1, nslab + 1):
        if nslab % d == 0 and d <= 11:
            best = d
    return best
```

```python
import functools

import jax
import jax.numpy as jnp
import numpy as np
from jax import lax
from jax.experimental import pallas as pl
from jax.experimental.pallas import tpu as pltpu

F32 = jnp.float32
BF16 = jnp.bfloat16

D_MODEL = 2048
N_HEADS = 16
N_KV_HEADS = 4
HEAD_DIM = 128
ROT_DIM = HEAD_DIM // 4
N_IDX_HEADS = 16
IDX_DIM = 64
IDX_ROT_DIM = IDX_DIM // 4
CHUNK = 64
TOPK_MAX = 256
ROPE_THETA = 500000.0
D_FF = 5632
RMS_EPS = 1e-6
NEG_INF = -1e30
Q_W = N_HEADS * HEAD_DIM
KV_W = N_KV_HEADS * HEAD_DIM
IQ_W = N_IDX_HEADS * IDX_DIM
GROUP = N_HEADS // N_KV_HEADS

LANES = 128
TM = 512
TN = 512
TF = 512
VMEM_LIMIT = 56 * 1024 * 1024

J_Q = 0
J_K = J_Q + Q_W // TN
J_V = J_K + 1
J_QI = J_V + 1
J_KIWI = J_QI + IQ_W // TN
J_CONV = J_KIWI + 1
N_CBLK = D_MODEL // TN
J_GA = J_CONV + 3 * N_CBLK
J_GC = J_GA + N_CBLK
NJ = J_GC + N_CBLK

Q_PRESCALE = float(np.float32((HEAD_DIM ** -0.5) * np.log2(np.e)))

INT_MIN = -(2 ** 31)
_VALID_KEY = int(np.array(0.5 * NEG_INF, np.float32).view(np.int32)) ^ 0x7FFFFFFF


def _rms(x, g):
    return x * lax.rsqrt(jnp.mean(x * x, axis=-1, keepdims=True) + RMS_EPS) * g


def _sortable(x):
    bits = lax.bitcast_convert_type(x, jnp.int32)
    return jnp.where(bits < 0, bits ^ jnp.int32(0x7FFFFFFF), bits)


def _dwconv_seg(u, prev2, w3):
    row = lax.broadcasted_iota(jnp.int32, u.shape, 0)
    p0, p1 = prev2[0:1], prev2[1:2]
    s1 = jnp.where(row == 0, p1, pltpu.roll(u, 1, 0))
    s2 = jnp.where(row == 0, p0, jnp.where(row == 1, p1, pltpu.roll(u, 2, 0)))
    return w3[0:1] * s2 + w3[1:2] * s1 + w3[2:3] * u


def _conv_tile(u, w3, prevs, seg):
    nseg = len(prevs)
    ys, news = [], []
    for s in range(nseg):
        us = u[s * seg:(s + 1) * seg]
        ys.append(_dwconv_seg(us, prevs[s], w3))
        news.append(us[seg - 2:seg])
    y = ys[0] if nseg == 1 else jnp.concatenate(ys, axis=0)
    return y, news


def _proj_kernel(x_ref, g_ref, w_ref, c128_ref, d128_ref, c64_ref, d64_ref, cw_ref, st_ref,
                 q_ref, k_ref, kb_ref, v_ref, vb_ref, qi_ref, kiwi_ref, kia_ref, kib_ref,
                 yc_ref, cn_ref, ga_ref, gc_ref,
                 xn_s, cb_s, cc_s, carry_s, *, seg, carried):
    i = pl.program_id(0)
    j = pl.program_id(1)

    @pl.when(j == 0)
    def _():
        xn_s[...] = _rms(x_ref[...], g_ref[...]).astype(BF16)

    def mm():
        return jnp.dot(xn_s[...], w_ref[...], preferred_element_type=F32)

    lane = lax.broadcasted_iota(jnp.int32, (TM, LANES), 1)

    def rope128(xh):
        partner = jnp.where(lane < ROT_DIM // 2, pltpu.roll(xh, LANES - ROT_DIM // 2, 1),
                            pltpu.roll(xh, ROT_DIM // 2, 1))
        return xh * c128_ref[...] + partner * d128_ref[...]

    def rope64(xh, c, d):
        first = (lane & (IDX_DIM - 1)) < IDX_ROT_DIM // 2
        partner = jnp.where(first, pltpu.roll(xh, LANES - IDX_ROT_DIM // 2, 1),
                            pltpu.roll(xh, IDX_ROT_DIM // 2, 1))
        return xh * c + partner * d

    @pl.when(j < J_K)
    def _():
        acc = mm()
        for h in range(TN // LANES):
            sl = slice(h * LANES, (h + 1) * LANES)
            q_ref[:, sl] = (rope128(acc[:, sl]) * Q_PRESCALE).astype(BF16)

    @pl.when(j == J_K)
    def _():
        acc = mm()
        for h in range(TN // LANES):
            sl = slice(h * LANES, (h + 1) * LANES)
            r = rope128(acc[:, sl])
            k_ref[:, sl] = r
            kb_ref[:, sl] = r.astype(BF16)

    @pl.when(j == J_V)
    def _():
        acc = mm()
        v_ref[...] = acc
        vb_ref[...] = acc.astype(BF16)

    @pl.when(jnp.logical_and(j >= J_QI, j < J_KIWI))
    def _():
        acc = mm()
        for h in range(TN // LANES):
            sl = slice(h * LANES, (h + 1) * LANES)
            qi_ref[:, sl] = rope64(acc[:, sl], c64_ref[...], d64_ref[...]).astype(BF16)

    @pl.when(j == J_KIWI)
    def _():
        is_ki = lane < IDX_DIM
        c = jnp.where(is_ki, c64_ref[...], 1.0)
        d = jnp.where(is_ki, d64_ref[...], 0.0)
        acc = jnp.dot(xn_s[...], w_ref[:, 0:LANES], preferred_element_type=F32)
        r = rope64(acc, c, d)
        kiwi_ref[...] = r
        ka = jnp.where(is_ki, r, 0.0)
        kia_ref[...] = ka.astype(BF16)
        kib_ref[...] = pltpu.roll(ka, IDX_DIM, 1).astype(BF16)

    jc = jnp.maximum(j - J_CONV, 0)
    in_conv = jnp.logical_and(j >= J_CONV, j < J_GA)
    cblk = jnp.minimum(jc // 3, N_CBLK - 1)
    part = jc % 3

    @pl.when(jnp.logical_and(in_conv, part == 0))
    def _():
        cb_s[...] = mm()

    @pl.when(jnp.logical_and(in_conv, part == 1))
    def _():
        cc_s[...] = mm()

    @pl.when(jnp.logical_and(in_conv, part == 2))
    def _():
        nseg = TM // seg
        if carried:
            @pl.when(i == 0)
            def _():
                carry_s[cblk] = jnp.zeros((8, TN), F32)
                carry_s[cblk, 0:2, :] = st_ref[0]
            prevs = [carry_s[cblk, 0:2, :]]
        else:
            prevs = [st_ref[s] for s in range(nseg)]
        u = cc_s[...] * mm()
        y, news = _conv_tile(u, cw_ref[...], prevs, seg)
        yc_ref[...] = (cb_s[...] * y).astype(BF16)
        for s in range(nseg):
            cn_ref[s] = news[s]
        if carried:
            carry_s[cblk, 0:2, :] = news[0]

    @pl.when(jnp.logical_and(j >= J_GA, j < J_GC))
    def _():
        ga_ref[...] = 1.0 / (1.0 + jnp.exp(-mm()))

    @pl.when(j >= J_GC)
    def _():
        gc_ref[...] = 1.0 / (1.0 + jnp.exp(-mm()))


def _proj(x, g, w, tabs, conv_w, state, seg):
    t = x.shape[0]
    ni = t // TM
    carried = seg == t
    nseg_tile = 1 if carried else TM // seg
    c128, d128, c64, d64 = tabs

    def row(i, j):
        return (i, 0)

    def const(i, j):
        return (0, 0)

    def cblk(j):
        return jnp.minimum(jnp.maximum(j - J_CONV, 0) // 3, N_CBLK - 1)

    def st_map(i, j):
        return (0 if carried else i, 0, cblk(j))

    in_specs = [
        pl.BlockSpec((TM, D_MODEL), row),
        pl.BlockSpec((1, D_MODEL), const),
        pl.BlockSpec((D_MODEL, TN), lambda i, j: (0, j)),
        pl.BlockSpec((TM, LANES), row),
        pl.BlockSpec((TM, LANES), row),
        pl.BlockSpec((TM, LANES), row),
        pl.BlockSpec((TM, LANES), row),
        pl.BlockSpec((3, TN), lambda i, j: (0, cblk(j))),
        pl.BlockSpec((nseg_tile, 2, TN), st_map),
    ]
    out_shape = [
        jax.ShapeDtypeStruct((t, Q_W), BF16),
        jax.ShapeDtypeStruct((t, KV_W), F32),
        jax.ShapeDtypeStruct((t, KV_W), BF16),
        jax.ShapeDtypeStruct((t, KV_W), F32),
        jax.ShapeDtypeStruct((t, KV_W), BF16),
        jax.ShapeDtypeStruct((t, IQ_W), BF16),
        jax.ShapeDtypeStruct((t, LANES), F32),
        jax.ShapeDtypeStruct((t, LANES), BF16),
        jax.ShapeDtypeStruct((t, LANES), BF16),
        jax.ShapeDtypeStruct((t, D_MODEL), BF16),
        jax.ShapeDtypeStruct((ni * nseg_tile, 2, D_MODEL), F32),
        jax.ShapeDtypeStruct((t, D_MODEL), F32),
        jax.ShapeDtypeStruct((t, D_MODEL), F32),
    ]
    out_specs = [
        pl.BlockSpec((TM, TN), lambda i, j: (i, jnp.clip(j - J_Q, 0, J_K - J_Q - 1))),
        pl.BlockSpec((TM, TN), row),
        pl.BlockSpec((TM, TN), row),
        pl.BlockSpec((TM, TN), row),
        pl.BlockSpec((TM, TN), row),
        pl.BlockSpec((TM, TN), lambda i, j: (i, jnp.clip(j - J_QI, 0, J_KIWI - J_QI - 1))),
        pl.BlockSpec((TM, LANES), row),
        pl.BlockSpec((TM, LANES), row),
        pl.BlockSpec((TM, LANES), row),
        pl.BlockSpec((TM, TN), lambda i, j: (i, cblk(j))),
        pl.BlockSpec((nseg_tile, 2, TN), lambda i, j: (i, 0, cblk(j))),
        pl.BlockSpec((TM, TN), lambda i, j: (i, jnp.clip(j - J_GA, 0, N_CBLK - 1))),
        pl.BlockSpec((TM, TN), lambda i, j: (i, jnp.clip(j - J_GC, 0, N_CBLK - 1))),
    ]
    return pl.pallas_call(
        functools.partial(_proj_kernel, seg=min(seg, TM), carried=carried),
        grid=(ni, NJ),
        in_specs=in_specs,
        out_specs=out_specs,
        out_shape=out_shape,
        scratch_shapes=[
            pltpu.VMEM((TM, D_MODEL), BF16),
            pltpu.VMEM((TM, TN), F32),
            pltpu.VMEM((TM, TN), F32),
            pltpu.VMEM((N_CBLK, 8, TN), F32),
        ],
        compiler_params=pltpu.CompilerParams(
            dimension_semantics=("arbitrary", "arbitrary"), vmem_limit_bytes=VMEM_LIMIT),
        name="proj",
    )(x, g, w, c128, d128, c64, d64, conv_w, state)


def _select_kernel(qi_ref, kiwi_ref, kia_ref, kib_ref, bias_ref, key_s, wb_s, lg_s, lh_s, *,
                   tq, spb, nkb_total, topk, causal, n_valid, q_pos0):
    n = pl.program_id(0)
    npair = N_IDX_HEADS // 2
    blk = spb * LANES
    kiwi = kiwi_ref[...]
    wscale = (IDX_DIM ** -0.5) * (N_IDX_HEADS ** -0.5)
    for h in range(N_IDX_HEADS):
        wb_s[h] = jnp.broadcast_to(kiwi[:, IDX_DIM + h:IDX_DIM + h + 1], (tq, LANES)) * wscale
    q2 = jnp.concatenate([qi_ref[:, p * LANES:(p + 1) * LANES] for p in range(npair)], axis=0)

    if causal:
        qpos0 = n * tq
        nkb = (qpos0 + tq + blk - 1) // blk
    else:
        qpos0 = q_pos0
        nkb = nkb_total
    lane = lax.broadcasted_iota(jnp.int32, (tq, LANES), 1)
    qchunk = (qpos0 + lax.broadcasted_iota(jnp.int32, (tq, LANES), 0)) >> 6
    nt = (((1,), (1,)), ((), ()))

    def logits_into(lg_ref, kb):
        base = pl.multiple_of(jnp.minimum(kb, nkb_total - 1) * blk, blk)
        lg_ref[0] = lax.dot_general(q2, kia_ref[pl.ds(base, blk), :], nt, preferred_element_type=F32)
        lg_ref[1] = lax.dot_general(q2, kib_ref[pl.ds(base, blk), :], nt, preferred_element_type=F32)

    def head_sum(lg_ref, kb, carry):
        m1, m2 = carry
        base = kb * blk
        for c in range(spb):
            cs = slice(c * LANES, (c + 1) * LANES)
            acc = jnp.zeros((tq, LANES), F32)
            for p in range(npair):
                rs = slice(p * tq, (p + 1) * tq)
                acc = acc + jnp.maximum(lg_ref[0, rs, cs], 0.0) * wb_s[2 * p]
                acc = acc + jnp.maximum(lg_ref[1, rs, cs], 0.0) * wb_s[2 * p + 1]
            col = base + c * LANES + lane
            adm = jnp.logical_and((col >> 6) <= qchunk, col < n_valid)
            sc = jnp.where(adm, acc, NEG_INF)
            m2 = jnp.maximum(m2, jnp.minimum(m1, sc))
            m1 = jnp.maximum(m1, sc)
            key_s[kb * spb + c] = _sortable(sc)
        return m1, m2

    logits_into(lg_s, 0)

    def score_pair(i, carry):
        logits_into(lh_s, 2 * i + 1)
        carry = head_sum(lg_s, 2 * i, carry)
        logits_into(lg_s, 2 * i + 2)
        return head_sum(lh_s, 2 * i + 1, carry)

    neg = jnp.full((tq, LANES), NEG_INF, F32)
    m1, m2 = lax.fori_loop(0, (nkb + 1) // 2, score_pair, (neg, neg))

    def count_ge(cand):
        def body(kb, acc):
            for c in range(spb):
                acc = acc + jnp.where(key_s[kb * spb + c] >= cand, 1.0, 0.0)
            return acc
        acc = lax.fori_loop(0, nkb, body, jnp.zeros((tq, LANES), F32))
        return jnp.sum(acc, axis=1, keepdims=True)

    ones = jnp.ones((tq, LANES), jnp.int32)
    lo0 = _sortable(jnp.min(m2, axis=1, keepdims=True)) * ones
    hi0 = _sortable(jnp.max(m1, axis=1, keepdims=True)) * ones + 1
    kf = float(topk)

    def unresolved(lo_k, hi_k, c_lo):
        open_ = jnp.logical_and(c_lo != kf, (hi_k - lo_k) != 1)
        return jnp.max(jnp.where(open_, 1.0, 0.0))

    def bis_cond(st):
        return jnp.logical_and(st[0] < 33, st[-1] > 0.0)

    def bis_body(st):
        it, lo_k, hi_k, c_lo, _ = st
        mid = lo_k + lax.shift_right_logical(hi_k - lo_k, 1)
        cnt = count_ge(mid)
        ge = cnt >= kf
        lo_k = jnp.where(ge, mid, lo_k)
        hi_k = jnp.where(ge, hi_k, mid)
        c_lo = jnp.where(ge, cnt, c_lo)
        return it + 1, lo_k, hi_k, c_lo, unresolved(lo_k, hi_k, c_lo)

    c0 = jnp.full((tq, LANES), -1.0, F32)
    _, thr, _, c_thr, _ = lax.while_loop(
        bis_cond, bis_body, (jnp.int32(0), lo0, hi0, c0, unresolved(lo0, hi0, c0)))

    tied = jnp.max(jnp.where(jnp.logical_and(c_thr != kf, thr > _VALID_KEY), 1.0, 0.0)) > 0.0

    @pl.when(jnp.logical_not(tied))
    def _():
        def emit(kb, carry):
            for c in range(spb):
                k = key_s[kb * spb + c]
                sel = jnp.logical_and(k >= thr, k > _VALID_KEY)
                bias_ref[kb * spb + c] = jnp.where(sel, 0.0, NEG_INF).astype(BF16)
            return carry

        lax.fori_loop(0, nkb, emit, 0)

    @pl.when(tied)
    def _():
        need = kf - count_ge(thr + 1)
        tri = (lax.broadcasted_iota(jnp.int32, (LANES, LANES), 0)
               <= lax.broadcasted_iota(jnp.int32, (LANES, LANES), 1)).astype(BF16)

        def emit(kb, seen):
            for c in range(spb):
                k = key_s[kb * spb + c]
                eq = jnp.where(k == thr, 1.0, 0.0)
                rank = seen + jnp.dot(eq.astype(BF16), tri, preferred_element_type=F32)
                keep = jnp.logical_or(k > thr, jnp.logical_and(k == thr, rank <= need))
                sel = jnp.logical_and(keep, k > _VALID_KEY)
                bias_ref[kb * spb + c] = jnp.where(sel, 0.0, NEG_INF).astype(BF16)
                seen = seen + jnp.sum(eq, axis=1, keepdims=True)
            return seen

        lax.fori_loop(0, nkb, emit, jnp.zeros((tq, LANES), F32))

    def fill(kb, carry):
        for c in range(spb):
            bias_ref[kb * spb + c] = jnp.full((tq, LANES), NEG_INF, BF16)
        return carry

    lax.fori_loop(nkb, nkb_total, fill, 0)


def _select(qi, kiwi, kia, kib, *, tq, spb, topk, causal, n_valid, q_pos0):
    t = qi.shape[0]
    nb = t // tq
    lk = kia.shape[1]
    nslab = lk // LANES
    nkb_total = nslab // spb
    assert topk <= 2 * LANES and nslab >= 2, "the bisection's starting lower bound needs two keys per lane"

    def kmap(n):
        return (0 if causal else n, 0, 0)

    return pl.pallas_call(
        functools.partial(_select_kernel, tq=tq, spb=spb, nkb_total=nkb_total, topk=topk,
                          causal=causal, n_valid=n_valid, q_pos0=q_pos0),
        grid=(nb,),
        in_specs=[
            pl.BlockSpec((tq, IQ_W), lambda n: (n, 0)),
            pl.BlockSpec((tq, LANES), lambda n: (n, 0)),
            pl.BlockSpec((None, lk, LANES), kmap),
            pl.BlockSpec((None, lk, LANES), kmap),
        ],
        out_specs=pl.BlockSpec((nslab, tq, LANES), lambda n: (0, n, 0)),
        out_shape=jax.ShapeDtypeStruct((nslab, t, LANES), BF16),
        scratch_shapes=[
            pltpu.VMEM((nslab + spb, tq, LANES), jnp.int32),
            pltpu.VMEM((N_IDX_HEADS, tq, LANES), F32),
            pltpu.VMEM((2, N_IDX_HEADS // 2 * tq, spb * LANES), F32),
            pltpu.VMEM((2, N_IDX_HEADS // 2 * tq, spb * LANES), F32),
        ],
        compiler_params=pltpu.CompilerParams(
            dimension_semantics=("arbitrary",), vmem_limit_bytes=VMEM_LIMIT),
        name="select",
    )(qi, kiwi, kia, kib)


def _attend_kernel(qb_ref, kb_ref, kbat_ref, last_ref, q_ref, k_ref, v_ref, b_ref, o_ref,
                   m_s, l_s, acc_s, *, tq, spb):
    s = pl.program_id(0)

    @pl.when(kb_ref[s] == 0)
    def _():
        m_s[...] = jnp.full(m_s.shape, 0.1 * NEG_INF, F32)
        l_s[...] = jnp.zeros(l_s.shape, F32)
        acc_s[...] = jnp.zeros(acc_s.shape, F32)

    biases = [b_ref[c].astype(F32)[None] for c in range(spb)]
    ones = jnp.ones((spb * LANES, LANES), BF16)

    for g in range(N_KV_HEADS):
        qg = [q_ref[:, (g * GROUP + h) * HEAD_DIM:(g * GROUP + h + 1) * HEAD_DIM] for h in range(GROUP)]
        q4 = jnp.concatenate(qg, axis=0)
        kg = k_ref[:, g * HEAD_DIM:(g + 1) * HEAD_DIM]
        sc = lax.dot_general(q4, kg, (((1,), (1,)), ((), ())), preferred_element_type=F32)
        slabs = []
        mx = None
        for c in range(spb):
            sl = sc[:, c * LANES:(c + 1) * LANES].reshape(GROUP, tq, LANES) + biases[c]
            sl = sl.reshape(GROUP * tq, LANES)
            slabs.append(sl)
            mx = sl if mx is None else jnp.maximum(mx, sl)
        m_prev = m_s[g]
        m_new = jnp.maximum(m_prev, jnp.max(mx, axis=1, keepdims=True))
        alpha = jnp.exp2(m_prev - m_new)
        pmat = jnp.concatenate([jnp.exp2(sl - m_new).astype(BF16) for sl in slabs], axis=1)
        v1 = jnp.concatenate([v_ref[:, g * HEAD_DIM:(g + 1) * HEAD_DIM], ones], axis=1)
        pv = jnp.dot(pmat, v1, preferred_element_type=F32)
        acc_s[g] = alpha * acc_s[g] + pv[:, 0:HEAD_DIM]
        l_s[g] = alpha * l_s[g] + pv[:, HEAD_DIM:HEAD_DIM + LANES]
        m_s[g] = m_new

    @pl.when(last_ref[s] == 1)
    def _():
        for g in range(N_KV_HEADS):
            o = acc_s[g] / l_s[g]
            for h in range(GROUP):
                col = (g * GROUP + h) * HEAD_DIM
                o_ref[:, col:col + HEAD_DIM] = o[h * tq:(h + 1) * tq].astype(BF16)


def _attend(q, k_all, v_all, bias, sched, *, tq, spb):
    t = q.shape[0]
    qb, kb, kbat, last = sched
    nsteps = qb.shape[0]
    blk = spb * LANES
    grid_spec = pltpu.PrefetchScalarGridSpec(
        num_scalar_prefetch=4,
        grid=(nsteps,),
        in_specs=[
            pl.BlockSpec((tq, Q_W), lambda s, qb, kb, kbat, last: (qb[s], 0)),
            pl.BlockSpec((None, blk, KV_W), lambda s, qb, kb, kbat, last: (kbat[s], kb[s], 0)),
            pl.BlockSpec((None, blk, KV_W), lambda s, qb, kb, kbat, last: (kbat[s], kb[s], 0)),
            pl.BlockSpec((spb, tq, LANES), lambda s, qb, kb, kbat, last: (kb[s], qb[s], 0)),
        ],
        out_specs=pl.BlockSpec((tq, Q_W), lambda s, qb, kb, kbat, last: (qb[s], 0)),
        scratch_shapes=[
            pltpu.VMEM((N_KV_HEADS, GROUP * tq, LANES), F32),
            pltpu.VMEM((N_KV_HEADS, GROUP * tq, LANES), F32),
            pltpu.VMEM((N_KV_HEADS, GROUP * tq, HEAD_DIM), F32),
        ],
    )
    return pl.pallas_call(
        functools.partial(_attend_kernel, tq=tq, spb=spb),
        grid_spec=grid_spec,
        out_shape=jax.ShapeDtypeStruct((t, Q_W), BF16),
        compiler_params=pltpu.CompilerParams(
            dimension_semantics=("arbitrary",), vmem_limit_bytes=VMEM_LIMIT),
        name="attend",
    )(qb, kb, kbat, last, q, k_all, v_all, bias)


def _causal_schedule(t, tq, blk):
    qb, kb, last = [], [], []
    for n in range(t // tq):
        nk = ((n + 1) * tq + blk - 1) // blk
        for k in range(nk):
            qb.append(n)
            kb.append(k)
            last.append(1 if k == nk - 1 else 0)
    z = np.zeros(len(qb), np.int32)
    return (jnp.asarray(qb, jnp.int32), jnp.asarray(kb, jnp.int32), jnp.asarray(z), jnp.asarray(last, jnp.int32))


def _batched_schedule(nbatch, nk):
    qb = np.repeat(np.arange(nbatch, dtype=np.int32), nk)
    kb = np.tile(np.arange(nk, dtype=np.int32), nbatch)
    last = (kb == nk - 1).astype(np.int32)
    return (jnp.asarray(qb), jnp.asarray(kb), jnp.asarray(qb), jnp.asarray(last))


def _merge_kernel(o_ref, yc_ref, ga_ref, gc_ref, woa_ref, woc_ref, wout_ref, x_ref, gq_ref, gp_ref,
                  x1_ref, xn2_ref, mg_s, m_s):
    j = pl.program_id(1)
    nblk = D_MODEL // TN

    @pl.when(j < nblk)
    def _():
        a = jnp.dot(o_ref[...], woa_ref[...], preferred_element_type=F32)
        c = jnp.dot(yc_ref[...], woc_ref[...], preferred_element_type=F32)
        mg_s[j] = (ga_ref[...] * a + gc_ref[...] * c).astype(BF16)

    @pl.when(j >= nblk)
    def _():
        mg = jnp.concatenate([mg_s[b] for b in range(nblk)], axis=1)
        m_s[j - nblk] = jnp.dot(mg, wout_ref[...], preferred_element_type=F32)

    @pl.when(j == 2 * nblk - 1)
    def _():
        m = jnp.concatenate([m_s[b] for b in range(nblk)], axis=1)
        x1 = x_ref[...] + _rms(m, gq_ref[...])
        x1_ref[...] = x1
        xn2_ref[...] = _rms(x1, gp_ref[...]).astype(BF16)


def _merge(o, yc, ga, gc, woa, woc, wout, x, gq, gp):
    t = x.shape[0]
    nblk = D_MODEL // TN

    def row(i, j):
        return (i, 0)

    def lo(i, j):
        return (i, jnp.minimum(j, nblk - 1))

    return pl.pallas_call(
        _merge_kernel,
        grid=(t // TM, 2 * nblk),
        in_specs=[
            pl.BlockSpec((TM, Q_W), row),
            pl.BlockSpec((TM, D_MODEL), row),
            pl.BlockSpec((TM, TN), lo),
            pl.BlockSpec((TM, TN), lo),
            pl.BlockSpec((Q_W, TN), lambda i, j: (0, jnp.minimum(j, nblk - 1))),
            pl.BlockSpec((D_MODEL, TN), lambda i, j: (0, jnp.minimum(j, nblk - 1))),
            pl.BlockSpec((D_MODEL, TN), lambda i, j: (0, jnp.maximum(j - nblk, 0))),
            pl.BlockSpec((TM, D_MODEL), row),
            pl.BlockSpec((1, D_MODEL), lambda i, j: (0, 0)),
            pl.BlockSpec((1, D_MODEL), lambda i, j: (0, 0)),
        ],
        out_specs=[pl.BlockSpec((TM, D_MODEL), row), pl.BlockSpec((TM, D_MODEL), row)],
        out_shape=[jax.ShapeDtypeStruct((t, D_MODEL), F32), jax.ShapeDtypeStruct((t, D_MODEL), BF16)],
        scratch_shapes=[pltpu.VMEM((nblk, TM, TN), BF16), pltpu.VMEM((nblk, TM, TN), F32)],
        compiler_params=pltpu.CompilerParams(
            dimension_semantics=("arbitrary", "arbitrary"), vmem_limit_bytes=VMEM_LIMIT),
        name="merge",
    )(o, yc, ga, gc, woa, woc, wout, x, gq, gp)


def _ffn_kernel(xn_ref, wg_ref, wv_ref, cwg_ref, cwv_ref, wd_ref, x1_ref, gq_ref, stg_ref, stv_ref,
                y_ref, ng_ref, nv_ref, acc_s, cg_s, cv_s, *, seg, carried):
    i = pl.program_id(0)
    jf = pl.program_id(1)
    nseg = TM // seg

    @pl.when(jf == 0)
    def _():
        acc_s[...] = jnp.zeros(acc_s.shape, F32)

    if carried:
        @pl.when(i == 0)
        def _():
            for carry_s, st_ref in ((cg_s, stg_ref), (cv_s, stv_ref)):
                carry_s[jf] = jnp.zeros((8, TF), F32)
                carry_s[jf, 0:2, :] = st_ref[0]

    xn = xn_ref[...]

    def branch(w_ref, cw_ref, st_ref, carry_s, new_ref):
        up = jnp.dot(xn, w_ref[...], preferred_element_type=F32)
        if carried:
            prevs = [carry_s[jf, 0:2, :]]
        else:
            prevs = [st_ref[s] for s in range(nseg)]
        y, news = _conv_tile(up, cw_ref[...], prevs, seg)
        for s in range(nseg):
            new_ref[s] = news[s]
        if carried:
            carry_s[jf, 0:2, :] = news[0]
        return y

    gate = branch(wg_ref, cwg_ref, stg_ref, cg_s, ng_ref)
    val = branch(wv_ref, cwv_ref, stv_ref, cv_s, nv_ref)
    c0 = np.float32(np.sqrt(2.0 / np.pi))
    gelu = 0.5 * gate * (1.0 + jnp.tanh(c0 * (gate + 0.044715 * (gate * gate * gate))))
    hid = (gelu * val).astype(BF16)
    acc_s[...] += jnp.dot(hid, wd_ref[...], preferred_element_type=F32)

    @pl.when(jf == pl.num_programs(1) - 1)
    def _():
        y_ref[...] = x1_ref[...] + _rms(acc_s[...], gq_ref[...])


def _ffn(xn2, w_up, cw, w_down, x1, gq, state, seg):
    t = x1.shape[0]
    nf = D_FF // TF
    carried = seg == t
    nseg_tile = 1 if carried else TM // seg

    def row(i, j):
        return (i, 0)

    def stg(i, j):
        return (0 if carried else i, 0, j)

    def stv(i, j):
        return (0 if carried else i, 0, j + nf)

    return pl.pallas_call(
        functools.partial(_ffn_kernel, seg=min(seg, TM), carried=carried),
        grid=(t // TM, nf),
        in_specs=[
            pl.BlockSpec((TM, D_MODEL), row),
            pl.BlockSpec((D_MODEL, TF), lambda i, j: (0, j)),
            pl.BlockSpec((D_MODEL, TF), lambda i, j: (0, j + nf)),
            pl.BlockSpec((3, TF), lambda i, j: (0, j)),
            pl.BlockSpec((3, TF), lambda i, j: (0, j + nf)),
            pl.BlockSpec((TF, D_MODEL), lambda i, j: (j, 0)),
            pl.BlockSpec((TM, D_MODEL), row),
            pl.BlockSpec((1, D_MODEL), lambda i, j: (0, 0)),
            pl.BlockSpec((nseg_tile, 2, TF), stg),
            pl.BlockSpec((nseg_tile, 2, TF), stv),
        ],
        out_specs=[
            pl.BlockSpec((TM, D_MODEL), row),
            pl.BlockSpec((nseg_tile, 2, TF), lambda i, j: (i, 0, j)),
            pl.BlockSpec((nseg_tile, 2, TF), lambda i, j: (i, 0, j)),
        ],
        out_shape=[
            jax.ShapeDtypeStruct((t, D_MODEL), F32),
            jax.ShapeDtypeStruct((t // TM * nseg_tile, 2, D_FF), F32),
            jax.ShapeDtypeStruct((t // TM * nseg_tile, 2, D_FF), F32),
        ],
        scratch_shapes=[
            pltpu.VMEM((TM, D_MODEL), F32),
            pltpu.VMEM((nf, 8, TF), F32),
            pltpu.VMEM((nf, 8, TF), F32),
        ],
        compiler_params=pltpu.CompilerParams(
            dimension_semantics=("arbitrary", "arbitrary"), vmem_limit_bytes=VMEM_LIMIT),
        name="ffn",
    )(xn2, w_up, w_up, cw, cw, w_down, x1, gq, state, state)


PACK_ROWS = 1024


def _pack_kernel(ck_ref, cv_ref, kt_ref, vt_ref, ko_ref, vo_ref, *, rows):
    del kt_ref, vt_ref
    for src, dst in ((ck_ref, ko_ref), (cv_ref, vo_ref)):
        for g in range(N_KV_HEADS):
            dst[:, g * HEAD_DIM:(g + 1) * HEAD_DIM] = src[pl.ds(g, rows, stride=N_KV_HEADS), :].astype(BF16)


def _pack_cache(cache_k, cache_v, k_tail, v_tail):
    nb, plen = cache_k.shape[0], cache_k.shape[1]
    rows = int(np.gcd(plen, PACK_ROWS))
    assert rows % 16 == 0
    cspec = pl.BlockSpec((None, rows * N_KV_HEADS, HEAD_DIM), lambda b, r: (b, r, 0))
    ospec = pl.BlockSpec((None, rows, KV_W), lambda b, r: (b, r, 0))
    anyspec = pl.BlockSpec(memory_space=pl.ANY)
    flat = (nb, plen * N_KV_HEADS, HEAD_DIM)
    return pl.pallas_call(
        functools.partial(_pack_kernel, rows=rows),
        grid=(nb, plen // rows),
        in_specs=[cspec, cspec, anyspec, anyspec],
        out_specs=[ospec, ospec],
        out_shape=[jax.ShapeDtypeStruct(k_tail.shape, BF16), jax.ShapeDtypeStruct(v_tail.shape, BF16)],
        input_output_aliases={2: 0, 3: 1},
        compiler_params=pltpu.CompilerParams(
            dimension_semantics=("arbitrary", "arbitrary"), vmem_limit_bytes=VMEM_LIMIT),
        name="pack_cache",
    )(cache_k.reshape(flat), cache_v.reshape(flat), k_tail, v_tail)


def _rope_tables(pos, rot, width):
    half = rot // 2
    freqs = ROPE_THETA ** (-jnp.arange(half, dtype=F32) / half)
    ang = pos.astype(F32)[:, None] * freqs[None, :]
    cos, sin = jnp.cos(ang), jnp.sin(ang)
    t = pos.shape[0]
    c = jnp.concatenate([cos, cos, jnp.ones((t, width - rot), F32)], axis=1)
    d = jnp.concatenate([-sin, sin, jnp.zeros((t, width - rot), F32)], axis=1)
    reps = LANES // width
    return jnp.tile(c, (1, reps)), jnp.tile(d, (1, reps))


def _relayout_w_in(w):
    o = np.cumsum([0, Q_W, KV_W, KV_W, IQ_W, IDX_DIM, N_IDX_HEADS, D_MODEL, D_MODEL, D_MODEL, D_MODEL, D_MODEL])
    q, k, v, qi = (w[:, o[a]:o[a + 1]] for a in range(4))
    kiwi = w[:, o[4]:o[6]]
    cb, cc, ch, ga, gc = (w[:, o[a]:o[a + 1]] for a in range(6, 11))
    parts = [q, k, v, qi, kiwi, jnp.zeros((D_MODEL, TN - kiwi.shape[1]), w.dtype)]
    for c in range(N_CBLK):
        sl = slice(c * TN, (c + 1) * TN)
        parts += [cb[:, sl], cc[:, sl], ch[:, sl]]
    parts += [ga, gc]
    return jnp.concatenate(parts, axis=1).astype(BF16)


def _stream(x, pos, seg, past, weights, *, tq, spb_sel, spb_att):
    (g_mp, g_mq, w_in_r, conv_w, woa, woc, wout, g_fp, g_fq, w_up, fconv_w, w_down) = weights
    t = x.shape[0]
    nseq = t // seg
    tabs = _rope_tables(pos, ROT_DIM, HEAD_DIM) + _rope_tables(pos, IDX_ROT_DIM, IDX_DIM)
    if past is None:
        conv_state = jnp.zeros((nseq, 2, D_MODEL), F32)
        ffn_state = jnp.zeros((nseq, 2, 2 * D_FF), F32)
    else:
        conv_state, ffn_state = past[3], past[4]

    (q, k, kb, v, vb, qi, kiwi, kia, kib, yc, conv_new, ga, gc) = _proj(
        x, g_mp, w_in_r, tabs, conv_w, conv_state, seg)

    if past is None:
        k_all, v_all = kb[None], vb[None]
        kia_all, kib_all = kia[None], kib[None]
        n_keys = t
        sched = _causal_schedule(t, tq, spb_att * LANES)
        q_pos0 = 0
    else:
        cache_k, cache_v, cache_ki = past[0], past[1], past[2]
        plen = cache_k.shape[1]
        n_keys = plen + seg
        lk = -(-n_keys // (spb_att * LANES)) * (spb_att * LANES)
        pad = lk - n_keys

        def cat(c, new, width):
            parts = [c, new.reshape(nseq, seg, width)]
            if pad:
                parts.append(jnp.zeros((nseq, pad, width), BF16))
            return jnp.concatenate(parts, axis=1)

        def tail(new):
            return jnp.pad(new.reshape(nseq, seg, KV_W), ((0, 0), (plen, pad), (0, 0)))

        k_all, v_all = _pack_cache(cache_k, cache_v, tail(kb), tail(vb))
        cki = cache_ki.astype(BF16)
        zk = jnp.zeros_like(cki)
        kia_all = cat(jnp.concatenate([cki, zk], axis=-1), kia, LANES)
        kib_all = cat(jnp.concatenate([zk, cki], axis=-1), kib, LANES)
        sched = _batched_schedule(nseq, lk // (spb_att * LANES))
        q_pos0 = plen
    topk = min(TOPK_MAX, n_keys // 4)

    bias = _select(qi, kiwi, kia_all, kib_all, tq=tq, spb=spb_sel, topk=topk,
                   causal=past is None, n_valid=n_keys, q_pos0=q_pos0)
    o = _attend(q, k_all, v_all, bias, sched, tq=tq, spb=spb_att)
    x1, xn2 = _merge(o, yc, ga, gc, woa, woc, wout, x, g_mq, g_fp)
    y, ffn_g, ffn_v = _ffn(xn2, w_up, fconv_w, w_down, x1, g_fq, ffn_state, seg)
    ffn_new = jnp.concatenate([ffn_g, ffn_v], axis=-1)
    return y, k, v, kiwi[:, :IDX_DIM], conv_new[-nseq:], ffn_new[-nseq:]


def kernel(x_prompt, x_sample, cache_k, cache_v, cache_k_idx, state_conv, state_ffn_conv, norm_mix_pre, norm_mix_post, w_in, conv_w, w_o_attn, w_o_conv, w_out, norm_ffn_pre, norm_ffn_post, w_ffn_up, ffn_conv_w, w_ffn_down):
    depth = w_in.shape[0]
    assert depth == 1, "single-layer step"
    b, seq, _ = x_prompt.shape
    assert b == 1
    db, dseq, _ = x_sample.shape
    plen = cache_k.shape[2]
    assert dseq == CHUNK and plen % CHUNK == 0

    weights = (
        norm_mix_pre, norm_mix_post, _relayout_w_in(w_in[0]), conv_w[0],
        w_o_attn[0].astype(BF16), w_o_conv[0].astype(BF16), w_out[0].astype(BF16),
        norm_ffn_pre, norm_ffn_post, w_ffn_up[0].astype(BF16), ffn_conv_w[0], w_ffn_down[0].astype(BF16),
    )

    pos_p = jnp.arange(seq, dtype=jnp.int32)
    yp, kp, vp, kip, convp, ffnp = _stream(
        x_prompt.reshape(seq, D_MODEL), pos_p, seq, None, weights, tq=128, spb_sel=4, spb_att=8)

    pos_s = jnp.tile(jnp.arange(dseq, dtype=jnp.int32) + plen, db)
    past = (cache_k[0], cache_v[0], cache_k_idx[0], state_conv[0], state_ffn_conv[0])
    n_keys = plen + dseq
    spb_s = _sample_slabs(n_keys)
    ys, ks, vs, kis, convs, ffns = _stream(
        x_sample.reshape(db * dseq, D_MODEL), pos_s, dseq, past, weights, tq=dseq, spb_sel=spb_s, spb_att=spb_s)

    return (
        yp.reshape(1, seq, D_MODEL), ys.reshape(db, dseq, D_MODEL),
        kp.reshape(1, 1, seq, N_KV_HEADS, HEAD_DIM), vp.reshape(1, 1, seq, N_KV_HEADS, HEAD_DIM),
        kip.reshape(1, 1, seq, IDX_DIM), convp.reshape(1, 1, 2, D_MODEL), ffnp.reshape(1, 1, 2, 2 * D_FF),
        ks.reshape(1, db, dseq, N_KV_HEADS, HEAD_DIM), vs.reshape(1, db, dseq, N_KV_HEADS, HEAD_DIM),
        kis.reshape(1, db, dseq, IDX_DIM), convs.reshape(1, db, 2, D_MODEL), ffns.reshape(1, db, 2, 2 * D_FF),
    )


def _sample_slabs(n_keys):
    nslab = -(-n_keys // LANES)
    best = 1
    for d in range(1, nslab + 1):
        if nslab % d == 0 and d <= 11:
            best = d
    return best
```

```python
import functools

import jax
import jax.numpy as jnp
import numpy as np
from jax import lax
from jax.experimental import pallas as pl
from jax.experimental.pallas import tpu as pltpu

F32 = jnp.float32
BF16 = jnp.bfloat16

D_MODEL = 2048
N_HEADS = 16
N_KV_HEADS = 4
HEAD_DIM = 128
ROT_DIM = HEAD_DIM // 4
N_IDX_HEADS = 16
IDX_DIM = 64
IDX_ROT_DIM = IDX_DIM // 4
CHUNK = 64
TOPK_MAX = 256
ROPE_THETA = 500000.0
D_FF = 5632
RMS_EPS = 1e-6
NEG_INF = -1e30
Q_W = N_HEADS * HEAD_DIM
KV_W = N_KV_HEADS * HEAD_DIM
IQ_W = N_IDX_HEADS * IDX_DIM
GROUP = N_HEADS // N_KV_HEADS

LANES = 128
TM = 512
TN = 1024
MERGE_TN = 512
TF = 512
VMEM_LIMIT = 56 * 1024 * 1024

J_Q = 0
J_KV = J_Q + Q_W // TN
assert 2 * KV_W == TN
J_QI = J_KV + 1
J_KIWI = J_QI + IQ_W // TN
J_CONV = J_KIWI + 1
N_CBLK = D_MODEL // TN
J_GA = J_CONV + 3 * N_CBLK
J_GC = J_GA + N_CBLK
NJ = J_GC + N_CBLK

Q_PRESCALE = float(np.float32((HEAD_DIM ** -0.5) * np.log2(np.e)))

INT_MIN = -(2 ** 31)
_VALID_KEY = int(np.array(0.5 * NEG_INF, np.float32).view(np.int32)) ^ 0x7FFFFFFF


def _rms(x, g):
    return x * lax.rsqrt(jnp.mean(x * x, axis=-1, keepdims=True) + RMS_EPS) * g


def _sortable(x):
    bits = lax.bitcast_convert_type(x, jnp.int32)
    return jnp.where(bits < 0, bits ^ jnp.int32(0x7FFFFFFF), bits)


def _dwconv_seg(u, prev2, w3):
    row = lax.broadcasted_iota(jnp.int32, u.shape, 0)
    p0, p1 = prev2[0:1], prev2[1:2]
    s1 = jnp.where(row == 0, p1, pltpu.roll(u, 1, 0))
    s2 = jnp.where(row == 0, p0, jnp.where(row == 1, p1, pltpu.roll(u, 2, 0)))
    return w3[0:1] * s2 + w3[1:2] * s1 + w3[2:3] * u


def _conv_tile(u, w3, prevs, seg):
    nseg = len(prevs)
    ys, news = [], []
    for s in range(nseg):
        us = u[s * seg:(s + 1) * seg]
        ys.append(_dwconv_seg(us, prevs[s], w3))
        news.append(us[seg - 2:seg])
    y = ys[0] if nseg == 1 else jnp.concatenate(ys, axis=0)
    return y, news


def _proj_kernel(x_ref, g_ref, w_ref, c128_ref, d128_ref, c64_ref, d64_ref, cw_ref, st_ref,
                 q_ref, k_ref, kb_ref, v_ref, vb_ref, qi_ref, kiwi_ref, kia_ref, kib_ref,
                 yc_ref, cn_ref, ga_ref, gc_ref,
                 xn_s, cb_s, cc_s, carry_s, *, seg, carried):
    i = pl.program_id(0)
    j = pl.program_id(1)

    @pl.when(j == 0)
    def _():
        xn_s[...] = _rms(x_ref[...], g_ref[...]).astype(BF16)

    def mm():
        return jnp.dot(xn_s[...], w_ref[...], preferred_element_type=F32)

    lane = lax.broadcasted_iota(jnp.int32, (TM, LANES), 1)

    def rope128(xh):
        partner = jnp.where(lane < ROT_DIM // 2, pltpu.roll(xh, LANES - ROT_DIM // 2, 1),
                            pltpu.roll(xh, ROT_DIM // 2, 1))
        return xh * c128_ref[...] + partner * d128_ref[...]

    def rope64(xh, c, d):
        first = (lane & (IDX_DIM - 1)) < IDX_ROT_DIM // 2
        partner = jnp.where(first, pltpu.roll(xh, LANES - IDX_ROT_DIM // 2, 1),
                            pltpu.roll(xh, IDX_ROT_DIM // 2, 1))
        return xh * c + partner * d

    @pl.when(j < J_KV)
    def _():
        acc = mm()
        for h in range(TN // LANES):
            sl = slice(h * LANES, (h + 1) * LANES)
            q_ref[:, sl] = (rope128(acc[:, sl]) * Q_PRESCALE).astype(BF16)

    @pl.when(j == J_KV)
    def _():
        acc = mm()
        for h in range(KV_W // LANES):
            sl = slice(h * LANES, (h + 1) * LANES)
            r = rope128(acc[:, sl])
            k_ref[:, sl] = r
            kb_ref[:, sl] = r.astype(BF16)
        v = acc[:, KV_W:2 * KV_W]
        v_ref[...] = v
        vb_ref[...] = v.astype(BF16)

    @pl.when(jnp.logical_and(j >= J_QI, j < J_KIWI))
    def _():
        acc = mm()
        for h in range(TN // LANES):
            sl = slice(h * LANES, (h + 1) * LANES)
            qi_ref[:, sl] = rope64(acc[:, sl], c64_ref[...], d64_ref[...]).astype(BF16)

    @pl.when(j == J_KIWI)
    def _():
        is_ki = lane < IDX_DIM
        c = jnp.where(is_ki, c64_ref[...], 1.0)
        d = jnp.where(is_ki, d64_ref[...], 0.0)
        acc = jnp.dot(xn_s[...], w_ref[:, 0:LANES], preferred_element_type=F32)
        r = rope64(acc, c, d)
        kiwi_ref[...] = r
        ka = jnp.where(is_ki, r, 0.0)
        kia_ref[...] = ka.astype(BF16)
        kib_ref[...] = pltpu.roll(ka, IDX_DIM, 1).astype(BF16)

    jc = jnp.maximum(j - J_CONV, 0)
    in_conv = jnp.logical_and(j >= J_CONV, j < J_GA)
    cblk = jnp.minimum(jc // 3, N_CBLK - 1)
    part = jc % 3

    @pl.when(jnp.logical_and(in_conv, part == 0))
    def _():
        cb_s[...] = mm()

    @pl.when(jnp.logical_and(in_conv, part == 1))
    def _():
        cc_s[...] = mm()

    @pl.when(jnp.logical_and(in_conv, part == 2))
    def _():
        nseg = TM // seg
        if carried:
            @pl.when(i == 0)
            def _():
                carry_s[cblk] = jnp.zeros((8, TN), F32)
                carry_s[cblk, 0:2, :] = st_ref[0]
            prevs = [carry_s[cblk, 0:2, :]]
        else:
            prevs = [st_ref[s] for s in range(nseg)]
        u = cc_s[...] * mm()
        y, news = _conv_tile(u, cw_ref[...], prevs, seg)
        yc_ref[...] = (cb_s[...] * y).astype(BF16)
        for s in range(nseg):
            cn_ref[s] = news[s]
        if carried:
            carry_s[cblk, 0:2, :] = news[0]

    @pl.when(jnp.logical_and(j >= J_GA, j < J_GC))
    def _():
        ga_ref[...] = 1.0 / (1.0 + jnp.exp(-mm()))

    @pl.when(j >= J_GC)
    def _():
        gc_ref[...] = 1.0 / (1.0 + jnp.exp(-mm()))


def _proj(x, g, w, tabs, conv_w, state, seg):
    t = x.shape[0]
    ni = t // TM
    carried = seg == t
    nseg_tile = 1 if carried else TM // seg
    c128, d128, c64, d64 = tabs

    def row(i, j):
        return (i, 0)

    def const(i, j):
        return (0, 0)

    def cblk(j):
        return jnp.minimum(jnp.maximum(j - J_CONV, 0) // 3, N_CBLK - 1)

    def st_map(i, j):
        return (0 if carried else i, 0, cblk(j))

    in_specs = [
        pl.BlockSpec((TM, D_MODEL), row),
        pl.BlockSpec((1, D_MODEL), const),
        pl.BlockSpec((D_MODEL, TN), lambda i, j: (0, j)),
        pl.BlockSpec((TM, LANES), row),
        pl.BlockSpec((TM, LANES), row),
        pl.BlockSpec((TM, LANES), row),
        pl.BlockSpec((TM, LANES), row),
        pl.BlockSpec((3, TN), lambda i, j: (0, cblk(j))),
        pl.BlockSpec((nseg_tile, 2, TN), st_map),
    ]
    out_shape = [
        jax.ShapeDtypeStruct((t, Q_W), BF16),
        jax.ShapeDtypeStruct((t, KV_W), F32),
        jax.ShapeDtypeStruct((t, KV_W), BF16),
        jax.ShapeDtypeStruct((t, KV_W), F32),
        jax.ShapeDtypeStruct((t, KV_W), BF16),
        jax.ShapeDtypeStruct((t, IQ_W), BF16),
        jax.ShapeDtypeStruct((t, LANES), F32),
        jax.ShapeDtypeStruct((t, LANES), BF16),
        jax.ShapeDtypeStruct((t, LANES), BF16),
        jax.ShapeDtypeStruct((t, D_MODEL), BF16),
        jax.ShapeDtypeStruct((ni * nseg_tile, 2, D_MODEL), F32),
        jax.ShapeDtypeStruct((t, D_MODEL), F32),
        jax.ShapeDtypeStruct((t, D_MODEL), F32),
    ]
    out_specs = [
        pl.BlockSpec((TM, TN), lambda i, j: (i, jnp.clip(j - J_Q, 0, J_KV - J_Q - 1))),
        pl.BlockSpec((TM, KV_W), row),
        pl.BlockSpec((TM, KV_W), row),
        pl.BlockSpec((TM, KV_W), row),
        pl.BlockSpec((TM, KV_W), row),
        pl.BlockSpec((TM, TN), lambda i, j: (i, jnp.clip(j - J_QI, 0, J_KIWI - J_QI - 1))),
        pl.BlockSpec((TM, LANES), row),
        pl.BlockSpec((TM, LANES), row),
        pl.BlockSpec((TM, LANES), row),
        pl.BlockSpec((TM, TN), lambda i, j: (i, cblk(j))),
        pl.BlockSpec((nseg_tile, 2, TN), lambda i, j: (i, 0, cblk(j))),
        pl.BlockSpec((TM, TN), lambda i, j: (i, jnp.clip(j - J_GA, 0, N_CBLK - 1))),
        pl.BlockSpec((TM, TN), lambda i, j: (i, jnp.clip(j - J_GC, 0, N_CBLK - 1))),
    ]
    return pl.pallas_call(
        functools.partial(_proj_kernel, seg=min(seg, TM), carried=carried),
        grid=(ni, NJ),
        in_specs=in_specs,
        out_specs=out_specs,
        out_shape=out_shape,
        scratch_shapes=[
            pltpu.VMEM((TM, D_MODEL), BF16),
            pltpu.VMEM((TM, TN), F32),
            pltpu.VMEM((TM, TN), F32),
            pltpu.VMEM((N_CBLK, 8, TN), F32),
        ],
        compiler_params=pltpu.CompilerParams(
            dimension_semantics=("arbitrary", "arbitrary"), vmem_limit_bytes=VMEM_LIMIT),
        name="proj",
    )(x, g, w, c128, d128, c64, d64, conv_w, state)


def _select_kernel(qi_ref, kiwi_ref, kia_ref, kib_ref, bias_ref, key_s, wb_s, lg_s, lh_s, *,
                   tq, spb, nkb_total, topk, causal, n_valid, q_pos0):
    n = pl.program_id(0)
    npair = N_IDX_HEADS // 2
    blk = spb * LANES
    kiwi = kiwi_ref[...]
    wscale = (IDX_DIM ** -0.5) * (N_IDX_HEADS ** -0.5)
    for h in range(N_IDX_HEADS):
        wb_s[h] = jnp.broadcast_to(kiwi[:, IDX_DIM + h:IDX_DIM + h + 1], (tq, LANES)) * wscale
    q2 = jnp.concatenate([qi_ref[:, p * LANES:(p + 1) * LANES] for p in range(npair)], axis=0)

    if causal:
        qpos0 = n * tq
        nkb = (qpos0 + tq + blk - 1) // blk
    else:
        qpos0 = q_pos0
        nkb = nkb_total
    lane = lax.broadcasted_iota(jnp.int32, (tq, LANES), 1)
    qchunk = (qpos0 + lax.broadcasted_iota(jnp.int32, (tq, LANES), 0)) >> 6
    nt = (((1,), (1,)), ((), ()))

    def logits_into(lg_ref, kb):
        base = pl.multiple_of(jnp.minimum(kb, nkb_total - 1) * blk, blk)
        lg_ref[0] = lax.dot_general(q2, kia_ref[pl.ds(base, blk), :], nt, preferred_element_type=F32)
        lg_ref[1] = lax.dot_general(q2, kib_ref[pl.ds(base, blk), :], nt, preferred_element_type=F32)

    def head_sum(lg_ref, kb, carry):
        m1, m2 = carry
        base = kb * blk
        for c in range(spb):
            cs = slice(c * LANES, (c + 1) * LANES)
            acc = jnp.zeros((tq, LANES), F32)
            for p in range(npair):
                rs = slice(p * tq, (p + 1) * tq)
                acc = acc + jnp.maximum(lg_ref[0, rs, cs], 0.0) * wb_s[2 * p]
                acc = acc + jnp.maximum(lg_ref[1, rs, cs], 0.0) * wb_s[2 * p + 1]
            col = base + c * LANES + lane
            adm = jnp.logical_and((col >> 6) <= qchunk, col < n_valid)
            sc = jnp.where(adm, acc, NEG_INF)
            m2 = jnp.maximum(m2, jnp.minimum(m1, sc))
            m1 = jnp.maximum(m1, sc)
            key_s[kb * spb + c] = _sortable(sc)
        return m1, m2

    logits_into(lg_s, 0)

    def score_pair(i, carry):
        logits_into(lh_s, 2 * i + 1)
        carry = head_sum(lg_s, 2 * i, carry)
        logits_into(lg_s, 2 * i + 2)
        return head_sum(lh_s, 2 * i + 1, carry)

    neg = jnp.full((tq, LANES), NEG_INF, F32)
    m1, m2 = lax.fori_loop(0, (nkb + 1) // 2, score_pair, (neg, neg))

    def count_ge(cand):
        def body(kb, acc):
            for c in range(spb):
                acc = acc + jnp.where(key_s[kb * spb + c] >= cand, 1.0, 0.0)
            return acc
        acc = lax.fori_loop(0, nkb, body, jnp.zeros((tq, LANES), F32))
        return jnp.sum(acc, axis=1, keepdims=True)

    ones = jnp.ones((tq, LANES), jnp.int32)
    lo0 = _sortable(jnp.min(m2, axis=1, keepdims=True)) * ones
    hi0 = _sortable(jnp.max(m1, axis=1, keepdims=True)) * ones + 1
    kf = float(topk)

    def unresolved(lo_k, hi_k, c_lo):
        open_ = jnp.logical_and(c_lo != kf, (hi_k - lo_k) != 1)
        return jnp.max(jnp.where(open_, 1.0, 0.0))

    def bis_cond(st):
        return jnp.logical_and(st[0] < 34, st[-1] > 0.0)

    def bis_body(st):
        it, lo_k, hi_k, c_lo, _ = st
        flag = unresolved(lo_k, hi_k, c_lo)
        mid = lo_k + lax.shift_right_logical(hi_k - lo_k, 1)
        cnt = count_ge(mid)
        ge = cnt >= kf
        lo_k = jnp.where(ge, mid, lo_k)
        hi_k = jnp.where(ge, hi_k, mid)
        c_lo = jnp.where(ge, cnt, c_lo)
        return it + 1, lo_k, hi_k, c_lo, flag

    c0 = jnp.full((tq, LANES), -1.0, F32)
    _, thr, _, c_thr, _ = lax.while_loop(
        bis_cond, bis_body, (jnp.int32(0), lo0, hi0, c0, jnp.float32(1.0)))

    tied = jnp.max(jnp.where(jnp.logical_and(c_thr != kf, thr > _VALID_KEY), 1.0, 0.0)) > 0.0

    @pl.when(jnp.logical_not(tied))
    def _():
        def emit(kb, carry):
            for c in range(spb):
                k = key_s[kb * spb + c]
                sel = jnp.logical_and(k >= thr, k > _VALID_KEY)
                bias_ref[kb * spb + c] = jnp.where(sel, 0.0, NEG_INF).astype(BF16)
            return carry

        lax.fori_loop(0, nkb, emit, 0)

    @pl.when(tied)
    def _():
        need = kf - count_ge(thr + 1)
        tri = (lax.broadcasted_iota(jnp.int32, (LANES, LANES), 0)
               <= lax.broadcasted_iota(jnp.int32, (LANES, LANES), 1)).astype(BF16)

        def emit(kb, seen):
            for c in range(spb):
                k = key_s[kb * spb + c]
                eq = jnp.where(k == thr, 1.0, 0.0)
                rank = seen + jnp.dot(eq.astype(BF16), tri, preferred_element_type=F32)
                keep = jnp.logical_or(k > thr, jnp.logical_and(k == thr, rank <= need))
                sel = jnp.logical_and(keep, k > _VALID_KEY)
                bias_ref[kb * spb + c] = jnp.where(sel, 0.0, NEG_INF).astype(BF16)
                seen = seen + jnp.sum(eq, axis=1, keepdims=True)
            return seen

        lax.fori_loop(0, nkb, emit, jnp.zeros((tq, LANES), F32))

    def fill(kb, carry):
        for c in range(spb):
            bias_ref[kb * spb + c] = jnp.full((tq, LANES), NEG_INF, BF16)
        return carry

    lax.fori_loop(nkb, nkb_total, fill, 0)


def _select(qi, kiwi, kia, kib, *, tq, spb, topk, causal, n_valid, q_pos0):
    t = qi.shape[0]
    nb = t // tq
    lk = kia.shape[1]
    nslab = lk // LANES
    nkb_total = nslab // spb
    assert topk <= 2 * LANES and nslab >= 2, "the bisection's starting lower bound needs two keys per lane"

    def kmap(n):
        return (0 if causal else n, 0, 0)

    return pl.pallas_call(
        functools.partial(_select_kernel, tq=tq, spb=spb, nkb_total=nkb_total, topk=topk,
                          causal=causal, n_valid=n_valid, q_pos0=q_pos0),
        grid=(nb,),
        in_specs=[
            pl.BlockSpec((tq, IQ_W), lambda n: (n, 0)),
            pl.BlockSpec((tq, LANES), lambda n: (n, 0)),
            pl.BlockSpec((None, lk, LANES), kmap),
            pl.BlockSpec((None, lk, LANES), kmap),
        ],
        out_specs=pl.BlockSpec((nslab, tq, LANES), lambda n: (0, n, 0)),
        out_shape=jax.ShapeDtypeStruct((nslab, t, LANES), BF16),
        scratch_shapes=[
            pltpu.VMEM((nslab + spb, tq, LANES), jnp.int32),
            pltpu.VMEM((N_IDX_HEADS, tq, LANES), F32),
            pltpu.VMEM((2, N_IDX_HEADS // 2 * tq, spb * LANES), F32),
            pltpu.VMEM((2, N_IDX_HEADS // 2 * tq, spb * LANES), F32),
        ],
        compiler_params=pltpu.CompilerParams(
            dimension_semantics=("arbitrary",), vmem_limit_bytes=VMEM_LIMIT),
        name="select",
    )(qi, kiwi, kia, kib)


def _attend_kernel(qb_ref, kb_ref, kbat_ref, last_ref, q_ref, k_ref, v_ref, b_ref, o_ref,
                   m_s, l_s, acc_s, *, tq, spb):
    s = pl.program_id(0)

    @pl.when(kb_ref[s] == 0)
    def _():
        m_s[...] = jnp.full(m_s.shape, 0.1 * NEG_INF, F32)
        l_s[...] = jnp.zeros(l_s.shape, F32)
        acc_s[...] = jnp.zeros(acc_s.shape, F32)

    biases = [b_ref[c].astype(F32)[None] for c in range(spb)]
    ones = jnp.ones((spb * LANES, LANES), BF16)

    for g in range(N_KV_HEADS):
        qg = [q_ref[:, (g * GROUP + h) * HEAD_DIM:(g * GROUP + h + 1) * HEAD_DIM] for h in range(GROUP)]
        q4 = jnp.concatenate(qg, axis=0)
        kg = k_ref[:, g * HEAD_DIM:(g + 1) * HEAD_DIM]
        sc = lax.dot_general(q4, kg, (((1,), (1,)), ((), ())), preferred_element_type=F32)
        slabs = []
        mx = None
        for c in range(spb):
            sl = sc[:, c * LANES:(c + 1) * LANES].reshape(GROUP, tq, LANES) + biases[c]
            sl = sl.reshape(GROUP * tq, LANES)
            slabs.append(sl)
            mx = sl if mx is None else jnp.maximum(mx, sl)
        m_prev = m_s[g]
        m_new = jnp.maximum(m_prev, jnp.max(mx, axis=1, keepdims=True))
        alpha = jnp.exp2(m_prev - m_new)
        pmat = jnp.concatenate([jnp.exp2(sl - m_new).astype(BF16) for sl in slabs], axis=1)
        v1 = jnp.concatenate([v_ref[:, g * HEAD_DIM:(g + 1) * HEAD_DIM], ones], axis=1)
        pv = jnp.dot(pmat, v1, preferred_element_type=F32)
        acc_s[g] = alpha * acc_s[g] + pv[:, 0:HEAD_DIM]
        l_s[g] = alpha * l_s[g] + pv[:, HEAD_DIM:HEAD_DIM + LANES]
        m_s[g] = m_new

    @pl.when(last_ref[s] == 1)
    def _():
        for g in range(N_KV_HEADS):
            o = acc_s[g] / l_s[g]
            for h in range(GROUP):
                col = (g * GROUP + h) * HEAD_DIM
                o_ref[:, col:col + HEAD_DIM] = o[h * tq:(h + 1) * tq].astype(BF16)


def _attend(q, k_all, v_all, bias, sched, *, tq, spb):
    t = q.shape[0]
    qb, kb, kbat, last = sched
    nsteps = qb.shape[0]
    blk = spb * LANES
    grid_spec = pltpu.PrefetchScalarGridSpec(
        num_scalar_prefetch=4,
        grid=(nsteps,),
        in_specs=[
            pl.BlockSpec((tq, Q_W), lambda s, qb, kb, kbat, last: (qb[s], 0)),
            pl.BlockSpec((None, blk, KV_W), lambda s, qb, kb, kbat, last: (kbat[s], kb[s], 0)),
            pl.BlockSpec((None, blk, KV_W), lambda s, qb, kb, kbat, last: (kbat[s], kb[s], 0)),
            pl.BlockSpec((spb, tq, LANES), lambda s, qb, kb, kbat, last: (kb[s], qb[s], 0)),
        ],
        out_specs=pl.BlockSpec((tq, Q_W), lambda s, qb, kb, kbat, last: (qb[s], 0)),
        scratch_shapes=[
            pltpu.VMEM((N_KV_HEADS, GROUP * tq, LANES), F32),
            pltpu.VMEM((N_KV_HEADS, GROUP * tq, LANES), F32),
            pltpu.VMEM((N_KV_HEADS, GROUP * tq, HEAD_DIM), F32),
        ],
    )
    return pl.pallas_call(
        functools.partial(_attend_kernel, tq=tq, spb=spb),
        grid_spec=grid_spec,
        out_shape=jax.ShapeDtypeStruct((t, Q_W), BF16),
        compiler_params=pltpu.CompilerParams(
            dimension_semantics=("arbitrary",), vmem_limit_bytes=VMEM_LIMIT),
        name="attend",
    )(qb, kb, kbat, last, q, k_all, v_all, bias)


def _causal_schedule(t, tq, blk):
    qb, kb, last = [], [], []
    for n in range(t // tq):
        nk = ((n + 1) * tq + blk - 1) // blk
        for k in range(nk):
            qb.append(n)
            kb.append(k)
            last.append(1 if k == nk - 1 else 0)
    z = np.zeros(len(qb), np.int32)
    return (jnp.asarray(qb, jnp.int32), jnp.asarray(kb, jnp.int32), jnp.asarray(z), jnp.asarray(last, jnp.int32))


def _batched_schedule(nbatch, nk):
    qb = np.repeat(np.arange(nbatch, dtype=np.int32), nk)
    kb = np.tile(np.arange(nk, dtype=np.int32), nbatch)
    last = (kb == nk - 1).astype(np.int32)
    return (jnp.asarray(qb), jnp.asarray(kb), jnp.asarray(qb), jnp.asarray(last))


def _merge_kernel(o_ref, yc_ref, ga_ref, gc_ref, woa_ref, woc_ref, wout_ref, x_ref, gq_ref, gp_ref,
                  x1_ref, xn2_ref, mg_s, m_s):
    j = pl.program_id(1)
    nblk = D_MODEL // MERGE_TN

    @pl.when(j < nblk)
    def _():
        a = jnp.dot(o_ref[...], woa_ref[...], preferred_element_type=F32)
        c = jnp.dot(yc_ref[...], woc_ref[...], preferred_element_type=F32)
        mg_s[j] = (ga_ref[...] * a + gc_ref[...] * c).astype(BF16)

    @pl.when(j >= nblk)
    def _():
        mg = jnp.concatenate([mg_s[b] for b in range(nblk)], axis=1)
        m_s[j - nblk] = jnp.dot(mg, wout_ref[...], preferred_element_type=F32)

    @pl.when(j == 2 * nblk - 1)
    def _():
        m = jnp.concatenate([m_s[b] for b in range(nblk)], axis=1)
        x1 = x_ref[...] + _rms(m, gq_ref[...])
        x1_ref[...] = x1
        xn2_ref[...] = _rms(x1, gp_ref[...]).astype(BF16)


def _merge(o, yc, ga, gc, woa, woc, wout, x, gq, gp):
    t = x.shape[0]
    tn = MERGE_TN
    nblk = D_MODEL // tn

    def row(i, j):
        return (i, 0)

    def lo(i, j):
        return (i, jnp.minimum(j, nblk - 1))

    return pl.pallas_call(
        _merge_kernel,
        grid=(t // TM, 2 * nblk),
        in_specs=[
            pl.BlockSpec((TM, Q_W), row),
            pl.BlockSpec((TM, D_MODEL), row),
            pl.BlockSpec((TM, tn), lo),
            pl.BlockSpec((TM, tn), lo),
            pl.BlockSpec((Q_W, tn), lambda i, j: (0, jnp.minimum(j, nblk - 1))),
            pl.BlockSpec((D_MODEL, tn), lambda i, j: (0, jnp.minimum(j, nblk - 1))),
            pl.BlockSpec((D_MODEL, tn), lambda i, j: (0, jnp.maximum(j - nblk, 0))),
            pl.BlockSpec((TM, D_MODEL), row),
            pl.BlockSpec((1, D_MODEL), lambda i, j: (0, 0)),
            pl.BlockSpec((1, D_MODEL), lambda i, j: (0, 0)),
        ],
        out_specs=[pl.BlockSpec((TM, D_MODEL), row), pl.BlockSpec((TM, D_MODEL), row)],
        out_shape=[jax.ShapeDtypeStruct((t, D_MODEL), F32), jax.ShapeDtypeStruct((t, D_MODEL), BF16)],
        scratch_shapes=[pltpu.VMEM((nblk, TM, tn), BF16), pltpu.VMEM((nblk, TM, tn), F32)],
        compiler_params=pltpu.CompilerParams(
            dimension_semantics=("arbitrary", "arbitrary"), vmem_limit_bytes=VMEM_LIMIT),
        name="merge",
    )(o, yc, ga, gc, woa, woc, wout, x, gq, gp)


def _ffn_kernel(xn_ref, wg_ref, wv_ref, cwg_ref, cwv_ref, wd_ref, x1_ref, gq_ref, stg_ref, stv_ref,
                y_ref, ng_ref, nv_ref, acc_s, cg_s, cv_s, *, seg, carried):
    i = pl.program_id(0)
    jf = pl.program_id(1)
    nseg = TM // seg

    @pl.when(jf == 0)
    def _():
        acc_s[...] = jnp.zeros(acc_s.shape, F32)

    if carried:
        @pl.when(i == 0)
        def _():
            for carry_s, st_ref in ((cg_s, stg_ref), (cv_s, stv_ref)):
                carry_s[jf] = jnp.zeros((8, TF), F32)
                carry_s[jf, 0:2, :] = st_ref[0]

    xn = xn_ref[...]

    def branch(w_ref, cw_ref, st_ref, carry_s, new_ref):
        up = jnp.dot(xn, w_ref[...], preferred_element_type=F32)
        if carried:
            prevs = [carry_s[jf, 0:2, :]]
        else:
            prevs = [st_ref[s] for s in range(nseg)]
        y, news = _conv_tile(up, cw_ref[...], prevs, seg)
        for s in range(nseg):
            new_ref[s] = news[s]
        if carried:
            carry_s[jf, 0:2, :] = news[0]
        return y

    gate = branch(wg_ref, cwg_ref, stg_ref, cg_s, ng_ref)
    val = branch(wv_ref, cwv_ref, stv_ref, cv_s, nv_ref)
    c0 = np.float32(np.sqrt(2.0 / np.pi))
    gelu = 0.5 * gate * (1.0 + jnp.tanh(c0 * (gate + 0.044715 * (gate * gate * gate))))
    hid = (gelu * val).astype(BF16)
    acc_s[...] += jnp.dot(hid, wd_ref[...], preferred_element_type=F32)

    @pl.when(jf == pl.num_programs(1) - 1)
    def _():
        y_ref[...] = x1_ref[...] + _rms(acc_s[...], gq_ref[...])


def _ffn(xn2, w_up, cw, w_down, x1, gq, state, seg):
    t = x1.shape[0]
    nf = D_FF // TF
    carried = seg == t
    nseg_tile = 1 if carried else TM // seg

    def row(i, j):
        return (i, 0)

    def stg(i, j):
        return (0 if carried else i, 0, j)

    def stv(i, j):
        return (0 if carried else i, 0, j + nf)

    return pl.pallas_call(
        functools.partial(_ffn_kernel, seg=min(seg, TM), carried=carried),
        grid=(t // TM, nf),
        in_specs=[
            pl.BlockSpec((TM, D_MODEL), row),
            pl.BlockSpec((D_MODEL, TF), lambda i, j: (0, j)),
            pl.BlockSpec((D_MODEL, TF), lambda i, j: (0, j + nf)),
            pl.BlockSpec((3, TF), lambda i, j: (0, j)),
            pl.BlockSpec((3, TF), lambda i, j: (0, j + nf)),
            pl.BlockSpec((TF, D_MODEL), lambda i, j: (j, 0)),
            pl.BlockSpec((TM, D_MODEL), row),
            pl.BlockSpec((1, D_MODEL), lambda i, j: (0, 0)),
            pl.BlockSpec((nseg_tile, 2, TF), stg),
            pl.BlockSpec((nseg_tile, 2, TF), stv),
        ],
        out_specs=[
            pl.BlockSpec((TM, D_MODEL), row),
            pl.BlockSpec((nseg_tile, 2, TF), lambda i, j: (i, 0, j)),
            pl.BlockSpec((nseg_tile, 2, TF), lambda i, j: (i, 0, j)),
        ],
        out_shape=[
            jax.ShapeDtypeStruct((t, D_MODEL), F32),
            jax.ShapeDtypeStruct((t // TM * nseg_tile, 2, D_FF), F32),
            jax.ShapeDtypeStruct((t // TM * nseg_tile, 2, D_FF), F32),
        ],
        scratch_shapes=[
            pltpu.VMEM((TM, D_MODEL), F32),
            pltpu.VMEM((nf, 8, TF), F32),
            pltpu.VMEM((nf, 8, TF), F32),
        ],
        compiler_params=pltpu.CompilerParams(
            dimension_semantics=("arbitrary", "arbitrary"), vmem_limit_bytes=VMEM_LIMIT),
        name="ffn",
    )(xn2, w_up, w_up, cw, cw, w_down, x1, gq, state, state)


PACK_ROWS = 1024


def _pack_kernel(ck_ref, cv_ref, kt_ref, vt_ref, ko_ref, vo_ref, *, rows):
    del kt_ref, vt_ref
    for src, dst in ((ck_ref, ko_ref), (cv_ref, vo_ref)):
        for g in range(N_KV_HEADS):
            dst[:, g * HEAD_DIM:(g + 1) * HEAD_DIM] = src[pl.ds(g, rows, stride=N_KV_HEADS), :].astype(BF16)


def _pack_cache(cache_k, cache_v, k_tail, v_tail):
    nb, plen = cache_k.shape[0], cache_k.shape[1]
    rows = int(np.gcd(plen, PACK_ROWS))
    assert rows % 16 == 0
    cspec = pl.BlockSpec((None, rows * N_KV_HEADS, HEAD_DIM), lambda b, r: (b, r, 0))
    ospec = pl.BlockSpec((None, rows, KV_W), lambda b, r: (b, r, 0))
    anyspec = pl.BlockSpec(memory_space=pl.ANY)
    flat = (nb, plen * N_KV_HEADS, HEAD_DIM)
    return pl.pallas_call(
        functools.partial(_pack_kernel, rows=rows),
        grid=(nb, plen // rows),
        in_specs=[cspec, cspec, anyspec, anyspec],
        out_specs=[ospec, ospec],
        out_shape=[jax.ShapeDtypeStruct(k_tail.shape, BF16), jax.ShapeDtypeStruct(v_tail.shape, BF16)],
        input_output_aliases={2: 0, 3: 1},
        compiler_params=pltpu.CompilerParams(
            dimension_semantics=("arbitrary", "arbitrary"), vmem_limit_bytes=VMEM_LIMIT),
        name="pack_cache",
    )(cache_k.reshape(flat), cache_v.reshape(flat), k_tail, v_tail)


def _rope_tables(pos, rot, width):
    half = rot // 2
    freqs = ROPE_THETA ** (-jnp.arange(half, dtype=F32) / half)
    ang = pos.astype(F32)[:, None] * freqs[None, :]
    cos, sin = jnp.cos(ang), jnp.sin(ang)
    t = pos.shape[0]
    c = jnp.concatenate([cos, cos, jnp.ones((t, width - rot), F32)], axis=1)
    d = jnp.concatenate([-sin, sin, jnp.zeros((t, width - rot), F32)], axis=1)
    reps = LANES // width
    return jnp.tile(c, (1, reps)), jnp.tile(d, (1, reps))


def _relayout_w_in(w):
    o = np.cumsum([0, Q_W, KV_W, KV_W, IQ_W, IDX_DIM, N_IDX_HEADS, D_MODEL, D_MODEL, D_MODEL, D_MODEL, D_MODEL])
    q, k, v, qi = (w[:, o[a]:o[a + 1]] for a in range(4))
    kiwi = w[:, o[4]:o[6]]
    cb, cc, ch, ga, gc = (w[:, o[a]:o[a + 1]] for a in range(6, 11))
    parts = [q, k, v, qi, kiwi, jnp.zeros((D_MODEL, TN - kiwi.shape[1]), w.dtype)]
    for c in range(N_CBLK):
        sl = slice(c * TN, (c + 1) * TN)
        parts += [cb[:, sl], cc[:, sl], ch[:, sl]]
    parts += [ga, gc]
    return jnp.concatenate(parts, axis=1).astype(BF16)


def _stream(x, pos, seg, past, weights, *, tq, spb_sel, spb_att):
    (g_mp, g_mq, w_in_r, conv_w, woa, woc, wout, g_fp, g_fq, w_up, fconv_w, w_down) = weights
    t = x.shape[0]
    nseq = t // seg
    tabs = _rope_tables(pos, ROT_DIM, HEAD_DIM) + _rope_tables(pos, IDX_ROT_DIM, IDX_DIM)
    if past is None:
        conv_state = jnp.zeros((nseq, 2, D_MODEL), F32)
        ffn_state = jnp.zeros((nseq, 2, 2 * D_FF), F32)
    else:
        conv_state, ffn_state = past[3], past[4]

    (q, k, kb, v, vb, qi, kiwi, kia, kib, yc, conv_new, ga, gc) = _proj(
        x, g_mp, w_in_r, tabs, conv_w, conv_state, seg)

    if past is None:
        k_all, v_all = kb[None], vb[None]
        kia_all, kib_all = kia[None], kib[None]
        n_keys = t
        sched = _causal_schedule(t, tq, spb_att * LANES)
        q_pos0 = 0
    else:
        cache_k, cache_v, cache_ki = past[0], past[1], past[2]
        plen = cache_k.shape[1]
        n_keys = plen + seg
        lk = -(-n_keys // (spb_att * LANES)) * (spb_att * LANES)
        pad = lk - n_keys

        def cat(c, new, width):
            parts = [c, new.reshape(nseq, seg, width)]
            if pad:
                parts.append(jnp.zeros((nseq, pad, width), BF16))
            return jnp.concatenate(parts, axis=1)

        def tail(new):
            return jnp.pad(new.reshape(nseq, seg, KV_W), ((0, 0), (plen, pad), (0, 0)))

        k_all, v_all = _pack_cache(cache_k, cache_v, tail(kb), tail(vb))
        cki = cache_ki.astype(BF16)
        zk = jnp.zeros_like(cki)
        kia_all = cat(jnp.concatenate([cki, zk], axis=-1), kia, LANES)
        kib_all = cat(jnp.concatenate([zk, cki], axis=-1), kib, LANES)
        sched = _batched_schedule(nseq, lk // (spb_att * LANES))
        q_pos0 = plen
    topk = min(TOPK_MAX, n_keys // 4)

    bias = _select(qi, kiwi, kia_all, kib_all, tq=tq, spb=spb_sel, topk=topk,
                   causal=past is None, n_valid=n_keys, q_pos0=q_pos0)
    o = _attend(q, k_all, v_all, bias, sched, tq=tq, spb=spb_att)
    x1, xn2 = _merge(o, yc, ga, gc, woa, woc, wout, x, g_mq, g_fp)
    y, ffn_g, ffn_v = _ffn(xn2, w_up, fconv_w, w_down, x1, g_fq, ffn_state, seg)
    ffn_new = jnp.concatenate([ffn_g, ffn_v], axis=-1)
    return y, k, v, kiwi[:, :IDX_DIM], conv_new[-nseq:], ffn_new[-nseq:]


def kernel(x_prompt, x_sample, cache_k, cache_v, cache_k_idx, state_conv, state_ffn_conv, norm_mix_pre, norm_mix_post, w_in, conv_w, w_o_attn, w_o_conv, w_out, norm_ffn_pre, norm_ffn_post, w_ffn_up, ffn_conv_w, w_ffn_down):
    depth = w_in.shape[0]
    assert depth == 1, "single-layer step"
    b, seq, _ = x_prompt.shape
    assert b == 1
    db, dseq, _ = x_sample.shape
    plen = cache_k.shape[2]
    assert dseq == CHUNK and plen % CHUNK == 0

    weights = (
        norm_mix_pre, norm_mix_post, _relayout_w_in(w_in[0]), conv_w[0],
        w_o_attn[0].astype(BF16), w_o_conv[0].astype(BF16), w_out[0].astype(BF16),
        norm_ffn_pre, norm_ffn_post, w_ffn_up[0].astype(BF16), ffn_conv_w[0], w_ffn_down[0].astype(BF16),
    )

    pos_p = jnp.arange(seq, dtype=jnp.int32)
    yp, kp, vp, kip, convp, ffnp = _stream(
        x_prompt.reshape(seq, D_MODEL), pos_p, seq, None, weights, tq=128, spb_sel=4, spb_att=8)

    pos_s = jnp.tile(jnp.arange(dseq, dtype=jnp.int32) + plen, db)
    past = (cache_k[0], cache_v[0], cache_k_idx[0], state_conv[0], state_ffn_conv[0])
    n_keys = plen + dseq
    spb_s = _sample_slabs(n_keys)
    ys, ks, vs, kis, convs, ffns = _stream(
        x_sample.reshape(db * dseq, D_MODEL), pos_s, dseq, past, weights, tq=dseq, spb_sel=spb_s, spb_att=spb_s)

    return (
        yp.reshape(1, seq, D_MODEL), ys.reshape(db, dseq, D_MODEL),
        kp.reshape(1, 1, seq, N_KV_HEADS, HEAD_DIM), vp.reshape(1, 1, seq, N_KV_HEADS, HEAD_DIM),
        kip.reshape(1, 1, seq, IDX_DIM), convp.reshape(1, 1, 2, D_MODEL), ffnp.reshape(1, 1, 2, 2 * D_FF),
        ks.reshape(1, db, dseq, N_KV_HEADS, HEAD_DIM), vs.reshape(1, db, dseq, N_KV_HEADS, HEAD_DIM),
        kis.reshape(1, db, dseq, IDX_DIM), convs.reshape(1, db, 2, D_MODEL), ffns.reshape(1, db, 2, 2 * D_FF),
    )


def _sample_slabs(n_keys):
    nslab = -(-n_keys // LANES)
    best = 1
    for d in range(1, nslab + 1):
        if nslab % d == 0 and d <= 11:
            best = d
    return best
```

```python
import functools

import jax
import jax.numpy as jnp
import numpy as np
from jax import lax
from jax.experimental import pallas as pl
from jax.experimental.pallas import tpu as pltpu

F32 = jnp.float32
BF16 = jnp.bfloat16

D_MODEL = 2048
N_HEADS = 16
N_KV_HEADS = 4
HEAD_DIM = 128
ROT_DIM = HEAD_DIM // 4
N_IDX_HEADS = 16
IDX_DIM = 64
IDX_ROT_DIM = IDX_DIM // 4
CHUNK = 64
TOPK_MAX = 256
ROPE_THETA = 500000.0
D_FF = 5632
RMS_EPS = 1e-6
NEG_INF = -1e30
Q_W = N_HEADS * HEAD_DIM
KV_W = N_KV_HEADS * HEAD_DIM
IQ_W = N_IDX_HEADS * IDX_DIM
GROUP = N_HEADS // N_KV_HEADS

LANES = 128
TM = 512
TN = 1024
MERGE_TN = 512
TF = 512
VMEM_LIMIT = 56 * 1024 * 1024

J_Q = 0
J_KV = J_Q + Q_W // TN
assert 2 * KV_W == TN
J_QI = J_KV + 1
J_KIWI = J_QI + IQ_W // TN
J_CONV = J_KIWI + 1
N_CBLK = D_MODEL // TN
J_GA = J_CONV + 3 * N_CBLK
J_GC = J_GA + N_CBLK
NJ = J_GC + N_CBLK

Q_PRESCALE = float(np.float32((HEAD_DIM ** -0.5) * np.log2(np.e)))


def _rms(x, g):
    return x * lax.rsqrt(jnp.mean(x * x, axis=-1, keepdims=True) + RMS_EPS) * g


def _sortable(x):
    bits = lax.bitcast_convert_type(x, jnp.int32)
    return jnp.where(bits < 0, bits ^ jnp.int32(0x7FFFFFFF), bits)


def _unsortable(k):
    return lax.bitcast_convert_type(jnp.where(k < 0, k ^ jnp.int32(0x7FFFFFFF), k), F32)


CAND = 16
PAD_SCORE = -3.0e38


def _oddeven_merge_sort_pairs(n):
    out, p = [], 1
    while p < n:
        k = p
        while k >= 1:
            for j in range(k % p, n - k, 2 * k):
                for i in range(min(k, n - j - k)):
                    if (i + j) // (2 * p) == (i + j + k) // (2 * p):
                        out.append((i + j, i + j + k))
            k //= 2
        p *= 2
    return out


_SORT_PAIRS = _oddeven_merge_sort_pairs(CAND)


def _sort_desc(v):
    v = list(v)
    for i, j in _SORT_PAIRS:
        v[i], v[j] = jnp.maximum(v[i], v[j]), jnp.minimum(v[i], v[j])
    return v


def _bitonic_merge_desc(v):
    v = list(v)
    d = len(v) // 2
    while d >= 1:
        for i in range(len(v)):
            if not i & d:
                v[i], v[i + d] = jnp.maximum(v[i], v[i + d]), jnp.minimum(v[i], v[i + d])
        d //= 2
    return v


def _dwconv_seg(u, prev2, w3):
    row = lax.broadcasted_iota(jnp.int32, u.shape, 0)
    p0, p1 = prev2[0:1], prev2[1:2]
    s1 = jnp.where(row == 0, p1, pltpu.roll(u, 1, 0))
    s2 = jnp.where(row == 0, p0, jnp.where(row == 1, p1, pltpu.roll(u, 2, 0)))
    return w3[0:1] * s2 + w3[1:2] * s1 + w3[2:3] * u


def _conv_tile(u, w3, prevs, seg):
    nseg = len(prevs)
    ys, news = [], []
    for s in range(nseg):
        us = u[s * seg:(s + 1) * seg]
        ys.append(_dwconv_seg(us, prevs[s], w3))
        news.append(us[seg - 2:seg])
    y = ys[0] if nseg == 1 else jnp.concatenate(ys, axis=0)
    return y, news


def _proj_kernel(x_ref, g_ref, w_ref, c128_ref, d128_ref, c64_ref, d64_ref, cw_ref, st_ref,
                 q_ref, k_ref, kb_ref, v_ref, vb_ref, qi_ref, kiwi_ref, kia_ref, kib_ref,
                 yc_ref, cn_ref, ga_ref, gc_ref,
                 xn_s, cb_s, cc_s, carry_s, *, seg, carried):
    i = pl.program_id(0)
    j = pl.program_id(1)

    @pl.when(j == 0)
    def _():
        xn_s[...] = _rms(x_ref[...], g_ref[...]).astype(BF16)

    def mm():
        return jnp.dot(xn_s[...], w_ref[...], preferred_element_type=F32)

    lane = lax.broadcasted_iota(jnp.int32, (TM, LANES), 1)

    def rope128(xh):
        partner = jnp.where(lane < ROT_DIM // 2, pltpu.roll(xh, LANES - ROT_DIM // 2, 1),
                            pltpu.roll(xh, ROT_DIM // 2, 1))
        return xh * c128_ref[...] + partner * d128_ref[...]

    def rope64(xh, c, d):
        first = (lane & (IDX_DIM - 1)) < IDX_ROT_DIM // 2
        partner = jnp.where(first, pltpu.roll(xh, LANES - IDX_ROT_DIM // 2, 1),
                            pltpu.roll(xh, IDX_ROT_DIM // 2, 1))
        return xh * c + partner * d

    @pl.when(j < J_KV)
    def _():
        acc = mm()
        for h in range(TN // LANES):
            sl = slice(h * LANES, (h + 1) * LANES)
            q_ref[:, sl] = (rope128(acc[:, sl]) * Q_PRESCALE).astype(BF16)

    @pl.when(j == J_KV)
    def _():
        acc = mm()
        for h in range(KV_W // LANES):
            sl = slice(h * LANES, (h + 1) * LANES)
            r = rope128(acc[:, sl])
            k_ref[:, sl] = r
            kb_ref[:, sl] = r.astype(BF16)
        v = acc[:, KV_W:2 * KV_W]
        v_ref[...] = v
        vb_ref[...] = v.astype(BF16)

    @pl.when(jnp.logical_and(j >= J_QI, j < J_KIWI))
    def _():
        acc = mm()
        for h in range(TN // LANES):
            sl = slice(h * LANES, (h + 1) * LANES)
            qi_ref[:, sl] = rope64(acc[:, sl], c64_ref[...], d64_ref[...]).astype(BF16)

    @pl.when(j == J_KIWI)
    def _():
        is_ki = lane < IDX_DIM
        c = jnp.where(is_ki, c64_ref[...], 1.0)
        d = jnp.where(is_ki, d64_ref[...], 0.0)
        acc = jnp.dot(xn_s[...], w_ref[:, 0:LANES], preferred_element_type=F32)
        r = rope64(acc, c, d)
        kiwi_ref[...] = r
        ka = jnp.where(is_ki, r, 0.0)
        kia_ref[...] = ka.astype(BF16)
        kib_ref[...] = pltpu.roll(ka, IDX_DIM, 1).astype(BF16)

    jc = jnp.maximum(j - J_CONV, 0)
    in_conv = jnp.logical_and(j >= J_CONV, j < J_GA)
    cblk = jnp.minimum(jc // 3, N_CBLK - 1)
    part = jc % 3

    @pl.when(jnp.logical_and(in_conv, part == 0))
    def _():
        cb_s[...] = mm()

    @pl.when(jnp.logical_and(in_conv, part == 1))
    def _():
        cc_s[...] = mm()

    @pl.when(jnp.logical_and(in_conv, part == 2))
    def _():
        nseg = TM // seg
        if carried:
            @pl.when(i == 0)
            def _():
                carry_s[cblk] = jnp.zeros((8, TN), F32)
                carry_s[cblk, 0:2, :] = st_ref[0]
            prevs = [carry_s[cblk, 0:2, :]]
        else:
            prevs = [st_ref[s] for s in range(nseg)]
        u = cc_s[...] * mm()
        y, news = _conv_tile(u, cw_ref[...], prevs, seg)
        yc_ref[...] = (cb_s[...] * y).astype(BF16)
        for s in range(nseg):
            cn_ref[s] = news[s]
        if carried:
            carry_s[cblk, 0:2, :] = news[0]

    @pl.when(jnp.logical_and(j >= J_GA, j < J_GC))
    def _():
        ga_ref[...] = 1.0 / (1.0 + jnp.exp(-mm()))

    @pl.when(j >= J_GC)
    def _():
        gc_ref[...] = 1.0 / (1.0 + jnp.exp(-mm()))


def _proj(x, g, w, tabs, conv_w, state, seg):
    t = x.shape[0]
    ni = t // TM
    carried = seg == t
    nseg_tile = 1 if carried else TM // seg
    c128, d128, c64, d64 = tabs

    def row(i, j):
        return (i, 0)

    def const(i, j):
        return (0, 0)

    def cblk(j):
        return jnp.minimum(jnp.maximum(j - J_CONV, 0) // 3, N_CBLK - 1)

    def st_map(i, j):
        return (0 if carried else i, 0, cblk(j))

    in_specs = [
        pl.BlockSpec((TM, D_MODEL), row),
        pl.BlockSpec((1, D_MODEL), const),
        pl.BlockSpec((D_MODEL, TN), lambda i, j: (0, j)),
        pl.BlockSpec((TM, LANES), row),
        pl.BlockSpec((TM, LANES), row),
        pl.BlockSpec((TM, LANES), row),
        pl.BlockSpec((TM, LANES), row),
        pl.BlockSpec((3, TN), lambda i, j: (0, cblk(j))),
        pl.BlockSpec((nseg_tile, 2, TN), st_map),
    ]
    out_shape = [
        jax.ShapeDtypeStruct((t, Q_W), BF16),
        jax.ShapeDtypeStruct((t, KV_W), F32),
        jax.ShapeDtypeStruct((t, KV_W), BF16),
        jax.ShapeDtypeStruct((t, KV_W), F32),
        jax.ShapeDtypeStruct((t, KV_W), BF16),
        jax.ShapeDtypeStruct((t, IQ_W), BF16),
        jax.ShapeDtypeStruct((t, LANES), F32),
        jax.ShapeDtypeStruct((t, LANES), BF16),
        jax.ShapeDtypeStruct((t, LANES), BF16),
        jax.ShapeDtypeStruct((t, D_MODEL), BF16),
        jax.ShapeDtypeStruct((ni * nseg_tile, 2, D_MODEL), F32),
        jax.ShapeDtypeStruct((t, D_MODEL), F32),
        jax.ShapeDtypeStruct((t, D_MODEL), F32),
    ]
    out_specs = [
        pl.BlockSpec((TM, TN), lambda i, j: (i, jnp.clip(j - J_Q, 0, J_KV - J_Q - 1))),
        pl.BlockSpec((TM, KV_W), row),
        pl.BlockSpec((TM, KV_W), row),
        pl.BlockSpec((TM, KV_W), row),
        pl.BlockSpec((TM, KV_W), row),
        pl.BlockSpec((TM, TN), lambda i, j: (i, jnp.clip(j - J_QI, 0, J_KIWI - J_QI - 1))),
        pl.BlockSpec((TM, LANES), row),
        pl.BlockSpec((TM, LANES), row),
        pl.BlockSpec((TM, LANES), row),
        pl.BlockSpec((TM, TN), lambda i, j: (i, cblk(j))),
        pl.BlockSpec((nseg_tile, 2, TN), lambda i, j: (i, 0, cblk(j))),
        pl.BlockSpec((TM, TN), lambda i, j: (i, jnp.clip(j - J_GA, 0, N_CBLK - 1))),
        pl.BlockSpec((TM, TN), lambda i, j: (i, jnp.clip(j - J_GC, 0, N_CBLK - 1))),
    ]
    return pl.pallas_call(
        functools.partial(_proj_kernel, seg=min(seg, TM), carried=carried),
        grid=(ni, NJ),
        in_specs=in_specs,
        out_specs=out_specs,
        out_shape=out_shape,
        scratch_shapes=[
            pltpu.VMEM((TM, D_MODEL), BF16),
            pltpu.VMEM((TM, TN), F32),
            pltpu.VMEM((TM, TN), F32),
            pltpu.VMEM((N_CBLK, 8, TN), F32),
        ],
        compiler_params=pltpu.CompilerParams(
            dimension_semantics=("arbitrary", "arbitrary"), vmem_limit_bytes=VMEM_LIMIT),
        name="proj",
    )(x, g, w, c128, d128, c64, d64, conv_w, state)


def _select_kernel(qi_ref, kiwi_ref, kia_ref, kib_ref, bias_ref, sc_s, wb_s, lg_s, lh_s, cand_s, thr_s, cnt_s, *,
                   tq, spb, nkb_total, topk, causal, n_valid, q_pos0):
    n = pl.program_id(0)
    npair = N_IDX_HEADS // 2
    blk = spb * LANES
    kiwi = kiwi_ref[...]
    wscale = (IDX_DIM ** -0.5) * (N_IDX_HEADS ** -0.5)
    for h in range(N_IDX_HEADS):
        wb_s[h] = jnp.broadcast_to(kiwi[:, IDX_DIM + h:IDX_DIM + h + 1], (tq, LANES)) * wscale
    q2 = jnp.concatenate([qi_ref[:, p * LANES:(p + 1) * LANES] for p in range(npair)], axis=0)

    if causal:
        qpos0 = n * tq
        nkb = (qpos0 + tq + blk - 1) // blk
    else:
        qpos0 = q_pos0
        nkb = nkb_total
    lane = lax.broadcasted_iota(jnp.int32, (tq, LANES), 1)
    qchunk = (qpos0 + lax.broadcasted_iota(jnp.int32, (tq, LANES), 0)) >> 6
    nt = (((1,), (1,)), ((), ()))

    def logits_into(lg_ref, kb):
        base = pl.multiple_of(jnp.minimum(kb, nkb_total - 1) * blk, blk)
        lg_ref[0] = lax.dot_general(q2, kia_ref[pl.ds(base, blk), :], nt, preferred_element_type=F32)
        lg_ref[1] = lax.dot_general(q2, kib_ref[pl.ds(base, blk), :], nt, preferred_element_type=F32)

    def head_sum(lg_ref, kb, carry):
        m1, m2 = carry
        base = kb * blk
        for c in range(spb):
            cs = slice(c * LANES, (c + 1) * LANES)
            acc = jnp.zeros((tq, LANES), F32)
            for p in range(npair):
                rs = slice(p * tq, (p + 1) * tq)
                acc = acc + jnp.maximum(lg_ref[0, rs, cs], 0.0) * wb_s[2 * p]
                acc = acc + jnp.maximum(lg_ref[1, rs, cs], 0.0) * wb_s[2 * p + 1]
            col = base + c * LANES + lane
            adm = jnp.logical_and((col >> 6) <= qchunk, col < n_valid)
            sc = jnp.where(adm, acc, NEG_INF)
            m2 = jnp.maximum(m2, jnp.minimum(m1, sc))
            m1 = jnp.maximum(m1, sc)
            sc_s[kb * spb + c] = sc
        return m1, m2

    logits_into(lg_s, 0)

    def score_pair(i, carry):
        logits_into(lh_s, 2 * i + 1)
        carry = head_sum(lg_s, 2 * i, carry)
        logits_into(lg_s, 2 * i + 2)
        return head_sum(lh_s, 2 * i + 1, carry)

    neg = jnp.full((tq, LANES), NEG_INF, F32)
    m1, m2 = lax.fori_loop(0, (nkb + 1) // 2, score_pair, (neg, neg))

    zeros = jnp.zeros((tq, LANES), F32)
    kf = float(topk)

    def count_all(thr_f, strict=False):
        def body(kb, acc):
            for c in range(spb):
                s = sc_s[kb * spb + c]
                acc = acc + jnp.where(s > thr_f if strict else s >= thr_f, 1.0, 0.0)
            return acc
        return jnp.sum(lax.fori_loop(0, nkb, body, zeros), axis=1, keepdims=True)

    def count_cand(thr_f):
        acc = zeros
        for i in range(CAND):
            acc = acc + jnp.where(cand_s[i] >= thr_f, 1.0, 0.0)
        return jnp.sum(acc, axis=1, keepdims=True)

    ones = jnp.ones((tq, LANES), jnp.int32)
    lo0 = _sortable(jnp.min(m2, axis=1, keepdims=True)) * ones
    hi0 = _sortable(jnp.max(m1, axis=1, keepdims=True)) * ones + 1

    def bisect(count):
        def unresolved(lo_k, hi_k, c_lo):
            open_ = jnp.logical_and(c_lo != kf, (hi_k - lo_k) != 1)
            return jnp.max(jnp.where(open_, 1.0, 0.0))

        def cond(st):
            return jnp.logical_and(st[0] < 33, st[-1] > 0.0)

        def body(st):
            it, lo_k, hi_k, c_lo, _ = st
            mid = lo_k + lax.shift_right_logical(hi_k - lo_k, 1)
            cnt = count(_unsortable(mid))
            ge = cnt >= kf
            lo_k = jnp.where(ge, mid, lo_k)
            hi_k = jnp.where(ge, hi_k, mid)
            c_lo = jnp.where(ge, cnt, c_lo)
            return it + 1, lo_k, hi_k, c_lo, unresolved(lo_k, hi_k, c_lo)

        c0 = jnp.full((tq, LANES), -1.0, F32)
        st = lax.while_loop(cond, body, (jnp.int32(0), lo0, hi0, c0, unresolved(lo0, hi0, c0)))
        return st[1], st[3]

    nsl = nkb * spb
    nchunk = (nsl + CAND - 1) // CAND
    pad_tile = jnp.full((tq, LANES), PAD_SCORE, F32)

    def pad(sidx, carry):
        sc_s[sidx] = pad_tile
        return carry

    lax.fori_loop(nsl, nchunk * CAND, pad, 0)

    def gather_rows(rg, carry):
        r0 = pl.multiple_of(rg * 8, 8)

        def chunk(ch, cand):
            new = _sort_desc([sc_s[ch * CAND + i, pl.ds(r0, 8), :] for i in range(CAND)])
            return tuple(_bitonic_merge_desc([jnp.maximum(cand[i], new[CAND - 1 - i]) for i in range(CAND)]))

        start = tuple(jnp.full((8, LANES), PAD_SCORE, F32) for _ in range(CAND))
        cand = lax.fori_loop(0, nchunk, chunk, start)
        for i in range(CAND):
            cand_s[i, pl.ds(r0, 8), :] = cand[i]
        return carry

    lax.fori_loop(0, tq // 8, gather_rows, 0)

    thr_c, cnt_c = bisect(count_cand)
    thr_s[...] = thr_c
    cnt_s[...] = cnt_c
    last = jnp.max(cand_s[CAND - 1], axis=1, keepdims=True)
    covered = jnp.logical_or(last < _unsortable(thr_c), last <= 0.5 * NEG_INF)

    @pl.when(jnp.min(jnp.where(covered, 1.0, 0.0)) == 0.0)
    def _():
        thr_a, cnt_a = bisect(count_all)
        thr_s[...] = thr_a
        cnt_s[...] = cnt_a

    thr = _unsortable(thr_s[...])
    c_thr = cnt_s[...]

    tied = jnp.max(jnp.where(jnp.logical_and(c_thr != kf, thr > 0.5 * NEG_INF), 1.0, 0.0)) > 0.0

    @pl.when(jnp.logical_not(tied))
    def _():
        def emit(kb, carry):
            for c in range(spb):
                s = sc_s[kb * spb + c]
                sel = jnp.logical_and(s >= thr, s > 0.5 * NEG_INF)
                bias_ref[kb * spb + c] = jnp.where(sel, 0.0, NEG_INF).astype(BF16)
            return carry

        lax.fori_loop(0, nkb, emit, 0)

    @pl.when(tied)
    def _():
        need = kf - count_all(thr, strict=True)
        tri = (lax.broadcasted_iota(jnp.int32, (LANES, LANES), 0)
               <= lax.broadcasted_iota(jnp.int32, (LANES, LANES), 1)).astype(BF16)

        def emit(kb, seen):
            for c in range(spb):
                s = sc_s[kb * spb + c]
                eq = jnp.where(s == thr, 1.0, 0.0)
                rank = seen + jnp.dot(eq.astype(BF16), tri, preferred_element_type=F32)
                keep = jnp.logical_or(s > thr, jnp.logical_and(s == thr, rank <= need))
                sel = jnp.logical_and(keep, s > 0.5 * NEG_INF)
                bias_ref[kb * spb + c] = jnp.where(sel, 0.0, NEG_INF).astype(BF16)
                seen = seen + jnp.sum(eq, axis=1, keepdims=True)
            return seen

        lax.fori_loop(0, nkb, emit, zeros)

    def fill(kb, carry):
        for c in range(spb):
            bias_ref[kb * spb + c] = jnp.full((tq, LANES), NEG_INF, BF16)
        return carry

    lax.fori_loop(nkb, nkb_total, fill, 0)


def _select(qi, kiwi, kia, kib, *, tq, spb, topk, causal, n_valid, q_pos0):
    t = qi.shape[0]
    nb = t // tq
    lk = kia.shape[1]
    nslab = lk // LANES
    nkb_total = nslab // spb
    assert topk <= 2 * LANES and nslab >= 2, "the bisection's starting lower bound needs two keys per lane"

    def kmap(n):
        return (0 if causal else n, 0, 0)

    return pl.pallas_call(
        functools.partial(_select_kernel, tq=tq, spb=spb, nkb_total=nkb_total, topk=topk,
                          causal=causal, n_valid=n_valid, q_pos0=q_pos0),
        grid=(nb,),
        in_specs=[
            pl.BlockSpec((tq, IQ_W), lambda n: (n, 0)),
            pl.BlockSpec((tq, LANES), lambda n: (n, 0)),
            pl.BlockSpec((None, lk, LANES), kmap),
            pl.BlockSpec((None, lk, LANES), kmap),
        ],
        out_specs=pl.BlockSpec((nslab, tq, LANES), lambda n: (0, n, 0)),
        out_shape=jax.ShapeDtypeStruct((nslab, t, LANES), BF16),
        scratch_shapes=[
            pltpu.VMEM((nslab + spb + CAND, tq, LANES), F32),
            pltpu.VMEM((N_IDX_HEADS, tq, LANES), F32),
            pltpu.VMEM((2, N_IDX_HEADS // 2 * tq, spb * LANES), F32),
            pltpu.VMEM((2, N_IDX_HEADS // 2 * tq, spb * LANES), F32),
            pltpu.VMEM((CAND, tq, LANES), F32),
            pltpu.VMEM((tq, LANES), jnp.int32),
            pltpu.VMEM((tq, LANES), F32),
        ],
        compiler_params=pltpu.CompilerParams(
            dimension_semantics=("arbitrary",), vmem_limit_bytes=VMEM_LIMIT),
        name="select",
    )(qi, kiwi, kia, kib)


def _attend_kernel(qb_ref, kb_ref, kbat_ref, last_ref, q_ref, k_ref, v_ref, b_ref, o_ref,
                   m_s, l_s, acc_s, *, tq, spb):
    s = pl.program_id(0)

    @pl.when(kb_ref[s] == 0)
    def _():
        m_s[...] = jnp.full(m_s.shape, 0.1 * NEG_INF, F32)
        l_s[...] = jnp.zeros(l_s.shape, F32)
        acc_s[...] = jnp.zeros(acc_s.shape, F32)

    biases = [b_ref[c].astype(F32)[None] for c in range(spb)]
    ones = jnp.ones((spb * LANES, LANES), BF16)

    for g in range(N_KV_HEADS):
        qg = [q_ref[:, (g * GROUP + h) * HEAD_DIM:(g * GROUP + h + 1) * HEAD_DIM] for h in range(GROUP)]
        q4 = jnp.concatenate(qg, axis=0)
        kg = k_ref[:, g * HEAD_DIM:(g + 1) * HEAD_DIM]
        sc = lax.dot_general(q4, kg, (((1,), (1,)), ((), ())), preferred_element_type=F32)
        slabs = []
        mx = None
        for c in range(spb):
            sl = sc[:, c * LANES:(c + 1) * LANES].reshape(GROUP, tq, LANES) + biases[c]
            sl = sl.reshape(GROUP * tq, LANES)
            slabs.append(sl)
            mx = sl if mx is None else jnp.maximum(mx, sl)
        m_prev = m_s[g]
        m_new = jnp.maximum(m_prev, jnp.max(mx, axis=1, keepdims=True))
        alpha = jnp.exp2(m_prev - m_new)
        pmat = jnp.concatenate([jnp.exp2(sl - m_new).astype(BF16) for sl in slabs], axis=1)
        v1 = jnp.concatenate([v_ref[:, g * HEAD_DIM:(g + 1) * HEAD_DIM], ones], axis=1)
        pv = jnp.dot(pmat, v1, preferred_element_type=F32)
        acc_s[g] = alpha * acc_s[g] + pv[:, 0:HEAD_DIM]
        l_s[g] = alpha * l_s[g] + pv[:, HEAD_DIM:HEAD_DIM + LANES]
        m_s[g] = m_new

    @pl.when(last_ref[s] == 1)
    def _():
        for g in range(N_KV_HEADS):
            o = acc_s[g] / l_s[g]
            for h in range(GROUP):
                col = (g * GROUP + h) * HEAD_DIM
                o_ref[:, col:col + HEAD_DIM] = o[h * tq:(h + 1) * tq].astype(BF16)


def _attend(q, k_all, v_all, bias, sched, *, tq, spb):
    t = q.shape[0]
    qb, kb, kbat, last = sched
    nsteps = qb.shape[0]
    blk = spb * LANES
    grid_spec = pltpu.PrefetchScalarGridSpec(
        num_scalar_prefetch=4,
        grid=(nsteps,),
        in_specs=[
            pl.BlockSpec((tq, Q_W), lambda s, qb, kb, kbat, last: (qb[s], 0)),
            pl.BlockSpec((None, blk, KV_W), lambda s, qb, kb, kbat, last: (kbat[s], kb[s], 0)),
            pl.BlockSpec((None, blk, KV_W), lambda s, qb, kb, kbat, last: (kbat[s], kb[s], 0)),
            pl.BlockSpec((spb, tq, LANES), lambda s, qb, kb, kbat, last: (kb[s], qb[s], 0)),
        ],
        out_specs=pl.BlockSpec((tq, Q_W), lambda s, qb, kb, kbat, last: (qb[s], 0)),
        scratch_shapes=[
            pltpu.VMEM((N_KV_HEADS, GROUP * tq, LANES), F32),
            pltpu.VMEM((N_KV_HEADS, GROUP * tq, LANES), F32),
            pltpu.VMEM((N_KV_HEADS, GROUP * tq, HEAD_DIM), F32),
        ],
    )
    return pl.pallas_call(
        functools.partial(_attend_kernel, tq=tq, spb=spb),
        grid_spec=grid_spec,
        out_shape=jax.ShapeDtypeStruct((t, Q_W), BF16),
        compiler_params=pltpu.CompilerParams(
            dimension_semantics=("arbitrary",), vmem_limit_bytes=VMEM_LIMIT),
        name="attend",
    )(qb, kb, kbat, last, q, k_all, v_all, bias)


def _causal_schedule(t, tq, blk):
    qb, kb, last = [], [], []
    for n in range(t // tq):
        nk = ((n + 1) * tq + blk - 1) // blk
        for k in range(nk):
            qb.append(n)
            kb.append(k)
            last.append(1 if k == nk - 1 else 0)
    z = np.zeros(len(qb), np.int32)
    return (jnp.asarray(qb, jnp.int32), jnp.asarray(kb, jnp.int32), jnp.asarray(z), jnp.asarray(last, jnp.int32))


def _batched_schedule(nbatch, nk):
    qb = np.repeat(np.arange(nbatch, dtype=np.int32), nk)
    kb = np.tile(np.arange(nk, dtype=np.int32), nbatch)
    last = (kb == nk - 1).astype(np.int32)
    return (jnp.asarray(qb), jnp.asarray(kb), jnp.asarray(qb), jnp.asarray(last))


def _merge_kernel(o_ref, yc_ref, ga_ref, gc_ref, woa_ref, woc_ref, wout_ref, x_ref, gq_ref, gp_ref,
                  x1_ref, xn2_ref, mg_s, m_s):
    j = pl.program_id(1)
    nblk = D_MODEL // MERGE_TN

    @pl.when(j < nblk)
    def _():
        a = jnp.dot(o_ref[...], woa_ref[...], preferred_element_type=F32)
        c = jnp.dot(yc_ref[...], woc_ref[...], preferred_element_type=F32)
        mg_s[j] = (ga_ref[...] * a + gc_ref[...] * c).astype(BF16)

    @pl.when(j >= nblk)
    def _():
        mg = jnp.concatenate([mg_s[b] for b in range(nblk)], axis=1)
        m_s[j - nblk] = jnp.dot(mg, wout_ref[...], preferred_element_type=F32)

    @pl.when(j == 2 * nblk - 1)
    def _():
        m = jnp.concatenate([m_s[b] for b in range(nblk)], axis=1)
        x1 = x_ref[...] + _rms(m, gq_ref[...])
        x1_ref[...] = x1
        xn2_ref[...] = _rms(x1, gp_ref[...]).astype(BF16)


def _merge(o, yc, ga, gc, woa, woc, wout, x, gq, gp):
    t = x.shape[0]
    tn = MERGE_TN
    nblk = D_MODEL // tn

    def row(i, j):
        return (i, 0)

    def lo(i, j):
        return (i, jnp.minimum(j, nblk - 1))

    return pl.pallas_call(
        _merge_kernel,
        grid=(t // TM, 2 * nblk),
        in_specs=[
            pl.BlockSpec((TM, Q_W), row),
            pl.BlockSpec((TM, D_MODEL), row),
            pl.BlockSpec((TM, tn), lo),
            pl.BlockSpec((TM, tn), lo),
            pl.BlockSpec((Q_W, tn), lambda i, j: (0, jnp.minimum(j, nblk - 1))),
            pl.BlockSpec((D_MODEL, tn), lambda i, j: (0, jnp.minimum(j, nblk - 1))),
            pl.BlockSpec((D_MODEL, tn), lambda i, j: (0, jnp.maximum(j - nblk, 0))),
            pl.BlockSpec((TM, D_MODEL), row),
            pl.BlockSpec((1, D_MODEL), lambda i, j: (0, 0)),
            pl.BlockSpec((1, D_MODEL), lambda i, j: (0, 0)),
        ],
        out_specs=[pl.BlockSpec((TM, D_MODEL), row), pl.BlockSpec((TM, D_MODEL), row)],
        out_shape=[jax.ShapeDtypeStruct((t, D_MODEL), F32), jax.ShapeDtypeStruct((t, D_MODEL), BF16)],
        scratch_shapes=[pltpu.VMEM((nblk, TM, tn), BF16), pltpu.VMEM((nblk, TM, tn), F32)],
        compiler_params=pltpu.CompilerParams(
            dimension_semantics=("arbitrary", "arbitrary"), vmem_limit_bytes=VMEM_LIMIT),
        name="merge",
    )(o, yc, ga, gc, woa, woc, wout, x, gq, gp)


def _ffn_kernel(xn_ref, wg_ref, wv_ref, cwg_ref, cwv_ref, wd_ref, x1_ref, gq_ref, stg_ref, stv_ref,
                y_ref, ng_ref, nv_ref, acc_s, cg_s, cv_s, *, seg, carried):
    i = pl.program_id(0)
    jf = pl.program_id(1)
    nseg = TM // seg

    @pl.when(jf == 0)
    def _():
        acc_s[...] = jnp.zeros(acc_s.shape, F32)

    if carried:
        @pl.when(i == 0)
        def _():
            for carry_s, st_ref in ((cg_s, stg_ref), (cv_s, stv_ref)):
                carry_s[jf] = jnp.zeros((8, TF), F32)
                carry_s[jf, 0:2, :] = st_ref[0]

    xn = xn_ref[...]

    def branch(w_ref, cw_ref, st_ref, carry_s, new_ref):
        up = jnp.dot(xn, w_ref[...], preferred_element_type=F32)
        if carried:
            prevs = [carry_s[jf, 0:2, :]]
        else:
            prevs = [st_ref[s] for s in range(nseg)]
        y, news = _conv_tile(up, cw_ref[...], prevs, seg)
        for s in range(nseg):
            new_ref[s] = news[s]
        if carried:
            carry_s[jf, 0:2, :] = news[0]
        return y

    gate = branch(wg_ref, cwg_ref, stg_ref, cg_s, ng_ref)
    val = branch(wv_ref, cwv_ref, stv_ref, cv_s, nv_ref)
    c0 = np.float32(np.sqrt(2.0 / np.pi))
    gelu = 0.5 * gate * (1.0 + jnp.tanh(c0 * (gate + 0.044715 * (gate * gate * gate))))
    hid = (gelu * val).astype(BF16)
    acc_s[...] += jnp.dot(hid, wd_ref[...], preferred_element_type=F32)

    @pl.when(jf == pl.num_programs(1) - 1)
    def _():
        y_ref[...] = x1_ref[...] + _rms(acc_s[...], gq_ref[...])


def _ffn(xn2, w_up, cw, w_down, x1, gq, state, seg):
    t = x1.shape[0]
    nf = D_FF // TF
    carried = seg == t
    nseg_tile = 1 if carried else TM // seg

    def row(i, j):
        return (i, 0)

    def stg(i, j):
        return (0 if carried else i, 0, j)

    def stv(i, j):
        return (0 if carried else i, 0, j + nf)

    return pl.pallas_call(
        functools.partial(_ffn_kernel, seg=min(seg, TM), carried=carried),
        grid=(t // TM, nf),
        in_specs=[
            pl.BlockSpec((TM, D_MODEL), row),
            pl.BlockSpec((D_MODEL, TF), lambda i, j: (0, j)),
            pl.BlockSpec((D_MODEL, TF), lambda i, j: (0, j + nf)),
            pl.BlockSpec((3, TF), lambda i, j: (0, j)),
            pl.BlockSpec((3, TF), lambda i, j: (0, j + nf)),
            pl.BlockSpec((TF, D_MODEL), lambda i, j: (j, 0)),
            pl.BlockSpec((TM, D_MODEL), row),
            pl.BlockSpec((1, D_MODEL), lambda i, j: (0, 0)),
            pl.BlockSpec((nseg_tile, 2, TF), stg),
            pl.BlockSpec((nseg_tile, 2, TF), stv),
        ],
        out_specs=[
            pl.BlockSpec((TM, D_MODEL), row),
            pl.BlockSpec((nseg_tile, 2, TF), lambda i, j: (i, 0, j)),
            pl.BlockSpec((nseg_tile, 2, TF), lambda i, j: (i, 0, j)),
        ],
        out_shape=[
            jax.ShapeDtypeStruct((t, D_MODEL), F32),
            jax.ShapeDtypeStruct((t // TM * nseg_tile, 2, D_FF), F32),
            jax.ShapeDtypeStruct((t // TM * nseg_tile, 2, D_FF), F32),
        ],
        scratch_shapes=[
            pltpu.VMEM((TM, D_MODEL), F32),
            pltpu.VMEM((nf, 8, TF), F32),
            pltpu.VMEM((nf, 8, TF), F32),
        ],
        compiler_params=pltpu.CompilerParams(
            dimension_semantics=("arbitrary", "arbitrary"), vmem_limit_bytes=VMEM_LIMIT),
        name="ffn",
    )(xn2, w_up, w_up, cw, cw, w_down, x1, gq, state, state)


PACK_ROWS = 1024


def _pack_kernel(ck_ref, cv_ref, kt_ref, vt_ref, ko_ref, vo_ref, *, rows):
    del kt_ref, vt_ref
    for src, dst in ((ck_ref, ko_ref), (cv_ref, vo_ref)):
        for g in range(N_KV_HEADS):
            dst[:, g * HEAD_DIM:(g + 1) * HEAD_DIM] = src[pl.ds(g, rows, stride=N_KV_HEADS), :].astype(BF16)


def _pack_cache(cache_k, cache_v, k_tail, v_tail):
    nb, plen = cache_k.shape[0], cache_k.shape[1]
    rows = int(np.gcd(plen, PACK_ROWS))
    assert rows % 16 == 0
    cspec = pl.BlockSpec((None, rows * N_KV_HEADS, HEAD_DIM), lambda b, r: (b, r, 0))
    ospec = pl.BlockSpec((None, rows, KV_W), lambda b, r: (b, r, 0))
    anyspec = pl.BlockSpec(memory_space=pl.ANY)
    flat = (nb, plen * N_KV_HEADS, HEAD_DIM)
    return pl.pallas_call(
        functools.partial(_pack_kernel, rows=rows),
        grid=(nb, plen // rows),
        in_specs=[cspec, cspec, anyspec, anyspec],
        out_specs=[ospec, ospec],
        out_shape=[jax.ShapeDtypeStruct(k_tail.shape, BF16), jax.ShapeDtypeStruct(v_tail.shape, BF16)],
        input_output_aliases={2: 0, 3: 1},
        compiler_params=pltpu.CompilerParams(
            dimension_semantics=("arbitrary", "arbitrary"), vmem_limit_bytes=VMEM_LIMIT),
        name="pack_cache",
    )(cache_k.reshape(flat), cache_v.reshape(flat), k_tail, v_tail)


def _rope_tables(pos, rot, width):
    half = rot // 2
    freqs = ROPE_THETA ** (-jnp.arange(half, dtype=F32) / half)
    ang = pos.astype(F32)[:, None] * freqs[None, :]
    cos, sin = jnp.cos(ang), jnp.sin(ang)
    t = pos.shape[0]
    c = jnp.concatenate([cos, cos, jnp.ones((t, width - rot), F32)], axis=1)
    d = jnp.concatenate([-sin, sin, jnp.zeros((t, width - rot), F32)], axis=1)
    reps = LANES // width
    return jnp.tile(c, (1, reps)), jnp.tile(d, (1, reps))


def _relayout_w_in(w):
    o = np.cumsum([0, Q_W, KV_W, KV_W, IQ_W, IDX_DIM, N_IDX_HEADS, D_MODEL, D_MODEL, D_MODEL, D_MODEL, D_MODEL])
    q, k, v, qi = (w[:, o[a]:o[a + 1]] for a in range(4))
    kiwi = w[:, o[4]:o[6]]
    cb, cc, ch, ga, gc = (w[:, o[a]:o[a + 1]] for a in range(6, 11))
    parts = [q, k, v, qi, kiwi, jnp.zeros((D_MODEL, TN - kiwi.shape[1]), w.dtype)]
    for c in range(N_CBLK):
        sl = slice(c * TN, (c + 1) * TN)
        parts += [cb[:, sl], cc[:, sl], ch[:, sl]]
    parts += [ga, gc]
    return jnp.concatenate(parts, axis=1).astype(BF16)


def _stream(x, pos, seg, past, weights, *, tq, spb_sel, spb_att):
    (g_mp, g_mq, w_in_r, conv_w, woa, woc, wout, g_fp, g_fq, w_up, fconv_w, w_down) = weights
    t = x.shape[0]
    nseq = t // seg
    tabs = _rope_tables(pos, ROT_DIM, HEAD_DIM) + _rope_tables(pos, IDX_ROT_DIM, IDX_DIM)
    if past is None:
        conv_state = jnp.zeros((nseq, 2, D_MODEL), F32)
        ffn_state = jnp.zeros((nseq, 2, 2 * D_FF), F32)
    else:
        conv_state, ffn_state = past[3], past[4]

    (q, k, kb, v, vb, qi, kiwi, kia, kib, yc, conv_new, ga, gc) = _proj(
        x, g_mp, w_in_r, tabs, conv_w, conv_state, seg)

    if past is None:
        k_all, v_all = kb[None], vb[None]
        kia_all, kib_all = kia[None], kib[None]
        n_keys = t
        sched = _causal_schedule(t, tq, spb_att * LANES)
        q_pos0 = 0
    else:
        cache_k, cache_v, cache_ki = past[0], past[1], past[2]
        plen = cache_k.shape[1]
        n_keys = plen + seg
        lk = -(-n_keys // (spb_att * LANES)) * (spb_att * LANES)
        pad = lk - n_keys

        def cat(c, new, width):
            parts = [c, new.reshape(nseq, seg, width)]
            if pad:
                parts.append(jnp.zeros((nseq, pad, width), BF16))
            return jnp.concatenate(parts, axis=1)

        def tail(new):
            return jnp.pad(new.reshape(nseq, seg, KV_W), ((0, 0), (plen, pad), (0, 0)))

        k_all, v_all = _pack_cache(cache_k, cache_v, tail(kb), tail(vb))
        cki = cache_ki.astype(BF16)
        zk = jnp.zeros_like(cki)
        kia_all = cat(jnp.concatenate([cki, zk], axis=-1), kia, LANES)
        kib_all = cat(jnp.concatenate([zk, cki], axis=-1), kib, LANES)
        sched = _batched_schedule(nseq, lk // (spb_att * LANES))
        q_pos0 = plen
    topk = min(TOPK_MAX, n_keys // 4)

    bias = _select(qi, kiwi, kia_all, kib_all, tq=tq, spb=spb_sel, topk=topk,
                   causal=past is None, n_valid=n_keys, q_pos0=q_pos0)
    o = _attend(q, k_all, v_all, bias, sched, tq=tq, spb=spb_att)
    x1, xn2 = _merge(o, yc, ga, gc, woa, woc, wout, x, g_mq, g_fp)
    y, ffn_g, ffn_v = _ffn(xn2, w_up, fconv_w, w_down, x1, g_fq, ffn_state, seg)
    ffn_new = jnp.concatenate([ffn_g, ffn_v], axis=-1)
    return y, k, v, kiwi[:, :IDX_DIM], conv_new[-nseq:], ffn_new[-nseq:]


def kernel(x_prompt, x_sample, cache_k, cache_v, cache_k_idx, state_conv, state_ffn_conv, norm_mix_pre, norm_mix_post, w_in, conv_w, w_o_attn, w_o_conv, w_out, norm_ffn_pre, norm_ffn_post, w_ffn_up, ffn_conv_w, w_ffn_down):
    depth = w_in.shape[0]
    assert depth == 1, "single-layer step"
    b, seq, _ = x_prompt.shape
    assert b == 1
    db, dseq, _ = x_sample.shape
    plen = cache_k.shape[2]
    assert dseq == CHUNK and plen % CHUNK == 0

    weights = (
        norm_mix_pre, norm_mix_post, _relayout_w_in(w_in[0]), conv_w[0],
        w_o_attn[0].astype(BF16), w_o_conv[0].astype(BF16), w_out[0].astype(BF16),
        norm_ffn_pre, norm_ffn_post, w_ffn_up[0].astype(BF16), ffn_conv_w[0], w_ffn_down[0].astype(BF16),
    )

    pos_p = jnp.arange(seq, dtype=jnp.int32)
    yp, kp, vp, kip, convp, ffnp = _stream(
        x_prompt.reshape(seq, D_MODEL), pos_p, seq, None, weights, tq=128, spb_sel=4, spb_att=8)

    pos_s = jnp.tile(jnp.arange(dseq, dtype=jnp.int32) + plen, db)
    past = (cache_k[0], cache_v[0], cache_k_idx[0], state_conv[0], state_ffn_conv[0])
    n_keys = plen + dseq
    spb_s = _sample_slabs(n_keys)
    ys, ks, vs, kis, convs, ffns = _stream(
        x_sample.reshape(db * dseq, D_MODEL), pos_s, dseq, past, weights, tq=dseq, spb_sel=spb_s, spb_att=spb_s)

    return (
        yp.reshape(1, seq, D_MODEL), ys.reshape(db, dseq, D_MODEL),
        kp.reshape(1, 1, seq, N_KV_HEADS, HEAD_DIM), vp.reshape(1, 1, seq, N_KV_HEADS, HEAD_DIM),
        kip.reshape(1, 1, seq, IDX_DIM), convp.reshape(1, 1, 2, D_MODEL), ffnp.reshape(1, 1, 2, 2 * D_FF),
        ks.reshape(1, db, dseq, N_KV_HEADS, HEAD_DIM), vs.reshape(1, db, dseq, N_KV_HEADS, HEAD_DIM),
        kis.reshape(1, db, dseq, IDX_DIM), convs.reshape(1, db, 2, D_MODEL), ffns.reshape(1, db, 2, 2 * D_FF),
    )


def _sample_slabs(n_keys):
    nslab = -(-n_keys // LANES)
    best = 1
    for d in range(1, nslab + 1):
        if nslab % d == 0 and d <= 11:
            best = d
    return best
```

```python
import functools

import jax
import jax.numpy as jnp
import numpy as np
from jax import lax
from jax.experimental import pallas as pl
from jax.experimental.pallas import tpu as pltpu

F32 = jnp.float32
BF16 = jnp.bfloat16

D_MODEL = 2048
N_HEADS = 16
N_KV_HEADS = 4
HEAD_DIM = 128
ROT_DIM = HEAD_DIM // 4
N_IDX_HEADS = 16
IDX_DIM = 64
IDX_ROT_DIM = IDX_DIM // 4
CHUNK = 64
TOPK_MAX = 256
ROPE_THETA = 500000.0
D_FF = 5632
RMS_EPS = 1e-6
NEG_INF = -1e30
Q_W = N_HEADS * HEAD_DIM
KV_W = N_KV_HEADS * HEAD_DIM
IQ_W = N_IDX_HEADS * IDX_DIM
GROUP = N_HEADS // N_KV_HEADS

LANES = 128
TM = 512
TN = 1024
MERGE_TN = 512
TF = 512
VMEM_LIMIT = 56 * 1024 * 1024

J_Q = 0
J_KV = J_Q + Q_W // TN
assert 2 * KV_W == TN
J_QI = J_KV + 1
J_KIWI = J_QI + IQ_W // TN
J_CONV = J_KIWI + 1
N_CBLK = D_MODEL // TN
J_GA = J_CONV + 3 * N_CBLK
J_GC = J_GA + N_CBLK
NJ = J_GC + N_CBLK

Q_PRESCALE = float(np.float32((HEAD_DIM ** -0.5) * np.log2(np.e)))


def _rms(x, g):
    return x * lax.rsqrt(jnp.mean(x * x, axis=-1, keepdims=True) + RMS_EPS) * g


def _sortable(x):
    bits = lax.bitcast_convert_type(x, jnp.int32)
    return jnp.where(bits < 0, bits ^ jnp.int32(0x7FFFFFFF), bits)


def _unsortable(k):
    return lax.bitcast_convert_type(jnp.where(k < 0, k ^ jnp.int32(0x7FFFFFFF), k), F32)


CAND = 16
PAD_SCORE = -3.0e38


def _oddeven_merge_sort_pairs(n):
    out, p = [], 1
    while p < n:
        k = p
        while k >= 1:
            for j in range(k % p, n - k, 2 * k):
                for i in range(min(k, n - j - k)):
                    if (i + j) // (2 * p) == (i + j + k) // (2 * p):
                        out.append((i + j, i + j + k))
            k //= 2
        p *= 2
    return out


_SORT_PAIRS = _oddeven_merge_sort_pairs(CAND)


def _sort_desc(v):
    v = list(v)
    for i, j in _SORT_PAIRS:
        v[i], v[j] = jnp.maximum(v[i], v[j]), jnp.minimum(v[i], v[j])
    return v


def _bitonic_merge_desc(v):
    v = list(v)
    d = len(v) // 2
    while d >= 1:
        for i in range(len(v)):
            if not i & d:
                v[i], v[i + d] = jnp.maximum(v[i], v[i + d]), jnp.minimum(v[i], v[i + d])
        d //= 2
    return v


def _dwconv_seg(u, prev2, w3):
    row = lax.broadcasted_iota(jnp.int32, u.shape, 0)
    p0, p1 = prev2[0:1], prev2[1:2]
    s1 = jnp.where(row == 0, p1, pltpu.roll(u, 1, 0))
    s2 = jnp.where(row == 0, p0, jnp.where(row == 1, p1, pltpu.roll(u, 2, 0)))
    return w3[0:1] * s2 + w3[1:2] * s1 + w3[2:3] * u


def _conv_tile(u, w3, prevs, seg):
    nseg = len(prevs)
    ys, news = [], []
    for s in range(nseg):
        us = u[s * seg:(s + 1) * seg]
        ys.append(_dwconv_seg(us, prevs[s], w3))
        news.append(us[seg - 2:seg])
    y = ys[0] if nseg == 1 else jnp.concatenate(ys, axis=0)
    return y, news


def _proj_kernel(x_ref, g_ref, w_ref, c128_ref, d128_ref, c64_ref, d64_ref, cw_ref, st_ref,
                 q_ref, k_ref, kb_ref, v_ref, vb_ref, qi_ref, kiwi_ref, kia_ref, kib_ref,
                 yc_ref, cn_ref, ga_ref, gc_ref,
                 xn_s, cb_s, cc_s, carry_s, *, seg, carried):
    i = pl.program_id(0)
    j = pl.program_id(1)

    @pl.when(j == 0)
    def _():
        xn_s[...] = _rms(x_ref[...], g_ref[...]).astype(BF16)

    def mm():
        return jnp.dot(xn_s[...], w_ref[...], preferred_element_type=F32)

    lane = lax.broadcasted_iota(jnp.int32, (TM, LANES), 1)

    def rope128(xh):
        partner = jnp.where(lane < ROT_DIM // 2, pltpu.roll(xh, LANES - ROT_DIM // 2, 1),
                            pltpu.roll(xh, ROT_DIM // 2, 1))
        return xh * c128_ref[...] + partner * d128_ref[...]

    def rope64(xh, c, d):
        first = (lane & (IDX_DIM - 1)) < IDX_ROT_DIM // 2
        partner = jnp.where(first, pltpu.roll(xh, LANES - IDX_ROT_DIM // 2, 1),
                            pltpu.roll(xh, IDX_ROT_DIM // 2, 1))
        return xh * c + partner * d

    @pl.when(j < J_KV)
    def _():
        acc = mm()
        for h in range(TN // LANES):
            sl = slice(h * LANES, (h + 1) * LANES)
            q_ref[:, sl] = (rope128(acc[:, sl]) * Q_PRESCALE).astype(BF16)

    @pl.when(j == J_KV)
    def _():
        acc = mm()
        for h in range(KV_W // LANES):
            sl = slice(h * LANES, (h + 1) * LANES)
            r = rope128(acc[:, sl])
            k_ref[:, sl] = r
            kb_ref[:, sl] = r.astype(BF16)
        v = acc[:, KV_W:2 * KV_W]
        v_ref[...] = v
        vb_ref[...] = v.astype(BF16)

    @pl.when(jnp.logical_and(j >= J_QI, j < J_KIWI))
    def _():
        acc = mm()
        for h in range(TN // LANES):
            sl = slice(h * LANES, (h + 1) * LANES)
            qi_ref[:, sl] = rope64(acc[:, sl], c64_ref[...], d64_ref[...]).astype(BF16)

    @pl.when(j == J_KIWI)
    def _():
        is_ki = lane < IDX_DIM
        c = jnp.where(is_ki, c64_ref[...], 1.0)
        d = jnp.where(is_ki, d64_ref[...], 0.0)
        acc = jnp.dot(xn_s[...], w_ref[:, 0:LANES], preferred_element_type=F32)
        r = rope64(acc, c, d)
        kiwi_ref[...] = r
        ka = jnp.where(is_ki, r, 0.0)
        kia_ref[...] = ka.astype(BF16)
        kib_ref[...] = pltpu.roll(ka, IDX_DIM, 1).astype(BF16)

    jc = jnp.maximum(j - J_CONV, 0)
    in_conv = jnp.logical_and(j >= J_CONV, j < J_GA)
    cblk = jnp.minimum(jc // 3, N_CBLK - 1)
    part = jc % 3

    @pl.when(jnp.logical_and(in_conv, part == 0))
    def _():
        cb_s[...] = mm()

    @pl.when(jnp.logical_and(in_conv, part == 1))
    def _():
        cc_s[...] = mm()

    @pl.when(jnp.logical_and(in_conv, part == 2))
    def _():
        nseg = TM // seg
        if carried:
            @pl.when(i == 0)
            def _():
                carry_s[cblk] = jnp.zeros((8, TN), F32)
                carry_s[cblk, 0:2, :] = st_ref[0]
            prevs = [carry_s[cblk, 0:2, :]]
        else:
            prevs = [st_ref[s] for s in range(nseg)]
        u = cc_s[...] * mm()
        y, news = _conv_tile(u, cw_ref[...], prevs, seg)
        yc_ref[...] = (cb_s[...] * y).astype(BF16)
        for s in range(nseg):
            cn_ref[s] = news[s]
        if carried:
            carry_s[cblk, 0:2, :] = news[0]

    @pl.when(jnp.logical_and(j >= J_GA, j < J_GC))
    def _():
        ga_ref[...] = 1.0 / (1.0 + jnp.exp(-mm()))

    @pl.when(j >= J_GC)
    def _():
        gc_ref[...] = 1.0 / (1.0 + jnp.exp(-mm()))


def _proj(x, g, w, tabs, conv_w, state, seg):
    t = x.shape[0]
    ni = t // TM
    carried = seg == t
    nseg_tile = 1 if carried else TM // seg
    c128, d128, c64, d64 = tabs

    def row(i, j):
        return (i, 0)

    def const(i, j):
        return (0, 0)

    def cblk(j):
        return jnp.minimum(jnp.maximum(j - J_CONV, 0) // 3, N_CBLK - 1)

    def st_map(i, j):
        return (0 if carried else i, 0, cblk(j))

    in_specs = [
        pl.BlockSpec((TM, D_MODEL), row),
        pl.BlockSpec((1, D_MODEL), const),
        pl.BlockSpec((D_MODEL, TN), lambda i, j: (0, j)),
        pl.BlockSpec((TM, LANES), row),
        pl.BlockSpec((TM, LANES), row),
        pl.BlockSpec((TM, LANES), row),
        pl.BlockSpec((TM, LANES), row),
        pl.BlockSpec((3, TN), lambda i, j: (0, cblk(j))),
        pl.BlockSpec((nseg_tile, 2, TN), st_map),
    ]
    out_shape = [
        jax.ShapeDtypeStruct((t, Q_W), BF16),
        jax.ShapeDtypeStruct((t, KV_W), F32),
        jax.ShapeDtypeStruct((t, KV_W), BF16),
        jax.ShapeDtypeStruct((t, KV_W), F32),
        jax.ShapeDtypeStruct((t, KV_W), BF16),
        jax.ShapeDtypeStruct((t, IQ_W), BF16),
        jax.ShapeDtypeStruct((t, LANES), F32),
        jax.ShapeDtypeStruct((t, LANES), BF16),
        jax.ShapeDtypeStruct((t, LANES), BF16),
        jax.ShapeDtypeStruct((t, D_MODEL), BF16),
        jax.ShapeDtypeStruct((ni * nseg_tile, 2, D_MODEL), F32),
        jax.ShapeDtypeStruct((t, D_MODEL), F32),
        jax.ShapeDtypeStruct((t, D_MODEL), F32),
    ]
    out_specs = [
        pl.BlockSpec((TM, TN), lambda i, j: (i, jnp.clip(j - J_Q, 0, J_KV - J_Q - 1))),
        pl.BlockSpec((TM, KV_W), row),
        pl.BlockSpec((TM, KV_W), row),
        pl.BlockSpec((TM, KV_W), row),
        pl.BlockSpec((TM, KV_W), row),
        pl.BlockSpec((TM, TN), lambda i, j: (i, jnp.clip(j - J_QI, 0, J_KIWI - J_QI - 1))),
        pl.BlockSpec((TM, LANES), row),
        pl.BlockSpec((TM, LANES), row),
        pl.BlockSpec((TM, LANES), row),
        pl.BlockSpec((TM, TN), lambda i, j: (i, cblk(j))),
        pl.BlockSpec((nseg_tile, 2, TN), lambda i, j: (i, 0, cblk(j))),
        pl.BlockSpec((TM, TN), lambda i, j: (i, jnp.clip(j - J_GA, 0, N_CBLK - 1))),
        pl.BlockSpec((TM, TN), lambda i, j: (i, jnp.clip(j - J_GC, 0, N_CBLK - 1))),
    ]
    return pl.pallas_call(
        functools.partial(_proj_kernel, seg=min(seg, TM), carried=carried),
        grid=(ni, NJ),
        in_specs=in_specs,
        out_specs=out_specs,
        out_shape=out_shape,
        scratch_shapes=[
            pltpu.VMEM((TM, D_MODEL), BF16),
            pltpu.VMEM((TM, TN), F32),
            pltpu.VMEM((TM, TN), F32),
            pltpu.VMEM((N_CBLK, 8, TN), F32),
        ],
        compiler_params=pltpu.CompilerParams(
            dimension_semantics=("arbitrary", "arbitrary"), vmem_limit_bytes=VMEM_LIMIT),
        name="proj",
    )(x, g, w, c128, d128, c64, d64, conv_w, state)


def _select_kernel(qi_ref, kiwi_ref, kia_ref, kib_ref, bias_ref, sc_s, wb_s, lg_s, lh_s, cand_s, thr_s, cnt_s, *,
                   tq, spb, nkb_total, topk, causal, n_valid, q_pos0):
    n = pl.program_id(0)
    npair = N_IDX_HEADS // 2
    blk = spb * LANES
    kiwi = kiwi_ref[...]
    wscale = (IDX_DIM ** -0.5) * (N_IDX_HEADS ** -0.5)
    for h in range(N_IDX_HEADS):
        wb_s[h] = jnp.broadcast_to(kiwi[:, IDX_DIM + h:IDX_DIM + h + 1], (tq, LANES)) * wscale
    q2 = jnp.concatenate([qi_ref[:, p * LANES:(p + 1) * LANES] for p in range(npair)], axis=0)

    if causal:
        qpos0 = n * tq
        nkb = (qpos0 + tq + blk - 1) // blk
    else:
        qpos0 = q_pos0
        nkb = nkb_total
    lane = lax.broadcasted_iota(jnp.int32, (tq, LANES), 1)
    qchunk = (qpos0 + lax.broadcasted_iota(jnp.int32, (tq, LANES), 0)) >> 6
    nt = (((1,), (1,)), ((), ()))

    def logits_into(lg_ref, kb):
        base = pl.multiple_of(jnp.minimum(kb, nkb_total - 1) * blk, blk)
        lg_ref[0] = lax.dot_general(q2, kia_ref[pl.ds(base, blk), :], nt, preferred_element_type=F32)
        lg_ref[1] = lax.dot_general(q2, kib_ref[pl.ds(base, blk), :], nt, preferred_element_type=F32)

    def head_sum(lg_ref, kb, carry):
        m1, m2 = carry
        base = kb * blk
        for c in range(spb):
            cs = slice(c * LANES, (c + 1) * LANES)
            acc = jnp.zeros((tq, LANES), F32)
            for p in range(npair):
                rs = slice(p * tq, (p + 1) * tq)
                acc = acc + jnp.maximum(lg_ref[0, rs, cs], 0.0) * wb_s[2 * p]
                acc = acc + jnp.maximum(lg_ref[1, rs, cs], 0.0) * wb_s[2 * p + 1]
            col = base + c * LANES + lane
            adm = jnp.logical_and((col >> 6) <= qchunk, col < n_valid)
            sc = jnp.where(adm, acc, NEG_INF)
            m2 = jnp.maximum(m2, jnp.minimum(m1, sc))
            m1 = jnp.maximum(m1, sc)
            sc_s[kb * spb + c] = sc
        return m1, m2

    logits_into(lg_s, 0)

    def score_pair(i, carry):
        logits_into(lh_s, 2 * i + 1)
        carry = head_sum(lg_s, 2 * i, carry)
        logits_into(lg_s, 2 * i + 2)
        return head_sum(lh_s, 2 * i + 1, carry)

    neg = jnp.full((tq, LANES), NEG_INF, F32)
    m1, m2 = lax.fori_loop(0, (nkb + 1) // 2, score_pair, (neg, neg))

    zeros = jnp.zeros((tq, LANES), F32)
    kf = float(topk)

    def count_all(thr_f, strict=False):
        def body(kb, acc):
            for c in range(spb):
                s = sc_s[kb * spb + c]
                acc = acc + jnp.where(s > thr_f if strict else s >= thr_f, 1.0, 0.0)
            return acc
        return jnp.sum(lax.fori_loop(0, nkb, body, zeros), axis=1, keepdims=True)

    def count_cand(thr_f):
        acc = zeros
        for i in range(CAND):
            acc = acc + jnp.where(cand_s[i] >= thr_f, 1.0, 0.0)
        return jnp.sum(acc, axis=1, keepdims=True)

    ones = jnp.ones((tq, LANES), jnp.int32)
    lo0 = _sortable(jnp.min(m2, axis=1, keepdims=True)) * ones
    hi0 = _sortable(jnp.max(m1, axis=1, keepdims=True)) * ones + 1

    def bisect(count):
        def unresolved(lo_k, hi_k, c_lo):
            open_ = jnp.logical_and(c_lo != kf, (hi_k - lo_k) != 1)
            return jnp.max(jnp.where(open_, 1.0, 0.0))

        def cond(st):
            return jnp.logical_and(st[0] < 33, st[-1] > 0.0)

        def body(st):
            it, lo_k, hi_k, c_lo, _ = st
            mid = lo_k + lax.shift_right_logical(hi_k - lo_k, 1)
            cnt = count(_unsortable(mid))
            ge = cnt >= kf
            lo_k = jnp.where(ge, mid, lo_k)
            hi_k = jnp.where(ge, hi_k, mid)
            c_lo = jnp.where(ge, cnt, c_lo)
            return it + 1, lo_k, hi_k, c_lo, unresolved(lo_k, hi_k, c_lo)

        c0 = jnp.full((tq, LANES), -1.0, F32)
        st = lax.while_loop(cond, body, (jnp.int32(0), lo0, hi0, c0, unresolved(lo0, hi0, c0)))
        return st[1], st[3]

    nsl = nkb * spb
    nchunk = (nsl + CAND - 1) // CAND
    pad_tile = jnp.full((tq, LANES), PAD_SCORE, F32)

    def pad(sidx, carry):
        sc_s[sidx] = pad_tile
        return carry

    lax.fori_loop(nsl, nchunk * CAND, pad, 0)

    def gather_rows(rg, carry):
        r0 = pl.multiple_of(rg * 8, 8)

        def chunk(ch, cand):
            new = _sort_desc([sc_s[ch * CAND + i, pl.ds(r0, 8), :] for i in range(CAND)])
            return tuple(_bitonic_merge_desc([jnp.maximum(cand[i], new[CAND - 1 - i]) for i in range(CAND)]))

        start = tuple(jnp.full((8, LANES), PAD_SCORE, F32) for _ in range(CAND))
        cand = lax.fori_loop(0, nchunk, chunk, start)
        for i in range(CAND):
            cand_s[i, pl.ds(r0, 8), :] = cand[i]
        return carry

    lax.fori_loop(0, tq // 8, gather_rows, 0)

    thr_c, cnt_c = bisect(count_cand)
    thr_s[...] = thr_c
    cnt_s[...] = cnt_c
    last = jnp.max(cand_s[CAND - 1], axis=1, keepdims=True)
    covered = jnp.logical_or(last < _unsortable(thr_c), last <= 0.5 * NEG_INF)

    @pl.when(jnp.min(jnp.where(covered, 1.0, 0.0)) == 0.0)
    def _():
        thr_a, cnt_a = bisect(count_all)
        thr_s[...] = thr_a
        cnt_s[...] = cnt_a

    thr = _unsortable(thr_s[...])
    c_thr = cnt_s[...]

    tied = jnp.max(jnp.where(jnp.logical_and(c_thr != kf, thr > 0.5 * NEG_INF), 1.0, 0.0)) > 0.0

    @pl.when(jnp.logical_not(tied))
    def _():
        def emit(kb, carry):
            for c in range(spb):
                s = sc_s[kb * spb + c]
                sel = jnp.logical_and(s >= thr, s > 0.5 * NEG_INF)
                bias_ref[kb * spb + c] = jnp.where(sel, 0.0, NEG_INF).astype(BF16)
            return carry

        lax.fori_loop(0, nkb, emit, 0)

    @pl.when(tied)
    def _():
        need = kf - count_all(thr, strict=True)
        tri = (lax.broadcasted_iota(jnp.int32, (LANES, LANES), 0)
               <= lax.broadcasted_iota(jnp.int32, (LANES, LANES), 1)).astype(BF16)

        def emit(kb, seen):
            for c in range(spb):
                s = sc_s[kb * spb + c]
                eq = jnp.where(s == thr, 1.0, 0.0)
                rank = seen + jnp.dot(eq.astype(BF16), tri, preferred_element_type=F32)
                keep = jnp.logical_or(s > thr, jnp.logical_and(s == thr, rank <= need))
                sel = jnp.logical_and(keep, s > 0.5 * NEG_INF)
                bias_ref[kb * spb + c] = jnp.where(sel, 0.0, NEG_INF).astype(BF16)
                seen = seen + jnp.sum(eq, axis=1, keepdims=True)
            return seen

        lax.fori_loop(0, nkb, emit, zeros)

    def fill(kb, carry):
        for c in range(spb):
            bias_ref[kb * spb + c] = jnp.full((tq, LANES), NEG_INF, BF16)
        return carry

    lax.fori_loop(nkb, nkb_total, fill, 0)


def _select(qi, kiwi, kia, kib, *, tq, spb, topk, causal, n_valid, q_pos0):
    t = qi.shape[0]
    nb = t // tq
    lk = kia.shape[1]
    nslab = lk // LANES
    nkb_total = nslab // spb
    assert topk <= 2 * LANES and nslab >= 2, "the bisection's starting lower bound needs two keys per lane"

    def kmap(n):
        return (0 if causal else n, 0, 0)

    return pl.pallas_call(
        functools.partial(_select_kernel, tq=tq, spb=spb, nkb_total=nkb_total, topk=topk,
                          causal=causal, n_valid=n_valid, q_pos0=q_pos0),
        grid=(nb,),
        in_specs=[
            pl.BlockSpec((tq, IQ_W), lambda n: (n, 0)),
            pl.BlockSpec((tq, LANES), lambda n: (n, 0)),
            pl.BlockSpec((None, lk, LANES), kmap),
            pl.BlockSpec((None, lk, LANES), kmap),
        ],
        out_specs=pl.BlockSpec((nslab, tq, LANES), lambda n: (0, n, 0)),
        out_shape=jax.ShapeDtypeStruct((nslab, t, LANES), BF16),
        scratch_shapes=[
            pltpu.VMEM((nslab + spb + CAND, tq, LANES), F32),
            pltpu.VMEM((N_IDX_HEADS, tq, LANES), F32),
            pltpu.VMEM((2, N_IDX_HEADS // 2 * tq, spb * LANES), F32),
            pltpu.VMEM((2, N_IDX_HEADS // 2 * tq, spb * LANES), F32),
            pltpu.VMEM((CAND, tq, LANES), F32),
            pltpu.VMEM((tq, LANES), jnp.int32),
            pltpu.VMEM((tq, LANES), F32),
        ],
        compiler_params=pltpu.CompilerParams(
            dimension_semantics=("arbitrary",), vmem_limit_bytes=VMEM_LIMIT),
        name="select",
    )(qi, kiwi, kia, kib)


FIXED_REF_LIMIT = 2.0 ** 40

def _attend_kernel(qb_ref, kb_ref, kbat_ref, last_ref, q_ref, k_ref, v_ref, b_ref, o_ref,
                   m_s, l_s, acc_s, par_s, exact_s, *, tq, spb):
    s = pl.program_id(0)
    first = kb_ref[s] == 0

    biases = [b_ref[c].astype(F32)[None] for c in range(spb)]
    ones = jnp.ones((spb * LANES, LANES), BF16)

    def masked_scores(g):
        qg = [q_ref[:, (g * GROUP + h) * HEAD_DIM:(g * GROUP + h + 1) * HEAD_DIM] for h in range(GROUP)]
        q4 = jnp.concatenate(qg, axis=0)
        kg = k_ref[:, g * HEAD_DIM:(g + 1) * HEAD_DIM]
        sc = lax.dot_general(q4, kg, (((1,), (1,)), ((), ())), preferred_element_type=F32)
        return [(sc[:, c * LANES:(c + 1) * LANES].reshape(GROUP, tq, LANES) + biases[c]).reshape(GROUP * tq, LANES)
                for c in range(spb)]

    def weighted_values(g, m_ref, slabs):
        pmat = jnp.concatenate([jnp.exp2(sl - m_ref).astype(BF16) for sl in slabs], axis=1)
        v1 = jnp.concatenate([v_ref[:, g * HEAD_DIM:(g + 1) * HEAD_DIM], ones], axis=1)
        pv = jnp.dot(pmat, v1, preferred_element_type=F32)
        return pv[:, 0:HEAD_DIM], pv[:, HEAD_DIM:HEAD_DIM + LANES]

    @pl.when(first)
    def _():
        par_s[0] = 0
        exact_s[0] = 1
        m_s[...] = jnp.full(m_s.shape, 0.1 * NEG_INF, F32)
        l_s[0] = jnp.zeros(l_s.shape[1:], F32)
        acc_s[0] = jnp.zeros(acc_s.shape[1:], F32)

    @pl.when(jnp.logical_not(first))
    def _():
        par = par_s[0]
        worst = jnp.zeros((GROUP * tq, LANES), F32)
        for g in range(N_KV_HEADS):
            pv, psum = weighted_values(g, m_s[g], masked_scores(g))
            acc_new = acc_s[par, g] + pv
            l_new = l_s[par, g] + psum
            acc_s[1 - par, g] = acc_new
            l_s[1 - par, g] = l_new
            worst = worst + l_new + jnp.abs(acc_new)
        ok = jnp.min(jnp.where(worst < FIXED_REF_LIMIT, 1.0, 0.0)) > 0.5
        exact_s[0] = jnp.where(ok, 0, 1)
        par_s[0] = jnp.where(ok, 1 - par, par)

    @pl.when(exact_s[0] == 1)
    def _():
        par = par_s[0]
        for g in range(N_KV_HEADS):
            slabs = masked_scores(g)
            mx = slabs[0]
            for sl in slabs[1:]:
                mx = jnp.maximum(mx, sl)
            m_prev = m_s[g]
            m_new = jnp.maximum(m_prev, jnp.max(mx, axis=1, keepdims=True))
            alpha = jnp.exp2(m_prev - m_new)
            pv, psum = weighted_values(g, m_new, slabs)
            acc_s[par, g] = alpha * acc_s[par, g] + pv
            l_s[par, g] = alpha * l_s[par, g] + psum
            m_s[g] = m_new

    @pl.when(last_ref[s] == 1)
    def _():
        par = par_s[0]
        for g in range(N_KV_HEADS):
            o = acc_s[par, g] / l_s[par, g]
            for h in range(GROUP):
                col = (g * GROUP + h) * HEAD_DIM
                o_ref[:, col:col + HEAD_DIM] = o[h * tq:(h + 1) * tq].astype(BF16)


def _attend(q, k_all, v_all, bias, sched, *, tq, spb):
    t = q.shape[0]
    qb, kb, kbat, last = sched
    nsteps = qb.shape[0]
    blk = spb * LANES
    grid_spec = pltpu.PrefetchScalarGridSpec(
        num_scalar_prefetch=4,
        grid=(nsteps,),
        in_specs=[
            pl.BlockSpec((tq, Q_W), lambda s, qb, kb, kbat, last: (qb[s], 0)),
            pl.BlockSpec((None, blk, KV_W), lambda s, qb, kb, kbat, last: (kbat[s], kb[s], 0)),
            pl.BlockSpec((None, blk, KV_W), lambda s, qb, kb, kbat, last: (kbat[s], kb[s], 0)),
            pl.BlockSpec((spb, tq, LANES), lambda s, qb, kb, kbat, last: (kb[s], qb[s], 0)),
        ],
        out_specs=pl.BlockSpec((tq, Q_W), lambda s, qb, kb, kbat, last: (qb[s], 0)),
        scratch_shapes=[
            pltpu.VMEM((N_KV_HEADS, GROUP * tq, LANES), F32),
            pltpu.VMEM((2, N_KV_HEADS, GROUP * tq, LANES), F32),
            pltpu.VMEM((2, N_KV_HEADS, GROUP * tq, HEAD_DIM), F32),
            pltpu.SMEM((1,), jnp.int32),
            pltpu.SMEM((1,), jnp.int32),
        ],
    )
    return pl.pallas_call(
        functools.partial(_attend_kernel, tq=tq, spb=spb),
        grid_spec=grid_spec,
        out_shape=jax.ShapeDtypeStruct((t, Q_W), BF16),
        compiler_params=pltpu.CompilerParams(
            dimension_semantics=("arbitrary",), vmem_limit_bytes=VMEM_LIMIT),
        name="attend",
    )(qb, kb, kbat, last, q, k_all, v_all, bias)


def _causal_schedule(t, tq, blk):
    qb, kb, last = [], [], []
    for n in range(t // tq):
        nk = ((n + 1) * tq + blk - 1) // blk
        for k in range(nk):
            qb.append(n)
            kb.append(k)
            last.append(1 if k == nk - 1 else 0)
    z = np.zeros(len(qb), np.int32)
    return (jnp.asarray(qb, jnp.int32), jnp.asarray(kb, jnp.int32), jnp.asarray(z), jnp.asarray(last, jnp.int32))


def _batched_schedule(nbatch, nk):
    qb = np.repeat(np.arange(nbatch, dtype=np.int32), nk)
    kb = np.tile(np.arange(nk, dtype=np.int32), nbatch)
    last = (kb == nk - 1).astype(np.int32)
    return (jnp.asarray(qb), jnp.asarray(kb), jnp.asarray(qb), jnp.asarray(last))


def _merge_kernel(o_ref, yc_ref, ga_ref, gc_ref, woa_ref, woc_ref, wout_ref, x_ref, gq_ref, gp_ref,
                  x1_ref, xn2_ref, mg_s, m_s):
    j = pl.program_id(1)
    nblk = D_MODEL // MERGE_TN

    @pl.when(j < nblk)
    def _():
        a = jnp.dot(o_ref[...], woa_ref[...], preferred_element_type=F32)
        c = jnp.dot(yc_ref[...], woc_ref[...], preferred_element_type=F32)
        mg_s[j] = (ga_ref[...] * a + gc_ref[...] * c).astype(BF16)

    @pl.when(j >= nblk)
    def _():
        mg = jnp.concatenate([mg_s[b] for b in range(nblk)], axis=1)
        m_s[j - nblk] = jnp.dot(mg, wout_ref[...], preferred_element_type=F32)

    @pl.when(j == 2 * nblk - 1)
    def _():
        m = jnp.concatenate([m_s[b] for b in range(nblk)], axis=1)
        x1 = x_ref[...] + _rms(m, gq_ref[...])
        x1_ref[...] = x1
        xn2_ref[...] = _rms(x1, gp_ref[...]).astype(BF16)


def _merge(o, yc, ga, gc, woa, woc, wout, x, gq, gp):
    t = x.shape[0]
    tn = MERGE_TN
    nblk = D_MODEL // tn

    def row(i, j):
        return (i, 0)

    def lo(i, j):
        return (i, jnp.minimum(j, nblk - 1))

    return pl.pallas_call(
        _merge_kernel,
        grid=(t // TM, 2 * nblk),
        in_specs=[
            pl.BlockSpec((TM, Q_W), row),
            pl.BlockSpec((TM, D_MODEL), row),
            pl.BlockSpec((TM, tn), lo),
            pl.BlockSpec((TM, tn), lo),
            pl.BlockSpec((Q_W, tn), lambda i, j: (0, jnp.minimum(j, nblk - 1))),
            pl.BlockSpec((D_MODEL, tn), lambda i, j: (0, jnp.minimum(j, nblk - 1))),
            pl.BlockSpec((D_MODEL, tn), lambda i, j: (0, jnp.maximum(j - nblk, 0))),
            pl.BlockSpec((TM, D_MODEL), row),
            pl.BlockSpec((1, D_MODEL), lambda i, j: (0, 0)),
            pl.BlockSpec((1, D_MODEL), lambda i, j: (0, 0)),
        ],
        out_specs=[pl.BlockSpec((TM, D_MODEL), row), pl.BlockSpec((TM, D_MODEL), row)],
        out_shape=[jax.ShapeDtypeStruct((t, D_MODEL), F32), jax.ShapeDtypeStruct((t, D_MODEL), BF16)],
        scratch_shapes=[pltpu.VMEM((nblk, TM, tn), BF16), pltpu.VMEM((nblk, TM, tn), F32)],
        compiler_params=pltpu.CompilerParams(
            dimension_semantics=("arbitrary", "arbitrary"), vmem_limit_bytes=VMEM_LIMIT),
        name="merge",
    )(o, yc, ga, gc, woa, woc, wout, x, gq, gp)


def _ffn_kernel(xn_ref, wg_ref, wv_ref, cwg_ref, cwv_ref, wd_ref, x1_ref, gq_ref, stg_ref, stv_ref,
                y_ref, ng_ref, nv_ref, acc_s, cg_s, cv_s, *, seg, carried):
    i = pl.program_id(0)
    jf = pl.program_id(1)
    nseg = TM // seg

    @pl.when(jf == 0)
    def _():
        acc_s[...] = jnp.zeros(acc_s.shape, F32)

    if carried:
        @pl.when(i == 0)
        def _():
            for carry_s, st_ref in ((cg_s, stg_ref), (cv_s, stv_ref)):
                carry_s[jf] = jnp.zeros((8, TF), F32)
                carry_s[jf, 0:2, :] = st_ref[0]

    xn = xn_ref[...]

    def branch(w_ref, cw_ref, st_ref, carry_s, new_ref):
        up = jnp.dot(xn, w_ref[...], preferred_element_type=F32)
        if carried:
            prevs = [carry_s[jf, 0:2, :]]
        else:
            prevs = [st_ref[s] for s in range(nseg)]
        y, news = _conv_tile(up, cw_ref[...], prevs, seg)
        for s in range(nseg):
            new_ref[s] = news[s]
        if carried:
            carry_s[jf, 0:2, :] = news[0]
        return y

    gate = branch(wg_ref, cwg_ref, stg_ref, cg_s, ng_ref)
    val = branch(wv_ref, cwv_ref, stv_ref, cv_s, nv_ref)
    c0 = np.float32(np.sqrt(2.0 / np.pi))
    gelu = 0.5 * gate * (1.0 + jnp.tanh(c0 * (gate + 0.044715 * (gate * gate * gate))))
    hid = (gelu * val).astype(BF16)
    acc_s[...] += jnp.dot(hid, wd_ref[...], preferred_element_type=F32)

    @pl.when(jf == pl.num_programs(1) - 1)
    def _():
        y_ref[...] = x1_ref[...] + _rms(acc_s[...], gq_ref[...])


def _ffn(xn2, w_up, cw, w_down, x1, gq, state, seg):
    t = x1.shape[0]
    nf = D_FF // TF
    carried = seg == t
    nseg_tile = 1 if carried else TM // seg

    def row(i, j):
        return (i, 0)

    def stg(i, j):
        return (0 if carried else i, 0, j)

    def stv(i, j):
        return (0 if carried else i, 0, j + nf)

    return pl.pallas_call(
        functools.partial(_ffn_kernel, seg=min(seg, TM), carried=carried),
        grid=(t // TM, nf),
        in_specs=[
            pl.BlockSpec((TM, D_MODEL), row),
            pl.BlockSpec((D_MODEL, TF), lambda i, j: (0, j)),
            pl.BlockSpec((D_MODEL, TF), lambda i, j: (0, j + nf)),
            pl.BlockSpec((3, TF), lambda i, j: (0, j)),
            pl.BlockSpec((3, TF), lambda i, j: (0, j + nf)),
            pl.BlockSpec((TF, D_MODEL), lambda i, j: (j, 0)),
            pl.BlockSpec((TM, D_MODEL), row),
            pl.BlockSpec((1, D_MODEL), lambda i, j: (0, 0)),
            pl.BlockSpec((nseg_tile, 2, TF), stg),
            pl.BlockSpec((nseg_tile, 2, TF), stv),
        ],
        out_specs=[
            pl.BlockSpec((TM, D_MODEL), row),
            pl.BlockSpec((nseg_tile, 2, TF), lambda i, j: (i, 0, j)),
            pl.BlockSpec((nseg_tile, 2, TF), lambda i, j: (i, 0, j)),
        ],
        out_shape=[
            jax.ShapeDtypeStruct((t, D_MODEL), F32),
            jax.ShapeDtypeStruct((t // TM * nseg_tile, 2, D_FF), F32),
            jax.ShapeDtypeStruct((t // TM * nseg_tile, 2, D_FF), F32),
        ],
        scratch_shapes=[
            pltpu.VMEM((TM, D_MODEL), F32),
            pltpu.VMEM((nf, 8, TF), F32),
            pltpu.VMEM((nf, 8, TF), F32),
        ],
        compiler_params=pltpu.CompilerParams(
            dimension_semantics=("arbitrary", "arbitrary"), vmem_limit_bytes=VMEM_LIMIT),
        name="ffn",
    )(xn2, w_up, w_up, cw, cw, w_down, x1, gq, state, state)


PACK_ROWS = 1024


def _pack_kernel(ck_ref, cv_ref, kt_ref, vt_ref, ko_ref, vo_ref, *, rows):
    del kt_ref, vt_ref
    for src, dst in ((ck_ref, ko_ref), (cv_ref, vo_ref)):
        for g in range(N_KV_HEADS):
            dst[:, g * HEAD_DIM:(g + 1) * HEAD_DIM] = src[pl.ds(g, rows, stride=N_KV_HEADS), :].astype(BF16)


def _pack_cache(cache_k, cache_v, k_tail, v_tail):
    nb, plen = cache_k.shape[0], cache_k.shape[1]
    rows = int(np.gcd(plen, PACK_ROWS))
    assert rows % 16 == 0
    cspec = pl.BlockSpec((None, rows * N_KV_HEADS, HEAD_DIM), lambda b, r: (b, r, 0))
    ospec = pl.BlockSpec((None, rows, KV_W), lambda b, r: (b, r, 0))
    anyspec = pl.BlockSpec(memory_space=pl.ANY)
    flat = (nb, plen * N_KV_HEADS, HEAD_DIM)
    return pl.pallas_call(
        functools.partial(_pack_kernel, rows=rows),
        grid=(nb, plen // rows),
        in_specs=[cspec, cspec, anyspec, anyspec],
        out_specs=[ospec, ospec],
        out_shape=[jax.ShapeDtypeStruct(k_tail.shape, BF16), jax.ShapeDtypeStruct(v_tail.shape, BF16)],
        input_output_aliases={2: 0, 3: 1},
        compiler_params=pltpu.CompilerParams(
            dimension_semantics=("arbitrary", "arbitrary"), vmem_limit_bytes=VMEM_LIMIT),
        name="pack_cache",
    )(cache_k.reshape(flat), cache_v.reshape(flat), k_tail, v_tail)


def _rope_tables(pos, rot, width):
    half = rot // 2
    freqs = ROPE_THETA ** (-jnp.arange(half, dtype=F32) / half)
    ang = pos.astype(F32)[:, None] * freqs[None, :]
    cos, sin = jnp.cos(ang), jnp.sin(ang)
    t = pos.shape[0]
    c = jnp.concatenate([cos, cos, jnp.ones((t, width - rot), F32)], axis=1)
    d = jnp.concatenate([-sin, sin, jnp.zeros((t, width - rot), F32)], axis=1)
    reps = LANES // width
    return jnp.tile(c, (1, reps)), jnp.tile(d, (1, reps))


def _relayout_w_in(w):
    o = np.cumsum([0, Q_W, KV_W, KV_W, IQ_W, IDX_DIM, N_IDX_HEADS, D_MODEL, D_MODEL, D_MODEL, D_MODEL, D_MODEL])
    q, k, v, qi = (w[:, o[a]:o[a + 1]] for a in range(4))
    kiwi = w[:, o[4]:o[6]]
    cb, cc, ch, ga, gc = (w[:, o[a]:o[a + 1]] for a in range(6, 11))
    parts = [q, k, v, qi, kiwi, jnp.zeros((D_MODEL, TN - kiwi.shape[1]), w.dtype)]
    for c in range(N_CBLK):
        sl = slice(c * TN, (c + 1) * TN)
        parts += [cb[:, sl], cc[:, sl], ch[:, sl]]
    parts += [ga, gc]
    return jnp.concatenate(parts, axis=1).astype(BF16)


def _stream(x, pos, seg, past, weights, *, tq, spb_sel, spb_att):
    (g_mp, g_mq, w_in_r, conv_w, woa, woc, wout, g_fp, g_fq, w_up, fconv_w, w_down) = weights
    t = x.shape[0]
    nseq = t // seg
    tabs = _rope_tables(pos, ROT_DIM, HEAD_DIM) + _rope_tables(pos, IDX_ROT_DIM, IDX_DIM)
    if past is None:
        conv_state = jnp.zeros((nseq, 2, D_MODEL), F32)
        ffn_state = jnp.zeros((nseq, 2, 2 * D_FF), F32)
    else:
        conv_state, ffn_state = past[3], past[4]

    (q, k, kb, v, vb, qi, kiwi, kia, kib, yc, conv_new, ga, gc) = _proj(
        x, g_mp, w_in_r, tabs, conv_w, conv_state, seg)

    if past is None:
        k_all, v_all = kb[None], vb[None]
        kia_all, kib_all = kia[None], kib[None]
        n_keys = t
        sched = _causal_schedule(t, tq, spb_att * LANES)
        q_pos0 = 0
    else:
        cache_k, cache_v, cache_ki = past[0], past[1], past[2]
        plen = cache_k.shape[1]
        n_keys = plen + seg
        lk = -(-n_keys // (spb_att * LANES)) * (spb_att * LANES)
        pad = lk - n_keys

        def cat(c, new, width):
            parts = [c, new.reshape(nseq, seg, width)]
            if pad:
                parts.append(jnp.zeros((nseq, pad, width), BF16))
            return jnp.concatenate(parts, axis=1)

        def tail(new):
            return jnp.pad(new.reshape(nseq, seg, KV_W), ((0, 0), (plen, pad), (0, 0)))

        k_all, v_all = _pack_cache(cache_k, cache_v, tail(kb), tail(vb))
        cki = cache_ki.astype(BF16)
        zk = jnp.zeros_like(cki)
        kia_all = cat(jnp.concatenate([cki, zk], axis=-1), kia, LANES)
        kib_all = cat(jnp.concatenate([zk, cki], axis=-1), kib, LANES)
        sched = _batched_schedule(nseq, lk // (spb_att * LANES))
        q_pos0 = plen
    topk = min(TOPK_MAX, n_keys // 4)

    bias = _select(qi, kiwi, kia_all, kib_all, tq=tq, spb=spb_sel, topk=topk,
                   causal=past is None, n_valid=n_keys, q_pos0=q_pos0)
    o = _attend(q, k_all, v_all, bias, sched, tq=tq, spb=spb_att)
    x1, xn2 = _merge(o, yc, ga, gc, woa, woc, wout, x, g_mq, g_fp)
    y, ffn_g, ffn_v = _ffn(xn2, w_up, fconv_w, w_down, x1, g_fq, ffn_state, seg)
    ffn_new = jnp.concatenate([ffn_g, ffn_v], axis=-1)
    return y, k, v, kiwi[:, :IDX_DIM], conv_new[-nseq:], ffn_new[-nseq:]


def kernel(x_prompt, x_sample, cache_k, cache_v, cache_k_idx, state_conv, state_ffn_conv, norm_mix_pre, norm_mix_post, w_in, conv_w, w_o_attn, w_o_conv, w_out, norm_ffn_pre, norm_ffn_post, w_ffn_up, ffn_conv_w, w_ffn_down):
    depth = w_in.shape[0]
    assert depth == 1, "single-layer step"
    b, seq, _ = x_prompt.shape
    assert b == 1
    db, dseq, _ = x_sample.shape
    plen = cache_k.shape[2]
    assert dseq == CHUNK and plen % CHUNK == 0

    weights = (
        norm_mix_pre, norm_mix_post, _relayout_w_in(w_in[0]), conv_w[0],
        w_o_attn[0].astype(BF16), w_o_conv[0].astype(BF16), w_out[0].astype(BF16),
        norm_ffn_pre, norm_ffn_post, w_ffn_up[0].astype(BF16), ffn_conv_w[0], w_ffn_down[0].astype(BF16),
    )

    pos_p = jnp.arange(seq, dtype=jnp.int32)
    yp, kp, vp, kip, convp, ffnp = _stream(
        x_prompt.reshape(seq, D_MODEL), pos_p, seq, None, weights, tq=128, spb_sel=4, spb_att=8)

    pos_s = jnp.tile(jnp.arange(dseq, dtype=jnp.int32) + plen, db)
    past = (cache_k[0], cache_v[0], cache_k_idx[0], state_conv[0], state_ffn_conv[0])
    n_keys = plen + dseq
    spb_s = _sample_slabs(n_keys)
    ys, ks, vs, kis, convs, ffns = _stream(
        x_sample.reshape(db * dseq, D_MODEL), pos_s, dseq, past, weights, tq=dseq, spb_sel=spb_s, spb_att=spb_s)

    return (
        yp.reshape(1, seq, D_MODEL), ys.reshape(db, dseq, D_MODEL),
        kp.reshape(1, 1, seq, N_KV_HEADS, HEAD_DIM), vp.reshape(1, 1, seq, N_KV_HEADS, HEAD_DIM),
        kip.reshape(1, 1, seq, IDX_DIM), convp.reshape(1, 1, 2, D_MODEL), ffnp.reshape(1, 1, 2, 2 * D_FF),
        ks.reshape(1, db, dseq, N_KV_HEADS, HEAD_DIM), vs.reshape(1, db, dseq, N_KV_HEADS, HEAD_DIM),
        kis.reshape(1, db, dseq, IDX_DIM), convs.reshape(1, db, 2, D_MODEL), ffns.reshape(1, db, 2, 2 * D_FF),
    )


def _sample_slabs(n_keys):
    nslab = -(-n_keys // LANES)
    best = 1
    for d in range(1, nslab + 1):
        if nslab % d == 0 and d <= 11:
            best = d
    return best
```

```python
import functools

import jax
import jax.numpy as jnp
import numpy as np
from jax import lax
from jax.experimental import pallas as pl
from jax.experimental.pallas import tpu as pltpu

F32 = jnp.float32
BF16 = jnp.bfloat16

D_MODEL = 2048
N_HEADS = 16
N_KV_HEADS = 4
HEAD_DIM = 128
ROT_DIM = HEAD_DIM // 4
N_IDX_HEADS = 16
IDX_DIM = 64
IDX_ROT_DIM = IDX_DIM // 4
CHUNK = 64
TOPK_MAX = 256
ROPE_THETA = 500000.0
D_FF = 5632
RMS_EPS = 1e-6
NEG_INF = -1e30
Q_W = N_HEADS * HEAD_DIM
KV_W = N_KV_HEADS * HEAD_DIM
IQ_W = N_IDX_HEADS * IDX_DIM
GROUP = N_HEADS // N_KV_HEADS

LANES = 128
TM = 512
TN = 1024
MERGE_TN = 512
TF = 512
VMEM_LIMIT = 56 * 1024 * 1024

J_Q = 0
J_KV = J_Q + Q_W // TN
assert 2 * KV_W == TN
J_QI = J_KV + 1
J_KIWI = J_QI + IQ_W // TN
J_CONV = J_KIWI + 1
N_CBLK = D_MODEL // TN
J_GA = J_CONV + 3 * N_CBLK
J_GC = J_GA + N_CBLK
NJ = J_GC + N_CBLK

Q_PRESCALE = float(np.float32((HEAD_DIM ** -0.5) * np.log2(np.e)))


def _rms(x, g):
    return x * lax.rsqrt(jnp.mean(x * x, axis=-1, keepdims=True) + RMS_EPS) * g


def _sortable(x):
    bits = lax.bitcast_convert_type(x, jnp.int32)
    return jnp.where(bits < 0, bits ^ jnp.int32(0x7FFFFFFF), bits)


def _unsortable(k):
    return lax.bitcast_convert_type(jnp.where(k < 0, k ^ jnp.int32(0x7FFFFFFF), k), F32)


CAND = 16
PAD_SCORE = -3.0e38


def _oddeven_merge_sort_pairs(n):
    out, p = [], 1
    while p < n:
        k = p
        while k >= 1:
            for j in range(k % p, n - k, 2 * k):
                for i in range(min(k, n - j - k)):
                    if (i + j) // (2 * p) == (i + j + k) // (2 * p):
                        out.append((i + j, i + j + k))
            k //= 2
        p *= 2
    return out


_SORT_PAIRS = _oddeven_merge_sort_pairs(CAND)


def _sort_desc(v):
    v = list(v)
    for i, j in _SORT_PAIRS:
        v[i], v[j] = jnp.maximum(v[i], v[j]), jnp.minimum(v[i], v[j])
    return v


def _bitonic_merge_desc(v):
    v = list(v)
    d = len(v) // 2
    while d >= 1:
        for i in range(len(v)):
            if not i & d:
                v[i], v[i + d] = jnp.maximum(v[i], v[i + d]), jnp.minimum(v[i], v[i + d])
        d //= 2
    return v


def _dwconv_seg(u, prev2, w3):
    row = lax.broadcasted_iota(jnp.int32, u.shape, 0)
    p0, p1 = prev2[0:1], prev2[1:2]
    s1 = jnp.where(row == 0, p1, pltpu.roll(u, 1, 0))
    s2 = jnp.where(row == 0, p0, jnp.where(row == 1, p1, pltpu.roll(u, 2, 0)))
    return w3[0:1] * s2 + w3[1:2] * s1 + w3[2:3] * u


def _conv_tile(u, w3, prevs, seg):
    nseg = len(prevs)
    ys, news = [], []
    for s in range(nseg):
        us = u[s * seg:(s + 1) * seg]
        ys.append(_dwconv_seg(us, prevs[s], w3))
        news.append(us[seg - 2:seg])
    y = ys[0] if nseg == 1 else jnp.concatenate(ys, axis=0)
    return y, news


def _proj_kernel(x_ref, g_ref, w_ref, c128_ref, d128_ref, c64_ref, d64_ref, cw_ref, st_ref,
                 q_ref, k_ref, kb_ref, v_ref, vb_ref, qi_ref, kiwi_ref, kia_ref, kib_ref,
                 yc_ref, cn_ref, ga_ref, gc_ref,
                 xn_s, cb_s, cc_s, carry_s, *, seg, carried):
    i = pl.program_id(0)
    j = pl.program_id(1)

    @pl.when(j == 0)
    def _():
        xn_s[...] = _rms(x_ref[...], g_ref[...]).astype(BF16)

    def mm():
        return jnp.dot(xn_s[...], w_ref[...], preferred_element_type=F32)

    lane = lax.broadcasted_iota(jnp.int32, (TM, LANES), 1)

    def rope128(xh):
        partner = jnp.where(lane < ROT_DIM // 2, pltpu.roll(xh, LANES - ROT_DIM // 2, 1),
                            pltpu.roll(xh, ROT_DIM // 2, 1))
        return xh * c128_ref[...] + partner * d128_ref[...]

    def rope64(xh, c, d):
        first = (lane & (IDX_DIM - 1)) < IDX_ROT_DIM // 2
        partner = jnp.where(first, pltpu.roll(xh, LANES - IDX_ROT_DIM // 2, 1),
                            pltpu.roll(xh, IDX_ROT_DIM // 2, 1))
        return xh * c + partner * d

    @pl.when(j < J_KV)
    def _():
        acc = mm()
        for h in range(TN // LANES):
            sl = slice(h * LANES, (h + 1) * LANES)
            q_ref[:, sl] = (rope128(acc[:, sl]) * Q_PRESCALE).astype(BF16)

    @pl.when(j == J_KV)
    def _():
        acc = mm()
        for h in range(N_KV_HEADS):
            sl = slice(h * HEAD_DIM, (h + 1) * HEAD_DIM)
            r = rope128(acc[:, sl])
            k_ref[pl.ds(h, TM, stride=N_KV_HEADS), :] = r
            kb_ref[:, sl] = r.astype(BF16)
            v = acc[:, KV_W + h * HEAD_DIM:KV_W + (h + 1) * HEAD_DIM]
            v_ref[pl.ds(h, TM, stride=N_KV_HEADS), :] = v
            vb_ref[:, sl] = v.astype(BF16)

    @pl.when(jnp.logical_and(j >= J_QI, j < J_KIWI))
    def _():
        acc = mm()
        for h in range(TN // LANES):
            sl = slice(h * LANES, (h + 1) * LANES)
            qi_ref[:, sl] = rope64(acc[:, sl], c64_ref[...], d64_ref[...]).astype(BF16)

    @pl.when(j == J_KIWI)
    def _():
        is_ki = lane < IDX_DIM
        c = jnp.where(is_ki, c64_ref[...], 1.0)
        d = jnp.where(is_ki, d64_ref[...], 0.0)
        acc = jnp.dot(xn_s[...], w_ref[:, 0:LANES], preferred_element_type=F32)
        r = rope64(acc, c, d)
        kiwi_ref[...] = r
        ka = jnp.where(is_ki, r, 0.0)
        kia_ref[...] = ka.astype(BF16)
        kib_ref[...] = pltpu.roll(ka, IDX_DIM, 1).astype(BF16)

    jc = jnp.maximum(j - J_CONV, 0)
    in_conv = jnp.logical_and(j >= J_CONV, j < J_GA)
    cblk = jnp.minimum(jc // 3, N_CBLK - 1)
    part = jc % 3

    @pl.when(jnp.logical_and(in_conv, part == 0))
    def _():
        cb_s[...] = mm()

    @pl.when(jnp.logical_and(in_conv, part == 1))
    def _():
        cc_s[...] = mm()

    @pl.when(jnp.logical_and(in_conv, part == 2))
    def _():
        nseg = TM // seg
        if carried:
            @pl.when(i == 0)
            def _():
                carry_s[cblk] = jnp.zeros((8, TN), F32)
                carry_s[cblk, 0:2, :] = st_ref[0]
            prevs = [carry_s[cblk, 0:2, :]]
        else:
            prevs = [st_ref[s] for s in range(nseg)]
        u = cc_s[...] * mm()
        y, news = _conv_tile(u, cw_ref[...], prevs, seg)
        yc_ref[...] = (cb_s[...] * y).astype(BF16)
        for s in range(nseg):
            cn_ref[s] = news[s]
        if carried:
            carry_s[cblk, 0:2, :] = news[0]

    @pl.when(jnp.logical_and(j >= J_GA, j < J_GC))
    def _():
        ga_ref[...] = 1.0 / (1.0 + jnp.exp(-mm()))

    @pl.when(j >= J_GC)
    def _():
        gc_ref[...] = 1.0 / (1.0 + jnp.exp(-mm()))


def _proj(x, g, w, tabs, conv_w, state, seg):
    t = x.shape[0]
    ni = t // TM
    carried = seg == t
    nseg_tile = 1 if carried else TM // seg
    c128, d128, c64, d64 = tabs

    def row(i, j):
        return (i, 0)

    def const(i, j):
        return (0, 0)

    def cblk(j):
        return jnp.minimum(jnp.maximum(j - J_CONV, 0) // 3, N_CBLK - 1)

    def st_map(i, j):
        return (0 if carried else i, 0, cblk(j))

    in_specs = [
        pl.BlockSpec((TM, D_MODEL), row),
        pl.BlockSpec((1, D_MODEL), const),
        pl.BlockSpec((D_MODEL, TN), lambda i, j: (0, j)),
        pl.BlockSpec((TM, LANES), row),
        pl.BlockSpec((TM, LANES), row),
        pl.BlockSpec((TM, LANES), row),
        pl.BlockSpec((TM, LANES), row),
        pl.BlockSpec((3, TN), lambda i, j: (0, cblk(j))),
        pl.BlockSpec((nseg_tile, 2, TN), st_map),
    ]
    out_shape = [
        jax.ShapeDtypeStruct((t, Q_W), BF16),
        jax.ShapeDtypeStruct((t * N_KV_HEADS, HEAD_DIM), F32),
        jax.ShapeDtypeStruct((t, KV_W), BF16),
        jax.ShapeDtypeStruct((t * N_KV_HEADS, HEAD_DIM), F32),
        jax.ShapeDtypeStruct((t, KV_W), BF16),
        jax.ShapeDtypeStruct((t, IQ_W), BF16),
        jax.ShapeDtypeStruct((t, LANES), F32),
        jax.ShapeDtypeStruct((t, LANES), BF16),
        jax.ShapeDtypeStruct((t, LANES), BF16),
        jax.ShapeDtypeStruct((t, D_MODEL), BF16),
        jax.ShapeDtypeStruct((ni * nseg_tile, 2, D_MODEL), F32),
        jax.ShapeDtypeStruct((t, D_MODEL), F32),
        jax.ShapeDtypeStruct((t, D_MODEL), F32),
    ]
    out_specs = [
        pl.BlockSpec((TM, TN), lambda i, j: (i, jnp.clip(j - J_Q, 0, J_KV - J_Q - 1))),
        pl.BlockSpec((TM * N_KV_HEADS, HEAD_DIM), row),
        pl.BlockSpec((TM, KV_W), row),
        pl.BlockSpec((TM * N_KV_HEADS, HEAD_DIM), row),
        pl.BlockSpec((TM, KV_W), row),
        pl.BlockSpec((TM, TN), lambda i, j: (i, jnp.clip(j - J_QI, 0, J_KIWI - J_QI - 1))),
        pl.BlockSpec((TM, LANES), row),
        pl.BlockSpec((TM, LANES), row),
        pl.BlockSpec((TM, LANES), row),
        pl.BlockSpec((TM, TN), lambda i, j: (i, cblk(j))),
        pl.BlockSpec((nseg_tile, 2, TN), lambda i, j: (i, 0, cblk(j))),
        pl.BlockSpec((TM, TN), lambda i, j: (i, jnp.clip(j - J_GA, 0, N_CBLK - 1))),
        pl.BlockSpec((TM, TN), lambda i, j: (i, jnp.clip(j - J_GC, 0, N_CBLK - 1))),
    ]
    return pl.pallas_call(
        functools.partial(_proj_kernel, seg=min(seg, TM), carried=carried),
        grid=(ni, NJ),
        in_specs=in_specs,
        out_specs=out_specs,
        out_shape=out_shape,
        scratch_shapes=[
            pltpu.VMEM((TM, D_MODEL), BF16),
            pltpu.VMEM((TM, TN), F32),
            pltpu.VMEM((TM, TN), F32),
            pltpu.VMEM((N_CBLK, 8, TN), F32),
        ],
        compiler_params=pltpu.CompilerParams(
            dimension_semantics=("arbitrary", "arbitrary"), vmem_limit_bytes=VMEM_LIMIT),
        name="proj",
    )(x, g, w, c128, d128, c64, d64, conv_w, state)


def _select_kernel(qi_ref, kiwi_ref, kia_ref, kib_ref, bias_ref, sc_s, wb_s, lg_s, lh_s, cand_s, thr_s, cnt_s, *,
                   tq, spb, nkb_total, topk, causal, n_valid, q_pos0):
    n = pl.program_id(0)
    npair = N_IDX_HEADS // 2
    blk = spb * LANES
    kiwi = kiwi_ref[...]
    wscale = (IDX_DIM ** -0.5) * (N_IDX_HEADS ** -0.5)
    for h in range(N_IDX_HEADS):
        wb_s[h] = jnp.broadcast_to(kiwi[:, IDX_DIM + h:IDX_DIM + h + 1], (tq, LANES)) * wscale
    q2 = jnp.concatenate([qi_ref[:, p * LANES:(p + 1) * LANES] for p in range(npair)], axis=0)

    if causal:
        qpos0 = n * tq
        nkb = (qpos0 + tq + blk - 1) // blk
    else:
        qpos0 = q_pos0
        nkb = nkb_total
    lane = lax.broadcasted_iota(jnp.int32, (tq, LANES), 1)
    qchunk = (qpos0 + lax.broadcasted_iota(jnp.int32, (tq, LANES), 0)) >> 6
    nt = (((1,), (1,)), ((), ()))

    def logits_into(lg_ref, kb):
        base = pl.multiple_of(jnp.minimum(kb, nkb_total - 1) * blk, blk)
        lg_ref[0] = lax.dot_general(q2, kia_ref[pl.ds(base, blk), :], nt, preferred_element_type=F32)
        lg_ref[1] = lax.dot_general(q2, kib_ref[pl.ds(base, blk), :], nt, preferred_element_type=F32)

    def head_sum(lg_ref, kb, carry):
        m1, m2 = carry
        base = kb * blk
        for c in range(spb):
            cs = slice(c * LANES, (c + 1) * LANES)
            acc = jnp.zeros((tq, LANES), F32)
            for p in range(npair):
                rs = slice(p * tq, (p + 1) * tq)
                acc = acc + jnp.maximum(lg_ref[0, rs, cs], 0.0) * wb_s[2 * p]
                acc = acc + jnp.maximum(lg_ref[1, rs, cs], 0.0) * wb_s[2 * p + 1]
            col = base + c * LANES + lane
            adm = jnp.logical_and((col >> 6) <= qchunk, col < n_valid)
            sc = jnp.where(adm, acc, NEG_INF)
            m2 = jnp.maximum(m2, jnp.minimum(m1, sc))
            m1 = jnp.maximum(m1, sc)
            sc_s[kb * spb + c] = sc
        return m1, m2

    logits_into(lg_s, 0)

    def score_pair(i, carry):
        logits_into(lh_s, 2 * i + 1)
        carry = head_sum(lg_s, 2 * i, carry)
        logits_into(lg_s, 2 * i + 2)
        return head_sum(lh_s, 2 * i + 1, carry)

    neg = jnp.full((tq, LANES), NEG_INF, F32)
    m1, m2 = lax.fori_loop(0, (nkb + 1) // 2, score_pair, (neg, neg))

    zeros = jnp.zeros((tq, LANES), F32)
    kf = float(topk)

    def count_all(thr_f, strict=False):
        def body(kb, acc):
            for c in range(spb):
                s = sc_s[kb * spb + c]
                acc = acc + jnp.where(s > thr_f if strict else s >= thr_f, 1.0, 0.0)
            return acc
        return jnp.sum(lax.fori_loop(0, nkb, body, zeros), axis=1, keepdims=True)

    def count_cand(thr_f):
        acc = zeros
        for i in range(CAND):
            acc = acc + jnp.where(cand_s[i] >= thr_f, 1.0, 0.0)
        return jnp.sum(acc, axis=1, keepdims=True)

    ones = jnp.ones((tq, LANES), jnp.int32)
    lo0 = _sortable(jnp.min(m2, axis=1, keepdims=True)) * ones
    hi0 = _sortable(jnp.max(m1, axis=1, keepdims=True)) * ones + 1

    def bisect(count):
        def unresolved(lo_k, hi_k, c_lo):
            open_ = jnp.logical_and(c_lo != kf, (hi_k - lo_k) != 1)
            return jnp.max(jnp.where(open_, 1.0, 0.0))

        def cond(st):
            return jnp.logical_and(st[0] < 33, st[-1] > 0.0)

        def body(st):
            it, lo_k, hi_k, c_lo, _ = st
            mid = lo_k + lax.shift_right_logical(hi_k - lo_k, 1)
            cnt = count(_unsortable(mid))
            ge = cnt >= kf
            lo_k = jnp.where(ge, mid, lo_k)
            hi_k = jnp.where(ge, hi_k, mid)
            c_lo = jnp.where(ge, cnt, c_lo)
            return it + 1, lo_k, hi_k, c_lo, unresolved(lo_k, hi_k, c_lo)

        c0 = jnp.full((tq, LANES), -1.0, F32)
        st = lax.while_loop(cond, body, (jnp.int32(0), lo0, hi0, c0, unresolved(lo0, hi0, c0)))
        return st[1], st[3]

    nsl = nkb * spb
    nchunk = (nsl + CAND - 1) // CAND
    pad_tile = jnp.full((tq, LANES), PAD_SCORE, F32)

    def pad(sidx, carry):
        sc_s[sidx] = pad_tile
        return carry

    lax.fori_loop(nsl, nchunk * CAND, pad, 0)

    def gather_rows(rg, carry):
        r0 = pl.multiple_of(rg * 8, 8)

        def chunk(ch, cand):
            new = _sort_desc([sc_s[ch * CAND + i, pl.ds(r0, 8), :] for i in range(CAND)])
            return tuple(_bitonic_merge_desc([jnp.maximum(cand[i], new[CAND - 1 - i]) for i in range(CAND)]))

        start = tuple(jnp.full((8, LANES), PAD_SCORE, F32) for _ in range(CAND))
        cand = lax.fori_loop(0, nchunk, chunk, start)
        for i in range(CAND):
            cand_s[i, pl.ds(r0, 8), :] = cand[i]
        return carry

    lax.fori_loop(0, tq // 8, gather_rows, 0)

    thr_c, cnt_c = bisect(count_cand)
    thr_s[...] = thr_c
    cnt_s[...] = cnt_c
    last = jnp.max(cand_s[CAND - 1], axis=1, keepdims=True)
    covered = jnp.logical_or(last < _unsortable(thr_c), last <= 0.5 * NEG_INF)

    @pl.when(jnp.min(jnp.where(covered, 1.0, 0.0)) == 0.0)
    def _():
        thr_a, cnt_a = bisect(count_all)
        thr_s[...] = thr_a
        cnt_s[...] = cnt_a

    thr = _unsortable(thr_s[...])
    c_thr = cnt_s[...]

    tied = jnp.max(jnp.where(jnp.logical_and(c_thr != kf, thr > 0.5 * NEG_INF), 1.0, 0.0)) > 0.0

    @pl.when(jnp.logical_not(tied))
    def _():
        def emit(kb, carry):
            for c in range(spb):
                s = sc_s[kb * spb + c]
                sel = jnp.logical_and(s >= thr, s > 0.5 * NEG_INF)
                bias_ref[kb * spb + c] = jnp.where(sel, 0.0, NEG_INF).astype(BF16)
            return carry

        lax.fori_loop(0, nkb, emit, 0)

    @pl.when(tied)
    def _():
        need = kf - count_all(thr, strict=True)
        tri = (lax.broadcasted_iota(jnp.int32, (LANES, LANES), 0)
               <= lax.broadcasted_iota(jnp.int32, (LANES, LANES), 1)).astype(BF16)

        def emit(kb, seen):
            for c in range(spb):
                s = sc_s[kb * spb + c]
                eq = jnp.where(s == thr, 1.0, 0.0)
                rank = seen + jnp.dot(eq.astype(BF16), tri, preferred_element_type=F32)
                keep = jnp.logical_or(s > thr, jnp.logical_and(s == thr, rank <= need))
                sel = jnp.logical_and(keep, s > 0.5 * NEG_INF)
                bias_ref[kb * spb + c] = jnp.where(sel, 0.0, NEG_INF).astype(BF16)
                seen = seen + jnp.sum(eq, axis=1, keepdims=True)
            return seen

        lax.fori_loop(0, nkb, emit, zeros)

    def fill(kb, carry):
        for c in range(spb):
            bias_ref[kb * spb + c] = jnp.full((tq, LANES), NEG_INF, BF16)
        return carry

    lax.fori_loop(nkb, nkb_total, fill, 0)


def _select(qi, kiwi, kia, kib, *, tq, spb, topk, causal, n_valid, q_pos0):
    t = qi.shape[0]
    nb = t // tq
    lk = kia.shape[1]
    nslab = lk // LANES
    nkb_total = nslab // spb
    assert topk <= 2 * LANES and nslab >= 2, "the bisection's starting lower bound needs two keys per lane"

    def kmap(n):
        return (0 if causal else n, 0, 0)

    return pl.pallas_call(
        functools.partial(_select_kernel, tq=tq, spb=spb, nkb_total=nkb_total, topk=topk,
                          causal=causal, n_valid=n_valid, q_pos0=q_pos0),
        grid=(nb,),
        in_specs=[
            pl.BlockSpec((tq, IQ_W), lambda n: (n, 0)),
            pl.BlockSpec((tq, LANES), lambda n: (n, 0)),
            pl.BlockSpec((None, lk, LANES), kmap),
            pl.BlockSpec((None, lk, LANES), kmap),
        ],
        out_specs=pl.BlockSpec((nslab, tq, LANES), lambda n: (0, n, 0)),
        out_shape=jax.ShapeDtypeStruct((nslab, t, LANES), BF16),
        scratch_shapes=[
            pltpu.VMEM((nslab + spb + CAND, tq, LANES), F32),
            pltpu.VMEM((N_IDX_HEADS, tq, LANES), F32),
            pltpu.VMEM((2, N_IDX_HEADS // 2 * tq, spb * LANES), F32),
            pltpu.VMEM((2, N_IDX_HEADS // 2 * tq, spb * LANES), F32),
            pltpu.VMEM((CAND, tq, LANES), F32),
            pltpu.VMEM((tq, LANES), jnp.int32),
            pltpu.VMEM((tq, LANES), F32),
        ],
        compiler_params=pltpu.CompilerParams(
            dimension_semantics=("arbitrary",), vmem_limit_bytes=VMEM_LIMIT),
        name="select",
    )(qi, kiwi, kia, kib)


FIXED_REF_LIMIT = 2.0 ** 40

def _attend_kernel(qb_ref, kb_ref, kbat_ref, last_ref, q_ref, k_ref, v_ref, b_ref, o_ref,
                   m_s, l_s, acc_s, par_s, exact_s, *, tq, spb):
    s = pl.program_id(0)
    first = kb_ref[s] == 0

    biases = [b_ref[c].astype(F32)[None] for c in range(spb)]
    ones = jnp.ones((spb * LANES, LANES), BF16)

    def masked_scores(g):
        qg = [q_ref[:, (g * GROUP + h) * HEAD_DIM:(g * GROUP + h + 1) * HEAD_DIM] for h in range(GROUP)]
        q4 = jnp.concatenate(qg, axis=0)
        kg = k_ref[:, g * HEAD_DIM:(g + 1) * HEAD_DIM]
        sc = lax.dot_general(q4, kg, (((1,), (1,)), ((), ())), preferred_element_type=F32)
        return [(sc[:, c * LANES:(c + 1) * LANES].reshape(GROUP, tq, LANES) + biases[c]).reshape(GROUP * tq, LANES)
                for c in range(spb)]

    def weighted_values(g, m_ref, slabs):
        pmat = jnp.concatenate([jnp.exp2(sl - m_ref).astype(BF16) for sl in slabs], axis=1)
        v1 = jnp.concatenate([v_ref[:, g * HEAD_DIM:(g + 1) * HEAD_DIM], ones], axis=1)
        pv = jnp.dot(pmat, v1, preferred_element_type=F32)
        return pv[:, 0:HEAD_DIM], pv[:, HEAD_DIM:HEAD_DIM + LANES]

    @pl.when(first)
    def _():
        par_s[0] = 0
        exact_s[0] = 1
        m_s[...] = jnp.full(m_s.shape, 0.1 * NEG_INF, F32)
        l_s[0] = jnp.zeros(l_s.shape[1:], F32)
        acc_s[0] = jnp.zeros(acc_s.shape[1:], F32)

    @pl.when(jnp.logical_not(first))
    def _():
        par = par_s[0]
        worst = jnp.zeros((GROUP * tq, LANES), F32)
        for g in range(N_KV_HEADS):
            pv, psum = weighted_values(g, m_s[g], masked_scores(g))
            acc_new = acc_s[par, g] + pv
            l_new = l_s[par, g] + psum
            acc_s[1 - par, g] = acc_new
            l_s[1 - par, g] = l_new
            worst = worst + l_new + jnp.abs(acc_new)
        ok = jnp.min(jnp.where(worst < FIXED_REF_LIMIT, 1.0, 0.0)) > 0.5
        exact_s[0] = jnp.where(ok, 0, 1)
        par_s[0] = jnp.where(ok, 1 - par, par)

    @pl.when(exact_s[0] == 1)
    def _():
        par = par_s[0]
        for g in range(N_KV_HEADS):
            slabs = masked_scores(g)
            mx = slabs[0]
            for sl in slabs[1:]:
                mx = jnp.maximum(mx, sl)
            m_prev = m_s[g]
            m_new = jnp.maximum(m_prev, jnp.max(mx, axis=1, keepdims=True))
            alpha = jnp.exp2(m_prev - m_new)
            pv, psum = weighted_values(g, m_new, slabs)
            acc_s[par, g] = alpha * acc_s[par, g] + pv
            l_s[par, g] = alpha * l_s[par, g] + psum
            m_s[g] = m_new

    @pl.when(last_ref[s] == 1)
    def _():
        par = par_s[0]
        for g in range(N_KV_HEADS):
            o = acc_s[par, g] / l_s[par, g]
            for h in range(GROUP):
                col = (g * GROUP + h) * HEAD_DIM
                o_ref[:, col:col + HEAD_DIM] = o[h * tq:(h + 1) * tq].astype(BF16)


def _attend(q, k_all, v_all, bias, sched, *, tq, spb):
    t = q.shape[0]
    qb, kb, kbat, last = sched
    nsteps = qb.shape[0]
    blk = spb * LANES
    grid_spec = pltpu.PrefetchScalarGridSpec(
        num_scalar_prefetch=4,
        grid=(nsteps,),
        in_specs=[
            pl.BlockSpec((tq, Q_W), lambda s, qb, kb, kbat, last: (qb[s], 0)),
            pl.BlockSpec((None, blk, KV_W), lambda s, qb, kb, kbat, last: (kbat[s], kb[s], 0)),
            pl.BlockSpec((None, blk, KV_W), lambda s, qb, kb, kbat, last: (kbat[s], kb[s], 0)),
            pl.BlockSpec((spb, tq, LANES), lambda s, qb, kb, kbat, last: (kb[s], qb[s], 0)),
        ],
        out_specs=pl.BlockSpec((tq, Q_W), lambda s, qb, kb, kbat, last: (qb[s], 0)),
        scratch_shapes=[
            pltpu.VMEM((N_KV_HEADS, GROUP * tq, LANES), F32),
            pltpu.VMEM((2, N_KV_HEADS, GROUP * tq, LANES), F32),
            pltpu.VMEM((2, N_KV_HEADS, GROUP * tq, HEAD_DIM), F32),
            pltpu.SMEM((1,), jnp.int32),
            pltpu.SMEM((1,), jnp.int32),
        ],
    )
    return pl.pallas_call(
        functools.partial(_attend_kernel, tq=tq, spb=spb),
        grid_spec=grid_spec,
        out_shape=jax.ShapeDtypeStruct((t, Q_W), BF16),
        compiler_params=pltpu.CompilerParams(
            dimension_semantics=("arbitrary",), vmem_limit_bytes=VMEM_LIMIT),
        name="attend",
    )(qb, kb, kbat, last, q, k_all, v_all, bias)


def _causal_schedule(t, tq, blk):
    qb, kb, last = [], [], []
    for n in range(t // tq):
        nk = ((n + 1) * tq + blk - 1) // blk
        for k in range(nk):
            qb.append(n)
            kb.append(k)
            last.append(1 if k == nk - 1 else 0)
    z = np.zeros(len(qb), np.int32)
    return (jnp.asarray(qb, jnp.int32), jnp.asarray(kb, jnp.int32), jnp.asarray(z), jnp.asarray(last, jnp.int32))


def _batched_schedule(nbatch, nk):
    qb = np.repeat(np.arange(nbatch, dtype=np.int32), nk)
    kb = np.tile(np.arange(nk, dtype=np.int32), nbatch)
    last = (kb == nk - 1).astype(np.int32)
    return (jnp.asarray(qb), jnp.asarray(kb), jnp.asarray(qb), jnp.asarray(last))


def _merge_kernel(o_ref, yc_ref, ga_ref, gc_ref, woa_ref, woc_ref, wout_ref, x_ref, gq_ref, gp_ref,
                  x1_ref, xn2_ref, mg_s, m_s):
    j = pl.program_id(1)
    nblk = D_MODEL // MERGE_TN

    @pl.when(j < nblk)
    def _():
        a = jnp.dot(o_ref[...], woa_ref[...], preferred_element_type=F32)
        c = jnp.dot(yc_ref[...], woc_ref[...], preferred_element_type=F32)
        mg_s[j] = (ga_ref[...] * a + gc_ref[...] * c).astype(BF16)

    @pl.when(j >= nblk)
    def _():
        mg = jnp.concatenate([mg_s[b] for b in range(nblk)], axis=1)
        m_s[j - nblk] = jnp.dot(mg, wout_ref[...], preferred_element_type=F32)

    @pl.when(j == 2 * nblk - 1)
    def _():
        m = jnp.concatenate([m_s[b] for b in range(nblk)], axis=1)
        x1 = x_ref[...] + _rms(m, gq_ref[...])
        x1_ref[...] = x1
        xn2_ref[...] = _rms(x1, gp_ref[...]).astype(BF16)


def _merge(o, yc, ga, gc, woa, woc, wout, x, gq, gp):
    t = x.shape[0]
    tn = MERGE_TN
    nblk = D_MODEL // tn

    def row(i, j):
        return (i, 0)

    def lo(i, j):
        return (i, jnp.minimum(j, nblk - 1))

    return pl.pallas_call(
        _merge_kernel,
        grid=(t // TM, 2 * nblk),
        in_specs=[
            pl.BlockSpec((TM, Q_W), row),
            pl.BlockSpec((TM, D_MODEL), row),
            pl.BlockSpec((TM, tn), lo),
            pl.BlockSpec((TM, tn), lo),
            pl.BlockSpec((Q_W, tn), lambda i, j: (0, jnp.minimum(j, nblk - 1))),
            pl.BlockSpec((D_MODEL, tn), lambda i, j: (0, jnp.minimum(j, nblk - 1))),
            pl.BlockSpec((D_MODEL, tn), lambda i, j: (0, jnp.maximum(j - nblk, 0))),
            pl.BlockSpec((TM, D_MODEL), row),
            pl.BlockSpec((1, D_MODEL), lambda i, j: (0, 0)),
            pl.BlockSpec((1, D_MODEL), lambda i, j: (0, 0)),
        ],
        out_specs=[pl.BlockSpec((TM, D_MODEL), row), pl.BlockSpec((TM, D_MODEL), row)],
        out_shape=[jax.ShapeDtypeStruct((t, D_MODEL), F32), jax.ShapeDtypeStruct((t, D_MODEL), BF16)],
        scratch_shapes=[pltpu.VMEM((nblk, TM, tn), BF16), pltpu.VMEM((nblk, TM, tn), F32)],
        compiler_params=pltpu.CompilerParams(
            dimension_semantics=("arbitrary", "arbitrary"), vmem_limit_bytes=VMEM_LIMIT),
        name="merge",
    )(o, yc, ga, gc, woa, woc, wout, x, gq, gp)


def _ffn_kernel(xn_ref, wg_ref, wv_ref, cwg_ref, cwv_ref, wd_ref, x1_ref, gq_ref, stg_ref, stv_ref,
                y_ref, ng_ref, nv_ref, acc_s, cg_s, cv_s, *, seg, carried):
    i = pl.program_id(0)
    jf = pl.program_id(1)
    nseg = TM // seg

    @pl.when(jf == 0)
    def _():
        acc_s[...] = jnp.zeros(acc_s.shape, F32)

    if carried:
        @pl.when(i == 0)
        def _():
            for carry_s, st_ref in ((cg_s, stg_ref), (cv_s, stv_ref)):
                carry_s[jf] = jnp.zeros((8, TF), F32)
                carry_s[jf, 0:2, :] = st_ref[0]

    xn = xn_ref[...]

    def branch(w_ref, cw_ref, st_ref, carry_s, new_ref):
        up = jnp.dot(xn, w_ref[...], preferred_element_type=F32)
        if carried:
            prevs = [carry_s[jf, 0:2, :]]
        else:
            prevs = [st_ref[s] for s in range(nseg)]
        y, news = _conv_tile(up, cw_ref[...], prevs, seg)
        for s in range(nseg):
            new_ref[s] = news[s]
        if carried:
            carry_s[jf, 0:2, :] = news[0]
        return y

    gate = branch(wg_ref, cwg_ref, stg_ref, cg_s, ng_ref)
    val = branch(wv_ref, cwv_ref, stv_ref, cv_s, nv_ref)
    c0 = np.float32(np.sqrt(2.0 / np.pi))
    gelu = 0.5 * gate * (1.0 + jnp.tanh(c0 * (gate + 0.044715 * (gate * gate * gate))))
    hid = (gelu * val).astype(BF16)
    acc_s[...] += jnp.dot(hid, wd_ref[...], preferred_element_type=F32)

    @pl.when(jf == pl.num_programs(1) - 1)
    def _():
        y_ref[...] = x1_ref[...] + _rms(acc_s[...], gq_ref[...])


def _ffn(xn2, w_up, cw, w_down, x1, gq, state, seg):
    t = x1.shape[0]
    nf = D_FF // TF
    carried = seg == t
    nseg_tile = 1 if carried else TM // seg

    def row(i, j):
        return (i, 0)

    def stg(i, j):
        return (0 if carried else i, 0, j)

    def stv(i, j):
        return (0 if carried else i, 0, j + nf)

    return pl.pallas_call(
        functools.partial(_ffn_kernel, seg=min(seg, TM), carried=carried),
        grid=(t // TM, nf),
        in_specs=[
            pl.BlockSpec((TM, D_MODEL), row),
            pl.BlockSpec((D_MODEL, TF), lambda i, j: (0, j)),
            pl.BlockSpec((D_MODEL, TF), lambda i, j: (0, j + nf)),
            pl.BlockSpec((3, TF), lambda i, j: (0, j)),
            pl.BlockSpec((3, TF), lambda i, j: (0, j + nf)),
            pl.BlockSpec((TF, D_MODEL), lambda i, j: (j, 0)),
            pl.BlockSpec((TM, D_MODEL), row),
            pl.BlockSpec((1, D_MODEL), lambda i, j: (0, 0)),
            pl.BlockSpec((nseg_tile, 2, TF), stg),
            pl.BlockSpec((nseg_tile, 2, TF), stv),
        ],
        out_specs=[
            pl.BlockSpec((TM, D_MODEL), row),
            pl.BlockSpec((nseg_tile, 2, TF), lambda i, j: (i, 0, j)),
            pl.BlockSpec((nseg_tile, 2, TF), lambda i, j: (i, 0, j)),
        ],
        out_shape=[
            jax.ShapeDtypeStruct((t, D_MODEL), F32),
            jax.ShapeDtypeStruct((t // TM * nseg_tile, 2, D_FF), F32),
            jax.ShapeDtypeStruct((t // TM * nseg_tile, 2, D_FF), F32),
        ],
        scratch_shapes=[
            pltpu.VMEM((TM, D_MODEL), F32),
            pltpu.VMEM((nf, 8, TF), F32),
            pltpu.VMEM((nf, 8, TF), F32),
        ],
        compiler_params=pltpu.CompilerParams(
            dimension_semantics=("arbitrary", "arbitrary"), vmem_limit_bytes=VMEM_LIMIT),
        name="ffn",
    )(xn2, w_up, w_up, cw, cw, w_down, x1, gq, state, state)


PACK_ROWS = 1024


def _pack_kernel(ck_ref, cv_ref, ko_ref, vo_ref, *, rows):
    for src, dst in ((ck_ref, ko_ref), (cv_ref, vo_ref)):
        for g in range(N_KV_HEADS):
            dst[:, g * HEAD_DIM:(g + 1) * HEAD_DIM] = src[pl.ds(g, rows, stride=N_KV_HEADS), :].astype(BF16)


def _tail_kernel(kn_ref, vn_ref, ki_ref, vi_ref, ko_ref, vo_ref, *, seg):
    del ki_ref, vi_ref
    for src, dst in ((kn_ref, ko_ref), (vn_ref, vo_ref)):
        dst[0:seg, :] = src[...]
        if dst.shape[0] > seg:
            dst[seg:, :] = jnp.zeros((dst.shape[0] - seg, KV_W), BF16)


def _pack_cache(cache_k, cache_v, k_new, v_new, seg, lk):
    nb, plen = cache_k.shape[0], cache_k.shape[1]
    rows = int(np.gcd(plen, PACK_ROWS))
    tail = lk - plen
    assert rows % 16 == 0 and tail % 16 == 0 and plen % tail == 0 and tail >= seg
    cspec = pl.BlockSpec((None, rows * N_KV_HEADS, HEAD_DIM), lambda b, r: (b, r, 0))
    ospec = pl.BlockSpec((None, rows, KV_W), lambda b, r: (b, r, 0))
    flat = (nb, plen * N_KV_HEADS, HEAD_DIM)
    slab = jax.ShapeDtypeStruct((nb, lk, KV_W), BF16)
    k_all, v_all = pl.pallas_call(
        functools.partial(_pack_kernel, rows=rows),
        grid=(nb, plen // rows),
        in_specs=[cspec, cspec],
        out_specs=[ospec, ospec],
        out_shape=[slab, slab],
        compiler_params=pltpu.CompilerParams(
            dimension_semantics=("arbitrary", "arbitrary"), vmem_limit_bytes=VMEM_LIMIT),
        name="pack_cache",
    )(cache_k.reshape(flat), cache_v.reshape(flat))
    nspec = pl.BlockSpec((seg, KV_W), lambda b: (b, 0))
    anyspec = pl.BlockSpec(memory_space=pl.ANY)
    tspec = pl.BlockSpec((None, tail, KV_W), lambda b: (b, plen // tail, 0))
    return pl.pallas_call(
        functools.partial(_tail_kernel, seg=seg),
        grid=(nb,),
        in_specs=[nspec, nspec, anyspec, anyspec],
        out_specs=[tspec, tspec],
        out_shape=[slab, slab],
        input_output_aliases={2: 0, 3: 1},
        compiler_params=pltpu.CompilerParams(dimension_semantics=("arbitrary",), vmem_limit_bytes=VMEM_LIMIT),
        name="pack_tail",
    )(k_new, v_new, k_all, v_all)


def _rope_tables(pos, rot, width):
    half = rot // 2
    freqs = ROPE_THETA ** (-jnp.arange(half, dtype=F32) / half)
    ang = pos.astype(F32)[:, None] * freqs[None, :]
    cos, sin = jnp.cos(ang), jnp.sin(ang)
    t = pos.shape[0]
    c = jnp.concatenate([cos, cos, jnp.ones((t, width - rot), F32)], axis=1)
    d = jnp.concatenate([-sin, sin, jnp.zeros((t, width - rot), F32)], axis=1)
    reps = LANES // width
    return jnp.tile(c, (1, reps)), jnp.tile(d, (1, reps))


def _relayout_w_in(w):
    o = np.cumsum([0, Q_W, KV_W, KV_W, IQ_W, IDX_DIM, N_IDX_HEADS, D_MODEL, D_MODEL, D_MODEL, D_MODEL, D_MODEL])
    q, k, v, qi = (w[:, o[a]:o[a + 1]] for a in range(4))
    kiwi = w[:, o[4]:o[6]]
    cb, cc, ch, ga, gc = (w[:, o[a]:o[a + 1]] for a in range(6, 11))
    parts = [q, k, v, qi, kiwi, jnp.zeros((D_MODEL, TN - kiwi.shape[1]), w.dtype)]
    for c in range(N_CBLK):
        sl = slice(c * TN, (c + 1) * TN)
        parts += [cb[:, sl], cc[:, sl], ch[:, sl]]
    parts += [ga, gc]
    return jnp.concatenate(parts, axis=1).astype(BF16)


def _stream(x, pos, seg, past, weights, *, tq, tq_att, spb_sel, spb_att):
    (g_mp, g_mq, w_in_r, conv_w, woa, woc, wout, g_fp, g_fq, w_up, fconv_w, w_down) = weights
    t = x.shape[0]
    nseq = t // seg
    tabs = _rope_tables(pos, ROT_DIM, HEAD_DIM) + _rope_tables(pos, IDX_ROT_DIM, IDX_DIM)
    if past is None:
        conv_state = jnp.zeros((nseq, 2, D_MODEL), F32)
        ffn_state = jnp.zeros((nseq, 2, 2 * D_FF), F32)
    else:
        conv_state, ffn_state = past[3], past[4]

    (q, k, kb, v, vb, qi, kiwi, kia, kib, yc, conv_new, ga, gc) = _proj(
        x, g_mp, w_in_r, tabs, conv_w, conv_state, seg)

    if past is None:
        k_all, v_all = kb[None], vb[None]
        kia_all, kib_all = kia[None], kib[None]
        n_keys = t
        sched = _causal_schedule(t, tq_att, spb_att * LANES)
        q_pos0 = 0
    else:
        cache_k, cache_v, cache_ki = past[0], past[1], past[2]
        plen = cache_k.shape[1]
        n_keys = plen + seg
        lk = -(-n_keys // (spb_att * LANES)) * (spb_att * LANES)
        pad = lk - n_keys

        def cat(c, new, width):
            parts = [c, new.reshape(nseq, seg, width)]
            if pad:
                parts.append(jnp.zeros((nseq, pad, width), BF16))
            return jnp.concatenate(parts, axis=1)

        k_all, v_all = _pack_cache(cache_k, cache_v, kb, vb, seg, lk)
        cki = cache_ki.astype(BF16)
        zk = jnp.zeros_like(cki)
        kia_all = cat(jnp.concatenate([cki, zk], axis=-1), kia, LANES)
        kib_all = cat(jnp.concatenate([zk, cki], axis=-1), kib, LANES)
        sched = _batched_schedule(nseq, lk // (spb_att * LANES))
        q_pos0 = plen
    topk = min(TOPK_MAX, n_keys // 4)

    bias = _select(qi, kiwi, kia_all, kib_all, tq=tq, spb=spb_sel, topk=topk,
                   causal=past is None, n_valid=n_keys, q_pos0=q_pos0)
    o = _attend(q, k_all, v_all, bias, sched, tq=tq_att, spb=spb_att)
    x1, xn2 = _merge(o, yc, ga, gc, woa, woc, wout, x, g_mq, g_fp)
    y, ffn_g, ffn_v = _ffn(xn2, w_up, fconv_w, w_down, x1, g_fq, ffn_state, seg)
    ffn_new = jnp.concatenate([ffn_g, ffn_v], axis=-1)
    return y, k, v, kiwi[:, :IDX_DIM], conv_new[-nseq:], ffn_new[-nseq:]


def kernel(x_prompt, x_sample, cache_k, cache_v, cache_k_idx, state_conv, state_ffn_conv, norm_mix_pre, norm_mix_post, w_in, conv_w, w_o_attn, w_o_conv, w_out, norm_ffn_pre, norm_ffn_post, w_ffn_up, ffn_conv_w, w_ffn_down):
    depth = w_in.shape[0]
    assert depth == 1, "single-layer step"
    b, seq, _ = x_prompt.shape
    assert b == 1
    db, dseq, _ = x_sample.shape
    plen = cache_k.shape[2]
    assert dseq == CHUNK and plen % CHUNK == 0

    weights = (
        norm_mix_pre, norm_mix_post, _relayout_w_in(w_in[0]), conv_w[0],
        w_o_attn[0].astype(BF16), w_o_conv[0].astype(BF16), w_out[0].astype(BF16),
        norm_ffn_pre, norm_ffn_post, w_ffn_up[0].astype(BF16), ffn_conv_w[0], w_ffn_down[0].astype(BF16),
    )

    pos_p = jnp.arange(seq, dtype=jnp.int32)
    yp, kp, vp, kip, convp, ffnp = _stream(
        x_prompt.reshape(seq, D_MODEL), pos_p, seq, None, weights, tq=128, tq_att=256, spb_sel=4, spb_att=8)

    pos_s = jnp.tile(jnp.arange(dseq, dtype=jnp.int32) + plen, db)
    past = (cache_k[0], cache_v[0], cache_k_idx[0], state_conv[0], state_ffn_conv[0])
    n_keys = plen + dseq
    spb_s = _sample_slabs(n_keys)
    ys, ks, vs, kis, convs, ffns = _stream(
        x_sample.reshape(db * dseq, D_MODEL), pos_s, dseq, past, weights,
        tq=dseq, tq_att=dseq, spb_sel=spb_s, spb_att=spb_s)

    return (
        yp.reshape(1, seq, D_MODEL), ys.reshape(db, dseq, D_MODEL),
        kp.reshape(1, 1, seq, N_KV_HEADS, HEAD_DIM), vp.reshape(1, 1, seq, N_KV_HEADS, HEAD_DIM),
        kip.reshape(1, 1, seq, IDX_DIM), convp.reshape(1, 1, 2, D_MODEL), ffnp.reshape(1, 1, 2, 2 * D_FF),
        ks.reshape(1, db, dseq, N_KV_HEADS, HEAD_DIM), vs.reshape(1, db, dseq, N_KV_HEADS, HEAD_DIM),
        kis.reshape(1, db, dseq, IDX_DIM), convs.reshape(1, db, 2, D_MODEL), ffns.reshape(1, db, 2, 2 * D_FF),
    )


def _sample_slabs(n_keys):
    nslab = -(-n_keys // LANES)
    best = 1
    for d in range(1, nslab + 1):
        if nslab % d == 0 and d <= 11:
            best = d
    return best
```

```python
import functools

import jax
import jax.numpy as jnp
import numpy as np
from jax import lax
from jax.experimental import pallas as pl
from jax.experimental.pallas import tpu as pltpu

F32 = jnp.float32
BF16 = jnp.bfloat16

D_MODEL = 2048
N_HEADS = 16
N_KV_HEADS = 4
HEAD_DIM = 128
ROT_DIM = HEAD_DIM // 4
N_IDX_HEADS = 16
IDX_DIM = 64
IDX_ROT_DIM = IDX_DIM // 4
CHUNK = 64
TOPK_MAX = 256
ROPE_THETA = 500000.0
D_FF = 5632
RMS_EPS = 1e-6
NEG_INF = -1e30
Q_W = N_HEADS * HEAD_DIM
KV_W = N_KV_HEADS * HEAD_DIM
IQ_W = N_IDX_HEADS * IDX_DIM
GROUP = N_HEADS // N_KV_HEADS

LANES = 128
TM = 512
TN = 1024
MERGE_TN = 512
TF = 512
VMEM_LIMIT = 56 * 1024 * 1024

J_Q = 0
J_KV = J_Q + Q_W // TN
assert 2 * KV_W == TN
J_QI = J_KV + 1
J_CONV = J_QI + IQ_W // TN
N_CBLK = D_MODEL // TN
J_GA = J_CONV + 3 * N_CBLK
J_GC = J_GA + N_CBLK
NJ = J_GC + N_CBLK

Q_PRESCALE = float(np.float32((HEAD_DIM ** -0.5) * np.log2(np.e)))


def _rms(x, g):
    return x * lax.rsqrt(jnp.mean(x * x, axis=-1, keepdims=True) + RMS_EPS) * g


def _sortable(x):
    bits = lax.bitcast_convert_type(x, jnp.int32)
    return jnp.where(bits < 0, bits ^ jnp.int32(0x7FFFFFFF), bits)


def _unsortable(k):
    return lax.bitcast_convert_type(jnp.where(k < 0, k ^ jnp.int32(0x7FFFFFFF), k), F32)


CAND = 16
PAD_SCORE = -3.0e38


def _oddeven_merge_sort_pairs(n):
    out, p = [], 1
    while p < n:
        k = p
        while k >= 1:
            for j in range(k % p, n - k, 2 * k):
                for i in range(min(k, n - j - k)):
                    if (i + j) // (2 * p) == (i + j + k) // (2 * p):
                        out.append((i + j, i + j + k))
            k //= 2
        p *= 2
    return out


_SORT_PAIRS = _oddeven_merge_sort_pairs(CAND)


def _sort_desc(v):
    v = list(v)
    for i, j in _SORT_PAIRS:
        v[i], v[j] = jnp.maximum(v[i], v[j]), jnp.minimum(v[i], v[j])
    return v


def _bitonic_merge_desc(v):
    v = list(v)
    d = len(v) // 2
    while d >= 1:
        for i in range(len(v)):
            if not i & d:
                v[i], v[i + d] = jnp.maximum(v[i], v[i + d]), jnp.minimum(v[i], v[i + d])
        d //= 2
    return v


def _dwconv_seg(u, prev2, w3):
    row = lax.broadcasted_iota(jnp.int32, u.shape, 0)
    p0, p1 = prev2[0:1], prev2[1:2]
    s1 = jnp.where(row == 0, p1, pltpu.roll(u, 1, 0))
    s2 = jnp.where(row == 0, p0, jnp.where(row == 1, p1, pltpu.roll(u, 2, 0)))
    return w3[0:1] * s2 + w3[1:2] * s1 + w3[2:3] * u


def _conv_tile(u, w3, prevs, seg):
    nseg = len(prevs)
    ys, news = [], []
    for s in range(nseg):
        us = u[s * seg:(s + 1) * seg]
        ys.append(_dwconv_seg(us, prevs[s], w3))
        news.append(us[seg - 2:seg])
    y = ys[0] if nseg == 1 else jnp.concatenate(ys, axis=0)
    return y, news


def _proj_kernel(x_ref, g_ref, w_ref, wk_ref, c128_ref, d128_ref, c64_ref, d64_ref, cw_ref, st_ref,
                 q_ref, k_ref, kb_ref, v_ref, vb_ref, qi_ref, kiwi_ref, kia_ref, kib_ref,
                 yc_ref, cn_ref, ga_ref, gc_ref,
                 xn_s, cb_s, cc_s, carry_s, *, seg, carried):
    i = pl.program_id(0)
    j = pl.program_id(1)

    @pl.when(j == 0)
    def _():
        xn_s[...] = _rms(x_ref[...], g_ref[...]).astype(BF16)

    def mm():
        return jnp.dot(xn_s[...], w_ref[...], preferred_element_type=F32)

    lane = lax.broadcasted_iota(jnp.int32, (TM, LANES), 1)

    def rope128(xh):
        partner = jnp.where(lane < ROT_DIM // 2, pltpu.roll(xh, LANES - ROT_DIM // 2, 1),
                            pltpu.roll(xh, ROT_DIM // 2, 1))
        return xh * c128_ref[...] + partner * d128_ref[...]

    def rope64(xh, c, d):
        first = (lane & (IDX_DIM - 1)) < IDX_ROT_DIM // 2
        partner = jnp.where(first, pltpu.roll(xh, LANES - IDX_ROT_DIM // 2, 1),
                            pltpu.roll(xh, IDX_ROT_DIM // 2, 1))
        return xh * c + partner * d

    @pl.when(j < J_KV)
    def _():
        acc = mm()
        for h in range(TN // LANES):
            sl = slice(h * LANES, (h + 1) * LANES)
            q_ref[:, sl] = (rope128(acc[:, sl]) * Q_PRESCALE).astype(BF16)

    @pl.when(j == J_KV)
    def _():
        acc = mm()
        for h in range(N_KV_HEADS):
            sl = slice(h * HEAD_DIM, (h + 1) * HEAD_DIM)
            r = rope128(acc[:, sl])
            k_ref[pl.ds(h, TM, stride=N_KV_HEADS), :] = r
            kb_ref[:, sl] = r.astype(BF16)
            v = acc[:, KV_W + h * HEAD_DIM:KV_W + (h + 1) * HEAD_DIM]
            v_ref[pl.ds(h, TM, stride=N_KV_HEADS), :] = v
            vb_ref[:, sl] = v.astype(BF16)

    @pl.when(jnp.logical_and(j >= J_QI, j < J_CONV))
    def _():
        acc = mm()
        for h in range(TN // LANES):
            sl = slice(h * LANES, (h + 1) * LANES)
            qi_ref[:, sl] = rope64(acc[:, sl], c64_ref[...], d64_ref[...]).astype(BF16)

    @pl.when(j == J_QI)
    def _():
        is_ki = lane < IDX_DIM
        c = jnp.where(is_ki, c64_ref[...], 1.0)
        d = jnp.where(is_ki, d64_ref[...], 0.0)
        acc = jnp.dot(xn_s[...], wk_ref[...], preferred_element_type=F32)
        r = rope64(acc, c, d)
        kiwi_ref[...] = r
        ka = jnp.where(is_ki, r, 0.0)
        kia_ref[...] = ka.astype(BF16)
        kib_ref[...] = pltpu.roll(ka, IDX_DIM, 1).astype(BF16)

    jc = jnp.maximum(j - J_CONV, 0)
    in_conv = jnp.logical_and(j >= J_CONV, j < J_GA)
    cblk = jnp.minimum(jc // 3, N_CBLK - 1)
    part = jc % 3

    @pl.when(jnp.logical_and(in_conv, part == 0))
    def _():
        cb_s[...] = mm()

    @pl.when(jnp.logical_and(in_conv, part == 1))
    def _():
        cc_s[...] = mm()

    @pl.when(jnp.logical_and(in_conv, part == 2))
    def _():
        nseg = TM // seg
        if carried:
            @pl.when(i == 0)
            def _():
                carry_s[cblk] = jnp.zeros((8, TN), F32)
                carry_s[cblk, 0:2, :] = st_ref[0]
            prevs = [carry_s[cblk, 0:2, :]]
        else:
            prevs = [st_ref[s] for s in range(nseg)]
        u = cc_s[...] * mm()
        y, news = _conv_tile(u, cw_ref[...], prevs, seg)
        yc_ref[...] = (cb_s[...] * y).astype(BF16)
        for s in range(nseg):
            cn_ref[s] = news[s]
        if carried:
            carry_s[cblk, 0:2, :] = news[0]

    @pl.when(jnp.logical_and(j >= J_GA, j < J_GC))
    def _():
        ga_ref[...] = 1.0 / (1.0 + jnp.exp(-mm()))

    @pl.when(j >= J_GC)
    def _():
        gc_ref[...] = 1.0 / (1.0 + jnp.exp(-mm()))


def _proj(x, g, w_pair, tabs, conv_w, state, seg):
    w, w_kiwi = w_pair
    t = x.shape[0]
    ni = t // TM
    carried = seg == t
    nseg_tile = 1 if carried else TM // seg
    c128, d128, c64, d64 = tabs

    def row(i, j):
        return (i, 0)

    def const(i, j):
        return (0, 0)

    def cblk(j):
        return jnp.minimum(jnp.maximum(j - J_CONV, 0) // 3, N_CBLK - 1)

    def st_map(i, j):
        return (0 if carried else i, 0, cblk(j))

    in_specs = [
        pl.BlockSpec((TM, D_MODEL), row),
        pl.BlockSpec((1, D_MODEL), const),
        pl.BlockSpec((D_MODEL, TN), lambda i, j: (0, j)),
        pl.BlockSpec((D_MODEL, LANES), const),
        pl.BlockSpec((TM, LANES), row),
        pl.BlockSpec((TM, LANES), row),
        pl.BlockSpec((TM, LANES), row),
        pl.BlockSpec((TM, LANES), row),
        pl.BlockSpec((3, TN), lambda i, j: (0, cblk(j))),
        pl.BlockSpec((nseg_tile, 2, TN), st_map),
    ]
    out_shape = [
        jax.ShapeDtypeStruct((t, Q_W), BF16),
        jax.ShapeDtypeStruct((t * N_KV_HEADS, HEAD_DIM), F32),
        jax.ShapeDtypeStruct((t, KV_W), BF16),
        jax.ShapeDtypeStruct((t * N_KV_HEADS, HEAD_DIM), F32),
        jax.ShapeDtypeStruct((t, KV_W), BF16),
        jax.ShapeDtypeStruct((t, IQ_W), BF16),
        jax.ShapeDtypeStruct((t, LANES), F32),
        jax.ShapeDtypeStruct((t, LANES), BF16),
        jax.ShapeDtypeStruct((t, LANES), BF16),
        jax.ShapeDtypeStruct((t, D_MODEL), BF16),
        jax.ShapeDtypeStruct((ni * nseg_tile, 2, D_MODEL), F32),
        jax.ShapeDtypeStruct((t, D_MODEL), F32),
        jax.ShapeDtypeStruct((t, D_MODEL), F32),
    ]
    out_specs = [
        pl.BlockSpec((TM, TN), lambda i, j: (i, jnp.clip(j - J_Q, 0, J_KV - J_Q - 1))),
        pl.BlockSpec((TM * N_KV_HEADS, HEAD_DIM), row),
        pl.BlockSpec((TM, KV_W), row),
        pl.BlockSpec((TM * N_KV_HEADS, HEAD_DIM), row),
        pl.BlockSpec((TM, KV_W), row),
        pl.BlockSpec((TM, TN), lambda i, j: (i, jnp.clip(j - J_QI, 0, J_CONV - J_QI - 1))),
        pl.BlockSpec((TM, LANES), row),
        pl.BlockSpec((TM, LANES), row),
        pl.BlockSpec((TM, LANES), row),
        pl.BlockSpec((TM, TN), lambda i, j: (i, cblk(j))),
        pl.BlockSpec((nseg_tile, 2, TN), lambda i, j: (i, 0, cblk(j))),
        pl.BlockSpec((TM, TN), lambda i, j: (i, jnp.clip(j - J_GA, 0, N_CBLK - 1))),
        pl.BlockSpec((TM, TN), lambda i, j: (i, jnp.clip(j - J_GC, 0, N_CBLK - 1))),
    ]
    return pl.pallas_call(
        functools.partial(_proj_kernel, seg=min(seg, TM), carried=carried),
        grid=(ni, NJ),
        in_specs=in_specs,
        out_specs=out_specs,
        out_shape=out_shape,
        scratch_shapes=[
            pltpu.VMEM((TM, D_MODEL), BF16),
            pltpu.VMEM((TM, TN), F32),
            pltpu.VMEM((TM, TN), F32),
            pltpu.VMEM((N_CBLK, 8, TN), F32),
        ],
        compiler_params=pltpu.CompilerParams(
            dimension_semantics=("arbitrary", "arbitrary"), vmem_limit_bytes=VMEM_LIMIT),
        name="proj",
    )(x, g, w, w_kiwi, c128, d128, c64, d64, conv_w, state)


def _select_kernel(qi_ref, kiwi_ref, kia_ref, kib_ref, bias_ref,
                   sc_s, wb_s, lg_s, lh_s, cand_s, thr_s, cnt_s, top_s, *,
                   tq, spb, nkb_total, topk, causal, n_valid, q_pos0):
    n = pl.program_id(0)
    npair = N_IDX_HEADS // 2
    blk = spb * LANES
    kiwi = kiwi_ref[...]
    wscale = (IDX_DIM ** -0.5) * (N_IDX_HEADS ** -0.5)
    for h in range(N_IDX_HEADS):
        wb_s[h] = jnp.broadcast_to(kiwi[:, IDX_DIM + h:IDX_DIM + h + 1], (tq, LANES)) * wscale
    q2 = jnp.concatenate([qi_ref[:, p * LANES:(p + 1) * LANES] for p in range(npair)], axis=0)

    if causal:
        qpos0 = n * tq
        nkb = (qpos0 + tq + blk - 1) // blk
    else:
        qpos0 = q_pos0
        nkb = nkb_total
    lane = lax.broadcasted_iota(jnp.int32, (tq, LANES), 1)
    qchunk = (qpos0 + lax.broadcasted_iota(jnp.int32, (tq, LANES), 0)) >> 6
    nt = (((1,), (1,)), ((), ()))

    def logits_into(lg_ref, kb):
        base = pl.multiple_of(jnp.minimum(kb, nkb_total - 1) * blk, blk)
        lg_ref[0] = lax.dot_general(q2, kia_ref[pl.ds(base, blk), :], nt, preferred_element_type=F32)
        lg_ref[1] = lax.dot_general(q2, kib_ref[pl.ds(base, blk), :], nt, preferred_element_type=F32)

    def head_sum(lg_ref, kb, carry):
        m1, m2 = carry
        base = kb * blk
        for c in range(spb):
            cs = slice(c * LANES, (c + 1) * LANES)
            acc = jnp.zeros((tq, LANES), F32)
            for p in range(npair):
                rs = slice(p * tq, (p + 1) * tq)
                acc = acc + jnp.maximum(lg_ref[0, rs, cs], 0.0) * wb_s[2 * p]
                acc = acc + jnp.maximum(lg_ref[1, rs, cs], 0.0) * wb_s[2 * p + 1]
            col = base + c * LANES + lane
            adm = jnp.logical_and((col >> 6) <= qchunk, col < n_valid)
            sc = jnp.where(adm, acc, NEG_INF)
            m2 = jnp.maximum(m2, jnp.minimum(m1, sc))
            m1 = jnp.maximum(m1, sc)
            sc_s[kb * spb + c] = sc
        return m1, m2

    logits_into(lg_s, 0)

    def score_pair(i, carry):
        logits_into(lh_s, 2 * i + 1)
        carry = head_sum(lg_s, 2 * i, carry)
        logits_into(lg_s, 2 * i + 2)
        return head_sum(lh_s, 2 * i + 1, carry)

    neg = jnp.full((tq, LANES), NEG_INF, F32)
    m1, m2 = lax.fori_loop(0, nkb // 2, score_pair, (neg, neg))
    top_s[0] = m1
    top_s[1] = m2

    @pl.when(nkb % 2 == 1)
    def _():
        last1, last2 = head_sum(lg_s, nkb - 1, (top_s[0], top_s[1]))
        top_s[0] = last1
        top_s[1] = last2

    m1, m2 = top_s[0], top_s[1]

    zeros = jnp.zeros((tq, LANES), F32)
    kf = float(topk)

    def count_all(thr_f, strict=False):
        def body(kb, acc):
            for c in range(spb):
                s = sc_s[kb * spb + c]
                acc = acc + jnp.where(s > thr_f if strict else s >= thr_f, 1.0, 0.0)
            return acc
        return jnp.sum(lax.fori_loop(0, nkb, body, zeros), axis=1, keepdims=True)

    def count_cand(thr_f):
        acc = zeros
        for i in range(CAND):
            acc = acc + jnp.where(cand_s[i] >= thr_f, 1.0, 0.0)
        return jnp.sum(acc, axis=1, keepdims=True)

    ones = jnp.ones((tq, LANES), jnp.int32)
    lo0 = _sortable(jnp.min(m2, axis=1, keepdims=True)) * ones
    hi0 = _sortable(jnp.max(m1, axis=1, keepdims=True)) * ones + 1

    def bisect(count):
        def unresolved(lo_k, hi_k, c_lo):
            open_ = jnp.logical_and(c_lo != kf, (hi_k - lo_k) != 1)
            return jnp.max(jnp.where(open_, 1.0, 0.0))

        def cond(st):
            return jnp.logical_and(st[0] < 33, st[-1] > 0.0)

        def body(st):
            it, lo_k, hi_k, c_lo, _ = st
            mid = lo_k + lax.shift_right_logical(hi_k - lo_k, 1)
            cnt = count(_unsortable(mid))
            ge = cnt >= kf
            lo_k = jnp.where(ge, mid, lo_k)
            hi_k = jnp.where(ge, hi_k, mid)
            c_lo = jnp.where(ge, cnt, c_lo)
            return it + 1, lo_k, hi_k, c_lo, unresolved(lo_k, hi_k, c_lo)

        c0 = jnp.full((tq, LANES), -1.0, F32)
        st = lax.while_loop(cond, body, (jnp.int32(0), lo0, hi0, c0, unresolved(lo0, hi0, c0)))
        return st[1], st[3]

    nsl = nkb * spb
    nchunk = (nsl + CAND - 1) // CAND
    pad_tile = jnp.full((tq, LANES), PAD_SCORE, F32)

    def pad(sidx, carry):
        sc_s[sidx] = pad_tile
        return carry

    lax.fori_loop(nsl, nchunk * CAND, pad, 0)

    def gather_rows(rg, carry):
        r0 = pl.multiple_of(rg * 8, 8)

        def chunk(ch, cand):
            new = _sort_desc([sc_s[ch * CAND + i, pl.ds(r0, 8), :] for i in range(CAND)])
            return tuple(_bitonic_merge_desc([jnp.maximum(cand[i], new[CAND - 1 - i]) for i in range(CAND)]))

        start = tuple(jnp.full((8, LANES), PAD_SCORE, F32) for _ in range(CAND))
        cand = lax.fori_loop(0, nchunk, chunk, start)
        for i in range(CAND):
            cand_s[i, pl.ds(r0, 8), :] = cand[i]
        return carry

    lax.fori_loop(0, tq // 8, gather_rows, 0)

    thr_c, cnt_c = bisect(count_cand)
    thr_s[...] = thr_c
    cnt_s[...] = cnt_c
    last = jnp.max(cand_s[CAND - 1], axis=1, keepdims=True)
    covered = jnp.logical_or(last < _unsortable(thr_c), last <= 0.5 * NEG_INF)

    @pl.when(jnp.min(jnp.where(covered, 1.0, 0.0)) == 0.0)
    def _():
        thr_a, cnt_a = bisect(count_all)
        thr_s[...] = thr_a
        cnt_s[...] = cnt_a

    thr = _unsortable(thr_s[...])
    c_thr = cnt_s[...]

    tied = jnp.max(jnp.where(jnp.logical_and(c_thr != kf, thr > 0.5 * NEG_INF), 1.0, 0.0)) > 0.0

    @pl.when(jnp.logical_not(tied))
    def _():
        def emit(kb, carry):
            for c in range(spb):
                s = sc_s[kb * spb + c]
                sel = jnp.logical_and(s >= thr, s > 0.5 * NEG_INF)
                bias_ref[kb * spb + c] = jnp.where(sel, 0.0, NEG_INF).astype(BF16)
            return carry

        lax.fori_loop(0, nkb, emit, 0)

    @pl.when(tied)
    def _():
        need = kf - count_all(thr, strict=True)
        tri = (lax.broadcasted_iota(jnp.int32, (LANES, LANES), 0)
               <= lax.broadcasted_iota(jnp.int32, (LANES, LANES), 1)).astype(BF16)

        def emit(kb, seen):
            for c in range(spb):
                s = sc_s[kb * spb + c]
                eq = jnp.where(s == thr, 1.0, 0.0)
                rank = seen + jnp.dot(eq.astype(BF16), tri, preferred_element_type=F32)
                keep = jnp.logical_or(s > thr, jnp.logical_and(s == thr, rank <= need))
                sel = jnp.logical_and(keep, s > 0.5 * NEG_INF)
                bias_ref[kb * spb + c] = jnp.where(sel, 0.0, NEG_INF).astype(BF16)
                seen = seen + jnp.sum(eq, axis=1, keepdims=True)
            return seen

        lax.fori_loop(0, nkb, emit, zeros)

    def fill(kb, carry):
        for c in range(spb):
            bias_ref[kb * spb + c] = jnp.full((tq, LANES), NEG_INF, BF16)
        return carry

    lax.fori_loop(nkb, nkb_total, fill, 0)


def _select(qi, kiwi, kia, kib, *, tq, spb, topk, causal, n_valid, q_pos0):
    t = qi.shape[0]
    nb = t // tq
    lk = kia.shape[1]
    nslab = lk // LANES
    nkb_total = nslab // spb
    assert topk <= 2 * LANES and nslab >= 2, "the bisection's starting lower bound needs two keys per lane"

    def kmap(n):
        return (0 if causal else n, 0, 0)

    return pl.pallas_call(
        functools.partial(_select_kernel, tq=tq, spb=spb, nkb_total=nkb_total, topk=topk,
                          causal=causal, n_valid=n_valid, q_pos0=q_pos0),
        grid=(nb,),
        in_specs=[
            pl.BlockSpec((tq, IQ_W), lambda n: (n, 0)),
            pl.BlockSpec((tq, LANES), lambda n: (n, 0)),
            pl.BlockSpec((None, lk, LANES), kmap),
            pl.BlockSpec((None, lk, LANES), kmap),
        ],
        out_specs=pl.BlockSpec((nslab, tq, LANES), lambda n: (0, n, 0)),
        out_shape=jax.ShapeDtypeStruct((nslab, t, LANES), BF16),
        scratch_shapes=[
            pltpu.VMEM((nslab + CAND, tq, LANES), F32),
            pltpu.VMEM((N_IDX_HEADS, tq, LANES), F32),
            pltpu.VMEM((2, N_IDX_HEADS // 2 * tq, spb * LANES), F32),
            pltpu.VMEM((2, N_IDX_HEADS // 2 * tq, spb * LANES), F32),
            pltpu.VMEM((CAND, tq, LANES), F32),
            pltpu.VMEM((tq, LANES), jnp.int32),
            pltpu.VMEM((tq, LANES), F32),
            pltpu.VMEM((2, tq, LANES), F32),
        ],
        compiler_params=pltpu.CompilerParams(
            dimension_semantics=("arbitrary",), vmem_limit_bytes=VMEM_LIMIT),
        name="select",
    )(qi, kiwi, kia, kib)


FIXED_REF_LIMIT = 2.0 ** 40

def _attend_kernel(qb_ref, kb_ref, kbat_ref, last_ref, q_ref, k_ref, v_ref, b_ref, o_ref,
                   m_s, l_s, acc_s, par_s, exact_s, *, tq, spb):
    s = pl.program_id(0)
    first = kb_ref[s] == 0

    biases = [b_ref[c].astype(F32)[None] for c in range(spb)]
    ones = jnp.ones((spb * LANES, LANES), BF16)

    def masked_scores(g):
        qg = [q_ref[:, (g * GROUP + h) * HEAD_DIM:(g * GROUP + h + 1) * HEAD_DIM] for h in range(GROUP)]
        q4 = jnp.concatenate(qg, axis=0)
        kg = k_ref[:, g * HEAD_DIM:(g + 1) * HEAD_DIM]
        sc = lax.dot_general(q4, kg, (((1,), (1,)), ((), ())), preferred_element_type=F32)
        return [(sc[:, c * LANES:(c + 1) * LANES].reshape(GROUP, tq, LANES) + biases[c]).reshape(GROUP * tq, LANES)
                for c in range(spb)]

    def weighted_values(g, m_ref, slabs):
        pmat = jnp.concatenate([jnp.exp2(sl - m_ref).astype(BF16) for sl in slabs], axis=1)
        v1 = jnp.concatenate([v_ref[:, g * HEAD_DIM:(g + 1) * HEAD_DIM], ones], axis=1)
        pv = jnp.dot(pmat, v1, preferred_element_type=F32)
        return pv[:, 0:HEAD_DIM], pv[:, HEAD_DIM:HEAD_DIM + LANES]

    @pl.when(first)
    def _():
        par_s[0] = 0
        exact_s[0] = 1
        m_s[...] = jnp.full(m_s.shape, 0.1 * NEG_INF, F32)
        l_s[0] = jnp.zeros(l_s.shape[1:], F32)
        acc_s[0] = jnp.zeros(acc_s.shape[1:], F32)

    @pl.when(jnp.logical_not(first))
    def _():
        par = par_s[0]
        worst = jnp.zeros((GROUP * tq, LANES), F32)
        for g in range(N_KV_HEADS):
            pv, psum = weighted_values(g, m_s[g], masked_scores(g))
            acc_new = acc_s[par, g] + pv
            l_new = l_s[par, g] + psum
            acc_s[1 - par, g] = acc_new
            l_s[1 - par, g] = l_new
            worst = worst + l_new + jnp.abs(acc_new)
        ok = jnp.min(jnp.where(worst < FIXED_REF_LIMIT, 1.0, 0.0)) > 0.5
        exact_s[0] = jnp.where(ok, 0, 1)
        par_s[0] = jnp.where(ok, 1 - par, par)

    @pl.when(exact_s[0] == 1)
    def _():
        par = par_s[0]
        for g in range(N_KV_HEADS):
            slabs = masked_scores(g)
            mx = slabs[0]
            for sl in slabs[1:]:
                mx = jnp.maximum(mx, sl)
            m_prev = m_s[g]
            m_new = jnp.maximum(m_prev, jnp.max(mx, axis=1, keepdims=True))
            alpha = jnp.exp2(m_prev - m_new)
            pv, psum = weighted_values(g, m_new, slabs)
            acc_s[par, g] = alpha * acc_s[par, g] + pv
            l_s[par, g] = alpha * l_s[par, g] + psum
            m_s[g] = m_new

    @pl.when(last_ref[s] == 1)
    def _():
        par = par_s[0]
        for g in range(N_KV_HEADS):
            o = acc_s[par, g] / l_s[par, g]
            for h in range(GROUP):
                col = (g * GROUP + h) * HEAD_DIM
                o_ref[:, col:col + HEAD_DIM] = o[h * tq:(h + 1) * tq].astype(BF16)


def _attend(q, k_all, v_all, bias, sched, *, tq, spb):
    t = q.shape[0]
    qb, kb, kbat, last = sched
    nsteps = qb.shape[0]
    blk = spb * LANES
    grid_spec = pltpu.PrefetchScalarGridSpec(
        num_scalar_prefetch=4,
        grid=(nsteps,),
        in_specs=[
            pl.BlockSpec((tq, Q_W), lambda s, qb, kb, kbat, last: (qb[s], 0)),
            pl.BlockSpec((None, blk, KV_W), lambda s, qb, kb, kbat, last: (kbat[s], kb[s], 0)),
            pl.BlockSpec((None, blk, KV_W), lambda s, qb, kb, kbat, last: (kbat[s], kb[s], 0)),
            pl.BlockSpec((spb, tq, LANES), lambda s, qb, kb, kbat, last: (kb[s], qb[s], 0)),
        ],
        out_specs=pl.BlockSpec((tq, Q_W), lambda s, qb, kb, kbat, last: (qb[s], 0)),
        scratch_shapes=[
            pltpu.VMEM((N_KV_HEADS, GROUP * tq, LANES), F32),
            pltpu.VMEM((2, N_KV_HEADS, GROUP * tq, LANES), F32),
            pltpu.VMEM((2, N_KV_HEADS, GROUP * tq, HEAD_DIM), F32),
            pltpu.SMEM((1,), jnp.int32),
            pltpu.SMEM((1,), jnp.int32),
        ],
    )
    return pl.pallas_call(
        functools.partial(_attend_kernel, tq=tq, spb=spb),
        grid_spec=grid_spec,
        out_shape=jax.ShapeDtypeStruct((t, Q_W), BF16),
        compiler_params=pltpu.CompilerParams(
            dimension_semantics=("arbitrary",), vmem_limit_bytes=VMEM_LIMIT),
        name="attend",
    )(qb, kb, kbat, last, q, k_all, v_all, bias)


def _causal_schedule(t, tq, blk):
    qb, kb, last = [], [], []
    for n in range(t // tq):
        nk = ((n + 1) * tq + blk - 1) // blk
        for k in range(nk):
            qb.append(n)
            kb.append(k)
            last.append(1 if k == nk - 1 else 0)
    z = np.zeros(len(qb), np.int32)
    return (jnp.asarray(qb, jnp.int32), jnp.asarray(kb, jnp.int32), jnp.asarray(z), jnp.asarray(last, jnp.int32))


def _batched_schedule(nbatch, nk):
    qb = np.repeat(np.arange(nbatch, dtype=np.int32), nk)
    kb = np.tile(np.arange(nk, dtype=np.int32), nbatch)
    last = (kb == nk - 1).astype(np.int32)
    return (jnp.asarray(qb), jnp.asarray(kb), jnp.asarray(qb), jnp.asarray(last))


def _merge_kernel(o_ref, yc_ref, ga_ref, gc_ref, woa_ref, woc_ref, wout_ref, x_ref, gq_ref, gp_ref,
                  x1_ref, xn2_ref, mg_s, m_s):
    j = pl.program_id(1)
    nblk = D_MODEL // MERGE_TN

    @pl.when(j < nblk)
    def _():
        a = jnp.dot(o_ref[...], woa_ref[...], preferred_element_type=F32)
        c = jnp.dot(yc_ref[...], woc_ref[...], preferred_element_type=F32)
        mg_s[j] = (ga_ref[...] * a + gc_ref[...] * c).astype(BF16)

    @pl.when(j >= nblk)
    def _():
        mg = jnp.concatenate([mg_s[b] for b in range(nblk)], axis=1)
        m_s[j - nblk] = jnp.dot(mg, wout_ref[...], preferred_element_type=F32)

    @pl.when(j == 2 * nblk - 1)
    def _():
        m = jnp.concatenate([m_s[b] for b in range(nblk)], axis=1)
        x1 = x_ref[...] + _rms(m, gq_ref[...])
        x1_ref[...] = x1
        xn2_ref[...] = _rms(x1, gp_ref[...]).astype(BF16)


def _merge(o, yc, ga, gc, woa, woc, wout, x, gq, gp):
    t = x.shape[0]
    tn = MERGE_TN
    nblk = D_MODEL // tn

    def row(i, j):
        return (i, 0)

    def lo(i, j):
        return (i, jnp.minimum(j, nblk - 1))

    return pl.pallas_call(
        _merge_kernel,
        grid=(t // TM, 2 * nblk),
        in_specs=[
            pl.BlockSpec((TM, Q_W), row),
            pl.BlockSpec((TM, D_MODEL), row),
            pl.BlockSpec((TM, tn), lo),
            pl.BlockSpec((TM, tn), lo),
            pl.BlockSpec((Q_W, tn), lambda i, j: (0, jnp.minimum(j, nblk - 1))),
            pl.BlockSpec((D_MODEL, tn), lambda i, j: (0, jnp.minimum(j, nblk - 1))),
            pl.BlockSpec((D_MODEL, tn), lambda i, j: (0, jnp.maximum(j - nblk, 0))),
            pl.BlockSpec((TM, D_MODEL), row),
            pl.BlockSpec((1, D_MODEL), lambda i, j: (0, 0)),
            pl.BlockSpec((1, D_MODEL), lambda i, j: (0, 0)),
        ],
        out_specs=[pl.BlockSpec((TM, D_MODEL), row), pl.BlockSpec((TM, D_MODEL), row)],
        out_shape=[jax.ShapeDtypeStruct((t, D_MODEL), F32), jax.ShapeDtypeStruct((t, D_MODEL), BF16)],
        scratch_shapes=[pltpu.VMEM((nblk, TM, tn), BF16), pltpu.VMEM((nblk, TM, tn), F32)],
        compiler_params=pltpu.CompilerParams(
            dimension_semantics=("arbitrary", "arbitrary"), vmem_limit_bytes=VMEM_LIMIT),
        name="merge",
    )(o, yc, ga, gc, woa, woc, wout, x, gq, gp)


def _ffn_kernel(xn_ref, wg_ref, wv_ref, cwg_ref, cwv_ref, wd_ref, x1_ref, gq_ref, stg_ref, stv_ref,
                y_ref, ng_ref, nv_ref, acc_s, cg_s, cv_s, *, seg, carried):
    i = pl.program_id(0)
    jf = pl.program_id(1)
    nseg = TM // seg

    @pl.when(jf == 0)
    def _():
        acc_s[...] = jnp.zeros(acc_s.shape, F32)

    if carried:
        @pl.when(i == 0)
        def _():
            for carry_s, st_ref in ((cg_s, stg_ref), (cv_s, stv_ref)):
                carry_s[jf] = jnp.zeros((8, TF), F32)
                carry_s[jf, 0:2, :] = st_ref[0]

    xn = xn_ref[...]

    def branch(w_ref, cw_ref, st_ref, carry_s, new_ref):
        up = jnp.dot(xn, w_ref[...], preferred_element_type=F32)
        if carried:
            prevs = [carry_s[jf, 0:2, :]]
        else:
            prevs = [st_ref[s] for s in range(nseg)]
        y, news = _conv_tile(up, cw_ref[...], prevs, seg)
        for s in range(nseg):
            new_ref[s] = news[s]
        if carried:
            carry_s[jf, 0:2, :] = news[0]
        return y

    gate = branch(wg_ref, cwg_ref, stg_ref, cg_s, ng_ref)
    val = branch(wv_ref, cwv_ref, stv_ref, cv_s, nv_ref)
    c0 = np.float32(np.sqrt(2.0 / np.pi))
    gelu = 0.5 * gate * (1.0 + jnp.tanh(c0 * (gate + 0.044715 * (gate * gate * gate))))
    hid = (gelu * val).astype(BF16)
    acc_s[...] += jnp.dot(hid, wd_ref[...], preferred_element_type=F32)

    @pl.when(jf == pl.num_programs(1) - 1)
    def _():
        y_ref[...] = x1_ref[...] + _rms(acc_s[...], gq_ref[...])


def _ffn(xn2, w_up, cw, w_down, x1, gq, state, seg):
    t = x1.shape[0]
    nf = D_FF // TF
    carried = seg == t
    nseg_tile = 1 if carried else TM // seg

    def row(i, j):
        return (i, 0)

    def stg(i, j):
        return (0 if carried else i, 0, j)

    def stv(i, j):
        return (0 if carried else i, 0, j + nf)

    return pl.pallas_call(
        functools.partial(_ffn_kernel, seg=min(seg, TM), carried=carried),
        grid=(t // TM, nf),
        in_specs=[
            pl.BlockSpec((TM, D_MODEL), row),
            pl.BlockSpec((D_MODEL, TF), lambda i, j: (0, j)),
            pl.BlockSpec((D_MODEL, TF), lambda i, j: (0, j + nf)),
            pl.BlockSpec((3, TF), lambda i, j: (0, j)),
            pl.BlockSpec((3, TF), lambda i, j: (0, j + nf)),
            pl.BlockSpec((TF, D_MODEL), lambda i, j: (j, 0)),
            pl.BlockSpec((TM, D_MODEL), row),
            pl.BlockSpec((1, D_MODEL), lambda i, j: (0, 0)),
            pl.BlockSpec((nseg_tile, 2, TF), stg),
            pl.BlockSpec((nseg_tile, 2, TF), stv),
        ],
        out_specs=[
            pl.BlockSpec((TM, D_MODEL), row),
            pl.BlockSpec((nseg_tile, 2, TF), lambda i, j: (i, 0, j)),
            pl.BlockSpec((nseg_tile, 2, TF), lambda i, j: (i, 0, j)),
        ],
        out_shape=[
            jax.ShapeDtypeStruct((t, D_MODEL), F32),
            jax.ShapeDtypeStruct((t // TM * nseg_tile, 2, D_FF), F32),
            jax.ShapeDtypeStruct((t // TM * nseg_tile, 2, D_FF), F32),
        ],
        scratch_shapes=[
            pltpu.VMEM((TM, D_MODEL), F32),
            pltpu.VMEM((nf, 8, TF), F32),
            pltpu.VMEM((nf, 8, TF), F32),
        ],
        compiler_params=pltpu.CompilerParams(
            dimension_semantics=("arbitrary", "arbitrary"), vmem_limit_bytes=VMEM_LIMIT),
        name="ffn",
    )(xn2, w_up, w_up, cw, cw, w_down, x1, gq, state, state)


PACK_ROWS = 1024


def _pack_kernel(ck_ref, cv_ref, ko_ref, vo_ref, *, rows):
    for src, dst in ((ck_ref, ko_ref), (cv_ref, vo_ref)):
        for g in range(N_KV_HEADS):
            dst[:, g * HEAD_DIM:(g + 1) * HEAD_DIM] = src[pl.ds(g, rows, stride=N_KV_HEADS), :].astype(BF16)


def _tail_kernel(kn_ref, vn_ref, ki_ref, vi_ref, ko_ref, vo_ref, *, seg):
    del ki_ref, vi_ref
    for src, dst in ((kn_ref, ko_ref), (vn_ref, vo_ref)):
        dst[0:seg, :] = src[...]
        if dst.shape[0] > seg:
            dst[seg:, :] = jnp.zeros((dst.shape[0] - seg, KV_W), BF16)


def _pack_cache(cache_k, cache_v, k_new, v_new, seg, lk):
    nb, plen = cache_k.shape[0], cache_k.shape[1]
    rows = int(np.gcd(plen, PACK_ROWS))
    tail = lk - plen
    assert rows % 16 == 0 and tail % 16 == 0 and plen % tail == 0 and tail >= seg
    cspec = pl.BlockSpec((None, rows * N_KV_HEADS, HEAD_DIM), lambda b, r: (b, r, 0))
    ospec = pl.BlockSpec((None, rows, KV_W), lambda b, r: (b, r, 0))
    flat = (nb, plen * N_KV_HEADS, HEAD_DIM)
    slab = jax.ShapeDtypeStruct((nb, lk, KV_W), BF16)
    k_all, v_all = pl.pallas_call(
        functools.partial(_pack_kernel, rows=rows),
        grid=(nb, plen // rows),
        in_specs=[cspec, cspec],
        out_specs=[ospec, ospec],
        out_shape=[slab, slab],
        compiler_params=pltpu.CompilerParams(
            dimension_semantics=("arbitrary", "arbitrary"), vmem_limit_bytes=VMEM_LIMIT),
        name="pack_cache",
    )(cache_k.reshape(flat), cache_v.reshape(flat))
    nspec = pl.BlockSpec((seg, KV_W), lambda b: (b, 0))
    anyspec = pl.BlockSpec(memory_space=pl.ANY)
    tspec = pl.BlockSpec((None, tail, KV_W), lambda b: (b, plen // tail, 0))
    return pl.pallas_call(
        functools.partial(_tail_kernel, seg=seg),
        grid=(nb,),
        in_specs=[nspec, nspec, anyspec, anyspec],
        out_specs=[tspec, tspec],
        out_shape=[slab, slab],
        input_output_aliases={2: 0, 3: 1},
        compiler_params=pltpu.CompilerParams(dimension_semantics=("arbitrary",), vmem_limit_bytes=VMEM_LIMIT),
        name="pack_tail",
    )(k_new, v_new, k_all, v_all)


def _rope_tables(pos, rot, width):
    half = rot // 2
    freqs = ROPE_THETA ** (-jnp.arange(half, dtype=F32) / half)
    ang = pos.astype(F32)[:, None] * freqs[None, :]
    cos, sin = jnp.cos(ang), jnp.sin(ang)
    t = pos.shape[0]
    c = jnp.concatenate([cos, cos, jnp.ones((t, width - rot), F32)], axis=1)
    d = jnp.concatenate([-sin, sin, jnp.zeros((t, width - rot), F32)], axis=1)
    reps = LANES // width
    return jnp.tile(c, (1, reps)), jnp.tile(d, (1, reps))


def _relayout_w_in(w):
    o = np.cumsum([0, Q_W, KV_W, KV_W, IQ_W, IDX_DIM, N_IDX_HEADS, D_MODEL, D_MODEL, D_MODEL, D_MODEL, D_MODEL])
    q, k, v, qi = (w[:, o[a]:o[a + 1]] for a in range(4))
    kiwi = w[:, o[4]:o[6]]
    cb, cc, ch, ga, gc = (w[:, o[a]:o[a + 1]] for a in range(6, 11))
    parts = [q, k, v, qi]
    for c in range(N_CBLK):
        sl = slice(c * TN, (c + 1) * TN)
        parts += [cb[:, sl], cc[:, sl], ch[:, sl]]
    parts += [ga, gc]
    w_kiwi = jnp.pad(kiwi, ((0, 0), (0, LANES - kiwi.shape[1]))).astype(BF16)
    return jnp.concatenate([p.astype(BF16) for p in parts], axis=1), w_kiwi


def _stream(x, pos, seg, past, weights, *, tq, tq_att, spb_sel, spb_att):
    (g_mp, g_mq, w_in_r, conv_w, woa, woc, wout, g_fp, g_fq, w_up, fconv_w, w_down) = weights
    t = x.shape[0]
    nseq = t // seg
    tabs = _rope_tables(pos, ROT_DIM, HEAD_DIM) + _rope_tables(pos, IDX_ROT_DIM, IDX_DIM)
    if past is None:
        conv_state = jnp.zeros((nseq, 2, D_MODEL), F32)
        ffn_state = jnp.zeros((nseq, 2, 2 * D_FF), F32)
    else:
        conv_state, ffn_state = past[3], past[4]

    (q, k, kb, v, vb, qi, kiwi, kia, kib, yc, conv_new, ga, gc) = _proj(
        x, g_mp, w_in_r, tabs, conv_w, conv_state, seg)

    if past is None:
        k_all, v_all = kb[None], vb[None]
        kia_all, kib_all = kia[None], kib[None]
        n_keys = t
        sched = _causal_schedule(t, tq_att, spb_att * LANES)
        q_pos0 = 0
    else:
        cache_k, cache_v, cache_ki = past[0], past[1], past[2]
        plen = cache_k.shape[1]
        n_keys = plen + seg
        lk = -(-n_keys // (spb_att * LANES)) * (spb_att * LANES)
        pad = lk - n_keys

        def cat(c, new, width):
            parts = [c, new.reshape(nseq, seg, width)]
            if pad:
                parts.append(jnp.zeros((nseq, pad, width), BF16))
            return jnp.concatenate(parts, axis=1)

        k_all, v_all = _pack_cache(cache_k, cache_v, kb, vb, seg, lk)
        cki = cache_ki.astype(BF16)
        zk = jnp.zeros_like(cki)
        kia_all = cat(jnp.concatenate([cki, zk], axis=-1), kia, LANES)
        kib_all = cat(jnp.concatenate([zk, cki], axis=-1), kib, LANES)
        sched = _batched_schedule(nseq, lk // (spb_att * LANES))
        q_pos0 = plen
    topk = min(TOPK_MAX, n_keys // 4)

    bias = _select(qi, kiwi, kia_all, kib_all, tq=tq, spb=spb_sel, topk=topk,
                   causal=past is None, n_valid=n_keys, q_pos0=q_pos0)
    o = _attend(q, k_all, v_all, bias, sched, tq=tq_att, spb=spb_att)
    x1, xn2 = _merge(o, yc, ga, gc, woa, woc, wout, x, g_mq, g_fp)
    y, ffn_g, ffn_v = _ffn(xn2, w_up, fconv_w, w_down, x1, g_fq, ffn_state, seg)
    ffn_new = jnp.concatenate([ffn_g, ffn_v], axis=-1)
    return y, k, v, kiwi[:, :IDX_DIM], conv_new[-nseq:], ffn_new[-nseq:]


def kernel(x_prompt, x_sample, cache_k, cache_v, cache_k_idx, state_conv, state_ffn_conv, norm_mix_pre, norm_mix_post, w_in, conv_w, w_o_attn, w_o_conv, w_out, norm_ffn_pre, norm_ffn_post, w_ffn_up, ffn_conv_w, w_ffn_down):
    depth = w_in.shape[0]
    assert depth == 1, "single-layer step"
    b, seq, _ = x_prompt.shape
    assert b == 1
    db, dseq, _ = x_sample.shape
    plen = cache_k.shape[2]
    assert dseq == CHUNK and plen % CHUNK == 0

    weights = (
        norm_mix_pre, norm_mix_post, _relayout_w_in(w_in[0]), conv_w[0],
        w_o_attn[0].astype(BF16), w_o_conv[0].astype(BF16), w_out[0].astype(BF16),
        norm_ffn_pre, norm_ffn_post, w_ffn_up[0].astype(BF16), ffn_conv_w[0], w_ffn_down[0].astype(BF16),
    )

    pos_p = jnp.arange(seq, dtype=jnp.int32)
    yp, kp, vp, kip, convp, ffnp = _stream(
        x_prompt.reshape(seq, D_MODEL), pos_p, seq, None, weights, tq=128, tq_att=256, spb_sel=4, spb_att=8)

    pos_s = jnp.tile(jnp.arange(dseq, dtype=jnp.int32) + plen, db)
    past = (cache_k[0], cache_v[0], cache_k_idx[0], state_conv[0], state_ffn_conv[0])
    n_keys = plen + dseq
    spb_s = _sample_slabs(n_keys)
    ys, ks, vs, kis, convs, ffns = _stream(
        x_sample.reshape(db * dseq, D_MODEL), pos_s, dseq, past, weights,
        tq=dseq, tq_att=dseq, spb_sel=spb_s, spb_att=spb_s)

    return (
        yp.reshape(1, seq, D_MODEL), ys.reshape(db, dseq, D_MODEL),
        kp.reshape(1, 1, seq, N_KV_HEADS, HEAD_DIM), vp.reshape(1, 1, seq, N_KV_HEADS, HEAD_DIM),
        kip.reshape(1, 1, seq, IDX_DIM), convp.reshape(1, 1, 2, D_MODEL), ffnp.reshape(1, 1, 2, 2 * D_FF),
        ks.reshape(1, db, dseq, N_KV_HEADS, HEAD_DIM), vs.reshape(1, db, dseq, N_KV_HEADS, HEAD_DIM),
        kis.reshape(1, db, dseq, IDX_DIM), convs.reshape(1, db, 2, D_MODEL), ffns.reshape(1, db, 2, 2 * D_FF),
    )


def _sample_slabs(n_keys):
    nslab = -(-n_keys // LANES)
    best = 1
    for d in range(1, nslab + 1):
        if nslab % d == 0 and d <= 11:
            best = d
    return best
```

```python
import functools

import jax
import jax.numpy as jnp
import numpy as np
from jax import lax
from jax.experimental import pallas as pl
from jax.experimental.pallas import tpu as pltpu

F32 = jnp.float32
BF16 = jnp.bfloat16

D_MODEL = 2048
N_HEADS = 16
N_KV_HEADS = 4
HEAD_DIM = 128
ROT_DIM = HEAD_DIM // 4
N_IDX_HEADS = 16
IDX_DIM = 64
IDX_ROT_DIM = IDX_DIM // 4
CHUNK = 64
TOPK_MAX = 256
ROPE_THETA = 500000.0
D_FF = 5632
RMS_EPS = 1e-6
NEG_INF = -1e30
Q_W = N_HEADS * HEAD_DIM
KV_W = N_KV_HEADS * HEAD_DIM
IQ_W = N_IDX_HEADS * IDX_DIM
GROUP = N_HEADS // N_KV_HEADS

LANES = 128
TM = 512
TN = 1024
MERGE_TN = 512
TF = 512
VMEM_LIMIT = 56 * 1024 * 1024

J_Q = 0
J_KV = J_Q + Q_W // TN
assert 2 * KV_W == TN
J_QI = J_KV + 1
J_CONV = J_QI + IQ_W // TN
N_CBLK = D_MODEL // TN
J_GA = J_CONV + 3 * N_CBLK
J_GC = J_GA + N_CBLK
NJ = J_GC + N_CBLK

Q_PRESCALE = float(np.float32((HEAD_DIM ** -0.5) * np.log2(np.e)))


def _rms(x, g):
    return x * lax.rsqrt(jnp.mean(x * x, axis=-1, keepdims=True) + RMS_EPS) * g


def _sortable(x):
    bits = lax.bitcast_convert_type(x, jnp.int32)
    return jnp.where(bits < 0, bits ^ jnp.int32(0x7FFFFFFF), bits)


def _unsortable(k):
    return lax.bitcast_convert_type(jnp.where(k < 0, k ^ jnp.int32(0x7FFFFFFF), k), F32)


CAND = 16
PAD_SCORE = -3.0e38


def _oddeven_merge_sort_pairs(n):
    out, p = [], 1
    while p < n:
        k = p
        while k >= 1:
            for j in range(k % p, n - k, 2 * k):
                for i in range(min(k, n - j - k)):
                    if (i + j) // (2 * p) == (i + j + k) // (2 * p):
                        out.append((i + j, i + j + k))
            k //= 2
        p *= 2
    return out


_SORT_PAIRS = _oddeven_merge_sort_pairs(CAND)


def _sort_desc(v):
    v = list(v)
    for i, j in _SORT_PAIRS:
        v[i], v[j] = jnp.maximum(v[i], v[j]), jnp.minimum(v[i], v[j])
    return v


def _bitonic_merge_desc(v):
    v = list(v)
    d = len(v) // 2
    while d >= 1:
        for i in range(len(v)):
            if not i & d:
                v[i], v[i + d] = jnp.maximum(v[i], v[i + d]), jnp.minimum(v[i], v[i + d])
        d //= 2
    return v


def _dwconv_seg(u, prev2, w3):
    row = lax.broadcasted_iota(jnp.int32, u.shape, 0)
    p0, p1 = prev2[0:1], prev2[1:2]
    s1 = jnp.where(row == 0, p1, pltpu.roll(u, 1, 0))
    s2 = jnp.where(row == 0, p0, jnp.where(row == 1, p1, pltpu.roll(u, 2, 0)))
    return w3[0:1] * s2 + w3[1:2] * s1 + w3[2:3] * u


def _conv_tile(u, w3, prevs, seg):
    nseg = len(prevs)
    ys, news = [], []
    for s in range(nseg):
        us = u[s * seg:(s + 1) * seg]
        ys.append(_dwconv_seg(us, prevs[s], w3))
        news.append(us[seg - 2:seg])
    y = ys[0] if nseg == 1 else jnp.concatenate(ys, axis=0)
    return y, news


def _proj_kernel(x_ref, g_ref, w_ref, wk_ref, c128_ref, d128_ref, c64_ref, d64_ref, cw_ref, st_ref,
                 q_ref, k_ref, kb_ref, v_ref, vb_ref, qi_ref, kiwi_ref, kia_ref, kib_ref,
                 yc_ref, cn_ref, ga_ref, gc_ref,
                 xn_s, cb_s, cc_s, carry_s, *, seg, carried):
    i = pl.program_id(0)
    j = pl.program_id(1)

    @pl.when(j == 0)
    def _():
        xn_s[...] = _rms(x_ref[...], g_ref[...]).astype(BF16)

    def mm():
        return jnp.dot(xn_s[...], w_ref[...], preferred_element_type=F32)

    lane = lax.broadcasted_iota(jnp.int32, (TM, LANES), 1)

    def rope128(xh):
        partner = jnp.where(lane < ROT_DIM // 2, pltpu.roll(xh, LANES - ROT_DIM // 2, 1),
                            pltpu.roll(xh, ROT_DIM // 2, 1))
        return xh * c128_ref[...] + partner * d128_ref[...]

    def rope64(xh, c, d):
        first = (lane & (IDX_DIM - 1)) < IDX_ROT_DIM // 2
        partner = jnp.where(first, pltpu.roll(xh, LANES - IDX_ROT_DIM // 2, 1),
                            pltpu.roll(xh, IDX_ROT_DIM // 2, 1))
        return xh * c + partner * d

    @pl.when(j < J_KV)
    def _():
        acc = mm()
        for h in range(TN // LANES):
            sl = slice(h * LANES, (h + 1) * LANES)
            q_ref[:, sl] = (rope128(acc[:, sl]) * Q_PRESCALE).astype(BF16)

    @pl.when(j == J_KV)
    def _():
        acc = mm()
        for h in range(N_KV_HEADS):
            sl = slice(h * HEAD_DIM, (h + 1) * HEAD_DIM)
            r = rope128(acc[:, sl])
            k_ref[pl.ds(h, TM, stride=N_KV_HEADS), :] = r
            kb_ref[:, sl] = r.astype(BF16)
            v = acc[:, KV_W + h * HEAD_DIM:KV_W + (h + 1) * HEAD_DIM]
            v_ref[pl.ds(h, TM, stride=N_KV_HEADS), :] = v
            vb_ref[:, sl] = v.astype(BF16)

    @pl.when(jnp.logical_and(j >= J_QI, j < J_CONV))
    def _():
        acc = mm()
        for h in range(TN // LANES):
            sl = slice(h * LANES, (h + 1) * LANES)
            qi_ref[:, sl] = rope64(acc[:, sl], c64_ref[...], d64_ref[...]).astype(BF16)

    @pl.when(j == J_QI)
    def _():
        is_ki = lane < IDX_DIM
        c = jnp.where(is_ki, c64_ref[...], 1.0)
        d = jnp.where(is_ki, d64_ref[...], 0.0)
        acc = jnp.dot(xn_s[...], wk_ref[...], preferred_element_type=F32)
        r = rope64(acc, c, d)
        kiwi_ref[...] = r
        ka = jnp.where(is_ki, r, 0.0)
        kia_ref[...] = ka.astype(BF16)
        kib_ref[...] = pltpu.roll(ka, IDX_DIM, 1).astype(BF16)

    jc = jnp.maximum(j - J_CONV, 0)
    in_conv = jnp.logical_and(j >= J_CONV, j < J_GA)
    cblk = jnp.minimum(jc // 3, N_CBLK - 1)
    part = jc % 3

    @pl.when(jnp.logical_and(in_conv, part == 0))
    def _():
        cb_s[...] = mm()

    @pl.when(jnp.logical_and(in_conv, part == 1))
    def _():
        cc_s[...] = mm()

    @pl.when(jnp.logical_and(in_conv, part == 2))
    def _():
        nseg = TM // seg
        if carried:
            @pl.when(i == 0)
            def _():
                carry_s[cblk] = jnp.zeros((8, TN), F32)
                carry_s[cblk, 0:2, :] = st_ref[0]
            prevs = [carry_s[cblk, 0:2, :]]
        else:
            prevs = [st_ref[s] for s in range(nseg)]
        u = cc_s[...] * mm()
        y, news = _conv_tile(u, cw_ref[...], prevs, seg)
        yc_ref[...] = (cb_s[...] * y).astype(BF16)
        for s in range(nseg):
            cn_ref[s] = news[s]
        if carried:
            carry_s[cblk, 0:2, :] = news[0]

    @pl.when(jnp.logical_and(j >= J_GA, j < J_GC))
    def _():
        ga_ref[...] = 1.0 / (1.0 + jnp.exp(-mm()))

    @pl.when(j >= J_GC)
    def _():
        gc_ref[...] = 1.0 / (1.0 + jnp.exp(-mm()))


def _proj(x, g, w_pair, tabs, conv_w, state, seg):
    w, w_kiwi = w_pair
    t = x.shape[0]
    ni = t // TM
    carried = seg == t
    nseg_tile = 1 if carried else TM // seg
    c128, d128, c64, d64 = tabs

    def row(i, j):
        return (i, 0)

    def const(i, j):
        return (0, 0)

    def cblk(j):
        return jnp.minimum(jnp.maximum(j - J_CONV, 0) // 3, N_CBLK - 1)

    def st_map(i, j):
        return (0 if carried else i, 0, cblk(j))

    in_specs = [
        pl.BlockSpec((TM, D_MODEL), row),
        pl.BlockSpec((1, D_MODEL), const),
        pl.BlockSpec((D_MODEL, TN), lambda i, j: (0, j)),
        pl.BlockSpec((D_MODEL, LANES), const),
        pl.BlockSpec((TM, LANES), row),
        pl.BlockSpec((TM, LANES), row),
        pl.BlockSpec((TM, LANES), row),
        pl.BlockSpec((TM, LANES), row),
        pl.BlockSpec((3, TN), lambda i, j: (0, cblk(j))),
        pl.BlockSpec((nseg_tile, 2, TN), st_map),
    ]
    out_shape = [
        jax.ShapeDtypeStruct((t, Q_W), BF16),
        jax.ShapeDtypeStruct((t * N_KV_HEADS, HEAD_DIM), F32),
        jax.ShapeDtypeStruct((t, KV_W), BF16),
        jax.ShapeDtypeStruct((t * N_KV_HEADS, HEAD_DIM), F32),
        jax.ShapeDtypeStruct((t, KV_W), BF16),
        jax.ShapeDtypeStruct((t, IQ_W), BF16),
        jax.ShapeDtypeStruct((t, LANES), F32),
        jax.ShapeDtypeStruct((t, LANES), BF16),
        jax.ShapeDtypeStruct((t, LANES), BF16),
        jax.ShapeDtypeStruct((t, D_MODEL), BF16),
        jax.ShapeDtypeStruct((ni * nseg_tile, 2, D_MODEL), F32),
        jax.ShapeDtypeStruct((t, D_MODEL), F32),
        jax.ShapeDtypeStruct((t, D_MODEL), F32),
    ]
    out_specs = [
        pl.BlockSpec((TM, TN), lambda i, j: (i, jnp.clip(j - J_Q, 0, J_KV - J_Q - 1))),
        pl.BlockSpec((TM * N_KV_HEADS, HEAD_DIM), row),
        pl.BlockSpec((TM, KV_W), row),
        pl.BlockSpec((TM * N_KV_HEADS, HEAD_DIM), row),
        pl.BlockSpec((TM, KV_W), row),
        pl.BlockSpec((TM, TN), lambda i, j: (i, jnp.clip(j - J_QI, 0, J_CONV - J_QI - 1))),
        pl.BlockSpec((TM, LANES), row),
        pl.BlockSpec((TM, LANES), row),
        pl.BlockSpec((TM, LANES), row),
        pl.BlockSpec((TM, TN), lambda i, j: (i, cblk(j))),
        pl.BlockSpec((nseg_tile, 2, TN), lambda i, j: (i, 0, cblk(j))),
        pl.BlockSpec((TM, TN), lambda i, j: (i, jnp.clip(j - J_GA, 0, N_CBLK - 1))),
        pl.BlockSpec((TM, TN), lambda i, j: (i, jnp.clip(j - J_GC, 0, N_CBLK - 1))),
    ]
    return pl.pallas_call(
        functools.partial(_proj_kernel, seg=min(seg, TM), carried=carried),
        grid=(ni, NJ),
        in_specs=in_specs,
        out_specs=out_specs,
        out_shape=out_shape,
        scratch_shapes=[
            pltpu.VMEM((TM, D_MODEL), BF16),
            pltpu.VMEM((TM, TN), F32),
            pltpu.VMEM((TM, TN), F32),
            pltpu.VMEM((N_CBLK, 8, TN), F32),
        ],
        compiler_params=pltpu.CompilerParams(
            dimension_semantics=("arbitrary", "arbitrary"), vmem_limit_bytes=VMEM_LIMIT),
        name="proj",
    )(x, g, w, w_kiwi, c128, d128, c64, d64, conv_w, state)


def _select_kernel(qi_ref, kiwi_ref, kia_ref, kib_ref, bias_ref,
                   sc_s, wb_s, lg_s, lh_s, cand_s, thr_s, cnt_s, top_s, *,
                   tq, spb, nkb_total, topk, causal, n_valid, q_pos0):
    n = pl.program_id(0)
    npair = N_IDX_HEADS // 2
    blk = spb * LANES
    kiwi = kiwi_ref[...]
    wscale = (IDX_DIM ** -0.5) * (N_IDX_HEADS ** -0.5)
    for h in range(N_IDX_HEADS):
        wb_s[h] = jnp.broadcast_to(kiwi[:, IDX_DIM + h:IDX_DIM + h + 1], (tq, LANES)) * wscale
    q2 = jnp.concatenate([qi_ref[:, p * LANES:(p + 1) * LANES] for p in range(npair)], axis=0)

    if causal:
        qpos0 = n * tq
        nkb = (qpos0 + tq + blk - 1) // blk
    else:
        qpos0 = q_pos0
        nkb = nkb_total
    lane = lax.broadcasted_iota(jnp.int32, (tq, LANES), 1)
    qchunk = (qpos0 + lax.broadcasted_iota(jnp.int32, (tq, LANES), 0)) >> 6
    nt = (((1,), (1,)), ((), ()))

    def logits_into(lg_ref, kb):
        base = pl.multiple_of(jnp.minimum(kb, nkb_total - 1) * blk, blk)
        lg_ref[0] = lax.dot_general(q2, kia_ref[pl.ds(base, blk), :], nt, preferred_element_type=F32)
        lg_ref[1] = lax.dot_general(q2, kib_ref[pl.ds(base, blk), :], nt, preferred_element_type=F32)

    def head_sum(lg_ref, kb, carry):
        m1, m2 = carry
        base = kb * blk
        for c in range(spb):
            cs = slice(c * LANES, (c + 1) * LANES)
            acc = jnp.zeros((tq, LANES), F32)
            for p in range(npair):
                rs = slice(p * tq, (p + 1) * tq)
                acc = acc + jnp.maximum(lg_ref[0, rs, cs], 0.0) * wb_s[2 * p]
                acc = acc + jnp.maximum(lg_ref[1, rs, cs], 0.0) * wb_s[2 * p + 1]
            col = base + c * LANES + lane
            adm = jnp.logical_and((col >> 6) <= qchunk, col < n_valid)
            sc = jnp.where(adm, acc, NEG_INF)
            m2 = jnp.maximum(m2, jnp.minimum(m1, sc))
            m1 = jnp.maximum(m1, sc)
            sc_s[kb * spb + c] = sc
        return m1, m2

    logits_into(lg_s, 0)

    def score_pair(i, carry):
        logits_into(lh_s, 2 * i + 1)
        carry = head_sum(lg_s, 2 * i, carry)
        logits_into(lg_s, 2 * i + 2)
        return head_sum(lh_s, 2 * i + 1, carry)

    neg = jnp.full((tq, LANES), NEG_INF, F32)
    m1, m2 = lax.fori_loop(0, nkb // 2, score_pair, (neg, neg))
    top_s[0] = m1
    top_s[1] = m2

    @pl.when(nkb % 2 == 1)
    def _():
        last1, last2 = head_sum(lg_s, nkb - 1, (top_s[0], top_s[1]))
        top_s[0] = last1
        top_s[1] = last2

    m1, m2 = top_s[0], top_s[1]

    zeros = jnp.zeros((tq, LANES), F32)
    kf = float(topk)

    def count_all(thr_f, strict=False):
        def body(kb, acc):
            for c in range(spb):
                s = sc_s[kb * spb + c]
                acc = acc + jnp.where(s > thr_f if strict else s >= thr_f, 1.0, 0.0)
            return acc
        return jnp.sum(lax.fori_loop(0, nkb, body, zeros), axis=1, keepdims=True)

    def count_cand(thr_f):
        acc = zeros
        for i in range(CAND):
            acc = acc + jnp.where(cand_s[i] >= thr_f, 1.0, 0.0)
        return jnp.sum(acc, axis=1, keepdims=True)

    ones = jnp.ones((tq, LANES), jnp.int32)
    lo0 = _sortable(jnp.min(m2, axis=1, keepdims=True)) * ones
    hi0 = _sortable(jnp.max(m1, axis=1, keepdims=True)) * ones + 1

    def bisect(count):
        def unresolved(lo_k, hi_k, c_lo):
            open_ = jnp.logical_and(c_lo != kf, (hi_k - lo_k) != 1)
            return jnp.max(jnp.where(open_, 1.0, 0.0))

        def cond(st):
            return jnp.logical_and(st[0] < 33, st[-1] > 0.0)

        def body(st):
            it, lo_k, hi_k, c_lo, _ = st
            mid = lo_k + lax.shift_right_logical(hi_k - lo_k, 1)
            cnt = count(_unsortable(mid))
            ge = cnt >= kf
            lo_k = jnp.where(ge, mid, lo_k)
            hi_k = jnp.where(ge, hi_k, mid)
            c_lo = jnp.where(ge, cnt, c_lo)
            return it + 1, lo_k, hi_k, c_lo, unresolved(lo_k, hi_k, c_lo)

        c0 = jnp.full((tq, LANES), -1.0, F32)
        st = lax.while_loop(cond, body, (jnp.int32(0), lo0, hi0, c0, unresolved(lo0, hi0, c0)))
        return st[1], st[3]

    nsl = nkb * spb
    nchunk = (nsl + CAND - 1) // CAND
    pad_tile = jnp.full((tq, LANES), PAD_SCORE, F32)

    def pad(sidx, carry):
        sc_s[sidx] = pad_tile
        return carry

    lax.fori_loop(nsl, nchunk * CAND, pad, 0)

    def gather_rows(rg, carry):
        r0 = pl.multiple_of(rg * 8, 8)

        def chunk(ch, cand):
            new = _sort_desc([sc_s[ch * CAND + i, pl.ds(r0, 8), :] for i in range(CAND)])
            return tuple(_bitonic_merge_desc([jnp.maximum(cand[i], new[CAND - 1 - i]) for i in range(CAND)]))

        start = tuple(jnp.full((8, LANES), PAD_SCORE, F32) for _ in range(CAND))
        cand = lax.fori_loop(0, nchunk, chunk, start)
        for i in range(CAND):
            cand_s[i, pl.ds(r0, 8), :] = cand[i]
        return carry

    lax.fori_loop(0, tq // 8, gather_rows, 0)

    thr_c, cnt_c = bisect(count_cand)
    thr_s[...] = thr_c
    cnt_s[...] = cnt_c
    last = jnp.max(cand_s[CAND - 1], axis=1, keepdims=True)
    covered = jnp.logical_or(last < _unsortable(thr_c), last <= 0.5 * NEG_INF)

    @pl.when(jnp.min(jnp.where(covered, 1.0, 0.0)) == 0.0)
    def _():
        thr_a, cnt_a = bisect(count_all)
        thr_s[...] = thr_a
        cnt_s[...] = cnt_a

    thr = _unsortable(thr_s[...])
    c_thr = cnt_s[...]

    tied = jnp.max(jnp.where(jnp.logical_and(c_thr != kf, thr > 0.5 * NEG_INF), 1.0, 0.0)) > 0.0

    @pl.when(jnp.logical_not(tied))
    def _():
        def emit(kb, carry):
            for c in range(spb):
                s = sc_s[kb * spb + c]
                sel = jnp.logical_and(s >= thr, s > 0.5 * NEG_INF)
                bias_ref[kb * spb + c] = jnp.where(sel, 0.0, NEG_INF).astype(BF16)
            return carry

        lax.fori_loop(0, nkb, emit, 0)

    @pl.when(tied)
    def _():
        need = kf - count_all(thr, strict=True)
        tri = (lax.broadcasted_iota(jnp.int32, (LANES, LANES), 0)
               <= lax.broadcasted_iota(jnp.int32, (LANES, LANES), 1)).astype(BF16)

        def emit(kb, seen):
            for c in range(spb):
                s = sc_s[kb * spb + c]
                eq = jnp.where(s == thr, 1.0, 0.0)
                rank = seen + jnp.dot(eq.astype(BF16), tri, preferred_element_type=F32)
                keep = jnp.logical_or(s > thr, jnp.logical_and(s == thr, rank <= need))
                sel = jnp.logical_and(keep, s > 0.5 * NEG_INF)
                bias_ref[kb * spb + c] = jnp.where(sel, 0.0, NEG_INF).astype(BF16)
                seen = seen + jnp.sum(eq, axis=1, keepdims=True)
            return seen

        lax.fori_loop(0, nkb, emit, zeros)

    def fill(kb, carry):
        for c in range(spb):
            bias_ref[kb * spb + c] = jnp.full((tq, LANES), NEG_INF, BF16)
        return carry

    lax.fori_loop(nkb, nkb_total, fill, 0)


def _select(qi, kiwi, kia, kib, *, tq, spb, topk, causal, n_valid, q_pos0):
    t = qi.shape[0]
    nb = t // tq
    lk = kia.shape[1]
    nslab = lk // LANES
    nkb_total = nslab // spb
    assert topk <= 2 * LANES and nslab >= 2, "the bisection's starting lower bound needs two keys per lane"

    def kmap(n):
        return (0 if causal else n, 0, 0)

    return pl.pallas_call(
        functools.partial(_select_kernel, tq=tq, spb=spb, nkb_total=nkb_total, topk=topk,
                          causal=causal, n_valid=n_valid, q_pos0=q_pos0),
        grid=(nb,),
        in_specs=[
            pl.BlockSpec((tq, IQ_W), lambda n: (n, 0)),
            pl.BlockSpec((tq, LANES), lambda n: (n, 0)),
            pl.BlockSpec((None, lk, LANES), kmap),
            pl.BlockSpec((None, lk, LANES), kmap),
        ],
        out_specs=pl.BlockSpec((nslab, tq, LANES), lambda n: (0, n, 0)),
        out_shape=jax.ShapeDtypeStruct((nslab, t, LANES), BF16),
        scratch_shapes=[
            pltpu.VMEM((nslab + CAND, tq, LANES), F32),
            pltpu.VMEM((N_IDX_HEADS, tq, LANES), F32),
            pltpu.VMEM((2, N_IDX_HEADS // 2 * tq, spb * LANES), F32),
            pltpu.VMEM((2, N_IDX_HEADS // 2 * tq, spb * LANES), F32),
            pltpu.VMEM((CAND, tq, LANES), F32),
            pltpu.VMEM((tq, LANES), jnp.int32),
            pltpu.VMEM((tq, LANES), F32),
            pltpu.VMEM((2, tq, LANES), F32),
        ],
        compiler_params=pltpu.CompilerParams(
            dimension_semantics=("arbitrary",), vmem_limit_bytes=VMEM_LIMIT),
        name="select",
    )(qi, kiwi, kia, kib)


FIXED_REF_LIMIT = 2.0 ** 40

def _attend_kernel(qb_ref, kb_ref, kbat_ref, last_ref, q_ref, k_ref, v_ref, b_ref, o_ref,
                   m_s, l_s, acc_s, par_s, exact_s, *, tq, spb):
    s = pl.program_id(0)
    first = kb_ref[s] == 0

    biases = [b_ref[c].astype(F32)[None] for c in range(spb)]
    ones = jnp.ones((spb * LANES, LANES), BF16)

    def masked_scores(g):
        qg = [q_ref[:, (g * GROUP + h) * HEAD_DIM:(g * GROUP + h + 1) * HEAD_DIM] for h in range(GROUP)]
        q4 = jnp.concatenate(qg, axis=0)
        kg = k_ref[:, g * HEAD_DIM:(g + 1) * HEAD_DIM]
        sc = lax.dot_general(q4, kg, (((1,), (1,)), ((), ())), preferred_element_type=F32)
        return [(sc[:, c * LANES:(c + 1) * LANES].reshape(GROUP, tq, LANES) + biases[c]).reshape(GROUP * tq, LANES)
                for c in range(spb)]

    def weighted_values(g, m_ref, slabs):
        pmat = jnp.concatenate([jnp.exp2(sl - m_ref).astype(BF16) for sl in slabs], axis=1)
        v1 = jnp.concatenate([v_ref[:, g * HEAD_DIM:(g + 1) * HEAD_DIM], ones], axis=1)
        pv = jnp.dot(pmat, v1, preferred_element_type=F32)
        return pv[:, 0:HEAD_DIM], pv[:, HEAD_DIM:HEAD_DIM + LANES]

    @pl.when(first)
    def _():
        par_s[0] = 0
        exact_s[0] = 1
        m_s[...] = jnp.full(m_s.shape, 0.1 * NEG_INF, F32)
        l_s[0] = jnp.zeros(l_s.shape[1:], F32)
        acc_s[0] = jnp.zeros(acc_s.shape[1:], F32)

    @pl.when(jnp.logical_not(first))
    def _():
        par = par_s[0]
        worst = jnp.zeros((GROUP * tq, LANES), F32)
        for g in range(N_KV_HEADS):
            pv, psum = weighted_values(g, m_s[g], masked_scores(g))
            acc_new = acc_s[par, g] + pv
            l_new = l_s[par, g] + psum
            acc_s[1 - par, g] = acc_new
            l_s[1 - par, g] = l_new
            worst = worst + l_new + jnp.abs(acc_new)
        ok = jnp.min(jnp.where(worst < FIXED_REF_LIMIT, 1.0, 0.0)) > 0.5
        exact_s[0] = jnp.where(ok, 0, 1)
        par_s[0] = jnp.where(ok, 1 - par, par)

    @pl.when(exact_s[0] == 1)
    def _():
        par = par_s[0]
        for g in range(N_KV_HEADS):
            slabs = masked_scores(g)
            mx = slabs[0]
            for sl in slabs[1:]:
                mx = jnp.maximum(mx, sl)
            m_prev = m_s[g]
            m_new = jnp.maximum(m_prev, jnp.max(mx, axis=1, keepdims=True))
            alpha = jnp.exp2(m_prev - m_new)
            pv, psum = weighted_values(g, m_new, slabs)
            acc_s[par, g] = alpha * acc_s[par, g] + pv
            l_s[par, g] = alpha * l_s[par, g] + psum
            m_s[g] = m_new

    @pl.when(last_ref[s] == 1)
    def _():
        par = par_s[0]
        for g in range(N_KV_HEADS):
            o = acc_s[par, g] / l_s[par, g]
            for h in range(GROUP):
                col = (g * GROUP + h) * HEAD_DIM
                o_ref[:, col:col + HEAD_DIM] = o[h * tq:(h + 1) * tq].astype(BF16)


def _attend(q, k_all, v_all, bias, sched, *, tq, spb):
    t = q.shape[0]
    qb, kb, kbat, last = sched
    nsteps = qb.shape[0]
    blk = spb * LANES
    grid_spec = pltpu.PrefetchScalarGridSpec(
        num_scalar_prefetch=4,
        grid=(nsteps,),
        in_specs=[
            pl.BlockSpec((tq, Q_W), lambda s, qb, kb, kbat, last: (qb[s], 0)),
            pl.BlockSpec((None, blk, KV_W), lambda s, qb, kb, kbat, last: (kbat[s], kb[s], 0)),
            pl.BlockSpec((None, blk, KV_W), lambda s, qb, kb, kbat, last: (kbat[s], kb[s], 0)),
            pl.BlockSpec((spb, tq, LANES), lambda s, qb, kb, kbat, last: (kb[s], qb[s], 0)),
        ],
        out_specs=pl.BlockSpec((tq, Q_W), lambda s, qb, kb, kbat, last: (qb[s], 0)),
        scratch_shapes=[
            pltpu.VMEM((N_KV_HEADS, GROUP * tq, LANES), F32),
            pltpu.VMEM((2, N_KV_HEADS, GROUP * tq, LANES), F32),
            pltpu.VMEM((2, N_KV_HEADS, GROUP * tq, HEAD_DIM), F32),
            pltpu.SMEM((1,), jnp.int32),
            pltpu.SMEM((1,), jnp.int32),
        ],
    )
    return pl.pallas_call(
        functools.partial(_attend_kernel, tq=tq, spb=spb),
        grid_spec=grid_spec,
        out_shape=jax.ShapeDtypeStruct((t, Q_W), BF16),
        compiler_params=pltpu.CompilerParams(
            dimension_semantics=("arbitrary",), vmem_limit_bytes=VMEM_LIMIT),
        name="attend",
    )(qb, kb, kbat, last, q, k_all, v_all, bias)


def _causal_schedule(t, tq, blk):
    qb, kb, last = [], [], []
    for n in range(t // tq):
        nk = ((n + 1) * tq + blk - 1) // blk
        for k in range(nk):
            qb.append(n)
            kb.append(k)
            last.append(1 if k == nk - 1 else 0)
    z = np.zeros(len(qb), np.int32)
    return (jnp.asarray(qb, jnp.int32), jnp.asarray(kb, jnp.int32), jnp.asarray(z), jnp.asarray(last, jnp.int32))


def _batched_schedule(nbatch, nk):
    qb = np.repeat(np.arange(nbatch, dtype=np.int32), nk)
    kb = np.tile(np.arange(nk, dtype=np.int32), nbatch)
    last = (kb == nk - 1).astype(np.int32)
    return (jnp.asarray(qb), jnp.asarray(kb), jnp.asarray(qb), jnp.asarray(last))


def _merge_kernel(o_ref, yc_ref, ga_ref, gc_ref, woa_ref, woc_ref, wout_ref, x_ref, gq_ref, gp_ref,
                  x1_ref, xn2_ref, mg_s, m_s):
    j = pl.program_id(1)
    nblk = D_MODEL // MERGE_TN

    @pl.when(j < nblk)
    def _():
        a = jnp.dot(o_ref[...], woa_ref[...], preferred_element_type=F32)
        c = jnp.dot(yc_ref[...], woc_ref[...], preferred_element_type=F32)
        mg_s[j] = (ga_ref[...] * a + gc_ref[...] * c).astype(BF16)

    @pl.when(j >= nblk)
    def _():
        mg = jnp.concatenate([mg_s[b] for b in range(nblk)], axis=1)
        m_s[j - nblk] = jnp.dot(mg, wout_ref[...], preferred_element_type=F32)

    @pl.when(j == 2 * nblk - 1)
    def _():
        m = jnp.concatenate([m_s[b] for b in range(nblk)], axis=1)
        x1 = x_ref[...] + _rms(m, gq_ref[...])
        x1_ref[...] = x1
        xn2_ref[...] = _rms(x1, gp_ref[...]).astype(BF16)


def _merge(o, yc, ga, gc, woa, woc, wout, x, gq, gp):
    t = x.shape[0]
    tn = MERGE_TN
    nblk = D_MODEL // tn

    def row(i, j):
        return (i, 0)

    def lo(i, j):
        return (i, jnp.minimum(j, nblk - 1))

    return pl.pallas_call(
        _merge_kernel,
        grid=(t // TM, 2 * nblk),
        in_specs=[
            pl.BlockSpec((TM, Q_W), row),
            pl.BlockSpec((TM, D_MODEL), row),
            pl.BlockSpec((TM, tn), lo),
            pl.BlockSpec((TM, tn), lo),
            pl.BlockSpec((Q_W, tn), lambda i, j: (0, jnp.minimum(j, nblk - 1))),
            pl.BlockSpec((D_MODEL, tn), lambda i, j: (0, jnp.minimum(j, nblk - 1))),
            pl.BlockSpec((D_MODEL, tn), lambda i, j: (0, jnp.maximum(j - nblk, 0))),
            pl.BlockSpec((TM, D_MODEL), row),
            pl.BlockSpec((1, D_MODEL), lambda i, j: (0, 0)),
            pl.BlockSpec((1, D_MODEL), lambda i, j: (0, 0)),
        ],
        out_specs=[pl.BlockSpec((TM, D_MODEL), row), pl.BlockSpec((TM, D_MODEL), row)],
        out_shape=[jax.ShapeDtypeStruct((t, D_MODEL), F32), jax.ShapeDtypeStruct((t, D_MODEL), BF16)],
        scratch_shapes=[pltpu.VMEM((nblk, TM, tn), BF16), pltpu.VMEM((nblk, TM, tn), F32)],
        compiler_params=pltpu.CompilerParams(
            dimension_semantics=("arbitrary", "arbitrary"), vmem_limit_bytes=VMEM_LIMIT),
        name="merge",
    )(o, yc, ga, gc, woa, woc, wout, x, gq, gp)


def _ffn_kernel(xn_ref, wg_ref, wv_ref, cwg_ref, cwv_ref, wd_ref, x1_ref, gq_ref, stg_ref, stv_ref,
                y_ref, ng_ref, nv_ref, acc_s, cg_s, cv_s, *, seg, carried):
    i = pl.program_id(0)
    jf = pl.program_id(1)
    nseg = TM // seg

    @pl.when(jf == 0)
    def _():
        acc_s[...] = jnp.zeros(acc_s.shape, F32)

    if carried:
        @pl.when(i == 0)
        def _():
            for carry_s, st_ref in ((cg_s, stg_ref), (cv_s, stv_ref)):
                carry_s[jf] = jnp.zeros((8, TF), F32)
                carry_s[jf, 0:2, :] = st_ref[0]

    xn = xn_ref[...]

    def branch(w_ref, cw_ref, st_ref, carry_s, new_ref):
        up = jnp.dot(xn, w_ref[...], preferred_element_type=F32)
        if carried:
            prevs = [carry_s[jf, 0:2, :]]
        else:
            prevs = [st_ref[s] for s in range(nseg)]
        y, news = _conv_tile(up, cw_ref[...], prevs, seg)
        for s in range(nseg):
            new_ref[s] = news[s]
        if carried:
            carry_s[jf, 0:2, :] = news[0]
        return y

    gate = branch(wg_ref, cwg_ref, stg_ref, cg_s, ng_ref)
    val = branch(wv_ref, cwv_ref, stv_ref, cv_s, nv_ref)
    c0 = np.float32(np.sqrt(2.0 / np.pi))
    gelu = 0.5 * gate * (1.0 + jnp.tanh(c0 * (gate + 0.044715 * (gate * gate * gate))))
    hid = (gelu * val).astype(BF16)
    acc_s[...] += jnp.dot(hid, wd_ref[...], preferred_element_type=F32)

    @pl.when(jf == pl.num_programs(1) - 1)
    def _():
        y_ref[...] = x1_ref[...] + _rms(acc_s[...], gq_ref[...])


def _ffn(xn2, w_up, cw, w_down, x1, gq, state, seg):
    t = x1.shape[0]
    nf = D_FF // TF
    carried = seg == t
    nseg_tile = 1 if carried else TM // seg

    def row(i, j):
        return (i, 0)

    def stg(i, j):
        return (0 if carried else i, 0, j)

    def stv(i, j):
        return (0 if carried else i, 0, j + nf)

    return pl.pallas_call(
        functools.partial(_ffn_kernel, seg=min(seg, TM), carried=carried),
        grid=(t // TM, nf),
        in_specs=[
            pl.BlockSpec((TM, D_MODEL), row),
            pl.BlockSpec((D_MODEL, TF), lambda i, j: (0, j)),
            pl.BlockSpec((D_MODEL, TF), lambda i, j: (0, j + nf)),
            pl.BlockSpec((3, TF), lambda i, j: (0, j)),
            pl.BlockSpec((3, TF), lambda i, j: (0, j + nf)),
            pl.BlockSpec((TF, D_MODEL), lambda i, j: (j, 0)),
            pl.BlockSpec((TM, D_MODEL), row),
            pl.BlockSpec((1, D_MODEL), lambda i, j: (0, 0)),
            pl.BlockSpec((nseg_tile, 2, TF), stg),
            pl.BlockSpec((nseg_tile, 2, TF), stv),
        ],
        out_specs=[
            pl.BlockSpec((TM, D_MODEL), row),
            pl.BlockSpec((nseg_tile, 2, TF), lambda i, j: (i, 0, j)),
            pl.BlockSpec((nseg_tile, 2, TF), lambda i, j: (i, 0, j)),
        ],
        out_shape=[
            jax.ShapeDtypeStruct((t, D_MODEL), F32),
            jax.ShapeDtypeStruct((t // TM * nseg_tile, 2, D_FF), F32),
            jax.ShapeDtypeStruct((t // TM * nseg_tile, 2, D_FF), F32),
        ],
        scratch_shapes=[
            pltpu.VMEM((TM, D_MODEL), F32),
            pltpu.VMEM((nf, 8, TF), F32),
            pltpu.VMEM((nf, 8, TF), F32),
        ],
        compiler_params=pltpu.CompilerParams(
            dimension_semantics=("arbitrary", "arbitrary"), vmem_limit_bytes=VMEM_LIMIT),
        name="ffn",
    )(xn2, w_up, w_up, cw, cw, w_down, x1, gq, state, state)


PACK_ROWS = 1024


def _pack_kernel(ck_ref, cv_ref, ko_ref, vo_ref, *, rows):
    for src, dst in ((ck_ref, ko_ref), (cv_ref, vo_ref)):
        for g in range(N_KV_HEADS):
            dst[:, g * HEAD_DIM:(g + 1) * HEAD_DIM] = src[pl.ds(g, rows, stride=N_KV_HEADS), :].astype(BF16)


def _tail_kernel(kn_ref, vn_ref, ki_ref, vi_ref, ko_ref, vo_ref, *, seg):
    del ki_ref, vi_ref
    for src, dst in ((kn_ref, ko_ref), (vn_ref, vo_ref)):
        dst[0:seg, :] = src[...]
        if dst.shape[0] > seg:
            dst[seg:, :] = jnp.zeros((dst.shape[0] - seg, KV_W), BF16)


def _pack_cache(cache_k, cache_v, k_new, v_new, seg, lk):
    nb, plen = cache_k.shape[0], cache_k.shape[1]
    rows = int(np.gcd(plen, PACK_ROWS))
    tail = lk - plen
    assert rows % 16 == 0 and tail % 16 == 0 and plen % tail == 0 and tail >= seg
    cspec = pl.BlockSpec((None, rows * N_KV_HEADS, HEAD_DIM), lambda b, r: (b, r, 0))
    ospec = pl.BlockSpec((None, rows, KV_W), lambda b, r: (b, r, 0))
    flat = (nb, plen * N_KV_HEADS, HEAD_DIM)
    slab = jax.ShapeDtypeStruct((nb, lk, KV_W), BF16)
    k_all, v_all = pl.pallas_call(
        functools.partial(_pack_kernel, rows=rows),
        grid=(nb, plen // rows),
        in_specs=[cspec, cspec],
        out_specs=[ospec, ospec],
        out_shape=[slab, slab],
        compiler_params=pltpu.CompilerParams(
            dimension_semantics=("arbitrary", "arbitrary"), vmem_limit_bytes=VMEM_LIMIT),
        name="pack_cache",
    )(cache_k.reshape(flat), cache_v.reshape(flat))
    nspec = pl.BlockSpec((seg, KV_W), lambda b: (b, 0))
    anyspec = pl.BlockSpec(memory_space=pl.ANY)
    tspec = pl.BlockSpec((None, tail, KV_W), lambda b: (b, plen // tail, 0))
    return pl.pallas_call(
        functools.partial(_tail_kernel, seg=seg),
        grid=(nb,),
        in_specs=[nspec, nspec, anyspec, anyspec],
        out_specs=[tspec, tspec],
        out_shape=[slab, slab],
        input_output_aliases={2: 0, 3: 1},
        compiler_params=pltpu.CompilerParams(dimension_semantics=("arbitrary",), vmem_limit_bytes=VMEM_LIMIT),
        name="pack_tail",
    )(k_new, v_new, k_all, v_all)


def _rope_tables(pos, rot, width):
    half = rot // 2
    freqs = ROPE_THETA ** (-jnp.arange(half, dtype=F32) / half)
    ang = pos.astype(F32)[:, None] * freqs[None, :]
    cos, sin = jnp.cos(ang), jnp.sin(ang)
    t = pos.shape[0]
    c = jnp.concatenate([cos, cos, jnp.ones((t, width - rot), F32)], axis=1)
    d = jnp.concatenate([-sin, sin, jnp.zeros((t, width - rot), F32)], axis=1)
    reps = LANES // width
    return jnp.tile(c, (1, reps)), jnp.tile(d, (1, reps))


def _relayout_w_in(w):
    o = np.cumsum([0, Q_W, KV_W, KV_W, IQ_W, IDX_DIM, N_IDX_HEADS, D_MODEL, D_MODEL, D_MODEL, D_MODEL, D_MODEL])
    q, k, v, qi = (w[:, o[a]:o[a + 1]] for a in range(4))
    kiwi = w[:, o[4]:o[6]]
    cb, cc, ch, ga, gc = (w[:, o[a]:o[a + 1]] for a in range(6, 11))
    parts = [q, k, v, qi]
    for c in range(N_CBLK):
        sl = slice(c * TN, (c + 1) * TN)
        parts += [cb[:, sl], cc[:, sl], ch[:, sl]]
    parts += [ga, gc]
    w_kiwi = jnp.pad(kiwi, ((0, 0), (0, LANES - kiwi.shape[1]))).astype(BF16)
    return jnp.concatenate([p.astype(BF16) for p in parts], axis=1), w_kiwi


def _stream(x, pos, seg, past, weights, *, tq, tq_att, spb_sel, spb_att):
    (g_mp, g_mq, w_in_r, conv_w, woa, woc, wout, g_fp, g_fq, w_up, fconv_w, w_down) = weights
    t = x.shape[0]
    nseq = t // seg
    tabs = _rope_tables(pos, ROT_DIM, HEAD_DIM) + _rope_tables(pos, IDX_ROT_DIM, IDX_DIM)
    if past is None:
        conv_state = jnp.zeros((nseq, 2, D_MODEL), F32)
        ffn_state = jnp.zeros((nseq, 2, 2 * D_FF), F32)
    else:
        conv_state, ffn_state = past[3], past[4]

    (q, k, kb, v, vb, qi, kiwi, kia, kib, yc, conv_new, ga, gc) = _proj(
        x, g_mp, w_in_r, tabs, conv_w, conv_state, seg)

    if past is None:
        k_all, v_all = kb[None], vb[None]
        kia_all, kib_all = kia[None], kib[None]
        n_keys = t
        sched = _causal_schedule(t, tq_att, spb_att * LANES)
        q_pos0 = 0
    else:
        cache_k, cache_v, cache_ki = past[0], past[1], past[2]
        plen = cache_k.shape[1]
        n_keys = plen + seg
        lk = -(-n_keys // (spb_att * LANES)) * (spb_att * LANES)
        pad = lk - n_keys

        def cat(c, new, width):
            parts = [c, new.reshape(nseq, seg, width)]
            if pad:
                parts.append(jnp.zeros((nseq, pad, width), BF16))
            return jnp.concatenate(parts, axis=1)

        k_all, v_all = _pack_cache(cache_k, cache_v, kb, vb, seg, lk)
        cki = cache_ki.astype(BF16)
        zk = jnp.zeros_like(cki)
        kia_all = cat(jnp.concatenate([cki, zk], axis=-1), kia, LANES)
        kib_all = cat(jnp.concatenate([zk, cki], axis=-1), kib, LANES)
        sched = _batched_schedule(nseq, lk // (spb_att * LANES))
        q_pos0 = plen
    topk = min(TOPK_MAX, n_keys // 4)

    bias = _select(qi, kiwi, kia_all, kib_all, tq=tq, spb=spb_sel, topk=topk,
                   causal=past is None, n_valid=n_keys, q_pos0=q_pos0)
    o = _attend(q, k_all, v_all, bias, sched, tq=tq_att, spb=spb_att)
    x1, xn2 = _merge(o, yc, ga, gc, woa, woc, wout, x, g_mq, g_fp)
    y, ffn_g, ffn_v = _ffn(xn2, w_up, fconv_w, w_down, x1, g_fq, ffn_state, seg)
    ffn_new = jnp.concatenate([ffn_g, ffn_v], axis=-1)
    return y, k, v, kiwi[:, :IDX_DIM], conv_new[-nseq:], ffn_new[-nseq:]


def kernel(x_prompt, x_sample, cache_k, cache_v, cache_k_idx, state_conv, state_ffn_conv, norm_mix_pre, norm_mix_post, w_in, conv_w, w_o_attn, w_o_conv, w_out, norm_ffn_pre, norm_ffn_post, w_ffn_up, ffn_conv_w, w_ffn_down):
    depth = w_in.shape[0]
    assert depth == 1, "single-layer step"
    b, seq, _ = x_prompt.shape
    assert b == 1
    db, dseq, _ = x_sample.shape
    plen = cache_k.shape[2]
    assert dseq == CHUNK and plen % CHUNK == 0

    weights = (
        norm_mix_pre, norm_mix_post, _relayout_w_in(w_in[0]), conv_w[0],
        w_o_attn[0].astype(BF16), w_o_conv[0].astype(BF16), w_out[0].astype(BF16),
        norm_ffn_pre, norm_ffn_post, w_ffn_up[0].astype(BF16), ffn_conv_w[0], w_ffn_down[0].astype(BF16),
    )

    pos_p = jnp.arange(seq, dtype=jnp.int32)
    yp, kp, vp, kip, convp, ffnp = _stream(
        x_prompt.reshape(seq, D_MODEL), pos_p, seq, None, weights, tq=128, tq_att=512, spb_sel=4, spb_att=8)

    pos_s = jnp.tile(jnp.arange(dseq, dtype=jnp.int32) + plen, db)
    past = (cache_k[0], cache_v[0], cache_k_idx[0], state_conv[0], state_ffn_conv[0])
    n_keys = plen + dseq
    spb_s = _sample_slabs(n_keys)
    ys, ks, vs, kis, convs, ffns = _stream(
        x_sample.reshape(db * dseq, D_MODEL), pos_s, dseq, past, weights,
        tq=dseq, tq_att=dseq, spb_sel=spb_s, spb_att=spb_s)

    return (
        yp.reshape(1, seq, D_MODEL), ys.reshape(db, dseq, D_MODEL),
        kp.reshape(1, 1, seq, N_KV_HEADS, HEAD_DIM), vp.reshape(1, 1, seq, N_KV_HEADS, HEAD_DIM),
        kip.reshape(1, 1, seq, IDX_DIM), convp.reshape(1, 1, 2, D_MODEL), ffnp.reshape(1, 1, 2, 2 * D_FF),
        ks.reshape(1, db, dseq, N_KV_HEADS, HEAD_DIM), vs.reshape(1, db, dseq, N_KV_HEADS, HEAD_DIM),
        kis.reshape(1, db, dseq, IDX_DIM), convs.reshape(1, db, 2, D_MODEL), ffns.reshape(1, db, 2, 2 * D_FF),
    )


def _sample_slabs(n_keys):
    nslab = -(-n_keys // LANES)
    best = 1
    for d in range(1, nslab + 1):
        if nslab % d == 0 and d <= 11:
            best = d
    return best
```

```python
import functools

import jax
import jax.numpy as jnp
import numpy as np
from jax import lax
from jax.experimental import pallas as pl
from jax.experimental.pallas import tpu as pltpu

F32 = jnp.float32
BF16 = jnp.bfloat16

D_MODEL = 2048
N_HEADS = 16
N_KV_HEADS = 4
HEAD_DIM = 128
ROT_DIM = HEAD_DIM // 4
N_IDX_HEADS = 16
IDX_DIM = 64
IDX_ROT_DIM = IDX_DIM // 4
CHUNK = 64
TOPK_MAX = 256
ROPE_THETA = 500000.0
D_FF = 5632
RMS_EPS = 1e-6
NEG_INF = -1e30
Q_W = N_HEADS * HEAD_DIM
KV_W = N_KV_HEADS * HEAD_DIM
IQ_W = N_IDX_HEADS * IDX_DIM
GROUP = N_HEADS // N_KV_HEADS

LANES = 128
TM = 512
TN = 1024
MERGE_TN = 512
TF = 512
VMEM_LIMIT = 56 * 1024 * 1024

J_Q = 0
J_KV = J_Q + Q_W // TN
assert 2 * KV_W == TN
J_QI = J_KV + 1
J_CONV = J_QI + IQ_W // TN
N_CBLK = D_MODEL // TN
J_GA = J_CONV + 3 * N_CBLK
J_GC = J_GA + N_CBLK
NJ = J_GC + N_CBLK

Q_PRESCALE = float(np.float32((HEAD_DIM ** -0.5) * np.log2(np.e)))


def _rms(x, g):
    return x * lax.rsqrt(jnp.mean(x * x, axis=-1, keepdims=True) + RMS_EPS) * g


def _sortable(x):
    bits = lax.bitcast_convert_type(x, jnp.int32)
    return jnp.where(bits < 0, bits ^ jnp.int32(0x7FFFFFFF), bits)


def _unsortable(k):
    return lax.bitcast_convert_type(jnp.where(k < 0, k ^ jnp.int32(0x7FFFFFFF), k), F32)


CHUNK_SHIFT = CHUNK.bit_length() - 1
assert 1 << CHUNK_SHIFT == CHUNK
KEY_BITS = 32
CAND = 16
PAD_SCORE = -3.0e38


def _oddeven_merge_sort_pairs(n):
    out, p = [], 1
    while p < n:
        k = p
        while k >= 1:
            for j in range(k % p, n - k, 2 * k):
                for i in range(min(k, n - j - k)):
                    if (i + j) // (2 * p) == (i + j + k) // (2 * p):
                        out.append((i + j, i + j + k))
            k //= 2
        p *= 2
    return out


_SORT_PAIRS = _oddeven_merge_sort_pairs(CAND)


def _sort_desc(v):
    v = list(v)
    for i, j in _SORT_PAIRS:
        v[i], v[j] = jnp.maximum(v[i], v[j]), jnp.minimum(v[i], v[j])
    return v


def _bitonic_merge_desc(v):
    v = list(v)
    d = len(v) // 2
    while d >= 1:
        for i in range(len(v)):
            if not i & d:
                v[i], v[i + d] = jnp.maximum(v[i], v[i + d]), jnp.minimum(v[i], v[i + d])
        d //= 2
    return v


def _dwconv_seg(u, prev2, w3):
    row = lax.broadcasted_iota(jnp.int32, u.shape, 0)
    p0, p1 = prev2[0:1], prev2[1:2]
    s1 = jnp.where(row == 0, p1, pltpu.roll(u, 1, 0))
    s2 = jnp.where(row == 0, p0, jnp.where(row == 1, p1, pltpu.roll(u, 2, 0)))
    return w3[0:1] * s2 + w3[1:2] * s1 + w3[2:3] * u


def _conv_tile(u, w3, prevs, seg):
    nseg = len(prevs)
    ys, news = [], []
    for s in range(nseg):
        us = u[s * seg:(s + 1) * seg]
        ys.append(_dwconv_seg(us, prevs[s], w3))
        news.append(us[seg - 2:seg])
    y = ys[0] if nseg == 1 else jnp.concatenate(ys, axis=0)
    return y, news


def _proj_kernel(x_ref, g_ref, w_ref, wk_ref, c128_ref, d128_ref, c64_ref, d64_ref, cw_ref, st_ref,
                 q_ref, k_ref, kb_ref, v_ref, vb_ref, qi_ref, kiwi_ref, kia_ref, kib_ref,
                 yc_ref, cn_ref, ga_ref, gc_ref,
                 xn_s, cb_s, cc_s, carry_s, *, seg, carried):
    i = pl.program_id(0)
    j = pl.program_id(1)

    @pl.when(j == 0)
    def _():
        xn_s[...] = _rms(x_ref[...], g_ref[...]).astype(BF16)

    def mm():
        return jnp.dot(xn_s[...], w_ref[...], preferred_element_type=F32)

    lane = lax.broadcasted_iota(jnp.int32, (TM, LANES), 1)

    def rope128(xh):
        partner = jnp.where(lane < ROT_DIM // 2, pltpu.roll(xh, LANES - ROT_DIM // 2, 1),
                            pltpu.roll(xh, ROT_DIM // 2, 1))
        return xh * c128_ref[...] + partner * d128_ref[...]

    def rope64(xh, c, d):
        first = (lane & (IDX_DIM - 1)) < IDX_ROT_DIM // 2
        partner = jnp.where(first, pltpu.roll(xh, LANES - IDX_ROT_DIM // 2, 1),
                            pltpu.roll(xh, IDX_ROT_DIM // 2, 1))
        return xh * c + partner * d

    @pl.when(j < J_KV)
    def _():
        acc = mm()
        for h in range(TN // LANES):
            sl = slice(h * LANES, (h + 1) * LANES)
            q_ref[:, sl] = (rope128(acc[:, sl]) * Q_PRESCALE).astype(BF16)

    @pl.when(j == J_KV)
    def _():
        acc = mm()
        for h in range(N_KV_HEADS):
            sl = slice(h * HEAD_DIM, (h + 1) * HEAD_DIM)
            r = rope128(acc[:, sl])
            k_ref[pl.ds(h, TM, stride=N_KV_HEADS), :] = r
            kb_ref[:, sl] = r.astype(BF16)
            v = acc[:, KV_W + h * HEAD_DIM:KV_W + (h + 1) * HEAD_DIM]
            v_ref[pl.ds(h, TM, stride=N_KV_HEADS), :] = v
            vb_ref[:, sl] = v.astype(BF16)

    @pl.when(jnp.logical_and(j >= J_QI, j < J_CONV))
    def _():
        acc = mm()
        for h in range(TN // LANES):
            sl = slice(h * LANES, (h + 1) * LANES)
            qi_ref[:, sl] = rope64(acc[:, sl], c64_ref[...], d64_ref[...]).astype(BF16)

    @pl.when(j == J_QI)
    def _():
        is_ki = lane < IDX_DIM
        c = jnp.where(is_ki, c64_ref[...], 1.0)
        d = jnp.where(is_ki, d64_ref[...], 0.0)
        acc = jnp.dot(xn_s[...], wk_ref[...], preferred_element_type=F32)
        r = rope64(acc, c, d)
        kiwi_ref[...] = r
        ka = jnp.where(is_ki, r, 0.0)
        kia_ref[...] = ka.astype(BF16)
        kib_ref[...] = pltpu.roll(ka, IDX_DIM, 1).astype(BF16)

    jc = jnp.maximum(j - J_CONV, 0)
    in_conv = jnp.logical_and(j >= J_CONV, j < J_GA)
    cblk = jnp.minimum(jc // 3, N_CBLK - 1)
    part = jc % 3

    @pl.when(jnp.logical_and(in_conv, part == 0))
    def _():
        cb_s[...] = mm()

    @pl.when(jnp.logical_and(in_conv, part == 1))
    def _():
        cc_s[...] = mm()

    @pl.when(jnp.logical_and(in_conv, part == 2))
    def _():
        nseg = TM // seg
        if carried:
            @pl.when(i == 0)
            def _():
                carry_s[cblk] = jnp.zeros((8, TN), F32)
                carry_s[cblk, 0:2, :] = st_ref[0]
            prevs = [carry_s[cblk, 0:2, :]]
        else:
            prevs = [st_ref[s] for s in range(nseg)]
        u = cc_s[...] * mm()
        y, news = _conv_tile(u, cw_ref[...], prevs, seg)
        yc_ref[...] = (cb_s[...] * y).astype(BF16)
        for s in range(nseg):
            cn_ref[s] = news[s]
        if carried:
            carry_s[cblk, 0:2, :] = news[0]

    @pl.when(jnp.logical_and(j >= J_GA, j < J_GC))
    def _():
        ga_ref[...] = 1.0 / (1.0 + jnp.exp(-mm()))

    @pl.when(j >= J_GC)
    def _():
        gc_ref[...] = 1.0 / (1.0 + jnp.exp(-mm()))


def _proj(x, g, w_pair, tabs, conv_w, state, seg):
    w, w_kiwi = w_pair
    t = x.shape[0]
    ni = t // TM
    carried = seg == t
    nseg_tile = 1 if carried else TM // seg
    c128, d128, c64, d64 = tabs

    def row(i, j):
        return (i, 0)

    def const(i, j):
        return (0, 0)

    def cblk(j):
        return jnp.minimum(jnp.maximum(j - J_CONV, 0) // 3, N_CBLK - 1)

    def st_map(i, j):
        return (0 if carried else i, 0, cblk(j))

    in_specs = [
        pl.BlockSpec((TM, D_MODEL), row),
        pl.BlockSpec((1, D_MODEL), const),
        pl.BlockSpec((D_MODEL, TN), lambda i, j: (0, j)),
        pl.BlockSpec((D_MODEL, LANES), const),
        pl.BlockSpec((TM, LANES), row),
        pl.BlockSpec((TM, LANES), row),
        pl.BlockSpec((TM, LANES), row),
        pl.BlockSpec((TM, LANES), row),
        pl.BlockSpec((3, TN), lambda i, j: (0, cblk(j))),
        pl.BlockSpec((nseg_tile, 2, TN), st_map),
    ]
    out_shape = [
        jax.ShapeDtypeStruct((t, Q_W), BF16),
        jax.ShapeDtypeStruct((t * N_KV_HEADS, HEAD_DIM), F32),
        jax.ShapeDtypeStruct((t, KV_W), BF16),
        jax.ShapeDtypeStruct((t * N_KV_HEADS, HEAD_DIM), F32),
        jax.ShapeDtypeStruct((t, KV_W), BF16),
        jax.ShapeDtypeStruct((t, IQ_W), BF16),
        jax.ShapeDtypeStruct((t, LANES), F32),
        jax.ShapeDtypeStruct((t, LANES), BF16),
        jax.ShapeDtypeStruct((t, LANES), BF16),
        jax.ShapeDtypeStruct((t, D_MODEL), BF16),
        jax.ShapeDtypeStruct((ni * nseg_tile, 2, D_MODEL), F32),
        jax.ShapeDtypeStruct((t, D_MODEL), F32),
        jax.ShapeDtypeStruct((t, D_MODEL), F32),
    ]
    out_specs = [
        pl.BlockSpec((TM, TN), lambda i, j: (i, jnp.clip(j - J_Q, 0, J_KV - J_Q - 1))),
        pl.BlockSpec((TM * N_KV_HEADS, HEAD_DIM), row),
        pl.BlockSpec((TM, KV_W), row),
        pl.BlockSpec((TM * N_KV_HEADS, HEAD_DIM), row),
        pl.BlockSpec((TM, KV_W), row),
        pl.BlockSpec((TM, TN), lambda i, j: (i, jnp.clip(j - J_QI, 0, J_CONV - J_QI - 1))),
        pl.BlockSpec((TM, LANES), row),
        pl.BlockSpec((TM, LANES), row),
        pl.BlockSpec((TM, LANES), row),
        pl.BlockSpec((TM, TN), lambda i, j: (i, cblk(j))),
        pl.BlockSpec((nseg_tile, 2, TN), lambda i, j: (i, 0, cblk(j))),
        pl.BlockSpec((TM, TN), lambda i, j: (i, jnp.clip(j - J_GA, 0, N_CBLK - 1))),
        pl.BlockSpec((TM, TN), lambda i, j: (i, jnp.clip(j - J_GC, 0, N_CBLK - 1))),
    ]
    return pl.pallas_call(
        functools.partial(_proj_kernel, seg=min(seg, TM), carried=carried),
        grid=(ni, NJ),
        in_specs=in_specs,
        out_specs=out_specs,
        out_shape=out_shape,
        scratch_shapes=[
            pltpu.VMEM((TM, D_MODEL), BF16),
            pltpu.VMEM((TM, TN), F32),
            pltpu.VMEM((TM, TN), F32),
            pltpu.VMEM((N_CBLK, 8, TN), F32),
        ],
        compiler_params=pltpu.CompilerParams(
            dimension_semantics=("arbitrary", "arbitrary"), vmem_limit_bytes=VMEM_LIMIT),
        name="proj",
    )(x, g, w, w_kiwi, c128, d128, c64, d64, conv_w, state)


def _select_kernel(qi_ref, kiwi_ref, kia_ref, kib_ref, bias_ref,
                   sc_s, wb_s, lg_s, lh_s, cand_s, thr_s, cnt_s, top_s, *,
                   tq, spb, nkb_total, topk, causal, n_valid, q_pos0):
    n = pl.program_id(0)
    npair = N_IDX_HEADS // 2
    blk = spb * LANES
    kiwi = kiwi_ref[...]
    wscale = (IDX_DIM ** -0.5) * (N_IDX_HEADS ** -0.5)
    for h in range(N_IDX_HEADS):
        wb_s[h] = jnp.broadcast_to(kiwi[:, IDX_DIM + h:IDX_DIM + h + 1], (tq, LANES)) * wscale
    q2 = jnp.concatenate([qi_ref[:, p * LANES:(p + 1) * LANES] for p in range(npair)], axis=0)

    if causal:
        qpos0 = n * tq
        nkb = (qpos0 + tq + blk - 1) // blk
    else:
        qpos0 = q_pos0
        nkb = nkb_total
    lane = lax.broadcasted_iota(jnp.int32, (tq, LANES), 1)
    qchunk = (qpos0 + lax.broadcasted_iota(jnp.int32, (tq, LANES), 0)) >> CHUNK_SHIFT
    nt = (((1,), (1,)), ((), ()))

    def logits_into(lg_ref, kb):
        base = pl.multiple_of(jnp.minimum(kb, nkb_total - 1) * blk, blk)
        lg_ref[0] = lax.dot_general(q2, kia_ref[pl.ds(base, blk), :], nt, preferred_element_type=F32)
        lg_ref[1] = lax.dot_general(q2, kib_ref[pl.ds(base, blk), :], nt, preferred_element_type=F32)

    def head_sum(lg_ref, kb, carry):
        m1, m2 = carry
        base = kb * blk
        for c in range(spb):
            cs = slice(c * LANES, (c + 1) * LANES)
            acc = jnp.zeros((tq, LANES), F32)
            for p in range(npair):
                rs = slice(p * tq, (p + 1) * tq)
                acc = acc + jnp.maximum(lg_ref[0, rs, cs], 0.0) * wb_s[2 * p]
                acc = acc + jnp.maximum(lg_ref[1, rs, cs], 0.0) * wb_s[2 * p + 1]
            col = base + c * LANES + lane
            adm = jnp.logical_and((col >> CHUNK_SHIFT) <= qchunk, col < n_valid)
            sc = jnp.where(adm, acc, NEG_INF)
            m2 = jnp.maximum(m2, jnp.minimum(m1, sc))
            m1 = jnp.maximum(m1, sc)
            sc_s[kb * spb + c] = sc
        return m1, m2

    logits_into(lg_s, 0)

    def score_pair(i, carry):
        logits_into(lh_s, 2 * i + 1)
        carry = head_sum(lg_s, 2 * i, carry)
        logits_into(lg_s, 2 * i + 2)
        return head_sum(lh_s, 2 * i + 1, carry)

    neg = jnp.full((tq, LANES), NEG_INF, F32)
    m1, m2 = lax.fori_loop(0, nkb // 2, score_pair, (neg, neg))
    top_s[0] = m1
    top_s[1] = m2

    @pl.when(nkb % 2 == 1)
    def _():
        last1, last2 = head_sum(lg_s, nkb - 1, (top_s[0], top_s[1]))
        top_s[0] = last1
        top_s[1] = last2

    m1, m2 = top_s[0], top_s[1]

    zeros = jnp.zeros((tq, LANES), F32)
    kf = float(topk)

    def count_all(thr_f, strict=False):
        def body(kb, acc):
            for c in range(spb):
                s = sc_s[kb * spb + c]
                acc = acc + jnp.where(s > thr_f if strict else s >= thr_f, 1.0, 0.0)
            return acc
        return jnp.sum(lax.fori_loop(0, nkb, body, zeros), axis=1, keepdims=True)

    def count_cand(thr_f):
        acc = zeros
        for i in range(CAND):
            acc = acc + jnp.where(cand_s[i] >= thr_f, 1.0, 0.0)
        return jnp.sum(acc, axis=1, keepdims=True)

    ones = jnp.ones((tq, LANES), jnp.int32)
    lo0 = _sortable(jnp.min(m2, axis=1, keepdims=True)) * ones
    hi0 = _sortable(jnp.max(m1, axis=1, keepdims=True)) * ones + 1

    def bisect(count):
        def unresolved(lo_k, hi_k, c_lo):
            open_ = jnp.logical_and(c_lo != kf, (hi_k - lo_k) != 1)
            return jnp.max(jnp.where(open_, 1.0, 0.0))

        def cond(st):
            return jnp.logical_and(st[0] <= KEY_BITS, st[-1] > 0.0)

        def body(st):
            it, lo_k, hi_k, c_lo, _ = st
            mid = lo_k + lax.shift_right_logical(hi_k - lo_k, 1)
            cnt = count(_unsortable(mid))
            ge = cnt >= kf
            lo_k = jnp.where(ge, mid, lo_k)
            hi_k = jnp.where(ge, hi_k, mid)
            c_lo = jnp.where(ge, cnt, c_lo)
            return it + 1, lo_k, hi_k, c_lo, unresolved(lo_k, hi_k, c_lo)

        c0 = jnp.full((tq, LANES), -1.0, F32)
        st = lax.while_loop(cond, body, (jnp.int32(0), lo0, hi0, c0, unresolved(lo0, hi0, c0)))
        return st[1], st[3]

    nsl = nkb * spb
    nchunk = (nsl + CAND - 1) // CAND
    pad_tile = jnp.full((tq, LANES), PAD_SCORE, F32)

    def pad(sidx, carry):
        sc_s[sidx] = pad_tile
        return carry

    lax.fori_loop(nsl, nchunk * CAND, pad, 0)

    def gather_rows(rg, carry):
        r0 = pl.multiple_of(rg * 8, 8)

        def chunk(ch, cand):
            new = _sort_desc([sc_s[ch * CAND + i, pl.ds(r0, 8), :] for i in range(CAND)])
            return tuple(_bitonic_merge_desc([jnp.maximum(cand[i], new[CAND - 1 - i]) for i in range(CAND)]))

        start = tuple(jnp.full((8, LANES), PAD_SCORE, F32) for _ in range(CAND))
        cand = lax.fori_loop(0, nchunk, chunk, start)
        for i in range(CAND):
            cand_s[i, pl.ds(r0, 8), :] = cand[i]
        return carry

    lax.fori_loop(0, tq // 8, gather_rows, 0)

    thr_c, cnt_c = bisect(count_cand)
    thr_s[...] = thr_c
    cnt_s[...] = cnt_c
    last = jnp.max(cand_s[CAND - 1], axis=1, keepdims=True)
    covered = jnp.logical_or(last < _unsortable(thr_c), last <= 0.5 * NEG_INF)

    @pl.when(jnp.min(jnp.where(covered, 1.0, 0.0)) == 0.0)
    def _():
        thr_a, cnt_a = bisect(count_all)
        thr_s[...] = thr_a
        cnt_s[...] = cnt_a

    thr = _unsortable(thr_s[...])
    c_thr = cnt_s[...]

    tied = jnp.max(jnp.where(jnp.logical_and(c_thr != kf, thr > 0.5 * NEG_INF), 1.0, 0.0)) > 0.0

    @pl.when(jnp.logical_not(tied))
    def _():
        def emit(kb, carry):
            for c in range(spb):
                s = sc_s[kb * spb + c]
                sel = jnp.logical_and(s >= thr, s > 0.5 * NEG_INF)
                bias_ref[kb * spb + c] = jnp.where(sel, 0.0, NEG_INF).astype(BF16)
            return carry

        lax.fori_loop(0, nkb, emit, 0)

    @pl.when(tied)
    def _():
        need = kf - count_all(thr, strict=True)
        tri = (lax.broadcasted_iota(jnp.int32, (LANES, LANES), 0)
               <= lax.broadcasted_iota(jnp.int32, (LANES, LANES), 1)).astype(BF16)

        def emit(kb, seen):
            for c in range(spb):
                s = sc_s[kb * spb + c]
                eq = jnp.where(s == thr, 1.0, 0.0)
                rank = seen + jnp.dot(eq.astype(BF16), tri, preferred_element_type=F32)
                keep = jnp.logical_or(s > thr, jnp.logical_and(s == thr, rank <= need))
                sel = jnp.logical_and(keep, s > 0.5 * NEG_INF)
                bias_ref[kb * spb + c] = jnp.where(sel, 0.0, NEG_INF).astype(BF16)
                seen = seen + jnp.sum(eq, axis=1, keepdims=True)
            return seen

        lax.fori_loop(0, nkb, emit, zeros)

    def fill(kb, carry):
        for c in range(spb):
            bias_ref[kb * spb + c] = jnp.full((tq, LANES), NEG_INF, BF16)
        return carry

    lax.fori_loop(nkb, nkb_total, fill, 0)


def _select(qi, kiwi, kia, kib, *, tq, spb, topk, causal, n_valid, q_pos0):
    t = qi.shape[0]
    nb = t // tq
    lk = kia.shape[1]
    nslab = lk // LANES
    nkb_total = nslab // spb
    assert topk <= 2 * LANES and nslab >= 2, "the bisection's starting lower bound needs two keys per lane"

    def kmap(n):
        return (0 if causal else n, 0, 0)

    return pl.pallas_call(
        functools.partial(_select_kernel, tq=tq, spb=spb, nkb_total=nkb_total, topk=topk,
                          causal=causal, n_valid=n_valid, q_pos0=q_pos0),
        grid=(nb,),
        in_specs=[
            pl.BlockSpec((tq, IQ_W), lambda n: (n, 0)),
            pl.BlockSpec((tq, LANES), lambda n: (n, 0)),
            pl.BlockSpec((None, lk, LANES), kmap),
            pl.BlockSpec((None, lk, LANES), kmap),
        ],
        out_specs=pl.BlockSpec((nslab, tq, LANES), lambda n: (0, n, 0)),
        out_shape=jax.ShapeDtypeStruct((nslab, t, LANES), BF16),
        scratch_shapes=[
            pltpu.VMEM((nslab + CAND, tq, LANES), F32),
            pltpu.VMEM((N_IDX_HEADS, tq, LANES), F32),
            pltpu.VMEM((2, N_IDX_HEADS // 2 * tq, spb * LANES), F32),
            pltpu.VMEM((2, N_IDX_HEADS // 2 * tq, spb * LANES), F32),
            pltpu.VMEM((CAND, tq, LANES), F32),
            pltpu.VMEM((tq, LANES), jnp.int32),
            pltpu.VMEM((tq, LANES), F32),
            pltpu.VMEM((2, tq, LANES), F32),
        ],
        compiler_params=pltpu.CompilerParams(
            dimension_semantics=("arbitrary",), vmem_limit_bytes=VMEM_LIMIT),
        name="select",
    )(qi, kiwi, kia, kib)


FIXED_REF_LIMIT = 2.0 ** 40

def _attend_kernel(qb_ref, kb_ref, kbat_ref, last_ref, q_ref, k_ref, v_ref, b_ref, o_ref,
                   m_s, l_s, acc_s, par_s, exact_s, *, tq, spb):
    s = pl.program_id(0)
    first = kb_ref[s] == 0

    biases = [b_ref[c].astype(F32)[None] for c in range(spb)]
    ones = jnp.ones((spb * LANES, LANES), BF16)

    def masked_scores(g):
        qg = [q_ref[:, (g * GROUP + h) * HEAD_DIM:(g * GROUP + h + 1) * HEAD_DIM] for h in range(GROUP)]
        q4 = jnp.concatenate(qg, axis=0)
        kg = k_ref[:, g * HEAD_DIM:(g + 1) * HEAD_DIM]
        sc = lax.dot_general(q4, kg, (((1,), (1,)), ((), ())), preferred_element_type=F32)
        return [(sc[:, c * LANES:(c + 1) * LANES].reshape(GROUP, tq, LANES) + biases[c]).reshape(GROUP * tq, LANES)
                for c in range(spb)]

    def weighted_values(g, m_ref, slabs):
        pmat = jnp.concatenate([jnp.exp2(sl - m_ref).astype(BF16) for sl in slabs], axis=1)
        v1 = jnp.concatenate([v_ref[:, g * HEAD_DIM:(g + 1) * HEAD_DIM], ones], axis=1)
        pv = jnp.dot(pmat, v1, preferred_element_type=F32)
        return pv[:, 0:HEAD_DIM], pv[:, HEAD_DIM:HEAD_DIM + LANES]

    @pl.when(first)
    def _():
        par_s[0] = 0
        exact_s[0] = 1
        m_s[...] = jnp.full(m_s.shape, 0.1 * NEG_INF, F32)
        l_s[0] = jnp.zeros(l_s.shape[1:], F32)
        acc_s[0] = jnp.zeros(acc_s.shape[1:], F32)

    @pl.when(jnp.logical_not(first))
    def _():
        par = par_s[0]
        worst = jnp.zeros((GROUP * tq, LANES), F32)
        for g in range(N_KV_HEADS):
            pv, psum = weighted_values(g, m_s[g], masked_scores(g))
            acc_new = acc_s[par, g] + pv
            l_new = l_s[par, g] + psum
            acc_s[1 - par, g] = acc_new
            l_s[1 - par, g] = l_new
            worst = worst + l_new + jnp.abs(acc_new)
        ok = jnp.min(jnp.where(worst < FIXED_REF_LIMIT, 1.0, 0.0)) > 0.5
        exact_s[0] = jnp.where(ok, 0, 1)
        par_s[0] = jnp.where(ok, 1 - par, par)

    @pl.when(exact_s[0] == 1)
    def _():
        par = par_s[0]
        for g in range(N_KV_HEADS):
            slabs = masked_scores(g)
            mx = slabs[0]
            for sl in slabs[1:]:
                mx = jnp.maximum(mx, sl)
            m_prev = m_s[g]
            m_new = jnp.maximum(m_prev, jnp.max(mx, axis=1, keepdims=True))
            alpha = jnp.exp2(m_prev - m_new)
            pv, psum = weighted_values(g, m_new, slabs)
            acc_s[par, g] = alpha * acc_s[par, g] + pv
            l_s[par, g] = alpha * l_s[par, g] + psum
            m_s[g] = m_new

    @pl.when(last_ref[s] == 1)
    def _():
        par = par_s[0]
        for g in range(N_KV_HEADS):
            o = acc_s[par, g] / l_s[par, g]
            for h in range(GROUP):
                col = (g * GROUP + h) * HEAD_DIM
                o_ref[:, col:col + HEAD_DIM] = o[h * tq:(h + 1) * tq].astype(BF16)


def _attend(q, k_all, v_all, bias, sched, *, tq, spb):
    t = q.shape[0]
    qb, kb, kbat, last = sched
    nsteps = qb.shape[0]
    blk = spb * LANES
    grid_spec = pltpu.PrefetchScalarGridSpec(
        num_scalar_prefetch=4,
        grid=(nsteps,),
        in_specs=[
            pl.BlockSpec((tq, Q_W), lambda s, qb, kb, kbat, last: (qb[s], 0)),
            pl.BlockSpec((None, blk, KV_W), lambda s, qb, kb, kbat, last: (kbat[s], kb[s], 0)),
            pl.BlockSpec((None, blk, KV_W), lambda s, qb, kb, kbat, last: (kbat[s], kb[s], 0)),
            pl.BlockSpec((spb, tq, LANES), lambda s, qb, kb, kbat, last: (kb[s], qb[s], 0)),
        ],
        out_specs=pl.BlockSpec((tq, Q_W), lambda s, qb, kb, kbat, last: (qb[s], 0)),
        scratch_shapes=[
            pltpu.VMEM((N_KV_HEADS, GROUP * tq, LANES), F32),
            pltpu.VMEM((2, N_KV_HEADS, GROUP * tq, LANES), F32),
            pltpu.VMEM((2, N_KV_HEADS, GROUP * tq, HEAD_DIM), F32),
            pltpu.SMEM((1,), jnp.int32),
            pltpu.SMEM((1,), jnp.int32),
        ],
    )
    return pl.pallas_call(
        functools.partial(_attend_kernel, tq=tq, spb=spb),
        grid_spec=grid_spec,
        out_shape=jax.ShapeDtypeStruct((t, Q_W), BF16),
        compiler_params=pltpu.CompilerParams(
            dimension_semantics=("arbitrary",), vmem_limit_bytes=VMEM_LIMIT),
        name="attend",
    )(qb, kb, kbat, last, q, k_all, v_all, bias)


def _causal_schedule(t, tq, blk):
    qb, kb, last = [], [], []
    for n in range(t // tq):
        nk = ((n + 1) * tq + blk - 1) // blk
        for k in range(nk):
            qb.append(n)
            kb.append(k)
            last.append(1 if k == nk - 1 else 0)
    z = np.zeros(len(qb), np.int32)
    return (jnp.asarray(qb, jnp.int32), jnp.asarray(kb, jnp.int32), jnp.asarray(z), jnp.asarray(last, jnp.int32))


def _batched_schedule(nbatch, nk):
    qb = np.repeat(np.arange(nbatch, dtype=np.int32), nk)
    kb = np.tile(np.arange(nk, dtype=np.int32), nbatch)
    last = (kb == nk - 1).astype(np.int32)
    return (jnp.asarray(qb), jnp.asarray(kb), jnp.asarray(qb), jnp.asarray(last))


def _merge_kernel(o_ref, yc_ref, ga_ref, gc_ref, woa_ref, woc_ref, wout_ref, x_ref, gq_ref, gp_ref,
                  x1_ref, xn2_ref, mg_s, m_s):
    j = pl.program_id(1)
    nblk = D_MODEL // MERGE_TN

    @pl.when(j < nblk)
    def _():
        a = jnp.dot(o_ref[...], woa_ref[...], preferred_element_type=F32)
        c = jnp.dot(yc_ref[...], woc_ref[...], preferred_element_type=F32)
        mg_s[j] = (ga_ref[...] * a + gc_ref[...] * c).astype(BF16)

    @pl.when(j >= nblk)
    def _():
        mg = jnp.concatenate([mg_s[b] for b in range(nblk)], axis=1)
        m_s[j - nblk] = jnp.dot(mg, wout_ref[...], preferred_element_type=F32)

    @pl.when(j == 2 * nblk - 1)
    def _():
        m = jnp.concatenate([m_s[b] for b in range(nblk)], axis=1)
        x1 = x_ref[...] + _rms(m, gq_ref[...])
        x1_ref[...] = x1
        xn2_ref[...] = _rms(x1, gp_ref[...]).astype(BF16)


def _merge(o, yc, ga, gc, woa, woc, wout, x, gq, gp):
    t = x.shape[0]
    tn = MERGE_TN
    nblk = D_MODEL // tn

    def row(i, j):
        return (i, 0)

    def lo(i, j):
        return (i, jnp.minimum(j, nblk - 1))

    return pl.pallas_call(
        _merge_kernel,
        grid=(t // TM, 2 * nblk),
        in_specs=[
            pl.BlockSpec((TM, Q_W), row),
            pl.BlockSpec((TM, D_MODEL), row),
            pl.BlockSpec((TM, tn), lo),
            pl.BlockSpec((TM, tn), lo),
            pl.BlockSpec((Q_W, tn), lambda i, j: (0, jnp.minimum(j, nblk - 1))),
            pl.BlockSpec((D_MODEL, tn), lambda i, j: (0, jnp.minimum(j, nblk - 1))),
            pl.BlockSpec((D_MODEL, tn), lambda i, j: (0, jnp.maximum(j - nblk, 0))),
            pl.BlockSpec((TM, D_MODEL), row),
            pl.BlockSpec((1, D_MODEL), lambda i, j: (0, 0)),
            pl.BlockSpec((1, D_MODEL), lambda i, j: (0, 0)),
        ],
        out_specs=[pl.BlockSpec((TM, D_MODEL), row), pl.BlockSpec((TM, D_MODEL), row)],
        out_shape=[jax.ShapeDtypeStruct((t, D_MODEL), F32), jax.ShapeDtypeStruct((t, D_MODEL), BF16)],
        scratch_shapes=[pltpu.VMEM((nblk, TM, tn), BF16), pltpu.VMEM((nblk, TM, tn), F32)],
        compiler_params=pltpu.CompilerParams(
            dimension_semantics=("arbitrary", "arbitrary"), vmem_limit_bytes=VMEM_LIMIT),
        name="merge",
    )(o, yc, ga, gc, woa, woc, wout, x, gq, gp)


def _ffn_kernel(xn_ref, wg_ref, wv_ref, cwg_ref, cwv_ref, wd_ref, x1_ref, gq_ref, stg_ref, stv_ref,
                y_ref, ng_ref, nv_ref, acc_s, cg_s, cv_s, *, seg, carried):
    i = pl.program_id(0)
    jf = pl.program_id(1)
    nseg = TM // seg

    @pl.when(jf == 0)
    def _():
        acc_s[...] = jnp.zeros(acc_s.shape, F32)

    if carried:
        @pl.when(i == 0)
        def _():
            for carry_s, st_ref in ((cg_s, stg_ref), (cv_s, stv_ref)):
                carry_s[jf] = jnp.zeros((8, TF), F32)
                carry_s[jf, 0:2, :] = st_ref[0]

    xn = xn_ref[...]

    def branch(w_ref, cw_ref, st_ref, carry_s, new_ref):
        up = jnp.dot(xn, w_ref[...], preferred_element_type=F32)
        if carried:
            prevs = [carry_s[jf, 0:2, :]]
        else:
            prevs = [st_ref[s] for s in range(nseg)]
        y, news = _conv_tile(up, cw_ref[...], prevs, seg)
        for s in range(nseg):
            new_ref[s] = news[s]
        if carried:
            carry_s[jf, 0:2, :] = news[0]
        return y

    gate = branch(wg_ref, cwg_ref, stg_ref, cg_s, ng_ref)
    val = branch(wv_ref, cwv_ref, stv_ref, cv_s, nv_ref)
    c0 = np.float32(np.sqrt(2.0 / np.pi))
    gelu = 0.5 * gate * (1.0 + jnp.tanh(c0 * (gate + 0.044715 * (gate * gate * gate))))
    hid = (gelu * val).astype(BF16)
    acc_s[...] += jnp.dot(hid, wd_ref[...], preferred_element_type=F32)

    @pl.when(jf == pl.num_programs(1) - 1)
    def _():
        y_ref[...] = x1_ref[...] + _rms(acc_s[...], gq_ref[...])


def _ffn(xn2, w_up, cw, w_down, x1, gq, state, seg):
    t = x1.shape[0]
    nf = D_FF // TF
    carried = seg == t
    nseg_tile = 1 if carried else TM // seg

    def row(i, j):
        return (i, 0)

    def stg(i, j):
        return (0 if carried else i, 0, j)

    def stv(i, j):
        return (0 if carried else i, 0, j + nf)

    return pl.pallas_call(
        functools.partial(_ffn_kernel, seg=min(seg, TM), carried=carried),
        grid=(t // TM, nf),
        in_specs=[
            pl.BlockSpec((TM, D_MODEL), row),
            pl.BlockSpec((D_MODEL, TF), lambda i, j: (0, j)),
            pl.BlockSpec((D_MODEL, TF), lambda i, j: (0, j + nf)),
            pl.BlockSpec((3, TF), lambda i, j: (0, j)),
            pl.BlockSpec((3, TF), lambda i, j: (0, j + nf)),
            pl.BlockSpec((TF, D_MODEL), lambda i, j: (j, 0)),
            pl.BlockSpec((TM, D_MODEL), row),
            pl.BlockSpec((1, D_MODEL), lambda i, j: (0, 0)),
            pl.BlockSpec((nseg_tile, 2, TF), stg),
            pl.BlockSpec((nseg_tile, 2, TF), stv),
        ],
        out_specs=[
            pl.BlockSpec((TM, D_MODEL), row),
            pl.BlockSpec((nseg_tile, 2, TF), lambda i, j: (i, 0, j)),
            pl.BlockSpec((nseg_tile, 2, TF), lambda i, j: (i, 0, j)),
        ],
        out_shape=[
            jax.ShapeDtypeStruct((t, D_MODEL), F32),
            jax.ShapeDtypeStruct((t // TM * nseg_tile, 2, D_FF), F32),
            jax.ShapeDtypeStruct((t // TM * nseg_tile, 2, D_FF), F32),
        ],
        scratch_shapes=[
            pltpu.VMEM((TM, D_MODEL), F32),
            pltpu.VMEM((nf, 8, TF), F32),
            pltpu.VMEM((nf, 8, TF), F32),
        ],
        compiler_params=pltpu.CompilerParams(
            dimension_semantics=("arbitrary", "arbitrary"), vmem_limit_bytes=VMEM_LIMIT),
        name="ffn",
    )(xn2, w_up, w_up, cw, cw, w_down, x1, gq, state, state)


PACK_ROWS = 1024


def _pack_kernel(ck_ref, cv_ref, ko_ref, vo_ref, *, rows):
    for src, dst in ((ck_ref, ko_ref), (cv_ref, vo_ref)):
        for g in range(N_KV_HEADS):
            dst[:, g * HEAD_DIM:(g + 1) * HEAD_DIM] = src[pl.ds(g, rows, stride=N_KV_HEADS), :].astype(BF16)


def _tail_kernel(kn_ref, vn_ref, ki_ref, vi_ref, ko_ref, vo_ref, *, seg):
    del ki_ref, vi_ref
    for src, dst in ((kn_ref, ko_ref), (vn_ref, vo_ref)):
        dst[0:seg, :] = src[...]
        if dst.shape[0] > seg:
            dst[seg:, :] = jnp.zeros((dst.shape[0] - seg, KV_W), BF16)


def _pack_cache(cache_k, cache_v, k_new, v_new, seg, lk):
    nb, plen = cache_k.shape[0], cache_k.shape[1]
    rows = int(np.gcd(plen, PACK_ROWS))
    tail = lk - plen
    assert rows % 16 == 0 and tail % 16 == 0 and plen % tail == 0 and tail >= seg
    cspec = pl.BlockSpec((None, rows * N_KV_HEADS, HEAD_DIM), lambda b, r: (b, r, 0))
    ospec = pl.BlockSpec((None, rows, KV_W), lambda b, r: (b, r, 0))
    flat = (nb, plen * N_KV_HEADS, HEAD_DIM)
    slab = jax.ShapeDtypeStruct((nb, lk, KV_W), BF16)
    k_all, v_all = pl.pallas_call(
        functools.partial(_pack_kernel, rows=rows),
        grid=(nb, plen // rows),
        in_specs=[cspec, cspec],
        out_specs=[ospec, ospec],
        out_shape=[slab, slab],
        compiler_params=pltpu.CompilerParams(
            dimension_semantics=("arbitrary", "arbitrary"), vmem_limit_bytes=VMEM_LIMIT),
        name="pack_cache",
    )(cache_k.reshape(flat), cache_v.reshape(flat))
    nspec = pl.BlockSpec((seg, KV_W), lambda b: (b, 0))
    anyspec = pl.BlockSpec(memory_space=pl.ANY)
    tspec = pl.BlockSpec((None, tail, KV_W), lambda b: (b, plen // tail, 0))
    return pl.pallas_call(
        functools.partial(_tail_kernel, seg=seg),
        grid=(nb,),
        in_specs=[nspec, nspec, anyspec, anyspec],
        out_specs=[tspec, tspec],
        out_shape=[slab, slab],
        input_output_aliases={2: 0, 3: 1},
        compiler_params=pltpu.CompilerParams(dimension_semantics=("arbitrary",), vmem_limit_bytes=VMEM_LIMIT),
        name="pack_tail",
    )(k_new, v_new, k_all, v_all)


def _rope_tables(pos, rot, width):
    half = rot // 2
    freqs = ROPE_THETA ** (-jnp.arange(half, dtype=F32) / half)
    ang = pos.astype(F32)[:, None] * freqs[None, :]
    cos, sin = jnp.cos(ang), jnp.sin(ang)
    t = pos.shape[0]
    c = jnp.concatenate([cos, cos, jnp.ones((t, width - rot), F32)], axis=1)
    d = jnp.concatenate([-sin, sin, jnp.zeros((t, width - rot), F32)], axis=1)
    reps = LANES // width
    return jnp.tile(c, (1, reps)), jnp.tile(d, (1, reps))


def _relayout_w_in(w):
    o = np.cumsum([0, Q_W, KV_W, KV_W, IQ_W, IDX_DIM, N_IDX_HEADS, D_MODEL, D_MODEL, D_MODEL, D_MODEL, D_MODEL])
    q, k, v, qi = (w[:, o[a]:o[a + 1]] for a in range(4))
    kiwi = w[:, o[4]:o[6]]
    cb, cc, ch, ga, gc = (w[:, o[a]:o[a + 1]] for a in range(6, 11))
    parts = [q, k, v, qi]
    for c in range(N_CBLK):
        sl = slice(c * TN, (c + 1) * TN)
        parts += [cb[:, sl], cc[:, sl], ch[:, sl]]
    parts += [ga, gc]
    w_kiwi = jnp.pad(kiwi, ((0, 0), (0, LANES - kiwi.shape[1]))).astype(BF16)
    return jnp.concatenate([p.astype(BF16) for p in parts], axis=1), w_kiwi


def _stream(x, pos, seg, past, weights, *, tq, tq_att, spb_sel, spb_att):
    (g_mp, g_mq, w_in_r, conv_w, woa, woc, wout, g_fp, g_fq, w_up, fconv_w, w_down) = weights
    t = x.shape[0]
    nseq = t // seg
    tabs = _rope_tables(pos, ROT_DIM, HEAD_DIM) + _rope_tables(pos, IDX_ROT_DIM, IDX_DIM)
    if past is None:
        conv_state = jnp.zeros((nseq, 2, D_MODEL), F32)
        ffn_state = jnp.zeros((nseq, 2, 2 * D_FF), F32)
    else:
        conv_state, ffn_state = past[3], past[4]

    (q, k, kb, v, vb, qi, kiwi, kia, kib, yc, conv_new, ga, gc) = _proj(
        x, g_mp, w_in_r, tabs, conv_w, conv_state, seg)

    if past is None:
        k_all, v_all = kb[None], vb[None]
        kia_all, kib_all = kia[None], kib[None]
        n_keys = t
        sched = _causal_schedule(t, tq_att, spb_att * LANES)
        q_pos0 = 0
    else:
        cache_k, cache_v, cache_ki = past[0], past[1], past[2]
        plen = cache_k.shape[1]
        n_keys = plen + seg
        lk = -(-n_keys // (spb_att * LANES)) * (spb_att * LANES)
        pad = lk - n_keys

        def cat(c, new, width):
            parts = [c, new.reshape(nseq, seg, width)]
            if pad:
                parts.append(jnp.zeros((nseq, pad, width), BF16))
            return jnp.concatenate(parts, axis=1)

        k_all, v_all = _pack_cache(cache_k, cache_v, kb, vb, seg, lk)
        cki = cache_ki.astype(BF16)
        zk = jnp.zeros_like(cki)
        kia_all = cat(jnp.concatenate([cki, zk], axis=-1), kia, LANES)
        kib_all = cat(jnp.concatenate([zk, cki], axis=-1), kib, LANES)
        sched = _batched_schedule(nseq, lk // (spb_att * LANES))
        q_pos0 = plen
    topk = min(TOPK_MAX, n_keys // 4)

    bias = _select(qi, kiwi, kia_all, kib_all, tq=tq, spb=spb_sel, topk=topk,
                   causal=past is None, n_valid=n_keys, q_pos0=q_pos0)
    o = _attend(q, k_all, v_all, bias, sched, tq=tq_att, spb=spb_att)
    x1, xn2 = _merge(o, yc, ga, gc, woa, woc, wout, x, g_mq, g_fp)
    y, ffn_g, ffn_v = _ffn(xn2, w_up, fconv_w, w_down, x1, g_fq, ffn_state, seg)
    ffn_new = jnp.concatenate([ffn_g, ffn_v], axis=-1)
    return y, k, v, kiwi[:, :IDX_DIM], conv_new[-nseq:], ffn_new[-nseq:]


def kernel(x_prompt, x_sample, cache_k, cache_v, cache_k_idx, state_conv, state_ffn_conv, norm_mix_pre, norm_mix_post, w_in, conv_w, w_o_attn, w_o_conv, w_out, norm_ffn_pre, norm_ffn_post, w_ffn_up, ffn_conv_w, w_ffn_down):
    depth = w_in.shape[0]
    assert depth == 1, "single-layer step"
    b, seq, _ = x_prompt.shape
    assert b == 1
    db, dseq, _ = x_sample.shape
    plen = cache_k.shape[2]
    assert dseq == CHUNK and plen % CHUNK == 0

    weights = (
        norm_mix_pre, norm_mix_post, _relayout_w_in(w_in[0]), conv_w[0],
        w_o_attn[0].astype(BF16), w_o_conv[0].astype(BF16), w_out[0].astype(BF16),
        norm_ffn_pre, norm_ffn_post, w_ffn_up[0].astype(BF16), ffn_conv_w[0], w_ffn_down[0].astype(BF16),
    )

    pos_p = jnp.arange(seq, dtype=jnp.int32)
    yp, kp, vp, kip, convp, ffnp = _stream(
        x_prompt.reshape(seq, D_MODEL), pos_p, seq, None, weights, tq=128, tq_att=512, spb_sel=4, spb_att=8)

    pos_s = jnp.tile(jnp.arange(dseq, dtype=jnp.int32) + plen, db)
    past = (cache_k[0], cache_v[0], cache_k_idx[0], state_conv[0], state_ffn_conv[0])
    n_keys = plen + dseq
    spb_s = _sample_slabs(n_keys)
    ys, ks, vs, kis, convs, ffns = _stream(
        x_sample.reshape(db * dseq, D_MODEL), pos_s, dseq, past, weights,
        tq=dseq, tq_att=dseq, spb_sel=spb_s, spb_att=spb_s)

    return (
        yp.reshape(1, seq, D_MODEL), ys.reshape(db, dseq, D_MODEL),
        kp.reshape(1, 1, seq, N_KV_HEADS, HEAD_DIM), vp.reshape(1, 1, seq, N_KV_HEADS, HEAD_DIM),
        kip.reshape(1, 1, seq, IDX_DIM), convp.reshape(1, 1, 2, D_MODEL), ffnp.reshape(1, 1, 2, 2 * D_FF),
        ks.reshape(1, db, dseq, N_KV_HEADS, HEAD_DIM), vs.reshape(1, db, dseq, N_KV_HEADS, HEAD_DIM),
        kis.reshape(1, db, dseq, IDX_DIM), convs.reshape(1, db, 2, D_MODEL), ffns.reshape(1, db, 2, 2 * D_FF),
    )


SAMPLE_MAX_SLABS = 11


def _sample_slabs(n_keys):
    nslab = -(-n_keys // LANES)
    return max(d for d in range(1, SAMPLE_MAX_SLABS + 1) if nslab % d == 0)
```

```python
import functools

import jax
import jax.numpy as jnp
import numpy as np
from jax import lax
from jax.experimental import pallas as pl
from jax.experimental.pallas import tpu as pltpu

F32 = jnp.float32
BF16 = jnp.bfloat16

D_MODEL = 2048
N_HEADS = 16
N_KV_HEADS = 4
HEAD_DIM = 128
ROT_DIM = HEAD_DIM // 4
N_IDX_HEADS = 16
IDX_DIM = 64
IDX_ROT_DIM = IDX_DIM // 4
CHUNK = 64
TOPK_MAX = 256
ROPE_THETA = 500000.0
D_FF = 5632
RMS_EPS = 1e-6
NEG_INF = -1e30
Q_W = N_HEADS * HEAD_DIM
KV_W = N_KV_HEADS * HEAD_DIM
IQ_W = N_IDX_HEADS * IDX_DIM
GROUP = N_HEADS // N_KV_HEADS

LANES = 128
TM = 512
TN = 1024
MERGE_TN = 512
TF = 512
VMEM_LIMIT = 56 * 1024 * 1024

J_Q = 0
J_KV = J_Q + Q_W // TN
assert 2 * KV_W == TN
J_QI = J_KV + 1
J_CONV = J_QI + IQ_W // TN
N_CBLK = D_MODEL // TN
J_GA = J_CONV + 3 * N_CBLK
J_GC = J_GA + N_CBLK
NJ = J_GC + N_CBLK

Q_PRESCALE = float(np.float32((HEAD_DIM ** -0.5) * np.log2(np.e)))


def _rms(x, g):
    return x * lax.rsqrt(jnp.mean(x * x, axis=-1, keepdims=True) + RMS_EPS) * g


def _sortable(x):
    bits = lax.bitcast_convert_type(x, jnp.int32)
    return jnp.where(bits < 0, bits ^ jnp.int32(0x7FFFFFFF), bits)


def _unsortable(k):
    return lax.bitcast_convert_type(jnp.where(k < 0, k ^ jnp.int32(0x7FFFFFFF), k), F32)


CHUNK_SHIFT = CHUNK.bit_length() - 1
assert 1 << CHUNK_SHIFT == CHUNK
KEY_BITS = 32
BISECT_UNROLL = 2
CAND = 16
PAD_SCORE = -3.0e38


def _oddeven_merge_sort_pairs(n):
    out, p = [], 1
    while p < n:
        k = p
        while k >= 1:
            for j in range(k % p, n - k, 2 * k):
                for i in range(min(k, n - j - k)):
                    if (i + j) // (2 * p) == (i + j + k) // (2 * p):
                        out.append((i + j, i + j + k))
            k //= 2
        p *= 2
    return out


_SORT_PAIRS = _oddeven_merge_sort_pairs(CAND)


def _sort_desc(v):
    v = list(v)
    for i, j in _SORT_PAIRS:
        v[i], v[j] = jnp.maximum(v[i], v[j]), jnp.minimum(v[i], v[j])
    return v


def _bitonic_merge_desc(v):
    v = list(v)
    d = len(v) // 2
    while d >= 1:
        for i in range(len(v)):
            if not i & d:
                v[i], v[i + d] = jnp.maximum(v[i], v[i + d]), jnp.minimum(v[i], v[i + d])
        d //= 2
    return v


def _dwconv_seg(u, prev2, w3):
    row = lax.broadcasted_iota(jnp.int32, u.shape, 0)
    p0, p1 = prev2[0:1], prev2[1:2]
    s1 = jnp.where(row == 0, p1, pltpu.roll(u, 1, 0))
    s2 = jnp.where(row == 0, p0, jnp.where(row == 1, p1, pltpu.roll(u, 2, 0)))
    return w3[0:1] * s2 + w3[1:2] * s1 + w3[2:3] * u


def _conv_tile(u, w3, prevs, seg):
    nseg = len(prevs)
    ys, news = [], []
    for s in range(nseg):
        us = u[s * seg:(s + 1) * seg]
        ys.append(_dwconv_seg(us, prevs[s], w3))
        news.append(us[seg - 2:seg])
    y = ys[0] if nseg == 1 else jnp.concatenate(ys, axis=0)
    return y, news


def _proj_kernel(x_ref, g_ref, w_ref, wk_ref, c128_ref, d128_ref, c64_ref, d64_ref, cw_ref, st_ref,
                 q_ref, k_ref, kb_ref, v_ref, vb_ref, qi_ref, kiwi_ref, kia_ref, kib_ref,
                 yc_ref, cn_ref, ga_ref, gc_ref,
                 xn_s, cb_s, cc_s, carry_s, *, seg, carried):
    i = pl.program_id(0)
    j = pl.program_id(1)

    @pl.when(j == 0)
    def _():
        xn_s[...] = _rms(x_ref[...], g_ref[...]).astype(BF16)

    def mm():
        return jnp.dot(xn_s[...], w_ref[...], preferred_element_type=F32)

    lane = lax.broadcasted_iota(jnp.int32, (TM, LANES), 1)

    def rope128(xh):
        partner = jnp.where(lane < ROT_DIM // 2, pltpu.roll(xh, LANES - ROT_DIM // 2, 1),
                            pltpu.roll(xh, ROT_DIM // 2, 1))
        return xh * c128_ref[...] + partner * d128_ref[...]

    def rope64(xh, c, d):
        first = (lane & (IDX_DIM - 1)) < IDX_ROT_DIM // 2
        partner = jnp.where(first, pltpu.roll(xh, LANES - IDX_ROT_DIM // 2, 1),
                            pltpu.roll(xh, IDX_ROT_DIM // 2, 1))
        return xh * c + partner * d

    @pl.when(j < J_KV)
    def _():
        acc = mm()
        for h in range(TN // LANES):
            sl = slice(h * LANES, (h + 1) * LANES)
            q_ref[:, sl] = (rope128(acc[:, sl]) * Q_PRESCALE).astype(BF16)

    @pl.when(j == J_KV)
    def _():
        acc = mm()
        for h in range(N_KV_HEADS):
            sl = slice(h * HEAD_DIM, (h + 1) * HEAD_DIM)
            r = rope128(acc[:, sl])
            k_ref[pl.ds(h, TM, stride=N_KV_HEADS), :] = r
            kb_ref[:, sl] = r.astype(BF16)
            v = acc[:, KV_W + h * HEAD_DIM:KV_W + (h + 1) * HEAD_DIM]
            v_ref[pl.ds(h, TM, stride=N_KV_HEADS), :] = v
            vb_ref[:, sl] = v.astype(BF16)

    @pl.when(jnp.logical_and(j >= J_QI, j < J_CONV))
    def _():
        acc = mm()
        for h in range(TN // LANES):
            sl = slice(h * LANES, (h + 1) * LANES)
            qi_ref[:, sl] = rope64(acc[:, sl], c64_ref[...], d64_ref[...]).astype(BF16)

    @pl.when(j == J_QI)
    def _():
        is_ki = lane < IDX_DIM
        c = jnp.where(is_ki, c64_ref[...], 1.0)
        d = jnp.where(is_ki, d64_ref[...], 0.0)
        acc = jnp.dot(xn_s[...], wk_ref[...], preferred_element_type=F32)
        r = rope64(acc, c, d)
        kiwi_ref[...] = r
        ka = jnp.where(is_ki, r, 0.0)
        kia_ref[...] = ka.astype(BF16)
        kib_ref[...] = pltpu.roll(ka, IDX_DIM, 1).astype(BF16)

    jc = jnp.maximum(j - J_CONV, 0)
    in_conv = jnp.logical_and(j >= J_CONV, j < J_GA)
    cblk = jnp.minimum(jc // 3, N_CBLK - 1)
    part = jc % 3

    @pl.when(jnp.logical_and(in_conv, part == 0))
    def _():
        cb_s[...] = mm()

    @pl.when(jnp.logical_and(in_conv, part == 1))
    def _():
        cc_s[...] = mm()

    @pl.when(jnp.logical_and(in_conv, part == 2))
    def _():
        nseg = TM // seg
        if carried:
            @pl.when(i == 0)
            def _():
                carry_s[cblk] = jnp.zeros((8, TN), F32)
                carry_s[cblk, 0:2, :] = st_ref[0]
            prevs = [carry_s[cblk, 0:2, :]]
        else:
            prevs = [st_ref[s] for s in range(nseg)]
        u = cc_s[...] * mm()
        y, news = _conv_tile(u, cw_ref[...], prevs, seg)
        yc_ref[...] = (cb_s[...] * y).astype(BF16)
        for s in range(nseg):
            cn_ref[s] = news[s]
        if carried:
            carry_s[cblk, 0:2, :] = news[0]

    @pl.when(jnp.logical_and(j >= J_GA, j < J_GC))
    def _():
        ga_ref[...] = 1.0 / (1.0 + jnp.exp(-mm()))

    @pl.when(j >= J_GC)
    def _():
        gc_ref[...] = 1.0 / (1.0 + jnp.exp(-mm()))


def _proj(x, g, w_pair, tabs, conv_w, state, seg):
    w, w_kiwi = w_pair
    t = x.shape[0]
    ni = t // TM
    carried = seg == t
    nseg_tile = 1 if carried else TM // seg
    c128, d128, c64, d64 = tabs

    def row(i, j):
        return (i, 0)

    def const(i, j):
        return (0, 0)

    def cblk(j):
        return jnp.minimum(jnp.maximum(j - J_CONV, 0) // 3, N_CBLK - 1)

    def st_map(i, j):
        return (0 if carried else i, 0, cblk(j))

    in_specs = [
        pl.BlockSpec((TM, D_MODEL), row),
        pl.BlockSpec((1, D_MODEL), const),
        pl.BlockSpec((D_MODEL, TN), lambda i, j: (0, j)),
        pl.BlockSpec((D_MODEL, LANES), const),
        pl.BlockSpec((TM, LANES), row),
        pl.BlockSpec((TM, LANES), row),
        pl.BlockSpec((TM, LANES), row),
        pl.BlockSpec((TM, LANES), row),
        pl.BlockSpec((3, TN), lambda i, j: (0, cblk(j))),
        pl.BlockSpec((nseg_tile, 2, TN), st_map),
    ]
    out_shape = [
        jax.ShapeDtypeStruct((t, Q_W), BF16),
        jax.ShapeDtypeStruct((t * N_KV_HEADS, HEAD_DIM), F32),
        jax.ShapeDtypeStruct((t, KV_W), BF16),
        jax.ShapeDtypeStruct((t * N_KV_HEADS, HEAD_DIM), F32),
        jax.ShapeDtypeStruct((t, KV_W), BF16),
        jax.ShapeDtypeStruct((t, IQ_W), BF16),
        jax.ShapeDtypeStruct((t, LANES), F32),
        jax.ShapeDtypeStruct((t, LANES), BF16),
        jax.ShapeDtypeStruct((t, LANES), BF16),
        jax.ShapeDtypeStruct((t, D_MODEL), BF16),
        jax.ShapeDtypeStruct((ni * nseg_tile, 2, D_MODEL), F32),
        jax.ShapeDtypeStruct((t, D_MODEL), F32),
        jax.ShapeDtypeStruct((t, D_MODEL), F32),
    ]
    out_specs = [
        pl.BlockSpec((TM, TN), lambda i, j: (i, jnp.clip(j - J_Q, 0, J_KV - J_Q - 1))),
        pl.BlockSpec((TM * N_KV_HEADS, HEAD_DIM), row),
        pl.BlockSpec((TM, KV_W), row),
        pl.BlockSpec((TM * N_KV_HEADS, HEAD_DIM), row),
        pl.BlockSpec((TM, KV_W), row),
        pl.BlockSpec((TM, TN), lambda i, j: (i, jnp.clip(j - J_QI, 0, J_CONV - J_QI - 1))),
        pl.BlockSpec((TM, LANES), row),
        pl.BlockSpec((TM, LANES), row),
        pl.BlockSpec((TM, LANES), row),
        pl.BlockSpec((TM, TN), lambda i, j: (i, cblk(j))),
        pl.BlockSpec((nseg_tile, 2, TN), lambda i, j: (i, 0, cblk(j))),
        pl.BlockSpec((TM, TN), lambda i, j: (i, jnp.clip(j - J_GA, 0, N_CBLK - 1))),
        pl.BlockSpec((TM, TN), lambda i, j: (i, jnp.clip(j - J_GC, 0, N_CBLK - 1))),
    ]
    return pl.pallas_call(
        functools.partial(_proj_kernel, seg=min(seg, TM), carried=carried),
        grid=(ni, NJ),
        in_specs=in_specs,
        out_specs=out_specs,
        out_shape=out_shape,
        scratch_shapes=[
            pltpu.VMEM((TM, D_MODEL), BF16),
            pltpu.VMEM((TM, TN), F32),
            pltpu.VMEM((TM, TN), F32),
            pltpu.VMEM((N_CBLK, 8, TN), F32),
        ],
        compiler_params=pltpu.CompilerParams(
            dimension_semantics=("arbitrary", "arbitrary"), vmem_limit_bytes=VMEM_LIMIT),
        name="proj",
    )(x, g, w, w_kiwi, c128, d128, c64, d64, conv_w, state)


def _select_kernel(qi_ref, kiwi_ref, kia_ref, kib_ref, bias_ref,
                   sc_s, wb_s, lg_s, lh_s, cand_s, thr_s, cnt_s, top_s, *,
                   tq, spb, nkb_total, topk, causal, n_valid, q_pos0):
    n = pl.program_id(0)
    npair = N_IDX_HEADS // 2
    blk = spb * LANES
    kiwi = kiwi_ref[...]
    wscale = (IDX_DIM ** -0.5) * (N_IDX_HEADS ** -0.5)
    for h in range(N_IDX_HEADS):
        wb_s[h] = jnp.broadcast_to(kiwi[:, IDX_DIM + h:IDX_DIM + h + 1], (tq, LANES)) * wscale
    q2 = jnp.concatenate([qi_ref[:, p * LANES:(p + 1) * LANES] for p in range(npair)], axis=0)

    if causal:
        qpos0 = n * tq
        nkb = (qpos0 + tq + blk - 1) // blk
    else:
        qpos0 = q_pos0
        nkb = nkb_total
    lane = lax.broadcasted_iota(jnp.int32, (tq, LANES), 1)
    qchunk = (qpos0 + lax.broadcasted_iota(jnp.int32, (tq, LANES), 0)) >> CHUNK_SHIFT
    nt = (((1,), (1,)), ((), ()))

    def logits_into(lg_ref, kb):
        base = pl.multiple_of(jnp.minimum(kb, nkb_total - 1) * blk, blk)
        lg_ref[0] = lax.dot_general(q2, kia_ref[pl.ds(base, blk), :], nt, preferred_element_type=F32)
        lg_ref[1] = lax.dot_general(q2, kib_ref[pl.ds(base, blk), :], nt, preferred_element_type=F32)

    def head_sum(lg_ref, kb, carry):
        m1, m2 = carry
        base = kb * blk
        for c in range(spb):
            cs = slice(c * LANES, (c + 1) * LANES)
            acc = jnp.zeros((tq, LANES), F32)
            for p in range(npair):
                rs = slice(p * tq, (p + 1) * tq)
                acc = acc + jnp.maximum(lg_ref[0, rs, cs], 0.0) * wb_s[2 * p]
                acc = acc + jnp.maximum(lg_ref[1, rs, cs], 0.0) * wb_s[2 * p + 1]
            col = base + c * LANES + lane
            adm = jnp.logical_and((col >> CHUNK_SHIFT) <= qchunk, col < n_valid)
            sc = jnp.where(adm, acc, NEG_INF)
            m2 = jnp.maximum(m2, jnp.minimum(m1, sc))
            m1 = jnp.maximum(m1, sc)
            sc_s[kb * spb + c] = sc
        return m1, m2

    logits_into(lg_s, 0)

    def score_pair(i, carry):
        logits_into(lh_s, 2 * i + 1)
        carry = head_sum(lg_s, 2 * i, carry)
        logits_into(lg_s, 2 * i + 2)
        return head_sum(lh_s, 2 * i + 1, carry)

    neg = jnp.full((tq, LANES), NEG_INF, F32)
    m1, m2 = lax.fori_loop(0, nkb // 2, score_pair, (neg, neg))
    top_s[0] = m1
    top_s[1] = m2

    @pl.when(nkb % 2 == 1)
    def _():
        last1, last2 = head_sum(lg_s, nkb - 1, (top_s[0], top_s[1]))
        top_s[0] = last1
        top_s[1] = last2

    m1, m2 = top_s[0], top_s[1]

    zeros = jnp.zeros((tq, LANES), F32)
    kf = float(topk)

    def count_all(thr_f, strict=False):
        def body(kb, acc):
            for c in range(spb):
                s = sc_s[kb * spb + c]
                acc = acc + jnp.where(s > thr_f if strict else s >= thr_f, 1.0, 0.0)
            return acc
        return jnp.sum(lax.fori_loop(0, nkb, body, zeros), axis=1, keepdims=True)

    def count_cand(thr_f):
        acc = zeros
        for i in range(CAND):
            acc = acc + jnp.where(cand_s[i] >= thr_f, 1.0, 0.0)
        return jnp.sum(acc, axis=1, keepdims=True)

    ones = jnp.ones((tq, LANES), jnp.int32)
    lo0 = _sortable(jnp.min(m2, axis=1, keepdims=True)) * ones
    hi0 = _sortable(jnp.max(m1, axis=1, keepdims=True)) * ones + 1

    def bisect(count):
        def unresolved(lo_k, hi_k, c_lo):
            open_ = jnp.logical_and(c_lo != kf, (hi_k - lo_k) != 1)
            return jnp.max(jnp.where(open_, 1.0, 0.0))

        def cond(st):
            return jnp.logical_and(st[0] <= KEY_BITS, st[-1] > 0.0)

        def body(st):
            it, lo_k, hi_k, c_lo, _ = st
            for _ in range(BISECT_UNROLL):
                mid = lo_k + lax.shift_right_logical(hi_k - lo_k, 1)
                cnt = count(_unsortable(mid))
                ge = cnt >= kf
                lo_k = jnp.where(ge, mid, lo_k)
                hi_k = jnp.where(ge, hi_k, mid)
                c_lo = jnp.where(ge, cnt, c_lo)
            return it + BISECT_UNROLL, lo_k, hi_k, c_lo, unresolved(lo_k, hi_k, c_lo)

        c0 = jnp.full((tq, LANES), -1.0, F32)
        st = lax.while_loop(cond, body, (jnp.int32(0), lo0, hi0, c0, unresolved(lo0, hi0, c0)))
        return st[1], st[3]

    nsl = nkb * spb
    nchunk = (nsl + CAND - 1) // CAND
    pad_tile = jnp.full((tq, LANES), PAD_SCORE, F32)

    def pad(sidx, carry):
        sc_s[sidx] = pad_tile
        return carry

    lax.fori_loop(nsl, nchunk * CAND, pad, 0)

    def gather_rows(rg, carry):
        r0 = pl.multiple_of(rg * 8, 8)

        def chunk(ch, cand):
            new = _sort_desc([sc_s[ch * CAND + i, pl.ds(r0, 8), :] for i in range(CAND)])
            return tuple(_bitonic_merge_desc([jnp.maximum(cand[i], new[CAND - 1 - i]) for i in range(CAND)]))

        start = tuple(jnp.full((8, LANES), PAD_SCORE, F32) for _ in range(CAND))
        cand = lax.fori_loop(0, nchunk, chunk, start)
        for i in range(CAND):
            cand_s[i, pl.ds(r0, 8), :] = cand[i]
        return carry

    lax.fori_loop(0, tq // 8, gather_rows, 0)

    thr_c, cnt_c = bisect(count_cand)
    thr_s[...] = thr_c
    cnt_s[...] = cnt_c
    last = jnp.max(cand_s[CAND - 1], axis=1, keepdims=True)
    covered = jnp.logical_or(last < _unsortable(thr_c), last <= 0.5 * NEG_INF)

    @pl.when(jnp.min(jnp.where(covered, 1.0, 0.0)) == 0.0)
    def _():
        thr_a, cnt_a = bisect(count_all)
        thr_s[...] = thr_a
        cnt_s[...] = cnt_a

    thr = _unsortable(thr_s[...])
    c_thr = cnt_s[...]

    tied = jnp.max(jnp.where(jnp.logical_and(c_thr != kf, thr > 0.5 * NEG_INF), 1.0, 0.0)) > 0.0

    @pl.when(jnp.logical_not(tied))
    def _():
        def emit(kb, carry):
            for c in range(spb):
                s = sc_s[kb * spb + c]
                sel = jnp.logical_and(s >= thr, s > 0.5 * NEG_INF)
                bias_ref[kb * spb + c] = jnp.where(sel, 0.0, NEG_INF).astype(BF16)
            return carry

        lax.fori_loop(0, nkb, emit, 0)

    @pl.when(tied)
    def _():
        need = kf - count_all(thr, strict=True)
        tri = (lax.broadcasted_iota(jnp.int32, (LANES, LANES), 0)
               <= lax.broadcasted_iota(jnp.int32, (LANES, LANES), 1)).astype(BF16)

        def emit(kb, seen):
            for c in range(spb):
                s = sc_s[kb * spb + c]
                eq = jnp.where(s == thr, 1.0, 0.0)
                rank = seen + jnp.dot(eq.astype(BF16), tri, preferred_element_type=F32)
                keep = jnp.logical_or(s > thr, jnp.logical_and(s == thr, rank <= need))
                sel = jnp.logical_and(keep, s > 0.5 * NEG_INF)
                bias_ref[kb * spb + c] = jnp.where(sel, 0.0, NEG_INF).astype(BF16)
                seen = seen + jnp.sum(eq, axis=1, keepdims=True)
            return seen

        lax.fori_loop(0, nkb, emit, zeros)

    def fill(kb, carry):
        for c in range(spb):
            bias_ref[kb * spb + c] = jnp.full((tq, LANES), NEG_INF, BF16)
        return carry

    lax.fori_loop(nkb, nkb_total, fill, 0)


def _select(qi, kiwi, kia, kib, *, tq, spb, topk, causal, n_valid, q_pos0):
    t = qi.shape[0]
    nb = t // tq
    lk = kia.shape[1]
    nslab = lk // LANES
    nkb_total = nslab // spb
    assert topk <= 2 * LANES and nslab >= 2, "the bisection's starting lower bound needs two keys per lane"

    def kmap(n):
        return (0 if causal else n, 0, 0)

    return pl.pallas_call(
        functools.partial(_select_kernel, tq=tq, spb=spb, nkb_total=nkb_total, topk=topk,
                          causal=causal, n_valid=n_valid, q_pos0=q_pos0),
        grid=(nb,),
        in_specs=[
            pl.BlockSpec((tq, IQ_W), lambda n: (n, 0)),
            pl.BlockSpec((tq, LANES), lambda n: (n, 0)),
            pl.BlockSpec((None, lk, LANES), kmap),
            pl.BlockSpec((None, lk, LANES), kmap),
        ],
        out_specs=pl.BlockSpec((nslab, tq, LANES), lambda n: (0, n, 0)),
        out_shape=jax.ShapeDtypeStruct((nslab, t, LANES), BF16),
        scratch_shapes=[
            pltpu.VMEM((nslab + CAND, tq, LANES), F32),
            pltpu.VMEM((N_IDX_HEADS, tq, LANES), F32),
            pltpu.VMEM((2, N_IDX_HEADS // 2 * tq, spb * LANES), F32),
            pltpu.VMEM((2, N_IDX_HEADS // 2 * tq, spb * LANES), F32),
            pltpu.VMEM((CAND, tq, LANES), F32),
            pltpu.VMEM((tq, LANES), jnp.int32),
            pltpu.VMEM((tq, LANES), F32),
            pltpu.VMEM((2, tq, LANES), F32),
        ],
        compiler_params=pltpu.CompilerParams(
            dimension_semantics=("arbitrary",), vmem_limit_bytes=VMEM_LIMIT),
        name="select",
    )(qi, kiwi, kia, kib)


FIXED_REF_LIMIT = 2.0 ** 40

def _attend_kernel(qb_ref, kb_ref, kbat_ref, last_ref, q_ref, k_ref, v_ref, b_ref, o_ref,
                   m_s, l_s, acc_s, par_s, exact_s, *, tq, spb):
    s = pl.program_id(0)
    first = kb_ref[s] == 0

    biases = [b_ref[c].astype(F32)[None] for c in range(spb)]
    ones = jnp.ones((spb * LANES, LANES), BF16)

    def masked_scores(g):
        qg = [q_ref[:, (g * GROUP + h) * HEAD_DIM:(g * GROUP + h + 1) * HEAD_DIM] for h in range(GROUP)]
        q4 = jnp.concatenate(qg, axis=0)
        kg = k_ref[:, g * HEAD_DIM:(g + 1) * HEAD_DIM]
        sc = lax.dot_general(q4, kg, (((1,), (1,)), ((), ())), preferred_element_type=F32)
        return [(sc[:, c * LANES:(c + 1) * LANES].reshape(GROUP, tq, LANES) + biases[c]).reshape(GROUP * tq, LANES)
                for c in range(spb)]

    def weighted_values(g, m_ref, slabs):
        pmat = jnp.concatenate([jnp.exp2(sl - m_ref).astype(BF16) for sl in slabs], axis=1)
        v1 = jnp.concatenate([v_ref[:, g * HEAD_DIM:(g + 1) * HEAD_DIM], ones], axis=1)
        pv = jnp.dot(pmat, v1, preferred_element_type=F32)
        return pv[:, 0:HEAD_DIM], pv[:, HEAD_DIM:HEAD_DIM + LANES]

    @pl.when(first)
    def _():
        par_s[0] = 0
        exact_s[0] = 1
        m_s[...] = jnp.full(m_s.shape, 0.1 * NEG_INF, F32)
        l_s[0] = jnp.zeros(l_s.shape[1:], F32)
        acc_s[0] = jnp.zeros(acc_s.shape[1:], F32)

    @pl.when(jnp.logical_not(first))
    def _():
        par = par_s[0]
        worst = jnp.zeros((GROUP * tq, LANES), F32)
        for g in range(N_KV_HEADS):
            pv, psum = weighted_values(g, m_s[g], masked_scores(g))
            acc_new = acc_s[par, g] + pv
            l_new = l_s[par, g] + psum
            acc_s[1 - par, g] = acc_new
            l_s[1 - par, g] = l_new
            worst = worst + l_new + jnp.abs(acc_new)
        ok = jnp.min(jnp.where(worst < FIXED_REF_LIMIT, 1.0, 0.0)) > 0.5
        exact_s[0] = jnp.where(ok, 0, 1)
        par_s[0] = jnp.where(ok, 1 - par, par)

    @pl.when(exact_s[0] == 1)
    def _():
        par = par_s[0]
        for g in range(N_KV_HEADS):
            slabs = masked_scores(g)
            mx = slabs[0]
            for sl in slabs[1:]:
                mx = jnp.maximum(mx, sl)
            m_prev = m_s[g]
            m_new = jnp.maximum(m_prev, jnp.max(mx, axis=1, keepdims=True))
            alpha = jnp.exp2(m_prev - m_new)
            pv, psum = weighted_values(g, m_new, slabs)
            acc_s[par, g] = alpha * acc_s[par, g] + pv
            l_s[par, g] = alpha * l_s[par, g] + psum
            m_s[g] = m_new

    @pl.when(last_ref[s] == 1)
    def _():
        par = par_s[0]
        for g in range(N_KV_HEADS):
            o = acc_s[par, g] / l_s[par, g]
            for h in range(GROUP):
                col = (g * GROUP + h) * HEAD_DIM
                o_ref[:, col:col + HEAD_DIM] = o[h * tq:(h + 1) * tq].astype(BF16)


def _attend(q, k_all, v_all, bias, sched, *, tq, spb):
    t = q.shape[0]
    qb, kb, kbat, last = sched
    nsteps = qb.shape[0]
    blk = spb * LANES
    grid_spec = pltpu.PrefetchScalarGridSpec(
        num_scalar_prefetch=4,
        grid=(nsteps,),
        in_specs=[
            pl.BlockSpec((tq, Q_W), lambda s, qb, kb, kbat, last: (qb[s], 0)),
            pl.BlockSpec((None, blk, KV_W), lambda s, qb, kb, kbat, last: (kbat[s], kb[s], 0)),
            pl.BlockSpec((None, blk, KV_W), lambda s, qb, kb, kbat, last: (kbat[s], kb[s], 0)),
            pl.BlockSpec((spb, tq, LANES), lambda s, qb, kb, kbat, last: (kb[s], qb[s], 0)),
        ],
        out_specs=pl.BlockSpec((tq, Q_W), lambda s, qb, kb, kbat, last: (qb[s], 0)),
        scratch_shapes=[
            pltpu.VMEM((N_KV_HEADS, GROUP * tq, LANES), F32),
            pltpu.VMEM((2, N_KV_HEADS, GROUP * tq, LANES), F32),
            pltpu.VMEM((2, N_KV_HEADS, GROUP * tq, HEAD_DIM), F32),
            pltpu.SMEM((1,), jnp.int32),
            pltpu.SMEM((1,), jnp.int32),
        ],
    )
    return pl.pallas_call(
        functools.partial(_attend_kernel, tq=tq, spb=spb),
        grid_spec=grid_spec,
        out_shape=jax.ShapeDtypeStruct((t, Q_W), BF16),
        compiler_params=pltpu.CompilerParams(
            dimension_semantics=("arbitrary",), vmem_limit_bytes=VMEM_LIMIT),
        name="attend",
    )(qb, kb, kbat, last, q, k_all, v_all, bias)


def _causal_schedule(t, tq, blk):
    qb, kb, last = [], [], []
    for n in range(t // tq):
        nk = ((n + 1) * tq + blk - 1) // blk
        for k in range(nk):
            qb.append(n)
            kb.append(k)
            last.append(1 if k == nk - 1 else 0)
    z = np.zeros(len(qb), np.int32)
    return (jnp.asarray(qb, jnp.int32), jnp.asarray(kb, jnp.int32), jnp.asarray(z), jnp.asarray(last, jnp.int32))


def _batched_schedule(nbatch, nk):
    qb = np.repeat(np.arange(nbatch, dtype=np.int32), nk)
    kb = np.tile(np.arange(nk, dtype=np.int32), nbatch)
    last = (kb == nk - 1).astype(np.int32)
    return (jnp.asarray(qb), jnp.asarray(kb), jnp.asarray(qb), jnp.asarray(last))


def _merge_kernel(o_ref, yc_ref, ga_ref, gc_ref, woa_ref, woc_ref, wout_ref, x_ref, gq_ref, gp_ref,
                  x1_ref, xn2_ref, mg_s, m_s):
    j = pl.program_id(1)
    nblk = D_MODEL // MERGE_TN

    @pl.when(j < nblk)
    def _():
        a = jnp.dot(o_ref[...], woa_ref[...], preferred_element_type=F32)
        c = jnp.dot(yc_ref[...], woc_ref[...], preferred_element_type=F32)
        mg_s[j] = (ga_ref[...] * a + gc_ref[...] * c).astype(BF16)

    @pl.when(j >= nblk)
    def _():
        mg = jnp.concatenate([mg_s[b] for b in range(nblk)], axis=1)
        m_s[j - nblk] = jnp.dot(mg, wout_ref[...], preferred_element_type=F32)

    @pl.when(j == 2 * nblk - 1)
    def _():
        m = jnp.concatenate([m_s[b] for b in range(nblk)], axis=1)
        x1 = x_ref[...] + _rms(m, gq_ref[...])
        x1_ref[...] = x1
        xn2_ref[...] = _rms(x1, gp_ref[...]).astype(BF16)


def _merge(o, yc, ga, gc, woa, woc, wout, x, gq, gp):
    t = x.shape[0]
    tn = MERGE_TN
    nblk = D_MODEL // tn

    def row(i, j):
        return (i, 0)

    def lo(i, j):
        return (i, jnp.minimum(j, nblk - 1))

    return pl.pallas_call(
        _merge_kernel,
        grid=(t // TM, 2 * nblk),
        in_specs=[
            pl.BlockSpec((TM, Q_W), row),
            pl.BlockSpec((TM, D_MODEL), row),
            pl.BlockSpec((TM, tn), lo),
            pl.BlockSpec((TM, tn), lo),
            pl.BlockSpec((Q_W, tn), lambda i, j: (0, jnp.minimum(j, nblk - 1))),
            pl.BlockSpec((D_MODEL, tn), lambda i, j: (0, jnp.minimum(j, nblk - 1))),
            pl.BlockSpec((D_MODEL, tn), lambda i, j: (0, jnp.maximum(j - nblk, 0))),
            pl.BlockSpec((TM, D_MODEL), row),
            pl.BlockSpec((1, D_MODEL), lambda i, j: (0, 0)),
            pl.BlockSpec((1, D_MODEL), lambda i, j: (0, 0)),
        ],
        out_specs=[pl.BlockSpec((TM, D_MODEL), row), pl.BlockSpec((TM, D_MODEL), row)],
        out_shape=[jax.ShapeDtypeStruct((t, D_MODEL), F32), jax.ShapeDtypeStruct((t, D_MODEL), BF16)],
        scratch_shapes=[pltpu.VMEM((nblk, TM, tn), BF16), pltpu.VMEM((nblk, TM, tn), F32)],
        compiler_params=pltpu.CompilerParams(
            dimension_semantics=("arbitrary", "arbitrary"), vmem_limit_bytes=VMEM_LIMIT),
        name="merge",
    )(o, yc, ga, gc, woa, woc, wout, x, gq, gp)


def _ffn_kernel(xn_ref, wg_ref, wv_ref, cwg_ref, cwv_ref, wd_ref, x1_ref, gq_ref, stg_ref, stv_ref,
                y_ref, ng_ref, nv_ref, acc_s, cg_s, cv_s, *, seg, carried):
    i = pl.program_id(0)
    jf = pl.program_id(1)
    nseg = TM // seg

    @pl.when(jf == 0)
    def _():
        acc_s[...] = jnp.zeros(acc_s.shape, F32)

    if carried:
        @pl.when(i == 0)
        def _():
            for carry_s, st_ref in ((cg_s, stg_ref), (cv_s, stv_ref)):
                carry_s[jf] = jnp.zeros((8, TF), F32)
                carry_s[jf, 0:2, :] = st_ref[0]

    xn = xn_ref[...]

    def branch(w_ref, cw_ref, st_ref, carry_s, new_ref):
        up = jnp.dot(xn, w_ref[...], preferred_element_type=F32)
        if carried:
            prevs = [carry_s[jf, 0:2, :]]
        else:
            prevs = [st_ref[s] for s in range(nseg)]
        y, news = _conv_tile(up, cw_ref[...], prevs, seg)
        for s in range(nseg):
            new_ref[s] = news[s]
        if carried:
            carry_s[jf, 0:2, :] = news[0]
        return y

    gate = branch(wg_ref, cwg_ref, stg_ref, cg_s, ng_ref)
    val = branch(wv_ref, cwv_ref, stv_ref, cv_s, nv_ref)
    c0 = np.float32(np.sqrt(2.0 / np.pi))
    gelu = 0.5 * gate * (1.0 + jnp.tanh(c0 * (gate + 0.044715 * (gate * gate * gate))))
    hid = (gelu * val).astype(BF16)
    acc_s[...] += jnp.dot(hid, wd_ref[...], preferred_element_type=F32)

    @pl.when(jf == pl.num_programs(1) - 1)
    def _():
        y_ref[...] = x1_ref[...] + _rms(acc_s[...], gq_ref[...])


def _ffn(xn2, w_up, cw, w_down, x1, gq, state, seg):
    t = x1.shape[0]
    nf = D_FF // TF
    carried = seg == t
    nseg_tile = 1 if carried else TM // seg

    def row(i, j):
        return (i, 0)

    def stg(i, j):
        return (0 if carried else i, 0, j)

    def stv(i, j):
        return (0 if carried else i, 0, j + nf)

    return pl.pallas_call(
        functools.partial(_ffn_kernel, seg=min(seg, TM), carried=carried),
        grid=(t // TM, nf),
        in_specs=[
            pl.BlockSpec((TM, D_MODEL), row),
            pl.BlockSpec((D_MODEL, TF), lambda i, j: (0, j)),
            pl.BlockSpec((D_MODEL, TF), lambda i, j: (0, j + nf)),
            pl.BlockSpec((3, TF), lambda i, j: (0, j)),
            pl.BlockSpec((3, TF), lambda i, j: (0, j + nf)),
            pl.BlockSpec((TF, D_MODEL), lambda i, j: (j, 0)),
            pl.BlockSpec((TM, D_MODEL), row),
            pl.BlockSpec((1, D_MODEL), lambda i, j: (0, 0)),
            pl.BlockSpec((nseg_tile, 2, TF), stg),
            pl.BlockSpec((nseg_tile, 2, TF), stv),
        ],
        out_specs=[
            pl.BlockSpec((TM, D_MODEL), row),
            pl.BlockSpec((nseg_tile, 2, TF), lambda i, j: (i, 0, j)),
            pl.BlockSpec((nseg_tile, 2, TF), lambda i, j: (i, 0, j)),
        ],
        out_shape=[
            jax.ShapeDtypeStruct((t, D_MODEL), F32),
            jax.ShapeDtypeStruct((t // TM * nseg_tile, 2, D_FF), F32),
            jax.ShapeDtypeStruct((t // TM * nseg_tile, 2, D_FF), F32),
        ],
        scratch_shapes=[
            pltpu.VMEM((TM, D_MODEL), F32),
            pltpu.VMEM((nf, 8, TF), F32),
            pltpu.VMEM((nf, 8, TF), F32),
        ],
        compiler_params=pltpu.CompilerParams(
            dimension_semantics=("arbitrary", "arbitrary"), vmem_limit_bytes=VMEM_LIMIT),
        name="ffn",
    )(xn2, w_up, w_up, cw, cw, w_down, x1, gq, state, state)


PACK_ROWS = 1024


def _pack_kernel(ck_ref, cv_ref, ko_ref, vo_ref, *, rows):
    for src, dst in ((ck_ref, ko_ref), (cv_ref, vo_ref)):
        for g in range(N_KV_HEADS):
            dst[:, g * HEAD_DIM:(g + 1) * HEAD_DIM] = src[pl.ds(g, rows, stride=N_KV_HEADS), :].astype(BF16)


def _tail_kernel(kn_ref, vn_ref, ki_ref, vi_ref, ko_ref, vo_ref, *, seg):
    del ki_ref, vi_ref
    for src, dst in ((kn_ref, ko_ref), (vn_ref, vo_ref)):
        dst[0:seg, :] = src[...]
        if dst.shape[0] > seg:
            dst[seg:, :] = jnp.zeros((dst.shape[0] - seg, KV_W), BF16)


def _pack_cache(cache_k, cache_v, k_new, v_new, seg, lk):
    nb, plen = cache_k.shape[0], cache_k.shape[1]
    rows = int(np.gcd(plen, PACK_ROWS))
    tail = lk - plen
    assert rows % 16 == 0 and tail % 16 == 0 and plen % tail == 0 and tail >= seg
    cspec = pl.BlockSpec((None, rows * N_KV_HEADS, HEAD_DIM), lambda b, r: (b, r, 0))
    ospec = pl.BlockSpec((None, rows, KV_W), lambda b, r: (b, r, 0))
    flat = (nb, plen * N_KV_HEADS, HEAD_DIM)
    slab = jax.ShapeDtypeStruct((nb, lk, KV_W), BF16)
    k_all, v_all = pl.pallas_call(
        functools.partial(_pack_kernel, rows=rows),
        grid=(nb, plen // rows),
        in_specs=[cspec, cspec],
        out_specs=[ospec, ospec],
        out_shape=[slab, slab],
        compiler_params=pltpu.CompilerParams(
            dimension_semantics=("arbitrary", "arbitrary"), vmem_limit_bytes=VMEM_LIMIT),
        name="pack_cache",
    )(cache_k.reshape(flat), cache_v.reshape(flat))
    nspec = pl.BlockSpec((seg, KV_W), lambda b: (b, 0))
    anyspec = pl.BlockSpec(memory_space=pl.ANY)
    tspec = pl.BlockSpec((None, tail, KV_W), lambda b: (b, plen // tail, 0))
    return pl.pallas_call(
        functools.partial(_tail_kernel, seg=seg),
        grid=(nb,),
        in_specs=[nspec, nspec, anyspec, anyspec],
        out_specs=[tspec, tspec],
        out_shape=[slab, slab],
        input_output_aliases={2: 0, 3: 1},
        compiler_params=pltpu.CompilerParams(dimension_semantics=("arbitrary",), vmem_limit_bytes=VMEM_LIMIT),
        name="pack_tail",
    )(k_new, v_new, k_all, v_all)


def _rope_tables(pos, rot, width):
    half = rot // 2
    freqs = ROPE_THETA ** (-jnp.arange(half, dtype=F32) / half)
    ang = pos.astype(F32)[:, None] * freqs[None, :]
    cos, sin = jnp.cos(ang), jnp.sin(ang)
    t = pos.shape[0]
    c = jnp.concatenate([cos, cos, jnp.ones((t, width - rot), F32)], axis=1)
    d = jnp.concatenate([-sin, sin, jnp.zeros((t, width - rot), F32)], axis=1)
    reps = LANES // width
    return jnp.tile(c, (1, reps)), jnp.tile(d, (1, reps))


def _relayout_w_in(w):
    o = np.cumsum([0, Q_W, KV_W, KV_W, IQ_W, IDX_DIM, N_IDX_HEADS, D_MODEL, D_MODEL, D_MODEL, D_MODEL, D_MODEL])
    q, k, v, qi = (w[:, o[a]:o[a + 1]] for a in range(4))
    kiwi = w[:, o[4]:o[6]]
    cb, cc, ch, ga, gc = (w[:, o[a]:o[a + 1]] for a in range(6, 11))
    parts = [q, k, v, qi]
    for c in range(N_CBLK):
        sl = slice(c * TN, (c + 1) * TN)
        parts += [cb[:, sl], cc[:, sl], ch[:, sl]]
    parts += [ga, gc]
    w_kiwi = jnp.pad(kiwi, ((0, 0), (0, LANES - kiwi.shape[1]))).astype(BF16)
    return jnp.concatenate([p.astype(BF16) for p in parts], axis=1), w_kiwi


def _stream(x, pos, seg, past, weights, *, tq, tq_att, spb_sel, spb_att):
    (g_mp, g_mq, w_in_r, conv_w, woa, woc, wout, g_fp, g_fq, w_up, fconv_w, w_down) = weights
    t = x.shape[0]
    nseq = t // seg
    tabs = _rope_tables(pos, ROT_DIM, HEAD_DIM) + _rope_tables(pos, IDX_ROT_DIM, IDX_DIM)
    if past is None:
        conv_state = jnp.zeros((nseq, 2, D_MODEL), F32)
        ffn_state = jnp.zeros((nseq, 2, 2 * D_FF), F32)
    else:
        conv_state, ffn_state = past[3], past[4]

    (q, k, kb, v, vb, qi, kiwi, kia, kib, yc, conv_new, ga, gc) = _proj(
        x, g_mp, w_in_r, tabs, conv_w, conv_state, seg)

    if past is None:
        k_all, v_all = kb[None], vb[None]
        kia_all, kib_all = kia[None], kib[None]
        n_keys = t
        sched = _causal_schedule(t, tq_att, spb_att * LANES)
        q_pos0 = 0
    else:
        cache_k, cache_v, cache_ki = past[0], past[1], past[2]
        plen = cache_k.shape[1]
        n_keys = plen + seg
        lk = -(-n_keys // (spb_att * LANES)) * (spb_att * LANES)
        pad = lk - n_keys

        def cat(c, new, width):
            parts = [c, new.reshape(nseq, seg, width)]
            if pad:
                parts.append(jnp.zeros((nseq, pad, width), BF16))
            return jnp.concatenate(parts, axis=1)

        k_all, v_all = _pack_cache(cache_k, cache_v, kb, vb, seg, lk)
        cki = cache_ki.astype(BF16)
        zk = jnp.zeros_like(cki)
        kia_all = cat(jnp.concatenate([cki, zk], axis=-1), kia, LANES)
        kib_all = cat(jnp.concatenate([zk, cki], axis=-1), kib, LANES)
        sched = _batched_schedule(nseq, lk // (spb_att * LANES))
        q_pos0 = plen
    topk = min(TOPK_MAX, n_keys // 4)

    bias = _select(qi, kiwi, kia_all, kib_all, tq=tq, spb=spb_sel, topk=topk,
                   causal=past is None, n_valid=n_keys, q_pos0=q_pos0)
    o = _attend(q, k_all, v_all, bias, sched, tq=tq_att, spb=spb_att)
    x1, xn2 = _merge(o, yc, ga, gc, woa, woc, wout, x, g_mq, g_fp)
    y, ffn_g, ffn_v = _ffn(xn2, w_up, fconv_w, w_down, x1, g_fq, ffn_state, seg)
    ffn_new = jnp.concatenate([ffn_g, ffn_v], axis=-1)
    return y, k, v, kiwi[:, :IDX_DIM], conv_new[-nseq:], ffn_new[-nseq:]


def kernel(x_prompt, x_sample, cache_k, cache_v, cache_k_idx, state_conv, state_ffn_conv, norm_mix_pre, norm_mix_post, w_in, conv_w, w_o_attn, w_o_conv, w_out, norm_ffn_pre, norm_ffn_post, w_ffn_up, ffn_conv_w, w_ffn_down):
    depth = w_in.shape[0]
    assert depth == 1, "single-layer step"
    b, seq, _ = x_prompt.shape
    assert b == 1
    db, dseq, _ = x_sample.shape
    plen = cache_k.shape[2]
    assert dseq == CHUNK and plen % CHUNK == 0

    weights = (
        norm_mix_pre, norm_mix_post, _relayout_w_in(w_in[0]), conv_w[0],
        w_o_attn[0].astype(BF16), w_o_conv[0].astype(BF16), w_out[0].astype(BF16),
        norm_ffn_pre, norm_ffn_post, w_ffn_up[0].astype(BF16), ffn_conv_w[0], w_ffn_down[0].astype(BF16),
    )

    pos_p = jnp.arange(seq, dtype=jnp.int32)
    yp, kp, vp, kip, convp, ffnp = _stream(
        x_prompt.reshape(seq, D_MODEL), pos_p, seq, None, weights, tq=128, tq_att=512, spb_sel=4, spb_att=8)

    pos_s = jnp.tile(jnp.arange(dseq, dtype=jnp.int32) + plen, db)
    past = (cache_k[0], cache_v[0], cache_k_idx[0], state_conv[0], state_ffn_conv[0])
    n_keys = plen + dseq
    spb_s = _sample_slabs(n_keys)
    ys, ks, vs, kis, convs, ffns = _stream(
        x_sample.reshape(db * dseq, D_MODEL), pos_s, dseq, past, weights,
        tq=dseq, tq_att=dseq, spb_sel=spb_s, spb_att=spb_s)

    return (
        yp.reshape(1, seq, D_MODEL), ys.reshape(db, dseq, D_MODEL),
        kp.reshape(1, 1, seq, N_KV_HEADS, HEAD_DIM), vp.reshape(1, 1, seq, N_KV_HEADS, HEAD_DIM),
        kip.reshape(1, 1, seq, IDX_DIM), convp.reshape(1, 1, 2, D_MODEL), ffnp.reshape(1, 1, 2, 2 * D_FF),
        ks.reshape(1, db, dseq, N_KV_HEADS, HEAD_DIM), vs.reshape(1, db, dseq, N_KV_HEADS, HEAD_DIM),
        kis.reshape(1, db, dseq, IDX_DIM), convs.reshape(1, db, 2, D_MODEL), ffns.reshape(1, db, 2, 2 * D_FF),
    )


SAMPLE_MAX_SLABS = 11


def _sample_slabs(n_keys):
    nslab = -(-n_keys // LANES)
    return max(d for d in range(1, SAMPLE_MAX_SLABS + 1) if nslab % d == 0)
```

```python
import functools

import jax
import jax.numpy as jnp
import numpy as np
from jax import lax
from jax.experimental import pallas as pl
from jax.experimental.pallas import tpu as pltpu

F32 = jnp.float32
BF16 = jnp.bfloat16

D_MODEL = 2048
N_HEADS = 16
N_KV_HEADS = 4
HEAD_DIM = 128
ROT_DIM = HEAD_DIM // 4
N_IDX_HEADS = 16
IDX_DIM = 64
IDX_ROT_DIM = IDX_DIM // 4
CHUNK = 64
TOPK_MAX = 256
ROPE_THETA = 500000.0
D_FF = 5632
RMS_EPS = 1e-6
NEG_INF = -1e30
Q_W = N_HEADS * HEAD_DIM
KV_W = N_KV_HEADS * HEAD_DIM
IQ_W = N_IDX_HEADS * IDX_DIM
GROUP = N_HEADS // N_KV_HEADS

LANES = 128
TM = 512
TN = 1024
MERGE_TN = 512
TF = 512
VMEM_LIMIT = 56 * 1024 * 1024

J_Q = 0
J_KV = J_Q + Q_W // TN
assert 2 * KV_W == TN
J_QI = J_KV + 1
J_CONV = J_QI + IQ_W // TN
N_CBLK = D_MODEL // TN
J_GA = J_CONV + 3 * N_CBLK
J_GC = J_GA + N_CBLK
NJ = J_GC + N_CBLK

Q_PRESCALE = float(np.float32((HEAD_DIM ** -0.5) * np.log2(np.e)))


def _rms(x, g):
    return x * lax.rsqrt(jnp.mean(x * x, axis=-1, keepdims=True) + RMS_EPS) * g


def _sortable(x):
    bits = lax.bitcast_convert_type(x, jnp.int32)
    return jnp.where(bits < 0, bits ^ jnp.int32(0x7FFFFFFF), bits)


def _unsortable(k):
    return lax.bitcast_convert_type(jnp.where(k < 0, k ^ jnp.int32(0x7FFFFFFF), k), F32)


CHUNK_SHIFT = CHUNK.bit_length() - 1
assert 1 << CHUNK_SHIFT == CHUNK
KEY_BITS = 32
BISECT_UNROLL = 4
CAND = 16
PAD_SCORE = -3.0e38


def _oddeven_merge_sort_pairs(n):
    out, p = [], 1
    while p < n:
        k = p
        while k >= 1:
            for j in range(k % p, n - k, 2 * k):
                for i in range(min(k, n - j - k)):
                    if (i + j) // (2 * p) == (i + j + k) // (2 * p):
                        out.append((i + j, i + j + k))
            k //= 2
        p *= 2
    return out


_SORT_PAIRS = _oddeven_merge_sort_pairs(CAND)


def _sort_desc(v):
    v = list(v)
    for i, j in _SORT_PAIRS:
        v[i], v[j] = jnp.maximum(v[i], v[j]), jnp.minimum(v[i], v[j])
    return v


def _bitonic_merge_desc(v):
    v = list(v)
    d = len(v) // 2
    while d >= 1:
        for i in range(len(v)):
            if not i & d:
                v[i], v[i + d] = jnp.maximum(v[i], v[i + d]), jnp.minimum(v[i], v[i + d])
        d //= 2
    return v


def _dwconv_seg(u, prev2, w3):
    row = lax.broadcasted_iota(jnp.int32, u.shape, 0)
    p0, p1 = prev2[0:1], prev2[1:2]
    s1 = jnp.where(row == 0, p1, pltpu.roll(u, 1, 0))
    s2 = jnp.where(row == 0, p0, jnp.where(row == 1, p1, pltpu.roll(u, 2, 0)))
    return w3[0:1] * s2 + w3[1:2] * s1 + w3[2:3] * u


def _conv_tile(u, w3, prevs, seg):
    nseg = len(prevs)
    ys, news = [], []
    for s in range(nseg):
        us = u[s * seg:(s + 1) * seg]
        ys.append(_dwconv_seg(us, prevs[s], w3))
        news.append(us[seg - 2:seg])
    y = ys[0] if nseg == 1 else jnp.concatenate(ys, axis=0)
    return y, news


def _proj_kernel(x_ref, g_ref, w_ref, wk_ref, c128_ref, d128_ref, c64_ref, d64_ref, cw_ref, st_ref,
                 q_ref, k_ref, kb_ref, v_ref, vb_ref, qi_ref, kiwi_ref, kia_ref, kib_ref,
                 yc_ref, cn_ref, ga_ref, gc_ref,
                 xn_s, cb_s, cc_s, carry_s, *, seg, carried):
    i = pl.program_id(0)
    j = pl.program_id(1)

    @pl.when(j == 0)
    def _():
        xn_s[...] = _rms(x_ref[...], g_ref[...]).astype(BF16)

    def mm():
        return jnp.dot(xn_s[...], w_ref[...], preferred_element_type=F32)

    lane = lax.broadcasted_iota(jnp.int32, (TM, LANES), 1)

    def rope128(xh):
        partner = jnp.where(lane < ROT_DIM // 2, pltpu.roll(xh, LANES - ROT_DIM // 2, 1),
                            pltpu.roll(xh, ROT_DIM // 2, 1))
        return xh * c128_ref[...] + partner * d128_ref[...]

    def rope64(xh, c, d):
        first = (lane & (IDX_DIM - 1)) < IDX_ROT_DIM // 2
        partner = jnp.where(first, pltpu.roll(xh, LANES - IDX_ROT_DIM // 2, 1),
                            pltpu.roll(xh, IDX_ROT_DIM // 2, 1))
        return xh * c + partner * d

    @pl.when(j < J_KV)
    def _():
        acc = mm()
        for h in range(TN // LANES):
            sl = slice(h * LANES, (h + 1) * LANES)
            q_ref[:, sl] = (rope128(acc[:, sl]) * Q_PRESCALE).astype(BF16)

    @pl.when(j == J_KV)
    def _():
        acc = mm()
        for h in range(N_KV_HEADS):
            sl = slice(h * HEAD_DIM, (h + 1) * HEAD_DIM)
            r = rope128(acc[:, sl])
            k_ref[pl.ds(h, TM, stride=N_KV_HEADS), :] = r
            kb_ref[:, sl] = r.astype(BF16)
            v = acc[:, KV_W + h * HEAD_DIM:KV_W + (h + 1) * HEAD_DIM]
            v_ref[pl.ds(h, TM, stride=N_KV_HEADS), :] = v
            vb_ref[:, sl] = v.astype(BF16)

    @pl.when(jnp.logical_and(j >= J_QI, j < J_CONV))
    def _():
        acc = mm()
        for h in range(TN // LANES):
            sl = slice(h * LANES, (h + 1) * LANES)
            qi_ref[:, sl] = rope64(acc[:, sl], c64_ref[...], d64_ref[...]).astype(BF16)

    @pl.when(j == J_QI)
    def _():
        is_ki = lane < IDX_DIM
        c = jnp.where(is_ki, c64_ref[...], 1.0)
        d = jnp.where(is_ki, d64_ref[...], 0.0)
        acc = jnp.dot(xn_s[...], wk_ref[...], preferred_element_type=F32)
        r = rope64(acc, c, d)
        kiwi_ref[...] = r
        ka = jnp.where(is_ki, r, 0.0)
        kia_ref[...] = ka.astype(BF16)
        kib_ref[...] = pltpu.roll(ka, IDX_DIM, 1).astype(BF16)

    jc = jnp.maximum(j - J_CONV, 0)
    in_conv = jnp.logical_and(j >= J_CONV, j < J_GA)
    cblk = jnp.minimum(jc // 3, N_CBLK - 1)
    part = jc % 3

    @pl.when(jnp.logical_and(in_conv, part == 0))
    def _():
        cb_s[...] = mm()

    @pl.when(jnp.logical_and(in_conv, part == 1))
    def _():
        cc_s[...] = mm()

    @pl.when(jnp.logical_and(in_conv, part == 2))
    def _():
        nseg = TM // seg
        if carried:
            @pl.when(i == 0)
            def _():
                carry_s[cblk] = jnp.zeros((8, TN), F32)
                carry_s[cblk, 0:2, :] = st_ref[0]
            prevs = [carry_s[cblk, 0:2, :]]
        else:
            prevs = [st_ref[s] for s in range(nseg)]
        u = cc_s[...] * mm()
        y, news = _conv_tile(u, cw_ref[...], prevs, seg)
        yc_ref[...] = (cb_s[...] * y).astype(BF16)
        for s in range(nseg):
            cn_ref[s] = news[s]
        if carried:
            carry_s[cblk, 0:2, :] = news[0]

    @pl.when(jnp.logical_and(j >= J_GA, j < J_GC))
    def _():
        ga_ref[...] = 1.0 / (1.0 + jnp.exp(-mm()))

    @pl.when(j >= J_GC)
    def _():
        gc_ref[...] = 1.0 / (1.0 + jnp.exp(-mm()))


def _proj(x, g, w_pair, tabs, conv_w, state, seg):
    w, w_kiwi = w_pair
    t = x.shape[0]
    ni = t // TM
    carried = seg == t
    nseg_tile = 1 if carried else TM // seg
    c128, d128, c64, d64 = tabs

    def row(i, j):
        return (i, 0)

    def const(i, j):
        return (0, 0)

    def cblk(j):
        return jnp.minimum(jnp.maximum(j - J_CONV, 0) // 3, N_CBLK - 1)

    def st_map(i, j):
        return (0 if carried else i, 0, cblk(j))

    in_specs = [
        pl.BlockSpec((TM, D_MODEL), row),
        pl.BlockSpec((1, D_MODEL), const),
        pl.BlockSpec((D_MODEL, TN), lambda i, j: (0, j)),
        pl.BlockSpec((D_MODEL, LANES), const),
        pl.BlockSpec((TM, LANES), row),
        pl.BlockSpec((TM, LANES), row),
        pl.BlockSpec((TM, LANES), row),
        pl.BlockSpec((TM, LANES), row),
        pl.BlockSpec((3, TN), lambda i, j: (0, cblk(j))),
        pl.BlockSpec((nseg_tile, 2, TN), st_map),
    ]
    out_shape = [
        jax.ShapeDtypeStruct((t, Q_W), BF16),
        jax.ShapeDtypeStruct((t * N_KV_HEADS, HEAD_DIM), F32),
        jax.ShapeDtypeStruct((t, KV_W), BF16),
        jax.ShapeDtypeStruct((t * N_KV_HEADS, HEAD_DIM), F32),
        jax.ShapeDtypeStruct((t, KV_W), BF16),
        jax.ShapeDtypeStruct((t, IQ_W), BF16),
        jax.ShapeDtypeStruct((t, LANES), F32),
        jax.ShapeDtypeStruct((t, LANES), BF16),
        jax.ShapeDtypeStruct((t, LANES), BF16),
        jax.ShapeDtypeStruct((t, D_MODEL), BF16),
        jax.ShapeDtypeStruct((ni * nseg_tile, 2, D_MODEL), F32),
        jax.ShapeDtypeStruct((t, D_MODEL), F32),
        jax.ShapeDtypeStruct((t, D_MODEL), F32),
    ]
    out_specs = [
        pl.BlockSpec((TM, TN), lambda i, j: (i, jnp.clip(j - J_Q, 0, J_KV - J_Q - 1))),
        pl.BlockSpec((TM * N_KV_HEADS, HEAD_DIM), row),
        pl.BlockSpec((TM, KV_W), row),
        pl.BlockSpec((TM * N_KV_HEADS, HEAD_DIM), row),
        pl.BlockSpec((TM, KV_W), row),
        pl.BlockSpec((TM, TN), lambda i, j: (i, jnp.clip(j - J_QI, 0, J_CONV - J_QI - 1))),
        pl.BlockSpec((TM, LANES), row),
        pl.BlockSpec((TM, LANES), row),
        pl.BlockSpec((TM, LANES), row),
        pl.BlockSpec((TM, TN), lambda i, j: (i, cblk(j))),
        pl.BlockSpec((nseg_tile, 2, TN), lambda i, j: (i, 0, cblk(j))),
        pl.BlockSpec((TM, TN), lambda i, j: (i, jnp.clip(j - J_GA, 0, N_CBLK - 1))),
        pl.BlockSpec((TM, TN), lambda i, j: (i, jnp.clip(j - J_GC, 0, N_CBLK - 1))),
    ]
    return pl.pallas_call(
        functools.partial(_proj_kernel, seg=min(seg, TM), carried=carried),
        grid=(ni, NJ),
        in_specs=in_specs,
        out_specs=out_specs,
        out_shape=out_shape,
        scratch_shapes=[
            pltpu.VMEM((TM, D_MODEL), BF16),
            pltpu.VMEM((TM, TN), F32),
            pltpu.VMEM((TM, TN), F32),
            pltpu.VMEM((N_CBLK, 8, TN), F32),
        ],
        compiler_params=pltpu.CompilerParams(
            dimension_semantics=("arbitrary", "arbitrary"), vmem_limit_bytes=VMEM_LIMIT),
        name="proj",
    )(x, g, w, w_kiwi, c128, d128, c64, d64, conv_w, state)


def _select_kernel(qi_ref, kiwi_ref, kia_ref, kib_ref, bias_ref,
                   sc_s, wb_s, lg_s, lh_s, cand_s, thr_s, cnt_s, top_s, *,
                   tq, spb, nkb_total, topk, causal, n_valid, q_pos0):
    n = pl.program_id(0)
    npair = N_IDX_HEADS // 2
    blk = spb * LANES
    kiwi = kiwi_ref[...]
    wscale = (IDX_DIM ** -0.5) * (N_IDX_HEADS ** -0.5)
    for h in range(N_IDX_HEADS):
        wb_s[h] = jnp.broadcast_to(kiwi[:, IDX_DIM + h:IDX_DIM + h + 1], (tq, LANES)) * wscale
    q2 = jnp.concatenate([qi_ref[:, p * LANES:(p + 1) * LANES] for p in range(npair)], axis=0)

    if causal:
        qpos0 = n * tq
        nkb = (qpos0 + tq + blk - 1) // blk
    else:
        qpos0 = q_pos0
        nkb = nkb_total
    lane = lax.broadcasted_iota(jnp.int32, (tq, LANES), 1)
    qchunk = (qpos0 + lax.broadcasted_iota(jnp.int32, (tq, LANES), 0)) >> CHUNK_SHIFT
    nt = (((1,), (1,)), ((), ()))

    def logits_into(lg_ref, kb):
        base = pl.multiple_of(jnp.minimum(kb, nkb_total - 1) * blk, blk)
        lg_ref[0] = lax.dot_general(q2, kia_ref[pl.ds(base, blk), :], nt, preferred_element_type=F32)
        lg_ref[1] = lax.dot_general(q2, kib_ref[pl.ds(base, blk), :], nt, preferred_element_type=F32)

    def head_sum(lg_ref, kb, carry):
        m1, m2 = carry
        base = kb * blk
        for c in range(spb):
            cs = slice(c * LANES, (c + 1) * LANES)
            acc = jnp.zeros((tq, LANES), F32)
            for p in range(npair):
                rs = slice(p * tq, (p + 1) * tq)
                acc = acc + jnp.maximum(lg_ref[0, rs, cs], 0.0) * wb_s[2 * p]
                acc = acc + jnp.maximum(lg_ref[1, rs, cs], 0.0) * wb_s[2 * p + 1]
            col = base + c * LANES + lane
            adm = jnp.logical_and((col >> CHUNK_SHIFT) <= qchunk, col < n_valid)
            sc = jnp.where(adm, acc, NEG_INF)
            m2 = jnp.maximum(m2, jnp.minimum(m1, sc))
            m1 = jnp.maximum(m1, sc)
            sc_s[kb * spb + c] = sc
        return m1, m2

    logits_into(lg_s, 0)

    def score_pair(i, carry):
        logits_into(lh_s, 2 * i + 1)
        carry = head_sum(lg_s, 2 * i, carry)
        logits_into(lg_s, 2 * i + 2)
        return head_sum(lh_s, 2 * i + 1, carry)

    neg = jnp.full((tq, LANES), NEG_INF, F32)
    m1, m2 = lax.fori_loop(0, nkb // 2, score_pair, (neg, neg))
    top_s[0] = m1
    top_s[1] = m2

    @pl.when(nkb % 2 == 1)
    def _():
        last1, last2 = head_sum(lg_s, nkb - 1, (top_s[0], top_s[1]))
        top_s[0] = last1
        top_s[1] = last2

    m1, m2 = top_s[0], top_s[1]

    zeros = jnp.zeros((tq, LANES), F32)
    kf = float(topk)

    def count_all(thr_f, strict=False):
        def body(kb, acc):
            for c in range(spb):
                s = sc_s[kb * spb + c]
                acc = acc + jnp.where(s > thr_f if strict else s >= thr_f, 1.0, 0.0)
            return acc
        return jnp.sum(lax.fori_loop(0, nkb, body, zeros), axis=1, keepdims=True)

    def count_cand(thr_f):
        acc = zeros
        for i in range(CAND):
            acc = acc + jnp.where(cand_s[i] >= thr_f, 1.0, 0.0)
        return jnp.sum(acc, axis=1, keepdims=True)

    ones = jnp.ones((tq, LANES), jnp.int32)
    lo0 = _sortable(jnp.min(m2, axis=1, keepdims=True)) * ones
    hi0 = _sortable(jnp.max(m1, axis=1, keepdims=True)) * ones + 1

    def bisect(count):
        def unresolved(lo_k, hi_k, c_lo):
            open_ = jnp.logical_and(c_lo != kf, (hi_k - lo_k) != 1)
            return jnp.max(jnp.where(open_, 1.0, 0.0))

        def cond(st):
            return jnp.logical_and(st[0] <= KEY_BITS, st[-1] > 0.0)

        def body(st):
            it, lo_k, hi_k, c_lo, _ = st
            for _ in range(BISECT_UNROLL):
                mid = lo_k + lax.shift_right_logical(hi_k - lo_k, 1)
                cnt = count(_unsortable(mid))
                ge = cnt >= kf
                lo_k = jnp.where(ge, mid, lo_k)
                hi_k = jnp.where(ge, hi_k, mid)
                c_lo = jnp.where(ge, cnt, c_lo)
            return it + BISECT_UNROLL, lo_k, hi_k, c_lo, unresolved(lo_k, hi_k, c_lo)

        c0 = jnp.full((tq, LANES), -1.0, F32)
        st = lax.while_loop(cond, body, (jnp.int32(0), lo0, hi0, c0, unresolved(lo0, hi0, c0)))
        return st[1], st[3]

    nsl = nkb * spb
    nchunk = (nsl + CAND - 1) // CAND
    pad_tile = jnp.full((tq, LANES), PAD_SCORE, F32)

    def pad(sidx, carry):
        sc_s[sidx] = pad_tile
        return carry

    lax.fori_loop(nsl, nchunk * CAND, pad, 0)

    def gather_rows(rg, carry):
        r0 = pl.multiple_of(rg * 8, 8)

        def chunk(ch, cand):
            new = _sort_desc([sc_s[ch * CAND + i, pl.ds(r0, 8), :] for i in range(CAND)])
            return tuple(_bitonic_merge_desc([jnp.maximum(cand[i], new[CAND - 1 - i]) for i in range(CAND)]))

        start = tuple(jnp.full((8, LANES), PAD_SCORE, F32) for _ in range(CAND))
        cand = lax.fori_loop(0, nchunk, chunk, start)
        for i in range(CAND):
            cand_s[i, pl.ds(r0, 8), :] = cand[i]
        return carry

    lax.fori_loop(0, tq // 8, gather_rows, 0)

    thr_c, cnt_c = bisect(count_cand)
    thr_s[...] = thr_c
    cnt_s[...] = cnt_c
    last = jnp.max(cand_s[CAND - 1], axis=1, keepdims=True)
    covered = jnp.logical_or(last < _unsortable(thr_c), last <= 0.5 * NEG_INF)

    @pl.when(jnp.min(jnp.where(covered, 1.0, 0.0)) == 0.0)
    def _():
        thr_a, cnt_a = bisect(count_all)
        thr_s[...] = thr_a
        cnt_s[...] = cnt_a

    thr = _unsortable(thr_s[...])
    c_thr = cnt_s[...]

    tied = jnp.max(jnp.where(jnp.logical_and(c_thr != kf, thr > 0.5 * NEG_INF), 1.0, 0.0)) > 0.0

    @pl.when(jnp.logical_not(tied))
    def _():
        def emit(kb, carry):
            for c in range(spb):
                s = sc_s[kb * spb + c]
                sel = jnp.logical_and(s >= thr, s > 0.5 * NEG_INF)
                bias_ref[kb * spb + c] = jnp.where(sel, 0.0, NEG_INF).astype(BF16)
            return carry

        lax.fori_loop(0, nkb, emit, 0)

    @pl.when(tied)
    def _():
        need = kf - count_all(thr, strict=True)
        tri = (lax.broadcasted_iota(jnp.int32, (LANES, LANES), 0)
               <= lax.broadcasted_iota(jnp.int32, (LANES, LANES), 1)).astype(BF16)

        def emit(kb, seen):
            for c in range(spb):
                s = sc_s[kb * spb + c]
                eq = jnp.where(s == thr, 1.0, 0.0)
                rank = seen + jnp.dot(eq.astype(BF16), tri, preferred_element_type=F32)
                keep = jnp.logical_or(s > thr, jnp.logical_and(s == thr, rank <= need))
                sel = jnp.logical_and(keep, s > 0.5 * NEG_INF)
                bias_ref[kb * spb + c] = jnp.where(sel, 0.0, NEG_INF).astype(BF16)
                seen = seen + jnp.sum(eq, axis=1, keepdims=True)
            return seen

        lax.fori_loop(0, nkb, emit, zeros)

    def fill(kb, carry):
        for c in range(spb):
            bias_ref[kb * spb + c] = jnp.full((tq, LANES), NEG_INF, BF16)
        return carry

    lax.fori_loop(nkb, nkb_total, fill, 0)


def _select(qi, kiwi, kia, kib, *, tq, spb, topk, causal, n_valid, q_pos0):
    t = qi.shape[0]
    nb = t // tq
    lk = kia.shape[1]
    nslab = lk // LANES
    nkb_total = nslab // spb
    assert topk <= 2 * LANES and nslab >= 2, "the bisection's starting lower bound needs two keys per lane"

    def kmap(n):
        return (0 if causal else n, 0, 0)

    return pl.pallas_call(
        functools.partial(_select_kernel, tq=tq, spb=spb, nkb_total=nkb_total, topk=topk,
                          causal=causal, n_valid=n_valid, q_pos0=q_pos0),
        grid=(nb,),
        in_specs=[
            pl.BlockSpec((tq, IQ_W), lambda n: (n, 0)),
            pl.BlockSpec((tq, LANES), lambda n: (n, 0)),
            pl.BlockSpec((None, lk, LANES), kmap),
            pl.BlockSpec((None, lk, LANES), kmap),
        ],
        out_specs=pl.BlockSpec((nslab, tq, LANES), lambda n: (0, n, 0)),
        out_shape=jax.ShapeDtypeStruct((nslab, t, LANES), BF16),
        scratch_shapes=[
            pltpu.VMEM((nslab + CAND, tq, LANES), F32),
            pltpu.VMEM((N_IDX_HEADS, tq, LANES), F32),
            pltpu.VMEM((2, N_IDX_HEADS // 2 * tq, spb * LANES), F32),
            pltpu.VMEM((2, N_IDX_HEADS // 2 * tq, spb * LANES), F32),
            pltpu.VMEM((CAND, tq, LANES), F32),
            pltpu.VMEM((tq, LANES), jnp.int32),
            pltpu.VMEM((tq, LANES), F32),
            pltpu.VMEM((2, tq, LANES), F32),
        ],
        compiler_params=pltpu.CompilerParams(
            dimension_semantics=("arbitrary",), vmem_limit_bytes=VMEM_LIMIT),
        name="select",
    )(qi, kiwi, kia, kib)


FIXED_REF_LIMIT = 2.0 ** 40

def _attend_kernel(qb_ref, kb_ref, kbat_ref, last_ref, q_ref, k_ref, v_ref, b_ref, o_ref,
                   m_s, l_s, acc_s, par_s, exact_s, *, tq, spb):
    s = pl.program_id(0)
    first = kb_ref[s] == 0

    biases = [b_ref[c].astype(F32)[None] for c in range(spb)]
    ones = jnp.ones((spb * LANES, LANES), BF16)

    def masked_scores(g):
        qg = [q_ref[:, (g * GROUP + h) * HEAD_DIM:(g * GROUP + h + 1) * HEAD_DIM] for h in range(GROUP)]
        q4 = jnp.concatenate(qg, axis=0)
        kg = k_ref[:, g * HEAD_DIM:(g + 1) * HEAD_DIM]
        sc = lax.dot_general(q4, kg, (((1,), (1,)), ((), ())), preferred_element_type=F32)
        return [(sc[:, c * LANES:(c + 1) * LANES].reshape(GROUP, tq, LANES) + biases[c]).reshape(GROUP * tq, LANES)
                for c in range(spb)]

    def weighted_values(g, m_ref, slabs):
        pmat = jnp.concatenate([jnp.exp2(sl - m_ref).astype(BF16) for sl in slabs], axis=1)
        v1 = jnp.concatenate([v_ref[:, g * HEAD_DIM:(g + 1) * HEAD_DIM], ones], axis=1)
        pv = jnp.dot(pmat, v1, preferred_element_type=F32)
        return pv[:, 0:HEAD_DIM], pv[:, HEAD_DIM:HEAD_DIM + LANES]

    @pl.when(first)
    def _():
        par_s[0] = 0
        exact_s[0] = 1
        m_s[...] = jnp.full(m_s.shape, 0.1 * NEG_INF, F32)
        l_s[0] = jnp.zeros(l_s.shape[1:], F32)
        acc_s[0] = jnp.zeros(acc_s.shape[1:], F32)

    @pl.when(jnp.logical_not(first))
    def _():
        par = par_s[0]
        worst = jnp.zeros((GROUP * tq, LANES), F32)
        for g in range(N_KV_HEADS):
            pv, psum = weighted_values(g, m_s[g], masked_scores(g))
            acc_new = acc_s[par, g] + pv
            l_new = l_s[par, g] + psum
            acc_s[1 - par, g] = acc_new
            l_s[1 - par, g] = l_new
            worst = worst + l_new + jnp.abs(acc_new)
        ok = jnp.min(jnp.where(worst < FIXED_REF_LIMIT, 1.0, 0.0)) > 0.5
        exact_s[0] = jnp.where(ok, 0, 1)
        par_s[0] = jnp.where(ok, 1 - par, par)

    @pl.when(exact_s[0] == 1)
    def _():
        par = par_s[0]
        for g in range(N_KV_HEADS):
            slabs = masked_scores(g)
            mx = slabs[0]
            for sl in slabs[1:]:
                mx = jnp.maximum(mx, sl)
            m_prev = m_s[g]
            m_new = jnp.maximum(m_prev, jnp.max(mx, axis=1, keepdims=True))
            alpha = jnp.exp2(m_prev - m_new)
            pv, psum = weighted_values(g, m_new, slabs)
            acc_s[par, g] = alpha * acc_s[par, g] + pv
            l_s[par, g] = alpha * l_s[par, g] + psum
            m_s[g] = m_new

    @pl.when(last_ref[s] == 1)
    def _():
        par = par_s[0]
        for g in range(N_KV_HEADS):
            o = acc_s[par, g] / l_s[par, g]
            for h in range(GROUP):
                col = (g * GROUP + h) * HEAD_DIM
                o_ref[:, col:col + HEAD_DIM] = o[h * tq:(h + 1) * tq].astype(BF16)


def _attend(q, k_all, v_all, bias, sched, *, tq, spb):
    t = q.shape[0]
    qb, kb, kbat, last = sched
    nsteps = qb.shape[0]
    blk = spb * LANES
    grid_spec = pltpu.PrefetchScalarGridSpec(
        num_scalar_prefetch=4,
        grid=(nsteps,),
        in_specs=[
            pl.BlockSpec((tq, Q_W), lambda s, qb, kb, kbat, last: (qb[s], 0)),
            pl.BlockSpec((None, blk, KV_W), lambda s, qb, kb, kbat, last: (kbat[s], kb[s], 0)),
            pl.BlockSpec((None, blk, KV_W), lambda s, qb, kb, kbat, last: (kbat[s], kb[s], 0)),
            pl.BlockSpec((spb, tq, LANES), lambda s, qb, kb, kbat, last: (kb[s], qb[s], 0)),
        ],
        out_specs=pl.BlockSpec((tq, Q_W), lambda s, qb, kb, kbat, last: (qb[s], 0)),
        scratch_shapes=[
            pltpu.VMEM((N_KV_HEADS, GROUP * tq, LANES), F32),
            pltpu.VMEM((2, N_KV_HEADS, GROUP * tq, LANES), F32),
            pltpu.VMEM((2, N_KV_HEADS, GROUP * tq, HEAD_DIM), F32),
            pltpu.SMEM((1,), jnp.int32),
            pltpu.SMEM((1,), jnp.int32),
        ],
    )
    return pl.pallas_call(
        functools.partial(_attend_kernel, tq=tq, spb=spb),
        grid_spec=grid_spec,
        out_shape=jax.ShapeDtypeStruct((t, Q_W), BF16),
        compiler_params=pltpu.CompilerParams(
            dimension_semantics=("arbitrary",), vmem_limit_bytes=VMEM_LIMIT),
        name="attend",
    )(qb, kb, kbat, last, q, k_all, v_all, bias)


def _causal_schedule(t, tq, blk):
    qb, kb, last = [], [], []
    for n in range(t // tq):
        nk = ((n + 1) * tq + blk - 1) // blk
        for k in range(nk):
            qb.append(n)
            kb.append(k)
            last.append(1 if k == nk - 1 else 0)
    z = np.zeros(len(qb), np.int32)
    return (jnp.asarray(qb, jnp.int32), jnp.asarray(kb, jnp.int32), jnp.asarray(z), jnp.asarray(last, jnp.int32))


def _batched_schedule(nbatch, nk):
    qb = np.repeat(np.arange(nbatch, dtype=np.int32), nk)
    kb = np.tile(np.arange(nk, dtype=np.int32), nbatch)
    last = (kb == nk - 1).astype(np.int32)
    return (jnp.asarray(qb), jnp.asarray(kb), jnp.asarray(qb), jnp.asarray(last))


def _merge_kernel(o_ref, yc_ref, ga_ref, gc_ref, woa_ref, woc_ref, wout_ref, x_ref, gq_ref, gp_ref,
                  x1_ref, xn2_ref, mg_s, m_s):
    j = pl.program_id(1)
    nblk = D_MODEL // MERGE_TN

    @pl.when(j < nblk)
    def _():
        a = jnp.dot(o_ref[...], woa_ref[...], preferred_element_type=F32)
        c = jnp.dot(yc_ref[...], woc_ref[...], preferred_element_type=F32)
        mg_s[j] = (ga_ref[...] * a + gc_ref[...] * c).astype(BF16)

    @pl.when(j >= nblk)
    def _():
        mg = jnp.concatenate([mg_s[b] for b in range(nblk)], axis=1)
        m_s[j - nblk] = jnp.dot(mg, wout_ref[...], preferred_element_type=F32)

    @pl.when(j == 2 * nblk - 1)
    def _():
        m = jnp.concatenate([m_s[b] for b in range(nblk)], axis=1)
        x1 = x_ref[...] + _rms(m, gq_ref[...])
        x1_ref[...] = x1
        xn2_ref[...] = _rms(x1, gp_ref[...]).astype(BF16)


def _merge(o, yc, ga, gc, woa, woc, wout, x, gq, gp):
    t = x.shape[0]
    tn = MERGE_TN
    nblk = D_MODEL // tn

    def row(i, j):
        return (i, 0)

    def lo(i, j):
        return (i, jnp.minimum(j, nblk - 1))

    return pl.pallas_call(
        _merge_kernel,
        grid=(t // TM, 2 * nblk),
        in_specs=[
            pl.BlockSpec((TM, Q_W), row),
            pl.BlockSpec((TM, D_MODEL), row),
            pl.BlockSpec((TM, tn), lo),
            pl.BlockSpec((TM, tn), lo),
            pl.BlockSpec((Q_W, tn), lambda i, j: (0, jnp.minimum(j, nblk - 1))),
            pl.BlockSpec((D_MODEL, tn), lambda i, j: (0, jnp.minimum(j, nblk - 1))),
            pl.BlockSpec((D_MODEL, tn), lambda i, j: (0, jnp.maximum(j - nblk, 0))),
            pl.BlockSpec((TM, D_MODEL), row),
            pl.BlockSpec((1, D_MODEL), lambda i, j: (0, 0)),
            pl.BlockSpec((1, D_MODEL), lambda i, j: (0, 0)),
        ],
        out_specs=[pl.BlockSpec((TM, D_MODEL), row), pl.BlockSpec((TM, D_MODEL), row)],
        out_shape=[jax.ShapeDtypeStruct((t, D_MODEL), F32), jax.ShapeDtypeStruct((t, D_MODEL), BF16)],
        scratch_shapes=[pltpu.VMEM((nblk, TM, tn), BF16), pltpu.VMEM((nblk, TM, tn), F32)],
        compiler_params=pltpu.CompilerParams(
            dimension_semantics=("arbitrary", "arbitrary"), vmem_limit_bytes=VMEM_LIMIT),
        name="merge",
    )(o, yc, ga, gc, woa, woc, wout, x, gq, gp)


def _ffn_kernel(xn_ref, wg_ref, wv_ref, cwg_ref, cwv_ref, wd_ref, x1_ref, gq_ref, stg_ref, stv_ref,
                y_ref, ng_ref, nv_ref, acc_s, cg_s, cv_s, *, seg, carried):
    i = pl.program_id(0)
    jf = pl.program_id(1)
    nseg = TM // seg

    @pl.when(jf == 0)
    def _():
        acc_s[...] = jnp.zeros(acc_s.shape, F32)

    if carried:
        @pl.when(i == 0)
        def _():
            for carry_s, st_ref in ((cg_s, stg_ref), (cv_s, stv_ref)):
                carry_s[jf] = jnp.zeros((8, TF), F32)
                carry_s[jf, 0:2, :] = st_ref[0]

    xn = xn_ref[...]

    def branch(w_ref, cw_ref, st_ref, carry_s, new_ref):
        up = jnp.dot(xn, w_ref[...], preferred_element_type=F32)
        if carried:
            prevs = [carry_s[jf, 0:2, :]]
        else:
            prevs = [st_ref[s] for s in range(nseg)]
        y, news = _conv_tile(up, cw_ref[...], prevs, seg)
        for s in range(nseg):
            new_ref[s] = news[s]
        if carried:
            carry_s[jf, 0:2, :] = news[0]
        return y

    gate = branch(wg_ref, cwg_ref, stg_ref, cg_s, ng_ref)
    val = branch(wv_ref, cwv_ref, stv_ref, cv_s, nv_ref)
    c0 = np.float32(np.sqrt(2.0 / np.pi))
    gelu = 0.5 * gate * (1.0 + jnp.tanh(c0 * (gate + 0.044715 * (gate * gate * gate))))
    hid = (gelu * val).astype(BF16)
    acc_s[...] += jnp.dot(hid, wd_ref[...], preferred_element_type=F32)

    @pl.when(jf == pl.num_programs(1) - 1)
    def _():
        y_ref[...] = x1_ref[...] + _rms(acc_s[...], gq_ref[...])


def _ffn(xn2, w_up, cw, w_down, x1, gq, state, seg):
    t = x1.shape[0]
    nf = D_FF // TF
    carried = seg == t
    nseg_tile = 1 if carried else TM // seg

    def row(i, j):
        return (i, 0)

    def stg(i, j):
        return (0 if carried else i, 0, j)

    def stv(i, j):
        return (0 if carried else i, 0, j + nf)

    return pl.pallas_call(
        functools.partial(_ffn_kernel, seg=min(seg, TM), carried=carried),
        grid=(t // TM, nf),
        in_specs=[
            pl.BlockSpec((TM, D_MODEL), row),
            pl.BlockSpec((D_MODEL, TF), lambda i, j: (0, j)),
            pl.BlockSpec((D_MODEL, TF), lambda i, j: (0, j + nf)),
            pl.BlockSpec((3, TF), lambda i, j: (0, j)),
            pl.BlockSpec((3, TF), lambda i, j: (0, j + nf)),
            pl.BlockSpec((TF, D_MODEL), lambda i, j: (j, 0)),
            pl.BlockSpec((TM, D_MODEL), row),
            pl.BlockSpec((1, D_MODEL), lambda i, j: (0, 0)),
            pl.BlockSpec((nseg_tile, 2, TF), stg),
            pl.BlockSpec((nseg_tile, 2, TF), stv),
        ],
        out_specs=[
            pl.BlockSpec((TM, D_MODEL), row),
            pl.BlockSpec((nseg_tile, 2, TF), lambda i, j: (i, 0, j)),
            pl.BlockSpec((nseg_tile, 2, TF), lambda i, j: (i, 0, j)),
        ],
        out_shape=[
            jax.ShapeDtypeStruct((t, D_MODEL), F32),
            jax.ShapeDtypeStruct((t // TM * nseg_tile, 2, D_FF), F32),
            jax.ShapeDtypeStruct((t // TM * nseg_tile, 2, D_FF), F32),
        ],
        scratch_shapes=[
            pltpu.VMEM((TM, D_MODEL), F32),
            pltpu.VMEM((nf, 8, TF), F32),
            pltpu.VMEM((nf, 8, TF), F32),
        ],
        compiler_params=pltpu.CompilerParams(
            dimension_semantics=("arbitrary", "arbitrary"), vmem_limit_bytes=VMEM_LIMIT),
        name="ffn",
    )(xn2, w_up, w_up, cw, cw, w_down, x1, gq, state, state)


PACK_ROWS = 1024


def _pack_kernel(ck_ref, cv_ref, ko_ref, vo_ref, *, rows):
    for src, dst in ((ck_ref, ko_ref), (cv_ref, vo_ref)):
        for g in range(N_KV_HEADS):
            dst[:, g * HEAD_DIM:(g + 1) * HEAD_DIM] = src[pl.ds(g, rows, stride=N_KV_HEADS), :].astype(BF16)


def _tail_kernel(kn_ref, vn_ref, ki_ref, vi_ref, ko_ref, vo_ref, *, seg):
    del ki_ref, vi_ref
    for src, dst in ((kn_ref, ko_ref), (vn_ref, vo_ref)):
        dst[0:seg, :] = src[...]
        if dst.shape[0] > seg:
            dst[seg:, :] = jnp.zeros((dst.shape[0] - seg, KV_W), BF16)


def _pack_cache(cache_k, cache_v, k_new, v_new, seg, lk):
    nb, plen = cache_k.shape[0], cache_k.shape[1]
    rows = int(np.gcd(plen, PACK_ROWS))
    tail = lk - plen
    assert rows % 16 == 0 and tail % 16 == 0 and plen % tail == 0 and tail >= seg
    cspec = pl.BlockSpec((None, rows * N_KV_HEADS, HEAD_DIM), lambda b, r: (b, r, 0))
    ospec = pl.BlockSpec((None, rows, KV_W), lambda b, r: (b, r, 0))
    flat = (nb, plen * N_KV_HEADS, HEAD_DIM)
    slab = jax.ShapeDtypeStruct((nb, lk, KV_W), BF16)
    k_all, v_all = pl.pallas_call(
        functools.partial(_pack_kernel, rows=rows),
        grid=(nb, plen // rows),
        in_specs=[cspec, cspec],
        out_specs=[ospec, ospec],
        out_shape=[slab, slab],
        compiler_params=pltpu.CompilerParams(
            dimension_semantics=("arbitrary", "arbitrary"), vmem_limit_bytes=VMEM_LIMIT),
        name="pack_cache",
    )(cache_k.reshape(flat), cache_v.reshape(flat))
    nspec = pl.BlockSpec((seg, KV_W), lambda b: (b, 0))
    anyspec = pl.BlockSpec(memory_space=pl.ANY)
    tspec = pl.BlockSpec((None, tail, KV_W), lambda b: (b, plen // tail, 0))
    return pl.pallas_call(
        functools.partial(_tail_kernel, seg=seg),
        grid=(nb,),
        in_specs=[nspec, nspec, anyspec, anyspec],
        out_specs=[tspec, tspec],
        out_shape=[slab, slab],
        input_output_aliases={2: 0, 3: 1},
        compiler_params=pltpu.CompilerParams(dimension_semantics=("arbitrary",), vmem_limit_bytes=VMEM_LIMIT),
        name="pack_tail",
    )(k_new, v_new, k_all, v_all)


def _rope_tables(pos, rot, width):
    half = rot // 2
    freqs = ROPE_THETA ** (-jnp.arange(half, dtype=F32) / half)
    ang = pos.astype(F32)[:, None] * freqs[None, :]
    cos, sin = jnp.cos(ang), jnp.sin(ang)
    t = pos.shape[0]
    c = jnp.concatenate([cos, cos, jnp.ones((t, width - rot), F32)], axis=1)
    d = jnp.concatenate([-sin, sin, jnp.zeros((t, width - rot), F32)], axis=1)
    reps = LANES // width
    return jnp.tile(c, (1, reps)), jnp.tile(d, (1, reps))


def _relayout_w_in(w):
    o = np.cumsum([0, Q_W, KV_W, KV_W, IQ_W, IDX_DIM, N_IDX_HEADS, D_MODEL, D_MODEL, D_MODEL, D_MODEL, D_MODEL])
    q, k, v, qi = (w[:, o[a]:o[a + 1]] for a in range(4))
    kiwi = w[:, o[4]:o[6]]
    cb, cc, ch, ga, gc = (w[:, o[a]:o[a + 1]] for a in range(6, 11))
    parts = [q, k, v, qi]
    for c in range(N_CBLK):
        sl = slice(c * TN, (c + 1) * TN)
        parts += [cb[:, sl], cc[:, sl], ch[:, sl]]
    parts += [ga, gc]
    w_kiwi = jnp.pad(kiwi, ((0, 0), (0, LANES - kiwi.shape[1]))).astype(BF16)
    return jnp.concatenate([p.astype(BF16) for p in parts], axis=1), w_kiwi


def _stream(x, pos, seg, past, weights, *, tq, tq_att, spb_sel, spb_att):
    (g_mp, g_mq, w_in_r, conv_w, woa, woc, wout, g_fp, g_fq, w_up, fconv_w, w_down) = weights
    t = x.shape[0]
    nseq = t // seg
    tabs = _rope_tables(pos, ROT_DIM, HEAD_DIM) + _rope_tables(pos, IDX_ROT_DIM, IDX_DIM)
    if past is None:
        conv_state = jnp.zeros((nseq, 2, D_MODEL), F32)
        ffn_state = jnp.zeros((nseq, 2, 2 * D_FF), F32)
    else:
        conv_state, ffn_state = past[3], past[4]

    (q, k, kb, v, vb, qi, kiwi, kia, kib, yc, conv_new, ga, gc) = _proj(
        x, g_mp, w_in_r, tabs, conv_w, conv_state, seg)

    if past is None:
        k_all, v_all = kb[None], vb[None]
        kia_all, kib_all = kia[None], kib[None]
        n_keys = t
        sched = _causal_schedule(t, tq_att, spb_att * LANES)
        q_pos0 = 0
    else:
        cache_k, cache_v, cache_ki = past[0], past[1], past[2]
        plen = cache_k.shape[1]
        n_keys = plen + seg
        lk = -(-n_keys // (spb_att * LANES)) * (spb_att * LANES)
        pad = lk - n_keys

        def cat(c, new, width):
            parts = [c, new.reshape(nseq, seg, width)]
            if pad:
                parts.append(jnp.zeros((nseq, pad, width), BF16))
            return jnp.concatenate(parts, axis=1)

        k_all, v_all = _pack_cache(cache_k, cache_v, kb, vb, seg, lk)
        cki = cache_ki.astype(BF16)
        zk = jnp.zeros_like(cki)
        kia_all = cat(jnp.concatenate([cki, zk], axis=-1), kia, LANES)
        kib_all = cat(jnp.concatenate([zk, cki], axis=-1), kib, LANES)
        sched = _batched_schedule(nseq, lk // (spb_att * LANES))
        q_pos0 = plen
    topk = min(TOPK_MAX, n_keys // 4)

    bias = _select(qi, kiwi, kia_all, kib_all, tq=tq, spb=spb_sel, topk=topk,
                   causal=past is None, n_valid=n_keys, q_pos0=q_pos0)
    o = _attend(q, k_all, v_all, bias, sched, tq=tq_att, spb=spb_att)
    x1, xn2 = _merge(o, yc, ga, gc, woa, woc, wout, x, g_mq, g_fp)
    y, ffn_g, ffn_v = _ffn(xn2, w_up, fconv_w, w_down, x1, g_fq, ffn_state, seg)
    ffn_new = jnp.concatenate([ffn_g, ffn_v], axis=-1)
    return y, k, v, kiwi[:, :IDX_DIM], conv_new[-nseq:], ffn_new[-nseq:]


def kernel(x_prompt, x_sample, cache_k, cache_v, cache_k_idx, state_conv, state_ffn_conv, norm_mix_pre, norm_mix_post, w_in, conv_w, w_o_attn, w_o_conv, w_out, norm_ffn_pre, norm_ffn_post, w_ffn_up, ffn_conv_w, w_ffn_down):
    depth = w_in.shape[0]
    assert depth == 1, "single-layer step"
    b, seq, _ = x_prompt.shape
    assert b == 1
    db, dseq, _ = x_sample.shape
    plen = cache_k.shape[2]
    assert dseq == CHUNK and plen % CHUNK == 0

    weights = (
        norm_mix_pre, norm_mix_post, _relayout_w_in(w_in[0]), conv_w[0],
        w_o_attn[0].astype(BF16), w_o_conv[0].astype(BF16), w_out[0].astype(BF16),
        norm_ffn_pre, norm_ffn_post, w_ffn_up[0].astype(BF16), ffn_conv_w[0], w_ffn_down[0].astype(BF16),
    )

    pos_p = jnp.arange(seq, dtype=jnp.int32)
    yp, kp, vp, kip, convp, ffnp = _stream(
        x_prompt.reshape(seq, D_MODEL), pos_p, seq, None, weights, tq=128, tq_att=512, spb_sel=4, spb_att=8)

    pos_s = jnp.tile(jnp.arange(dseq, dtype=jnp.int32) + plen, db)
    past = (cache_k[0], cache_v[0], cache_k_idx[0], state_conv[0], state_ffn_conv[0])
    n_keys = plen + dseq
    spb_s = _sample_slabs(n_keys)
    ys, ks, vs, kis, convs, ffns = _stream(
        x_sample.reshape(db * dseq, D_MODEL), pos_s, dseq, past, weights,
        tq=dseq, tq_att=dseq, spb_sel=spb_s, spb_att=spb_s)

    return (
        yp.reshape(1, seq, D_MODEL), ys.reshape(db, dseq, D_MODEL),
        kp.reshape(1, 1, seq, N_KV_HEADS, HEAD_DIM), vp.reshape(1, 1, seq, N_KV_HEADS, HEAD_DIM),
        kip.reshape(1, 1, seq, IDX_DIM), convp.reshape(1, 1, 2, D_MODEL), ffnp.reshape(1, 1, 2, 2 * D_FF),
        ks.reshape(1, db, dseq, N_KV_HEADS, HEAD_DIM), vs.reshape(1, db, dseq, N_KV_HEADS, HEAD_DIM),
        kis.reshape(1, db, dseq, IDX_DIM), convs.reshape(1, db, 2, D_MODEL), ffns.reshape(1, db, 2, 2 * D_FF),
    )


SAMPLE_MAX_SLABS = 11


def _sample_slabs(n_keys):
    nslab = -(-n_keys // LANES)
    return max(d for d in range(1, SAMPLE_MAX_SLABS + 1) if nslab % d == 0)
```
